```python
import math
import jax, jax.numpy as jnp
from jax import lax
import numpy as np

D_MODEL = 1024
BATCH = 8
SEQ = 4096
DEPTH = 1
DEC_BATCH = 128
DEC_SEQ = 4
PAST_LEN = 8192
PAGE_SIZE = 128

HEAD_DIM = 64
A_GROUPS = 4
A_WIDTH = A_GROUPS * HEAD_DIM
CHUNK = 128
B_HEADS = 8
B_WIDTH = B_HEADS * HEAD_DIM
IDX_HEADS = 8
IDX_DIM = 64
TOPK_MAX = 256
Q_BLOCK = 128
IDX_W_SCALE = (IDX_HEADS * IDX_DIM) ** -0.5
N_MEM = 256
C_HEADS = 4
C_WIDTH = C_HEADS * HEAD_DIM
MIX_WIDTH = A_WIDTH + B_WIDTH + C_WIDTH
ROPE_THETA = 10000.0
EPS = 1e-6
SPLIT_SIZES = (A_WIDTH, A_WIDTH, A_WIDTH, B_WIDTH, B_WIDTH, B_WIDTH, B_WIDTH,
               IDX_HEADS * IDX_DIM, IDX_DIM, IDX_HEADS, C_WIDTH, C_WIDTH)
IN_WIDTH = 3 * A_WIDTH + 4 * B_WIDTH + IDX_HEADS * IDX_DIM + IDX_DIM + IDX_HEADS + 2 * C_WIDTH

kernel_name = "hymba_gmlp_dsa_memxattn_step"


def rmsnorm(x, g):
    xf = x.astype(jnp.float32)
    y = xf * lax.rsqrt(jnp.mean(xf * xf, axis=-1, keepdims=True) + EPS)
    return (y * g.astype(jnp.float32)).astype(x.dtype)


def group_layernorm(v, g):
    shp = v.shape
    vf = v.astype(jnp.float32).reshape(shp[:-1] + (A_GROUPS, HEAD_DIM))
    mu = jnp.mean(vf, axis=-1, keepdims=True)
    var = jnp.mean(jnp.square(vf - mu), axis=-1, keepdims=True)
    vn = ((vf - mu) * lax.rsqrt(var + EPS)).reshape(shp)
    return (vn * g.astype(jnp.float32)).astype(v.dtype)


def rope(x, pos):
    d = x.shape[-1]
    inv = ROPE_THETA ** (-jnp.arange(0, d, 2, dtype=jnp.float32) / d)
    ang = pos.astype(jnp.float32)[:, None] * inv[None, :]
    cos = jnp.cos(ang)[:, None, :].astype(x.dtype)
    sin = jnp.sin(ang)[:, None, :].astype(x.dtype)
    x1, x2 = x[..., : d // 2], x[..., d // 2:]
    return jnp.concatenate([x1 * cos - x2 * sin, x2 * cos + x1 * sin], axis=-1)


def split_columns(p):
    offs = []
    acc = 0
    for s in SPLIT_SIZES[:-1]:
        acc += s
        offs.append(acc)
    return jnp.split(p, offs, axis=-1)


def branch_inputs(x, pos, g_norm, w_in):
    b, s, _ = x.shape
    h = rmsnorm(x, g_norm)
    u, va, ga, q, k, v, gb, qi, ki, wi, qc, gc = split_columns(h @ w_in)
    q = rope(q.reshape(b, s, B_HEADS, HEAD_DIM), pos)
    k = rope(k.reshape(b, s, B_HEADS, HEAD_DIM), pos)
    v = v.reshape(b, s, B_HEADS, HEAD_DIM)
    qi = rope(qi.reshape(b, s, IDX_HEADS, IDX_DIM), pos)
    ki = rope(ki[:, :, None, :], pos)[:, :, 0, :]
    wi = wi * IDX_W_SCALE
    qc = qc.reshape(b, s, C_HEADS, HEAD_DIM)
    return u, va, ga, q, k, v, gb, qi, ki, wi, qc, gc


def chunk_mlp(u, v, w_spatial, b_spatial, g_v):
    b, s, _ = u.shape
    c = min(s, CHUNK)
    nc = s // c
    vn = group_layernorm(v, g_v)
    vb = vn.reshape(b, nc, c, A_GROUPS, HEAD_DIM)
    w = jnp.tril(w_spatial)[:, :c, :c]
    mix = jnp.einsum('gij,bnjgd->bnigd', w, vb) + b_spatial[:, :c].T[None, None, :, :, None]
    out = u.reshape(b, nc, c, A_GROUPS, HEAD_DIM) * mix
    return out.reshape(b, s, A_WIDTH), vn


def indexer_scores(q_idx, w_idx, k_idx):
    r = jax.nn.relu(jnp.einsum('bthd,bsd->bths', q_idx, k_idx)).astype(jnp.float32)
    return jnp.einsum('bths,bth->bts', r, w_idx.astype(jnp.float32))


def sparse_attend(q, k_sel, v_sel, valid):
    s = jnp.einsum('bthd,btkhd->bhtk', q, k_sel).astype(jnp.float32) * (HEAD_DIM ** -0.5)
    s = jnp.where(valid[:, None], s, -jnp.inf)
    p = jax.nn.softmax(s, axis=-1).astype(v_sel.dtype)
    return jnp.einsum('bhtk,btkhd->bthd', p, v_sel)


def sparse_attention_prompt(q, k, v, q_idx, k_idx, w_idx):
    b, s = q.shape[:2]
    topk = min(TOPK_MAX, s // 4)
    nb = s // Q_BLOCK
    bidx = jnp.arange(b)[:, None, None]
    key_pos = jnp.arange(s)

    def block(args):
        i, qb, qib, wb = args
        pos = i * Q_BLOCK + jnp.arange(Q_BLOCK)
        sc = indexer_scores(qib, wb, k_idx)
        sc = jnp.where((key_pos[None, :] <= pos[:, None])[None], sc, -jnp.inf)
        _, sel = lax.top_k(sc, topk)
        valid = sel <= pos[None, :, None]
        return sparse_attend(qb, k[bidx, sel], v[bidx, sel], valid)

    def to_blocks(t):
        return t.reshape((b, nb, Q_BLOCK) + t.shape[2:]).swapaxes(0, 1)

    out = lax.map(block, (jnp.arange(nb), to_blocks(q), to_blocks(q_idx), to_blocks(w_idx)))
    return out.swapaxes(0, 1).reshape(b, s, B_HEADS, HEAD_DIM)


def sparse_attention_sample(q, k_new, v_new, q_idx, k_idx_new, w_idx,
                            cache_k, cache_v, cache_idx_k, page_table):
    db, t = q.shape[:2]
    n_pages = PAST_LEN // PAGE_SIZE
    length = PAST_LEN + t
    topk = min(TOPK_MAX, length // 4)
    pos = PAST_LEN + jnp.arange(t)
    k_idx_past = cache_idx_k[page_table].reshape(db, n_pages * PAGE_SIZE, IDX_DIM)
    k_idx_all = jnp.concatenate([k_idx_past, k_idx_new], axis=1)
    sc = indexer_scores(q_idx, w_idx, k_idx_all)
    sc = jnp.where((jnp.arange(length)[None, :] <= pos[:, None])[None], sc, -jnp.inf)
    _, sel = lax.top_k(sc, topk)
    valid = sel <= pos[None, :, None]
    in_past = (sel < PAST_LEN)[..., None, None]
    bidx = jnp.arange(db)[:, None, None]
    sp = jnp.minimum(sel, PAST_LEN - 1)
    phys = page_table[bidx, sp // PAGE_SIZE]
    off = sp % PAGE_SIZE
    sn = jnp.clip(sel - PAST_LEN, 0, t - 1)
    k_sel = jnp.where(in_past, cache_k[phys, off], k_new[bidx, sn])
    v_sel = jnp.where(in_past, cache_v[phys, off], v_new[bidx, sn])
    return sparse_attend(q, k_sel, v_sel, valid)


def memory_kv(mem, g_mem, w_mem_kv):
    b = mem.shape[0]
    kv = rmsnorm(mem, g_mem) @ w_mem_kv
    mk, mv = jnp.split(kv, 2, axis=-1)
    return (mk.reshape(b, N_MEM, C_HEADS, HEAD_DIM), mv.reshape(b, N_MEM, C_HEADS, HEAD_DIM))


def memory_attention(q, mk, mv):
    s = jnp.einsum('bthd,bmhd->bhtm', q, mk).astype(jnp.float32) * (HEAD_DIM ** -0.5)
    p = jax.nn.softmax(s, axis=-1).astype(mv.dtype)
    return jnp.einsum('bhtm,bmhd->bthd', p, mv)


def merge_output(x, ya, ga, yb, gb, yc, gc, w_out, g_final):
    b, s, _ = x.shape
    mix = jnp.concatenate([
        jax.nn.silu(ga) * ya,
        jax.nn.silu(gb) * yb.reshape(b, s, B_WIDTH),
        jax.nn.silu(gc) * yc.reshape(b, s, C_WIDTH)], axis=-1)
    return rmsnorm(x + mix @ w_out, g_final)


def setup_inputs(seed: int = 0) -> dict:
    key = jax.random.key(seed)
    ks = jax.random.split(key, 20)
    n_pages = PAST_LEN // PAGE_SIZE
    n_used = DEC_BATCH * n_pages
    n_pool = n_used + max(1, n_used // 4)
    f32 = jnp.float32
    nrm = jax.random.normal
    page_table = jax.random.permutation(ks[0], n_pool)[:n_used].reshape(DEC_BATCH, n_pages).astype(jnp.int32)
    return {
        "x_prompt": nrm(ks[1], (BATCH, SEQ, D_MODEL), f32),
        "x_sample": nrm(ks[2], (DEC_BATCH, DEC_SEQ, D_MODEL), f32),
        "mem_prompt": nrm(ks[3], (BATCH, N_MEM, D_MODEL), f32),
        "cache_k": nrm(ks[4], (n_pool, PAGE_SIZE, B_HEADS, HEAD_DIM), f32),
        "cache_v": nrm(ks[5], (n_pool, PAGE_SIZE, B_HEADS, HEAD_DIM), f32),
        "cache_idx_k": nrm(ks[6], (n_pool, PAGE_SIZE, IDX_DIM), f32),
        "cache_mem_k": nrm(ks[7], (DEC_BATCH, N_MEM, C_HEADS, HEAD_DIM), f32),
        "cache_mem_v": nrm(ks[8], (DEC_BATCH, N_MEM, C_HEADS, HEAD_DIM), f32),
        "page_table": page_table,
        "g_norm": 1.0 + 0.02 * nrm(ks[9], (D_MODEL,), f32),
        "w_in": nrm(ks[10], (D_MODEL, IN_WIDTH), f32) * D_MODEL ** -0.5,
        "w_spatial": nrm(ks[11], (A_GROUPS, CHUNK, CHUNK), f32) * CHUNK ** -0.5,
        "b_spatial": 0.1 * nrm(ks[12], (A_GROUPS, CHUNK), f32),
        "g_v": 1.0 + 0.02 * nrm(ks[13], (A_WIDTH,), f32),
        "w_mem_kv": nrm(ks[14], (D_MODEL, 2 * C_WIDTH), f32) * D_MODEL ** -0.5,
        "g_mem": 1.0 + 0.02 * nrm(ks[15], (D_MODEL,), f32),
        "w_out": nrm(ks[16], (MIX_WIDTH, D_MODEL), f32) * MIX_WIDTH ** -0.5,
        "g_final": 1.0 + 0.02 * nrm(ks[17], (D_MODEL,), f32),
    }


def reference(x_prompt, x_sample, mem_prompt, cache_k, cache_v, cache_idx_k, cache_mem_k, cache_mem_v,
              page_table, g_norm, w_in, w_spatial, b_spatial, g_v, w_mem_kv, g_mem, w_out, g_final):
    pos_p = jnp.arange(x_prompt.shape[1], dtype=jnp.int32)
    pos_s = PAST_LEN + jnp.arange(x_sample.shape[1], dtype=jnp.int32)

    u, va, ga, q, k_prompt, v_prompt, gb, qi, idx_k_prompt, wi, qc, gc = branch_inputs(x_prompt, pos_p, g_norm, w_in)
    ya, _ = chunk_mlp(u, va, w_spatial, b_spatial, g_v)
    yb = sparse_attention_prompt(q, k_prompt, v_prompt, qi, idx_k_prompt, wi)
    mem_k_prompt, mem_v_prompt = memory_kv(mem_prompt, g_mem, w_mem_kv)
    yc = memory_attention(qc, mem_k_prompt, mem_v_prompt)
    y_prompt = merge_output(x_prompt, ya, ga, yb, gb, yc, gc, w_out, g_final)

    us, vas, gas, qs, k_sample, v_sample, gbs, qis, idx_k_sample, wis, qcs, gcs = branch_inputs(x_sample, pos_s, g_norm, w_in)
    yas, v_chunk_sample = chunk_mlp(us, vas, w_spatial, b_spatial, g_v)
    ybs = sparse_attention_sample(qs, k_sample, v_sample, qis, idx_k_sample, wis,
                                  cache_k, cache_v, cache_idx_k, page_table)
    ycs = memory_attention(qcs, cache_mem_k, cache_mem_v)
    y_sample = merge_output(x_sample, yas, gas, ybs, gbs, ycs, gcs, w_out, g_final)

    return (y_prompt, y_sample, k_prompt, v_prompt, idx_k_prompt, mem_k_prompt, mem_v_prompt,
            k_sample, v_sample, idx_k_sample, v_chunk_sample)
```

```python
import functools

import jax
import jax.numpy as jnp
from jax import lax
from jax.experimental import pallas as pl
from jax.experimental.pallas import tpu as pltpu

F32 = jnp.float32
BF16 = jnp.bfloat16

HEAD_DIM = 64
HALF = HEAD_DIM // 2
D_MODEL = 1024
A_GROUPS = 4
A_WIDTH = A_GROUPS * HEAD_DIM
CHUNK = 128
B_HEADS = 8
B_WIDTH = B_HEADS * HEAD_DIM
IDX_HEADS = 8
IDX_DIM = 64
TOPK_MAX = 256
N_MEM = 256
C_HEADS = 4
C_WIDTH = C_HEADS * HEAD_DIM
PAGE_SIZE = 128
ROPE_THETA = 10000.0
EPS = 1e-6
IDX_W_SCALE = (IDX_HEADS * IDX_DIM) ** -0.5
ATTN_SCALE = HEAD_DIM ** -0.5

LANES = 128
SUBLANES = 8
NEG = -1e30
FLT_MAX = 3.4028234663852886e38
INT_MIN = -2 ** 31

TM = 256
QB = 256
KC = 256
PAGES_PER_STEP = 8
THR_ROWS = 64
VMEM_LIMIT = 52 * 1024 * 1024

OFF_U, OFF_VA, OFF_GA = 0, 256, 512
OFF_Q, OFF_K, OFF_V, OFF_GB, OFF_QI = 768, 1280, 1792, 2304, 2816
OFF_QC, OFF_GC, OFF_KI2, OFF_WI, W_TOTAL = 3328, 3584, 3840, 3968, 4096


def _cparams(sem):
    return pltpu.CompilerParams(dimension_semantics=sem, vmem_limit_bytes=VMEM_LIMIT)


def _iota(shape, dim):
    return lax.broadcasted_iota(jnp.int32, shape, dim)


def _silu(x):
    return x / (1.0 + jnp.exp(-x))


def _dot(a, b):
    return jnp.dot(a, b, preferred_element_type=F32)


def _dot_nt(a, b):
    return lax.dot_general(a, b, (((1,), (1,)), ((), ())), preferred_element_type=F32)


def _key_to_float(k):
    bits = k ^ ((k >> 31) & jnp.int32(0x7FFFFFFF))
    return lax.bitcast_convert_type(bits, F32)


def _kth_largest(count_ge, kth, shape):
    cnt0 = count_ge(jnp.zeros(shape, F32))
    ok0 = cnt0 >= kth
    key = jnp.where(ok0, jnp.int32(0), jnp.int32(INT_MIN))
    cnt_key = jnp.where(ok0, cnt0, jnp.float32(2 ** 24))

    def bit_body(it, carry):
        key, cnt_key = carry
        trial = key + lax.shift_left(jnp.int32(1), 30 - it)
        cnt = count_ge(_key_to_float(trial))
        ok = cnt >= kth
        return jnp.where(ok, trial, key), jnp.where(ok, cnt, cnt_key)

    key, cnt_key = lax.fori_loop(0, 31, bit_body, (key, cnt_key))
    return _key_to_float(key), cnt_key


def _rope(t, cos, sin):
    outs = []
    first = (_iota((t.shape[0], LANES), 1) % HEAD_DIM) < HALF
    for j in range(t.shape[1] // LANES):
        x = t[:, j * LANES:(j + 1) * LANES]
        partner = jnp.where(first, pltpu.roll(x, LANES - HALF, 1), pltpu.roll(x, HALF, 1))
        outs.append(x * cos + partner * sin)
    return outs[0] if len(outs) == 1 else jnp.concatenate(outs, axis=1)


def _proj_kernel(chunk, want_vn, x_ref, g_ref, w_ref, cos_ref, sin_ref, ws_ref, bs_ref, gv_ref, *outs):
    (q_out, k_out, v_out, qi_out, ki_out, wi_out, kbf_out, vt_out, ki2_out, qz_out, qiz_out,
     wt_out, sgb_out, mixa_out, qc_out, sgc_out) = outs[:16]
    tm = x_ref.shape[0]
    x = x_ref[...]
    ms = jnp.mean(x * x, axis=-1, keepdims=True)
    h = (x * lax.rsqrt(ms + EPS) * g_ref[...]).astype(BF16)
    cos = cos_ref[...]
    sin = sin_ref[...]

    def seg(a, b):
        return _dot(h, w_ref[:, a:b])

    half_id = (_iota((tm, LANES), 1) // HEAD_DIM)

    def head_split(t, out):
        for hd in range(B_HEADS):
            pair = t[:, (hd // 2) * LANES:(hd // 2 + 1) * LANES]
            out[hd] = jnp.where(half_id == hd % 2, pair, 0.0).astype(BF16)

    q = _rope(seg(OFF_Q, OFF_K), cos, sin)
    q_out[...] = q
    head_split(q * ATTN_SCALE, qz_out)

    k = _rope(seg(OFF_K, OFF_V), cos, sin)
    k_out[...] = k
    kbf_out[...] = k.astype(BF16)

    v = seg(OFF_V, OFF_GB)
    v_out[...] = v
    vt_out[0] = v.T.astype(BF16)

    sgb_out[...] = _silu(seg(OFF_GB, OFF_QI)).astype(BF16)

    qi = _rope(seg(OFF_QI, OFF_QC), cos, sin)
    qi_out[...] = qi
    head_split(qi, qiz_out)

    ki2 = _rope(seg(OFF_KI2, OFF_WI), cos, sin)
    ki_out[...] = ki2[:, :IDX_DIM]
    ki2_out[...] = ki2.astype(BF16)

    wi = seg(OFF_WI, W_TOTAL) * IDX_W_SCALE
    wi_out[...] = wi[:, :IDX_HEADS]
    wt_out[...] = wi.T[:IDX_HEADS, :]

    qc_out[...] = (seg(OFF_QC, OFF_GC) * ATTN_SCALE).astype(BF16)
    sgc_out[...] = _silu(seg(OFF_GC, OFF_KI2)).astype(BF16)

    u = seg(OFF_U, OFF_VA)
    va = seg(OFF_VA, OFF_GA)
    ga = seg(OFF_GA, OFF_Q)
    grp_r = _iota((A_WIDTH, A_WIDTH), 0) // HEAD_DIM
    grp_c = _iota((A_WIDTH, A_WIDTH), 1) // HEAD_DIM
    gmat = jnp.where(grp_r == grp_c, 1.0 / HEAD_DIM, 0.0).astype(BF16)

    def group_mean(t):
        hi = t.astype(BF16)
        lo = (t - hi.astype(F32)).astype(BF16)
        return _dot(hi, gmat) + _dot(lo, gmat)

    mu = group_mean(va)
    d = va - mu
    var = group_mean(d * d)
    vn = d * lax.rsqrt(var + EPS) * gv_ref[...]
    if want_vn:
        outs[16][...] = vn
    vnb = vn.astype(BF16)
    row = _iota((CHUNK, CHUNK), 0)
    col = _iota((CHUNK, CHUNK), 1)
    causal = (row >= col) & ((row // chunk) == (col // chunk))
    lane_grp = _iota((CHUNK, A_WIDTH), 1) // HEAD_DIM
    wms = [jnp.where(causal, ws_ref[g], 0.0).astype(BF16) for g in range(A_GROUPS)]
    mixes = []
    for j in range(tm // CHUNK):
        vc = vnb[j * CHUNK:(j + 1) * CHUNK, :]
        m = bs_ref[...]
        for g in range(A_GROUPS):
            m = m + jnp.where(lane_grp == g, _dot(wms[g], vc), 0.0)
        mixes.append(m)
    mix = jnp.concatenate(mixes, axis=0)
    mixa_out[...] = (_silu(ga) * u * mix).astype(BF16)


def _proj(x, g_norm, w_r, cos_t, sin_t, ws, bs, g_v, chunk, table_blocks, want_vn):
    n = x.shape[0]
    nt = n // TM
    row = lambda w: pl.BlockSpec((TM, w), lambda i: (i, 0))
    const2 = lambda a: pl.BlockSpec(a.shape, lambda i: (0, 0))
    in_specs = [
        row(D_MODEL), const2(g_norm), const2(w_r),
        pl.BlockSpec((TM, LANES), lambda i: (i % table_blocks, 0)),
        pl.BlockSpec((TM, LANES), lambda i: (i % table_blocks, 0)),
        pl.BlockSpec(ws.shape, lambda i: (0, 0, 0)), const2(bs), const2(g_v),
    ]
    out_shape = [
        jax.ShapeDtypeStruct((n, B_WIDTH), F32),
        jax.ShapeDtypeStruct((n, B_WIDTH), F32),
        jax.ShapeDtypeStruct((n, B_WIDTH), F32),
        jax.ShapeDtypeStruct((n, IDX_HEADS * IDX_DIM), F32),
        jax.ShapeDtypeStruct((n, IDX_DIM), F32),
        jax.ShapeDtypeStruct((n, IDX_HEADS), F32),
        jax.ShapeDtypeStruct((n, B_WIDTH), BF16),
        jax.ShapeDtypeStruct((nt, B_WIDTH, TM), BF16),
        jax.ShapeDtypeStruct((n, LANES), BF16),
        jax.ShapeDtypeStruct((B_HEADS, n, LANES), BF16),
        jax.ShapeDtypeStruct((IDX_HEADS, n, LANES), BF16),
        jax.ShapeDtypeStruct((IDX_HEADS, n), F32),
        jax.ShapeDtypeStruct((n, B_WIDTH), BF16),
        jax.ShapeDtypeStruct((n, A_WIDTH), BF16),
        jax.ShapeDtypeStruct((n, C_WIDTH), BF16),
        jax.ShapeDtypeStruct((n, C_WIDTH), BF16),
    ]
    out_specs = [
        row(B_WIDTH), row(B_WIDTH), row(B_WIDTH), row(IDX_HEADS * IDX_DIM), row(IDX_DIM), row(IDX_HEADS),
        row(B_WIDTH),
        pl.BlockSpec((1, B_WIDTH, TM), lambda i: (i, 0, 0)),
        row(LANES),
        pl.BlockSpec((B_HEADS, TM, LANES), lambda i: (0, i, 0)),
        pl.BlockSpec((IDX_HEADS, TM, LANES), lambda i: (0, i, 0)),
        pl.BlockSpec((IDX_HEADS, TM), lambda i: (0, i)),
        row(B_WIDTH), row(A_WIDTH), row(C_WIDTH), row(C_WIDTH),
    ]
    if want_vn:
        out_shape.append(jax.ShapeDtypeStruct((n, A_WIDTH), F32))
        out_specs.append(row(A_WIDTH))
    return pl.pallas_call(
        functools.partial(_proj_kernel, chunk, want_vn),
        grid=(nt,), in_specs=in_specs, out_specs=out_specs, out_shape=out_shape,
        compiler_params=_cparams(("parallel",)), name="proj",
    )(x, g_norm, w_r, cos_t, sin_t, ws, bs, g_v)


def _memkv_kernel(m_ref, g_ref, w_ref, mk_out, mv_out):
    x = m_ref[...]
    ms = jnp.mean(x * x, axis=-1, keepdims=True)
    h = (x * lax.rsqrt(ms + EPS) * g_ref[...]).astype(BF16)
    kv = _dot(h, w_ref[...])
    mk_out[...] = kv[:, :C_WIDTH]
    mv_out[...] = kv[:, C_WIDTH:]


def _memkv(mem, g_mem, w_kv):
    n = mem.shape[0]
    return pl.pallas_call(
        _memkv_kernel, grid=(n // TM,),
        in_specs=[pl.BlockSpec((TM, D_MODEL), lambda i: (i, 0)),
                  pl.BlockSpec(g_mem.shape, lambda i: (0, 0)),
                  pl.BlockSpec(w_kv.shape, lambda i: (0, 0))],
        out_specs=[pl.BlockSpec((TM, C_WIDTH), lambda i: (i, 0))] * 2,
        out_shape=[jax.ShapeDtypeStruct((n, C_WIDTH), F32)] * 2,
        compiler_params=_cparams(("parallel",)), name="memkv",
    )(mem, g_mem, w_kv)


def _memattn_kernel(q_ref, mk_ref, mv_ref, o_ref):
    q = q_ref[0]
    mk = mk_ref[0].astype(BF16)
    mv = mv_ref[0].astype(BF16)
    t = q.shape[0]
    lane_head = _iota((t, C_WIDTH), 1) // HEAD_DIM
    out = jnp.zeros((t, C_WIDTH), F32)
    for hd in range(C_HEADS):
        qh = jnp.where(lane_head == hd, q, 0.0).astype(BF16)
        s = _dot_nt(qh, mk)
        p = jnp.exp(s - jnp.max(s, axis=-1, keepdims=True))
        o = _dot(p.astype(BF16), mv) / jnp.sum(p, axis=-1, keepdims=True)
        out = out + jnp.where(lane_head == hd, o, 0.0)
    o_ref[0] = out.astype(o_ref.dtype)


def _memattn(q3, mk3, mv3, tiles_per_mem):
    g, t, _ = q3.shape
    mem_spec = pl.BlockSpec((1, N_MEM, C_WIDTH), lambda i: (i // tiles_per_mem, 0, 0))
    return pl.pallas_call(
        _memattn_kernel, grid=(g,),
        in_specs=[pl.BlockSpec((1, t, C_WIDTH), lambda i: (i, 0, 0)), mem_spec, mem_spec],
        out_specs=pl.BlockSpec((1, t, C_WIDTH), lambda i: (i, 0, 0)),
        out_shape=jax.ShapeDtypeStruct((g, t, C_WIDTH), BF16),
        compiler_params=_cparams(("parallel",)), name="memattn",
    )(q3, mk3, mv3)


def _memattn_sample_kernel(q_ref, mk_ref, mv_ref, o_ref):
    q4 = q_ref[0]
    mk = mk_ref[0].astype(BF16)
    mv = mv_ref[0].astype(BF16)
    t = q4.shape[0]
    mask8 = (_iota((SUBLANES, C_WIDTH), 1) // HEAD_DIM) == _iota((SUBLANES, C_WIDTH), 0)
    qbd = jnp.concatenate(
        [jnp.where(mask8, jnp.broadcast_to(q4[i:i + 1, :], (SUBLANES, C_WIDTH)), 0.0) for i in range(t)], axis=0)
    s = _dot_nt(qbd.astype(BF16), mk)
    p = jnp.exp(s - jnp.max(s, axis=-1, keepdims=True))
    o = _dot(p.astype(BF16), mv) / jnp.sum(p, axis=-1, keepdims=True)
    for i in range(t):
        slab = jnp.where(mask8, o[i * SUBLANES:(i + 1) * SUBLANES, :], 0.0)
        o_ref[0, i * SUBLANES:(i + 1) * SUBLANES, :] = jnp.broadcast_to(
            jnp.sum(slab, axis=0, keepdims=True), (SUBLANES, C_WIDTH))


def _memattn_sample(q3, mk3, mv3):
    g, t, _ = q3.shape
    mem_spec = pl.BlockSpec((1, N_MEM, C_WIDTH), lambda i: (i, 0, 0))
    return pl.pallas_call(
        _memattn_sample_kernel, grid=(g,),
        in_specs=[pl.BlockSpec((1, t, C_WIDTH), lambda i: (i, 0, 0)), mem_spec, mem_spec],
        out_specs=pl.BlockSpec((1, t * SUBLANES, C_WIDTH), lambda i: (i, 0, 0)),
        out_shape=jax.ShapeDtypeStruct((g, t * SUBLANES, C_WIDTH), F32),
        compiler_params=_cparams(("parallel",)), name="memattn_sample",
    )(q3, mk3, mv3)


def _dsa_prompt_kernel(topk, qz_ref, qiz_ref, wt_ref, ki2_ref, kbf_ref, vt_ref, o_ref, sc_ref, acc_ref):
    i = pl.program_id(1)
    nk = i + 1
    kth = jnp.float32(topk)

    def score_chunk(c, _):
        keys = ki2_ref[0, c]
        acc = jnp.zeros((KC, QB), F32)
        for hd in range(IDX_HEADS):
            y = _dot_nt(keys, qiz_ref[hd])
            acc = acc + jnp.maximum(y, 0.0) * wt_ref[hd:hd + 1, :]
        kpos = c * KC + _iota((KC, QB), 0)
        qpos = i * QB + _iota((KC, QB), 1)
        sc_ref[c] = jnp.where(kpos <= qpos, acc, -jnp.inf)
        return 0

    lax.fori_loop(0, nk, score_chunk, 0)

    def reduce_rows(x):
        return jnp.sum(x.reshape(KC // SUBLANES, SUBLANES, QB), axis=0)

    def count(pred):
        def body(c, cnt):
            return cnt + reduce_rows(pred(sc_ref[c]).astype(F32))
        return jnp.sum(lax.fori_loop(0, nk, body, jnp.zeros((SUBLANES, QB), F32)), axis=0, keepdims=True)

    def write_ge(thr):
        def body(c, _):
            sc_ref[c] = jnp.where(sc_ref[c] >= thr, 0.0, NEG)
            return 0
        lax.fori_loop(0, nk, body, 0)

    @pl.when(nk * KC <= topk)
    def _():
        write_ge(jnp.full((1, QB), -FLT_MAX, F32))

    @pl.when(nk * KC > topk)
    def _():
        thr, cnt_thr = _kth_largest(lambda t: count(lambda x: x >= t), kth, (1, QB))
        has_tie = jnp.max(cnt_thr) > kth

        @pl.when(jnp.logical_not(has_tie))
        def _():
            write_ge(thr)

        @pl.when(has_tie)
        def _():
            need = kth - count(lambda x: x > thr)
            ltri = (_iota((KC, KC), 1) < _iota((KC, KC), 0)).astype(BF16)

            def body(c, carry):
                x = sc_ref[c]
                eq = x == thr
                before = _dot(ltri, eq.astype(BF16)) + carry
                sel = (x > thr) | (eq & (before < need))
                sc_ref[c] = jnp.where(sel, 0.0, NEG)
                return carry + jnp.sum(reduce_rows(eq.astype(F32)), axis=0, keepdims=True)

            lax.fori_loop(0, nk, body, jnp.zeros((1, QB), F32))

    for hd in range(B_HEADS):
        pair = hd // 2
        qh = qz_ref[hd]

        def attend(c, carry, qh=qh, pair=pair, hd=hd):
            m, l, acc = carry
            kc = kbf_ref[0, c, :, pair * LANES:(pair + 1) * LANES]
            s = _dot_nt(kc, qh) + sc_ref[c]
            m_new = jnp.maximum(m, jnp.max(reduce_max(s), axis=0, keepdims=True))
            alpha = jnp.exp(m - m_new)
            p = jnp.exp(s - m_new)
            l = alpha * l + jnp.sum(reduce_rows(p), axis=0, keepdims=True)
            vt = vt_ref[0, c, hd * HEAD_DIM:(hd + 1) * HEAD_DIM, :]
            acc = alpha * acc + _dot(vt, p.astype(BF16))
            return m_new, l, acc

        def reduce_max(x):
            return jnp.max(x.reshape(KC // SUBLANES, SUBLANES, QB), axis=0)

        m0 = jnp.full((1, QB), NEG, F32)
        l0 = jnp.zeros((1, QB), F32)
        a0 = jnp.zeros((HEAD_DIM, QB), F32)
        m, l, acc = lax.fori_loop(0, nk, attend, (m0, l0, a0))
        acc_ref[hd * HEAD_DIM:(hd + 1) * HEAD_DIM, :] = acc / l

    o_ref[...] = acc_ref[...].T.astype(o_ref.dtype)


def _dsa_prompt(qz, qiz, wt, ki2, kbf, vt, batch, seq):
    nq = seq // QB
    nc = seq // KC
    topk = min(TOPK_MAX, seq // 4)
    n = batch * seq
    qspec = pl.BlockSpec((B_HEADS, QB, LANES), lambda b, i: (0, b * nq + i, 0))
    return pl.pallas_call(
        functools.partial(_dsa_prompt_kernel, topk),
        grid=(batch, nq),
        in_specs=[qspec, qspec,
                  pl.BlockSpec((IDX_HEADS, QB), lambda b, i: (0, b * nq + i)),
                  pl.BlockSpec((1, nc, KC, LANES), lambda b, i: (b, 0, 0, 0)),
                  pl.BlockSpec((1, nc, KC, B_WIDTH), lambda b, i: (b, 0, 0, 0)),
                  pl.BlockSpec((1, nc, B_WIDTH, KC), lambda b, i: (b, 0, 0, 0))],
        out_specs=pl.BlockSpec((QB, B_WIDTH), lambda b, i: (b * nq + i, 0)),
        out_shape=jax.ShapeDtypeStruct((n, B_WIDTH), BF16),
        scratch_shapes=[pltpu.VMEM((nc, KC, QB), F32), pltpu.VMEM((B_WIDTH, QB), F32)],
        compiler_params=_cparams(("parallel", "arbitrary")), name="dsa_prompt",
    )(qz, qiz, wt, ki2.reshape(batch, nc, KC, LANES), kbf.reshape(batch, nc, KC, B_WIDTH),
      vt.reshape(batch, nc, B_WIDTH, KC))


def _page_specs(block, n_pages_per_step):
    def make(r):
        return pl.BlockSpec(block, lambda b, j, pt: (pt[b, j * n_pages_per_step + r], 0, 0))
    return [make(r) for r in range(n_pages_per_step)]


def _idx_sample_kernel(t_new, pt_ref, qi_ref, w_ref, kin_ref, *rest):
    pages = rest[:PAGES_PER_STEP]
    sc_out, scn_out = rest[PAGES_PER_STEP:]
    j = pl.program_id(1)
    qi = qi_ref[0]
    w = w_ref[0]
    qib = qi.astype(BF16)

    def token_scores(y):
        z = jnp.maximum(y, 0.0) * w
        return jnp.sum(z.reshape(t_new, IDX_HEADS, y.shape[1]), axis=1)

    for r in range(PAGES_PER_STEP):
        y = _dot_nt(qib, pages[r][0].astype(BF16))
        sc_out[0, :, r * PAGE_SIZE:(r + 1) * PAGE_SIZE] = token_scores(y)

    @pl.when(j == 0)
    def _():
        qf = qib.astype(F32)
        lane = _iota((t_new, LANES), 1)
        tok = _iota((t_new, LANES), 0)
        new = jnp.full((t_new, LANES), -jnp.inf, F32)
        for tk in range(t_new):
            kr = kin_ref[0, tk:tk + 1, :].astype(BF16).astype(F32)
            col = token_scores(jnp.sum(qf * kr, axis=-1, keepdims=True))
            new = jnp.where((lane == tk) & (tok >= tk), col, new)
        scn_out[0] = new


def _idx_sample(page_table, qi32, w32, ki_new, cache_idx_k):
    db, rows, _ = qi32.shape
    t_new = rows // IDX_HEADS
    n_pages = page_table.shape[1]
    steps = n_pages // PAGES_PER_STEP
    span = PAGES_PER_STEP * PAGE_SIZE
    grid_spec = pltpu.PrefetchScalarGridSpec(
        num_scalar_prefetch=1, grid=(db, steps),
        in_specs=[pl.BlockSpec((1, rows, IDX_DIM), lambda b, j, pt: (b, 0, 0)),
                  pl.BlockSpec((1, rows, 1), lambda b, j, pt: (b, 0, 0)),
                  pl.BlockSpec((1, t_new, IDX_DIM), lambda b, j, pt: (b, 0, 0))]
        + _page_specs((1, PAGE_SIZE, IDX_DIM), PAGES_PER_STEP),
        out_specs=[pl.BlockSpec((1, t_new, span), lambda b, j, pt: (b, 0, j)),
                   pl.BlockSpec((1, t_new, LANES), lambda b, j, pt: (b, 0, 0))])
    return pl.pallas_call(
        functools.partial(_idx_sample_kernel, t_new), grid_spec=grid_spec,
        out_shape=[jax.ShapeDtypeStruct((db, t_new, n_pages * PAGE_SIZE), F32),
                   jax.ShapeDtypeStruct((db, t_new, LANES), F32)],
        compiler_params=_cparams(("parallel", "arbitrary")), name="idx_sample",
    )(page_table, qi32, w32, ki_new, *([cache_idx_k] * PAGES_PER_STEP))


def _thr_sample_kernel(topk, sc_ref, scn_ref, b_out, bn_out):
    kth = jnp.float32(topk)
    n_chunks = sc_ref.shape[1] // LANES

    def count(pred):
        return (jnp.sum(pred(sc_ref[...]).astype(F32), axis=-1, keepdims=True)
                + jnp.sum(pred(scn_ref[...]).astype(F32), axis=-1, keepdims=True))

    thr, cnt_thr = _kth_largest(lambda t: count(lambda x: x >= t), kth, (THR_ROWS, 1))
    has_tie = jnp.max(cnt_thr) > kth

    @pl.when(jnp.logical_not(has_tie))
    def _():
        b_out[...] = jnp.where(sc_ref[...] >= thr, 0.0, NEG)
        bn_out[...] = jnp.where(scn_ref[...] >= thr, 0.0, NEG)

    @pl.when(has_tie)
    def _():
        need = kth - count(lambda x: x > thr)
        utri = (_iota((LANES, LANES), 0) < _iota((LANES, LANES), 1)).astype(BF16)
        carry = jnp.zeros((THR_ROWS, 1), F32)
        for c in range(n_chunks + 1):
            x = sc_ref[:, c * LANES:(c + 1) * LANES] if c < n_chunks else scn_ref[...]
            eq = x == thr
            before = _dot(eq.astype(BF16), utri) + carry
            sel = (x > thr) | (eq & (before < need))
            bias = jnp.where(sel, 0.0, NEG)
            if c < n_chunks:
                b_out[:, c * LANES:(c + 1) * LANES] = bias
            else:
                bn_out[...] = bias
            carry = carry + jnp.sum(eq.astype(F32), axis=-1, keepdims=True)


def _thr_sample(sc, scn, topk):
    n, width = sc.shape
    return pl.pallas_call(
        functools.partial(_thr_sample_kernel, topk), grid=(n // THR_ROWS,),
        in_specs=[pl.BlockSpec((THR_ROWS, width), lambda i: (i, 0)),
                  pl.BlockSpec((THR_ROWS, LANES), lambda i: (i, 0))],
        out_specs=[pl.BlockSpec((THR_ROWS, width), lambda i: (i, 0)),
                   pl.BlockSpec((THR_ROWS, LANES), lambda i: (i, 0))],
        out_shape=[jax.ShapeDtypeStruct((n, width), F32), jax.ShapeDtypeStruct((n, LANES), F32)],
        compiler_params=_cparams(("parallel",)), name="thr_sample",
    )(sc, scn)


def _attn_sample_kernel(t_new, pt_ref, q_ref, bias_ref, biasn_ref, kn_ref, vn_ref, *rest):
    kpages = rest[:PAGES_PER_STEP]
    vpages = rest[PAGES_PER_STEP:2 * PAGES_PER_STEP]
    o_ref, m_ref, l_ref, acc_ref, qbd_ref = rest[2 * PAGES_PER_STEP:]
    j = pl.program_id(1)
    rows = t_new * B_HEADS
    mask8 = (_iota((SUBLANES, B_WIDTH), 1) // HEAD_DIM) == _iota((SUBLANES, B_WIDTH), 0)

    def per_token(ref, width):
        return jnp.concatenate(
            [jnp.broadcast_to(ref[0, i:i + 1, :], (B_HEADS, width)) for i in range(t_new)], axis=0)

    @pl.when(j == 0)
    def _():
        m_ref[...] = jnp.full((rows, 1), NEG, F32)
        l_ref[...] = jnp.zeros((rows, 1), F32)
        acc_ref[...] = jnp.zeros((rows, B_WIDTH), F32)
        qbd_ref[...] = jnp.concatenate(
            [jnp.where(mask8, jnp.broadcast_to(q_ref[0, i:i + 1, :] * ATTN_SCALE, (B_HEADS, B_WIDTH)), 0.0)
             for i in range(t_new)], axis=0).astype(BF16)

    qbd = qbd_ref[...]
    span = PAGES_PER_STEP * PAGE_SIZE
    s = jnp.concatenate([_dot_nt(qbd, kpages[r][0].astype(BF16)) for r in range(PAGES_PER_STEP)], axis=1)
    s = s + per_token(bias_ref, span)
    m_old = m_ref[...]
    m_new = jnp.maximum(m_old, jnp.max(s, axis=-1, keepdims=True))
    alpha = jnp.exp(m_old - m_new)
    p = jnp.exp(s - m_new)
    l_ref[...] = alpha * l_ref[...] + jnp.sum(p, axis=-1, keepdims=True)
    pb = p.astype(BF16)
    pv = jnp.zeros((rows, B_WIDTH), F32)
    for r in range(PAGES_PER_STEP):
        pv = pv + _dot(pb[:, r * PAGE_SIZE:(r + 1) * PAGE_SIZE], vpages[r][0].astype(BF16))
    acc_ref[...] = alpha * acc_ref[...] + pv
    m_ref[...] = m_new

    @pl.when(j == pl.num_programs(1) - 1)
    def _():
        qf = qbd.astype(F32)
        bn = per_token(biasn_ref, LANES)
        cols = []
        for tk in range(t_new):
            kr = kn_ref[0, tk:tk + 1, :].astype(BF16).astype(F32)
            cols.append(jnp.sum(qf * kr, axis=-1, keepdims=True) + bn[:, tk:tk + 1])
        m_old = m_ref[...]
        m_new = m_old
        for cval in cols:
            m_new = jnp.maximum(m_new, cval)
        alpha = jnp.exp(m_old - m_new)
        l = alpha * l_ref[...]
        acc = alpha * acc_ref[...]
        for tk in range(t_new):
            pk = jnp.exp(cols[tk] - m_new)
            l = l + pk
            acc = acc + pk.astype(BF16).astype(F32) * vn_ref[0, tk:tk + 1, :].astype(BF16).astype(F32)
        o = acc / l
        for i in range(t_new):
            slab = jnp.where(mask8, o[i * B_HEADS:(i + 1) * B_HEADS, :], 0.0)
            o_ref[0, i * B_HEADS:(i + 1) * B_HEADS, :] = jnp.broadcast_to(
                jnp.sum(slab, axis=0, keepdims=True), (B_HEADS, B_WIDTH))


def _attn_sample(page_table, q3, bias3, biasn3, k_new, v_new, cache_k, cache_v):
    db, t_new, _ = q3.shape
    n_pages = page_table.shape[1]
    steps = n_pages // PAGES_PER_STEP
    span = PAGES_PER_STEP * PAGE_SIZE
    rows = t_new * B_HEADS
    per_b = lambda w: pl.BlockSpec((1, t_new, w), lambda b, j, pt: (b, 0, 0))
    grid_spec = pltpu.PrefetchScalarGridSpec(
        num_scalar_prefetch=1, grid=(db, steps),
        in_specs=[per_b(B_WIDTH),
                  pl.BlockSpec((1, t_new, span), lambda b, j, pt: (b, 0, j)),
                  per_b(LANES), per_b(B_WIDTH), per_b(B_WIDTH)]
        + _page_specs((1, PAGE_SIZE, B_WIDTH), PAGES_PER_STEP)
        + _page_specs((1, PAGE_SIZE, B_WIDTH), PAGES_PER_STEP),
        out_specs=pl.BlockSpec((1, rows, B_WIDTH), lambda b, j, pt: (b, 0, 0)),
        scratch_shapes=[pltpu.VMEM((rows, 1), F32), pltpu.VMEM((rows, 1), F32),
                        pltpu.VMEM((rows, B_WIDTH), F32), pltpu.VMEM((rows, B_WIDTH), BF16)])
    return pl.pallas_call(
        functools.partial(_attn_sample_kernel, t_new), grid_spec=grid_spec,
        out_shape=jax.ShapeDtypeStruct((db, rows, B_WIDTH), F32),
        compiler_params=_cparams(("parallel", "arbitrary")), name="attn_sample",
    )(page_table, q3, bias3, biasn3, k_new, v_new,
      *([cache_k] * PAGES_PER_STEP), *([cache_v] * PAGES_PER_STEP))


def _merge_kernel(x_ref, ma_ref, yb_ref, sgb_ref, yc_ref, sgc_ref, wo_ref, g_ref, y_out):
    mb = (yb_ref[...].astype(F32) * sgb_ref[...].astype(F32)).astype(BF16)
    mc = (yc_ref[...].astype(F32) * sgc_ref[...].astype(F32)).astype(BF16)
    o = (_dot(ma_ref[...], wo_ref[0:A_WIDTH, :])
         + _dot(mb, wo_ref[A_WIDTH:A_WIDTH + B_WIDTH, :])
         + _dot(mc, wo_ref[A_WIDTH + B_WIDTH:, :]))
    z = x_ref[...] + o
    ms = jnp.mean(z * z, axis=-1, keepdims=True)
    y_out[...] = z * lax.rsqrt(ms + EPS) * g_ref[...]


def _merge(x, mixa, yb, sgb, yc, sgc, wo, g_final):
    n = x.shape[0]
    row = lambda w: pl.BlockSpec((TM, w), lambda i: (i, 0))
    return pl.pallas_call(
        _merge_kernel, grid=(n // TM,),
        in_specs=[row(D_MODEL), row(A_WIDTH), row(B_WIDTH), row(B_WIDTH), row(C_WIDTH), row(C_WIDTH),
                  pl.BlockSpec(wo.shape, lambda i: (0, 0)), pl.BlockSpec(g_final.shape, lambda i: (0, 0))],
        out_specs=row(D_MODEL),
        out_shape=jax.ShapeDtypeStruct((n, D_MODEL), F32),
        compiler_params=_cparams(("parallel",)), name="merge",
    )(x, mixa, yb, sgb, yc, sgc, wo, g_final)


def _rope_tables(pos):
    inv = ROPE_THETA ** (-jnp.arange(0, HEAD_DIM, 2, dtype=F32) / HEAD_DIM)
    ang = pos.astype(F32)[:, None] * inv[None, :]
    c = jnp.cos(ang)
    s = jnp.sin(ang)
    return jnp.concatenate([c, c, c, c], axis=1), jnp.concatenate([-s, s, -s, s], axis=1)


def kernel(x_prompt, x_sample, mem_prompt, cache_k, cache_v, cache_idx_k, cache_mem_k, cache_mem_v,
           page_table, g_norm, w_in, w_spatial, b_spatial, g_v, w_mem_kv, g_mem, w_out, g_final):
    batch, seq, _ = x_prompt.shape
    db, t_new, _ = x_sample.shape
    n_pages = page_table.shape[1]
    past = n_pages * PAGE_SIZE
    n_pool = cache_k.shape[0]

    ki_cols = w_in[:, 3328:3392]
    wi_cols = w_in[:, 3392:3400]
    w_r = jnp.concatenate(
        [w_in[:, :3328], w_in[:, 3400:3912], ki_cols, ki_cols, wi_cols,
         jnp.zeros((D_MODEL, LANES - IDX_HEADS), F32)], axis=1).astype(BF16)
    g_norm2 = g_norm.reshape(1, D_MODEL)
    g_v2 = g_v.reshape(1, A_WIDTH)
    bs_prompt = jnp.repeat(b_spatial.T, HEAD_DIM, axis=1)
    reps = CHUNK // t_new
    ws_sample = jnp.tile(w_spatial[:, :t_new, :t_new], (1, reps, reps))
    bs_sample = jnp.tile(jnp.repeat(b_spatial[:, :t_new].T, HEAD_DIM, axis=1), (reps, 1))
    wo_b = w_out.astype(BF16)
    g_final2 = g_final.reshape(1, D_MODEL)

    cos_p, sin_p = _rope_tables(jnp.arange(seq, dtype=jnp.int32))
    pos_s = past + jnp.arange(t_new, dtype=jnp.int32)
    cos_s, sin_s = _rope_tables(jnp.tile(pos_s, TM // t_new))

    xp = x_prompt.reshape(batch * seq, D_MODEL)
    (_, k_p, v_p, _, ki_p, _, kbf, vt, ki2, qz, qiz, wt, sgb, mixa, qc, sgc) = _proj(
        xp, g_norm2, w_r, cos_p, sin_p, w_spatial, bs_prompt, g_v2, CHUNK, seq // TM, False)
    mk, mv = _memkv(mem_prompt.reshape(batch * N_MEM, D_MODEL), g_mem.reshape(1, D_MODEL), w_mem_kv.astype(BF16))
    mk3 = mk.reshape(batch, N_MEM, C_WIDTH)
    mv3 = mv.reshape(batch, N_MEM, C_WIDTH)
    yc = _memattn(qc.reshape(batch * seq // TM, TM, C_WIDTH), mk3, mv3, seq // TM).reshape(batch * seq, C_WIDTH)
    yb = _dsa_prompt(qz, qiz, wt, ki2, kbf, vt, batch, seq)
    y_prompt = _merge(xp, mixa, yb, sgb, yc, sgc, wo_b, g_final2).reshape(batch, seq, D_MODEL)

    ns = db * t_new
    xs = x_sample.reshape(ns, D_MODEL)
    (q_s, k_s, v_s, qi_s, ki_s, wi_s, _, _, _, _, _, _, sgb_s, mixa_s, qc_s, sgc_s, vn_s) = _proj(
        xs, g_norm2, w_r, cos_s, sin_s, ws_sample, bs_sample, g_v2, t_new, 1, True)
    sc, scn = _idx_sample(page_table, qi_s.reshape(db, t_new * IDX_HEADS, IDX_DIM),
                          wi_s.reshape(db, t_new * IDX_HEADS, 1), ki_s.reshape(db, t_new, IDX_DIM),
                          cache_idx_k)
    topk_s = min(TOPK_MAX, (past + t_new) // 4)
    bias, biasn = _thr_sample(sc.reshape(ns, past), scn.reshape(ns, LANES), topk_s)
    yb_pad = _attn_sample(page_table, q_s.reshape(db, t_new, B_WIDTH), bias.reshape(db, t_new, past),
                          biasn.reshape(db, t_new, LANES), k_s.reshape(db, t_new, B_WIDTH),
                          v_s.reshape(db, t_new, B_WIDTH),
                          cache_k.reshape(n_pool, PAGE_SIZE, B_WIDTH), cache_v.reshape(n_pool, PAGE_SIZE, B_WIDTH))
    yb_s = yb_pad[:, ::B_HEADS, :].reshape(ns, B_WIDTH)
    yc_pad = _memattn_sample(qc_s.astype(F32).reshape(db, t_new, C_WIDTH),
                             cache_mem_k.reshape(db, N_MEM, C_WIDTH), cache_mem_v.reshape(db, N_MEM, C_WIDTH))
    yc_s = yc_pad[:, ::SUBLANES, :].reshape(ns, C_WIDTH)
    y_sample = _merge(xs, mixa_s, yb_s, sgb_s, yc_s, sgc_s, wo_b, g_final2).reshape(db, t_new, D_MODEL)

    return (y_prompt, y_sample,
            k_p.reshape(batch, seq, B_HEADS, HEAD_DIM), v_p.reshape(batch, seq, B_HEADS, HEAD_DIM),
            ki_p.reshape(batch, seq, IDX_DIM),
            mk.reshape(batch, N_MEM, C_HEADS, HEAD_DIM), mv.reshape(batch, N_MEM, C_HEADS, HEAD_DIM),
            k_s.reshape(db, t_new, B_HEADS, HEAD_DIM), v_s.reshape(db, t_new, B_HEADS, HEAD_DIM),
            ki_s.reshape(db, t_new, IDX_DIM), vn_s.reshape(db, t_new, A_WIDTH))
```

```python
import functools

import jax
import jax.numpy as jnp
from jax import lax
from jax.experimental import pallas as pl
from jax.experimental.pallas import tpu as pltpu

F32 = jnp.float32
BF16 = jnp.bfloat16

HEAD_DIM = 64
HALF = HEAD_DIM // 2
D_MODEL = 1024
A_GROUPS = 4
A_WIDTH = A_GROUPS * HEAD_DIM
CHUNK = 128
B_HEADS = 8
B_WIDTH = B_HEADS * HEAD_DIM
IDX_HEADS = 8
IDX_DIM = 64
TOPK_MAX = 256
N_MEM = 256
C_HEADS = 4
C_WIDTH = C_HEADS * HEAD_DIM
PAGE_SIZE = 128
ROPE_THETA = 10000.0
EPS = 1e-6
IDX_W_SCALE = (IDX_HEADS * IDX_DIM) ** -0.5
ATTN_SCALE = HEAD_DIM ** -0.5

LANES = 128
SUBLANES = 8
NEG = -1e30
FLT_MAX = 3.4028234663852886e38
INT_MIN = -2 ** 31

TM = 256
QB = 256
KC = 256
PAGES_PER_STEP = 8
THR_ROWS = 64
VMEM_LIMIT = 52 * 1024 * 1024

OFF_U, OFF_VA, OFF_GA = 0, 256, 512
OFF_Q, OFF_K, OFF_V, OFF_GB, OFF_QI = 768, 1280, 1792, 2304, 2816
OFF_QC, OFF_GC, OFF_KI2, OFF_WI, W_TOTAL = 3328, 3584, 3840, 3968, 4096


def _cparams(sem):
    return pltpu.CompilerParams(dimension_semantics=sem, vmem_limit_bytes=VMEM_LIMIT)


def _iota(shape, dim):
    return lax.broadcasted_iota(jnp.int32, shape, dim)


def _silu(x):
    return x / (1.0 + jnp.exp(-x))


def _dot(a, b):
    return jnp.dot(a, b, preferred_element_type=F32)


def _dot_nt(a, b):
    return lax.dot_general(a, b, (((1,), (1,)), ((), ())), preferred_element_type=F32)


def _key_to_float(k):
    bits = k ^ ((k >> 31) & jnp.int32(0x7FFFFFFF))
    return lax.bitcast_convert_type(bits, F32)


def _kth_largest(count_ge, kth, shape):
    cnt0 = count_ge(jnp.zeros(shape, F32))
    ok0 = cnt0 >= kth
    key = jnp.where(ok0, jnp.int32(0), jnp.int32(INT_MIN))
    cnt_key = jnp.where(ok0, cnt0, jnp.float32(2 ** 24))

    def bit_body(it, carry):
        key, cnt_key = carry
        trial = key + lax.shift_left(jnp.int32(1), 30 - it)
        cnt = count_ge(_key_to_float(trial))
        ok = cnt >= kth
        return jnp.where(ok, trial, key), jnp.where(ok, cnt, cnt_key)

    key, cnt_key = lax.fori_loop(0, 31, bit_body, (key, cnt_key))
    return _key_to_float(key), cnt_key


def _rope(t, cos, sin):
    outs = []
    first = (_iota((t.shape[0], LANES), 1) % HEAD_DIM) < HALF
    for j in range(t.shape[1] // LANES):
        x = t[:, j * LANES:(j + 1) * LANES]
        partner = jnp.where(first, pltpu.roll(x, LANES - HALF, 1), pltpu.roll(x, HALF, 1))
        outs.append(x * cos + partner * sin)
    return outs[0] if len(outs) == 1 else jnp.concatenate(outs, axis=1)


def _proj_kernel(chunk, sample, x_ref, g_ref, w_ref, cos_ref, sin_ref, ws_ref, bs_ref, gv_ref, *outs):
    if sample:
        q_out, k_out, v_out, qi_out, ki_out, wi_out, sgb_out, mixa_out, qc_out, sgc_out, vn_out = outs
    else:
        (kt_out, vt_out, kit_out, kbf_out, vtb_out, ki2_out, qz_out, qiz_out, wt_out,
         sgb_out, mixa_out, qc_out, sgc_out) = outs
    tm = x_ref.shape[0]
    x = x_ref[...]
    ms = jnp.mean(x * x, axis=-1, keepdims=True)
    h = (x * lax.rsqrt(ms + EPS) * g_ref[...]).astype(BF16)
    cos = cos_ref[...]
    sin = sin_ref[...]

    def seg(a, b):
        return _dot(h, w_ref[:, a:b])

    half_id = (_iota((tm, LANES), 1) // HEAD_DIM)

    def head_split(t, out):
        for hd in range(B_HEADS):
            pair = t[:, (hd // 2) * LANES:(hd // 2 + 1) * LANES]
            out[hd] = jnp.where(half_id == hd % 2, pair, 0.0).astype(BF16)

    q = _rope(seg(OFF_Q, OFF_K), cos, sin)
    k = _rope(seg(OFF_K, OFF_V), cos, sin)
    v = seg(OFF_V, OFF_GB)
    qi = _rope(seg(OFF_QI, OFF_QC), cos, sin)
    ki2 = _rope(seg(OFF_KI2, OFF_WI), cos, sin)
    wi = seg(OFF_WI, W_TOTAL) * IDX_W_SCALE
    qc = seg(OFF_QC, OFF_GC) * ATTN_SCALE
    sgb_out[...] = _silu(seg(OFF_GB, OFF_QI)).astype(BF16)
    sgc_out[...] = _silu(seg(OFF_GC, OFF_KI2)).astype(BF16)
    qc_out[...] = qc.astype(qc_out.dtype)
    if sample:
        q_out[...] = q
        k_out[...] = k
        v_out[...] = v
        qi_out[...] = qi
        ki_out[...] = ki2[:, :IDX_DIM]
        wi_out[...] = wi[:, :IDX_HEADS]
    else:
        head_split(q * ATTN_SCALE, qz_out)
        head_split(qi, qiz_out)
        kt_out[0] = k.T
        kbf_out[...] = k.astype(BF16)
        vt = v.T
        vt_out[0] = vt
        vtb_out[0] = vt.astype(BF16)
        kit_out[0] = ki2.T[:IDX_DIM, :]
        ki2_out[...] = ki2.astype(BF16)
        wt_out[...] = wi.T[:IDX_HEADS, :]

    u = seg(OFF_U, OFF_VA)
    va = seg(OFF_VA, OFF_GA)
    ga = seg(OFF_GA, OFF_Q)
    grp_r = _iota((A_WIDTH, A_WIDTH), 0) // HEAD_DIM
    grp_c = _iota((A_WIDTH, A_WIDTH), 1) // HEAD_DIM
    gmat = jnp.where(grp_r == grp_c, 1.0 / HEAD_DIM, 0.0).astype(BF16)

    def group_mean(t):
        hi = t.astype(BF16)
        lo = (t - hi.astype(F32)).astype(BF16)
        return _dot(hi, gmat) + _dot(lo, gmat)

    mu = group_mean(va)
    d = va - mu
    var = group_mean(d * d)
    vn = d * lax.rsqrt(var + EPS) * gv_ref[...]
    if sample:
        vn_out[...] = vn
    vnb = vn.astype(BF16)
    row = _iota((CHUNK, CHUNK), 0)
    col = _iota((CHUNK, CHUNK), 1)
    causal = (row >= col) & ((row // chunk) == (col // chunk))
    lane_grp = _iota((CHUNK, A_WIDTH), 1) // HEAD_DIM
    wms = [jnp.where(causal, ws_ref[g], 0.0).astype(BF16) for g in range(A_GROUPS)]
    mixes = []
    for j in range(tm // CHUNK):
        vc = vnb[j * CHUNK:(j + 1) * CHUNK, :]
        m = bs_ref[...]
        for g in range(A_GROUPS):
            m = m + jnp.where(lane_grp == g, _dot(wms[g], vc), 0.0)
        mixes.append(m)
    mix = jnp.concatenate(mixes, axis=0)
    mixa_out[...] = (_silu(ga) * u * mix).astype(BF16)


def _proj(x, g_norm, w_r, cos_t, sin_t, ws, bs, g_v, chunk, seq_blocks, sample):
    n = x.shape[0]
    nt = n // TM
    nb = nt // seq_blocks
    row = lambda w: pl.BlockSpec((TM, w), lambda i: (i, 0))
    const2 = lambda a: pl.BlockSpec(a.shape, lambda i: (0, 0))
    sds = jax.ShapeDtypeStruct
    in_specs = [
        row(D_MODEL), const2(g_norm), const2(w_r),
        pl.BlockSpec((TM, LANES), lambda i: (i % seq_blocks, 0)),
        pl.BlockSpec((TM, LANES), lambda i: (i % seq_blocks, 0)),
        pl.BlockSpec(ws.shape, lambda i: (0, 0, 0)), const2(bs), const2(g_v),
    ]
    if sample:
        outs = [
            (sds((n, B_WIDTH), F32), row(B_WIDTH)),
            (sds((n, B_WIDTH), F32), row(B_WIDTH)),
            (sds((n, B_WIDTH), F32), row(B_WIDTH)),
            (sds((n, IDX_HEADS * IDX_DIM), F32), row(IDX_HEADS * IDX_DIM)),
            (sds((n, IDX_DIM), F32), row(IDX_DIM)),
            (sds((n, IDX_HEADS), F32), row(IDX_HEADS)),
            (sds((n, B_WIDTH), BF16), row(B_WIDTH)),
            (sds((n, A_WIDTH), BF16), row(A_WIDTH)),
            (sds((n, C_WIDTH), F32), row(C_WIDTH)),
            (sds((n, C_WIDTH), BF16), row(C_WIDTH)),
            (sds((n, A_WIDTH), F32), row(A_WIDTH)),
        ]
    else:
        seq = seq_blocks * TM
        tspec = lambda w: pl.BlockSpec((1, w, TM), lambda i: (i // seq_blocks, 0, i % seq_blocks))
        hspec = pl.BlockSpec((B_HEADS, TM, LANES), lambda i: (0, i, 0))
        outs = [
            (sds((nb, B_WIDTH, seq), F32), tspec(B_WIDTH)),
            (sds((nb, B_WIDTH, seq), F32), tspec(B_WIDTH)),
            (sds((nb, IDX_DIM, seq), F32), tspec(IDX_DIM)),
            (sds((n, B_WIDTH), BF16), row(B_WIDTH)),
            (sds((nt, B_WIDTH, TM), BF16), pl.BlockSpec((1, B_WIDTH, TM), lambda i: (i, 0, 0))),
            (sds((n, LANES), BF16), row(LANES)),
            (sds((B_HEADS, n, LANES), BF16), hspec),
            (sds((IDX_HEADS, n, LANES), BF16), hspec),
            (sds((IDX_HEADS, n), F32), pl.BlockSpec((IDX_HEADS, TM), lambda i: (0, i))),
            (sds((n, B_WIDTH), BF16), row(B_WIDTH)),
            (sds((n, A_WIDTH), BF16), row(A_WIDTH)),
            (sds((n, C_WIDTH), BF16), row(C_WIDTH)),
            (sds((n, C_WIDTH), BF16), row(C_WIDTH)),
        ]
    return pl.pallas_call(
        functools.partial(_proj_kernel, chunk, sample),
        grid=(nt,), in_specs=in_specs, out_specs=[o[1] for o in outs], out_shape=[o[0] for o in outs],
        compiler_params=_cparams(("parallel",)), name="proj",
    )(x, g_norm, w_r, cos_t, sin_t, ws, bs, g_v)


def _memkv_kernel(m_ref, g_ref, w_ref, mkt_out, mvt_out):
    x = m_ref[...]
    ms = jnp.mean(x * x, axis=-1, keepdims=True)
    h = (x * lax.rsqrt(ms + EPS) * g_ref[...]).astype(BF16)
    kv = _dot(h, w_ref[...])
    mkt_out[0] = kv[:, :C_WIDTH].T
    mvt_out[0] = kv[:, C_WIDTH:].T


def _memkv(mem, g_mem, w_kv):
    n = mem.shape[0]
    nb = n // N_MEM
    return pl.pallas_call(
        _memkv_kernel, grid=(nb,),
        in_specs=[pl.BlockSpec((N_MEM, D_MODEL), lambda i: (i, 0)),
                  pl.BlockSpec(g_mem.shape, lambda i: (0, 0)),
                  pl.BlockSpec(w_kv.shape, lambda i: (0, 0))],
        out_specs=[pl.BlockSpec((1, C_WIDTH, N_MEM), lambda i: (i, 0, 0))] * 2,
        out_shape=[jax.ShapeDtypeStruct((nb, C_WIDTH, N_MEM), F32)] * 2,
        compiler_params=_cparams(("parallel",)), name="memkv",
    )(mem, g_mem, w_kv)


def _memattn_kernel(q_ref, mkt_ref, mvt_ref, o_ref):
    q = q_ref[0]
    mkt = mkt_ref[0].astype(BF16)
    mvt = mvt_ref[0].astype(BF16)
    t = q.shape[0]
    lane_head = _iota((t, C_WIDTH), 1) // HEAD_DIM
    out = jnp.zeros((t, C_WIDTH), F32)
    for hd in range(C_HEADS):
        qh = jnp.where(lane_head == hd, q, 0.0).astype(BF16)
        s = _dot(qh, mkt)
        p = jnp.exp(s - jnp.max(s, axis=-1, keepdims=True))
        o = _dot_nt(p.astype(BF16), mvt) / jnp.sum(p, axis=-1, keepdims=True)
        out = out + jnp.where(lane_head == hd, o, 0.0)
    o_ref[0] = out.astype(o_ref.dtype)


def _memattn(q3, mkt3, mvt3, tiles_per_mem):
    g, t, _ = q3.shape
    mem_spec = pl.BlockSpec((1, C_WIDTH, N_MEM), lambda i: (i // tiles_per_mem, 0, 0))
    return pl.pallas_call(
        _memattn_kernel, grid=(g,),
        in_specs=[pl.BlockSpec((1, t, C_WIDTH), lambda i: (i, 0, 0)), mem_spec, mem_spec],
        out_specs=pl.BlockSpec((1, t, C_WIDTH), lambda i: (i, 0, 0)),
        out_shape=jax.ShapeDtypeStruct((g, t, C_WIDTH), BF16),
        compiler_params=_cparams(("parallel",)), name="memattn",
    )(q3, mkt3, mvt3)


def _memattn_sample_kernel(q_ref, mkt_ref, mvt_ref, o_ref):
    q4 = q_ref[0]
    mkt = mkt_ref[0].astype(BF16)
    mvt = mvt_ref[0].astype(BF16)
    t = q4.shape[0]
    mask8 = (_iota((SUBLANES, C_WIDTH), 1) // HEAD_DIM) == _iota((SUBLANES, C_WIDTH), 0)
    qbd = jnp.concatenate(
        [jnp.where(mask8, jnp.broadcast_to(q4[i:i + 1, :], (SUBLANES, C_WIDTH)), 0.0) for i in range(t)], axis=0)
    s = _dot(qbd.astype(BF16), mkt)
    p = jnp.exp(s - jnp.max(s, axis=-1, keepdims=True))
    o = _dot_nt(p.astype(BF16), mvt) / jnp.sum(p, axis=-1, keepdims=True)
    for i in range(t):
        slab = jnp.where(mask8, o[i * SUBLANES:(i + 1) * SUBLANES, :], 0.0)
        o_ref[0, i * SUBLANES:(i + 1) * SUBLANES, :] = jnp.broadcast_to(
            jnp.sum(slab, axis=0, keepdims=True), (SUBLANES, C_WIDTH))


def _memattn_sample(q3, mkt3, mvt3):
    g, t, _ = q3.shape
    mem_spec = pl.BlockSpec((1, C_WIDTH, N_MEM), lambda i: (i, 0, 0))
    return pl.pallas_call(
        _memattn_sample_kernel, grid=(g,),
        in_specs=[pl.BlockSpec((1, t, C_WIDTH), lambda i: (i, 0, 0)), mem_spec, mem_spec],
        out_specs=pl.BlockSpec((1, t * SUBLANES, C_WIDTH), lambda i: (i, 0, 0)),
        out_shape=jax.ShapeDtypeStruct((g, t * SUBLANES, C_WIDTH), F32),
        compiler_params=_cparams(("parallel",)), name="memattn_sample",
    )(q3, mkt3, mvt3)


def _dsa_prompt_kernel(topk, qz_ref, qiz_ref, wt_ref, ki2_ref, kbf_ref, vt_ref, o_ref,
                       sc_ref, acc_ref, m_ref, l_ref):
    i = pl.program_id(1)
    nk = i + 1
    kth = jnp.float32(topk)

    def sum_rows(x):
        return jnp.sum(x.reshape(KC // SUBLANES, SUBLANES, QB), axis=0)

    def max_rows(x):
        return jnp.max(x.reshape(KC // SUBLANES, SUBLANES, QB), axis=0)

    def score_chunk(c, _):
        keys = ki2_ref[0, c]
        acc = jnp.zeros((KC, QB), F32)
        for hd in range(IDX_HEADS):
            y = _dot_nt(keys, qiz_ref[hd])
            acc = acc + jnp.maximum(y, 0.0) * wt_ref[hd:hd + 1, :]
        kpos = c * KC + _iota((KC, QB), 0)
        qpos = i * QB + _iota((KC, QB), 1)
        sc_ref[c] = jnp.where(kpos <= qpos, acc, -jnp.inf)
        return 0

    lax.fori_loop(0, nk, score_chunk, 0)

    def count(pred):
        def body(c, cnt):
            return cnt + sum_rows(pred(sc_ref[c]).astype(F32))
        return jnp.sum(lax.fori_loop(0, nk, body, jnp.zeros((SUBLANES, QB), F32)), axis=0, keepdims=True)

    def write_ge(thr):
        def body(c, _):
            sc_ref[c] = jnp.where(sc_ref[c] >= thr, 0.0, NEG)
            return 0
        lax.fori_loop(0, nk, body, 0)

    @pl.when(nk * KC <= topk)
    def _():
        write_ge(jnp.full((1, QB), -FLT_MAX, F32))

    @pl.when(nk * KC > topk)
    def _():
        thr, cnt_thr = _kth_largest(lambda t: count(lambda x: x >= t), kth, (1, QB))
        has_tie = jnp.max(cnt_thr) > kth

        @pl.when(jnp.logical_not(has_tie))
        def _():
            write_ge(thr)

        @pl.when(has_tie)
        def _():
            need = kth - count(lambda x: x > thr)
            ltri = (_iota((KC, KC), 1) < _iota((KC, KC), 0)).astype(BF16)

            def body(c, carry):
                x = sc_ref[c]
                eq = x == thr
                before = _dot(ltri, eq.astype(BF16)) + carry
                sel = (x > thr) | (eq & (before < need))
                sc_ref[c] = jnp.where(sel, 0.0, NEG)
                return carry + jnp.sum(sum_rows(eq.astype(F32)), axis=0, keepdims=True)

            lax.fori_loop(0, nk, body, jnp.zeros((1, QB), F32))

    m_ref[...] = jnp.full((B_HEADS, QB), NEG, F32)
    l_ref[...] = jnp.zeros((B_HEADS, QB), F32)
    acc_ref[...] = jnp.zeros((B_WIDTH, QB), F32)

    def attend(c, _):
        bias = sc_ref[c]
        for hd in range(B_HEADS):
            pair = hd // 2
            rows = slice(hd * HEAD_DIM, (hd + 1) * HEAD_DIM)
            kc = kbf_ref[0, c, :, pair * LANES:(pair + 1) * LANES]
            s = _dot_nt(kc, qz_ref[hd]) + bias
            m_old = m_ref[hd:hd + 1, :]
            m_new = jnp.maximum(m_old, jnp.max(max_rows(s), axis=0, keepdims=True))
            alpha = jnp.exp(m_old - m_new)
            p = jnp.exp(s - m_new)
            l_ref[hd:hd + 1, :] = alpha * l_ref[hd:hd + 1, :] + jnp.sum(sum_rows(p), axis=0, keepdims=True)
            acc_ref[rows, :] = alpha * acc_ref[rows, :] + _dot(vt_ref[0, c, rows, :], p.astype(BF16))
            m_ref[hd:hd + 1, :] = m_new
        return 0

    lax.fori_loop(0, nk, attend, 0)
    for hd in range(B_HEADS):
        rows = slice(hd * HEAD_DIM, (hd + 1) * HEAD_DIM)
        acc_ref[rows, :] = acc_ref[rows, :] / l_ref[hd:hd + 1, :]
    o_ref[...] = acc_ref[...].T.astype(o_ref.dtype)


def _dsa_prompt(qz, qiz, wt, ki2, kbf, vtb, batch, seq):
    nq = seq // QB
    nc = seq // KC
    topk = min(TOPK_MAX, seq // 4)
    n = batch * seq
    qspec = pl.BlockSpec((B_HEADS, QB, LANES), lambda b, i: (0, b * nq + i, 0))
    return pl.pallas_call(
        functools.partial(_dsa_prompt_kernel, topk),
        grid=(batch, nq),
        in_specs=[qspec, qspec,
                  pl.BlockSpec((IDX_HEADS, QB), lambda b, i: (0, b * nq + i)),
                  pl.BlockSpec((1, nc, KC, LANES), lambda b, i: (b, 0, 0, 0)),
                  pl.BlockSpec((1, nc, KC, B_WIDTH), lambda b, i: (b, 0, 0, 0)),
                  pl.BlockSpec((1, nc, B_WIDTH, KC), lambda b, i: (b, 0, 0, 0))],
        out_specs=pl.BlockSpec((QB, B_WIDTH), lambda b, i: (b * nq + i, 0)),
        out_shape=jax.ShapeDtypeStruct((n, B_WIDTH), BF16),
        scratch_shapes=[pltpu.VMEM((nc, KC, QB), F32), pltpu.VMEM((B_WIDTH, QB), F32),
                        pltpu.VMEM((B_HEADS, QB), F32), pltpu.VMEM((B_HEADS, QB), F32)],
        compiler_params=_cparams(("parallel", "arbitrary")), name="dsa_prompt",
    )(qz, qiz, wt, ki2.reshape(batch, nc, KC, LANES), kbf.reshape(batch, nc, KC, B_WIDTH),
      vtb.reshape(batch, nc, B_WIDTH, KC))


def _page_specs(block, n_pages_per_step):
    def make(r):
        return pl.BlockSpec(block, lambda b, j, pt: (pt[b, j * n_pages_per_step + r], 0, 0))
    return [make(r) for r in range(n_pages_per_step)]


def _idx_sample_kernel(t_new, pt_ref, qi_ref, w_ref, kin_ref, *rest):
    pages = rest[:PAGES_PER_STEP]
    sc_out, scn_out = rest[PAGES_PER_STEP:]
    j = pl.program_id(1)
    qi = qi_ref[0]
    w = w_ref[0]
    qib = qi.astype(BF16)

    def token_scores(y):
        z = jnp.maximum(y, 0.0) * w
        return jnp.sum(z.reshape(t_new, IDX_HEADS, y.shape[1]), axis=1)

    for r in range(PAGES_PER_STEP):
        y = _dot(qib, pages[r][0].astype(BF16))
        sc_out[0, :, r * PAGE_SIZE:(r + 1) * PAGE_SIZE] = token_scores(y)

    @pl.when(j == 0)
    def _():
        qf = qib.astype(F32)
        lane = _iota((t_new, LANES), 1)
        tok = _iota((t_new, LANES), 0)
        new = jnp.full((t_new, LANES), -jnp.inf, F32)
        for tk in range(t_new):
            kr = kin_ref[0, tk:tk + 1, :].astype(BF16).astype(F32)
            col = token_scores(jnp.sum(qf * kr, axis=-1, keepdims=True))
            new = jnp.where((lane == tk) & (tok >= tk), col, new)
        scn_out[0] = new


def _idx_sample(page_table, qi32, w32, ki_new, cache_idx_kt):
    db, rows, _ = qi32.shape
    t_new = rows // IDX_HEADS
    n_pages = page_table.shape[1]
    steps = n_pages // PAGES_PER_STEP
    span = PAGES_PER_STEP * PAGE_SIZE
    grid_spec = pltpu.PrefetchScalarGridSpec(
        num_scalar_prefetch=1, grid=(db, steps),
        in_specs=[pl.BlockSpec((1, rows, IDX_DIM), lambda b, j, pt: (b, 0, 0)),
                  pl.BlockSpec((1, rows, 1), lambda b, j, pt: (b, 0, 0)),
                  pl.BlockSpec((1, t_new, IDX_DIM), lambda b, j, pt: (b, 0, 0))]
        + _page_specs((1, IDX_DIM, PAGE_SIZE), PAGES_PER_STEP),
        out_specs=[pl.BlockSpec((1, t_new, span), lambda b, j, pt: (b, 0, j)),
                   pl.BlockSpec((1, t_new, LANES), lambda b, j, pt: (b, 0, 0))])
    return pl.pallas_call(
        functools.partial(_idx_sample_kernel, t_new), grid_spec=grid_spec,
        out_shape=[jax.ShapeDtypeStruct((db, t_new, n_pages * PAGE_SIZE), F32),
                   jax.ShapeDtypeStruct((db, t_new, LANES), F32)],
        compiler_params=_cparams(("parallel", "arbitrary")), name="idx_sample",
    )(page_table, qi32, w32, ki_new, *([cache_idx_kt] * PAGES_PER_STEP))


def _thr_sample_kernel(topk, sc_ref, scn_ref, b_out, bn_out):
    kth = jnp.float32(topk)
    n_chunks = sc_ref.shape[1] // LANES

    def count(pred):
        return (jnp.sum(pred(sc_ref[...]).astype(F32), axis=-1, keepdims=True)
                + jnp.sum(pred(scn_ref[...]).astype(F32), axis=-1, keepdims=True))

    thr, cnt_thr = _kth_largest(lambda t: count(lambda x: x >= t), kth, (THR_ROWS, 1))
    has_tie = jnp.max(cnt_thr) > kth

    @pl.when(jnp.logical_not(has_tie))
    def _():
        b_out[...] = jnp.where(sc_ref[...] >= thr, 0.0, NEG)
        bn_out[...] = jnp.where(scn_ref[...] >= thr, 0.0, NEG)

    @pl.when(has_tie)
    def _():
        need = kth - count(lambda x: x > thr)
        utri = (_iota((LANES, LANES), 0) < _iota((LANES, LANES), 1)).astype(BF16)
        carry = jnp.zeros((THR_ROWS, 1), F32)
        for c in range(n_chunks + 1):
            x = sc_ref[:, c * LANES:(c + 1) * LANES] if c < n_chunks else scn_ref[...]
            eq = x == thr
            before = _dot(eq.astype(BF16), utri) + carry
            sel = (x > thr) | (eq & (before < need))
            bias = jnp.where(sel, 0.0, NEG)
            if c < n_chunks:
                b_out[:, c * LANES:(c + 1) * LANES] = bias
            else:
                bn_out[...] = bias
            carry = carry + jnp.sum(eq.astype(F32), axis=-1, keepdims=True)


def _thr_sample(sc, scn, topk):
    n, width = sc.shape
    return pl.pallas_call(
        functools.partial(_thr_sample_kernel, topk), grid=(n // THR_ROWS,),
        in_specs=[pl.BlockSpec((THR_ROWS, width), lambda i: (i, 0)),
                  pl.BlockSpec((THR_ROWS, LANES), lambda i: (i, 0))],
        out_specs=[pl.BlockSpec((THR_ROWS, width), lambda i: (i, 0)),
                   pl.BlockSpec((THR_ROWS, LANES), lambda i: (i, 0))],
        out_shape=[jax.ShapeDtypeStruct((n, width), F32), jax.ShapeDtypeStruct((n, LANES), F32)],
        compiler_params=_cparams(("parallel",)), name="thr_sample",
    )(sc, scn)


def _attn_sample_kernel(t_new, pt_ref, q_ref, bias_ref, biasn_ref, kn_ref, vn_ref, *rest):
    kpages = rest[:PAGES_PER_STEP]
    vpages = rest[PAGES_PER_STEP:2 * PAGES_PER_STEP]
    o_ref, m_ref, l_ref, acc_ref, qbd_ref = rest[2 * PAGES_PER_STEP:]
    j = pl.program_id(1)
    rows = t_new * B_HEADS
    mask8 = (_iota((SUBLANES, B_WIDTH), 1) // HEAD_DIM) == _iota((SUBLANES, B_WIDTH), 0)

    def per_token(ref, width):
        return jnp.concatenate(
            [jnp.broadcast_to(ref[0, i:i + 1, :], (B_HEADS, width)) for i in range(t_new)], axis=0)

    @pl.when(j == 0)
    def _():
        m_ref[...] = jnp.full((rows, 1), NEG, F32)
        l_ref[...] = jnp.zeros((rows, 1), F32)
        acc_ref[...] = jnp.zeros((rows, B_WIDTH), F32)
        qbd_ref[...] = jnp.concatenate(
            [jnp.where(mask8, jnp.broadcast_to(q_ref[0, i:i + 1, :] * ATTN_SCALE, (B_HEADS, B_WIDTH)), 0.0)
             for i in range(t_new)], axis=0).astype(BF16)

    qbd = qbd_ref[...]
    span = PAGES_PER_STEP * PAGE_SIZE
    s = jnp.concatenate([_dot(qbd, kpages[r][0].astype(BF16)) for r in range(PAGES_PER_STEP)], axis=1)
    s = s + per_token(bias_ref, span)
    m_old = m_ref[...]
    m_new = jnp.maximum(m_old, jnp.max(s, axis=-1, keepdims=True))
    alpha = jnp.exp(m_old - m_new)
    p = jnp.exp(s - m_new)
    l_ref[...] = alpha * l_ref[...] + jnp.sum(p, axis=-1, keepdims=True)
    pb = p.astype(BF16)
    pv = jnp.zeros((rows, B_WIDTH), F32)
    for r in range(PAGES_PER_STEP):
        pv = pv + _dot_nt(pb[:, r * PAGE_SIZE:(r + 1) * PAGE_SIZE], vpages[r][0].astype(BF16))
    acc_ref[...] = alpha * acc_ref[...] + pv
    m_ref[...] = m_new

    @pl.when(j == pl.num_programs(1) - 1)
    def _():
        qf = qbd.astype(F32)
        bn = per_token(biasn_ref, LANES)
        cols = []
        for tk in range(t_new):
            kr = kn_ref[0, tk:tk + 1, :].astype(BF16).astype(F32)
            cols.append(jnp.sum(qf * kr, axis=-1, keepdims=True) + bn[:, tk:tk + 1])
        m_old = m_ref[...]
        m_new = m_old
        for cval in cols:
            m_new = jnp.maximum(m_new, cval)
        alpha = jnp.exp(m_old - m_new)
        l = alpha * l_ref[...]
        acc = alpha * acc_ref[...]
        for tk in range(t_new):
            pk = jnp.exp(cols[tk] - m_new)
            l = l + pk
            acc = acc + pk.astype(BF16).astype(F32) * vn_ref[0, tk:tk + 1, :].astype(BF16).astype(F32)
        o = acc / l
        for i in range(t_new):
            slab = jnp.where(mask8, o[i * B_HEADS:(i + 1) * B_HEADS, :], 0.0)
            o_ref[0, i * B_HEADS:(i + 1) * B_HEADS, :] = jnp.broadcast_to(
                jnp.sum(slab, axis=0, keepdims=True), (B_HEADS, B_WIDTH))


def _attn_sample(page_table, q3, bias3, biasn3, k_new, v_new, cache_kt, cache_vt):
    db, t_new, _ = q3.shape
    n_pages = page_table.shape[1]
    steps = n_pages // PAGES_PER_STEP
    span = PAGES_PER_STEP * PAGE_SIZE
    rows = t_new * B_HEADS
    per_b = lambda w: pl.BlockSpec((1, t_new, w), lambda b, j, pt: (b, 0, 0))
    grid_spec = pltpu.PrefetchScalarGridSpec(
        num_scalar_prefetch=1, grid=(db, steps),
        in_specs=[per_b(B_WIDTH),
                  pl.BlockSpec((1, t_new, span), lambda b, j, pt: (b, 0, j)),
                  per_b(LANES), per_b(B_WIDTH), per_b(B_WIDTH)]
        + _page_specs((1, B_WIDTH, PAGE_SIZE), PAGES_PER_STEP)
        + _page_specs((1, B_WIDTH, PAGE_SIZE), PAGES_PER_STEP),
        out_specs=pl.BlockSpec((1, rows, B_WIDTH), lambda b, j, pt: (b, 0, 0)),
        scratch_shapes=[pltpu.VMEM((rows, 1), F32), pltpu.VMEM((rows, 1), F32),
                        pltpu.VMEM((rows, B_WIDTH), F32), pltpu.VMEM((rows, B_WIDTH), BF16)])
    return pl.pallas_call(
        functools.partial(_attn_sample_kernel, t_new), grid_spec=grid_spec,
        out_shape=jax.ShapeDtypeStruct((db, rows, B_WIDTH), F32),
        compiler_params=_cparams(("parallel", "arbitrary")), name="attn_sample",
    )(page_table, q3, bias3, biasn3, k_new, v_new,
      *([cache_kt] * PAGES_PER_STEP), *([cache_vt] * PAGES_PER_STEP))


def _merge_kernel(x_ref, ma_ref, yb_ref, sgb_ref, yc_ref, sgc_ref, wo_ref, g_ref, y_out):
    mb = (yb_ref[...].astype(F32) * sgb_ref[...].astype(F32)).astype(BF16)
    mc = (yc_ref[...].astype(F32) * sgc_ref[...].astype(F32)).astype(BF16)
    o = (_dot(ma_ref[...], wo_ref[0:A_WIDTH, :])
         + _dot(mb, wo_ref[A_WIDTH:A_WIDTH + B_WIDTH, :])
         + _dot(mc, wo_ref[A_WIDTH + B_WIDTH:, :]))
    z = x_ref[...] + o
    ms = jnp.mean(z * z, axis=-1, keepdims=True)
    y_out[...] = z * lax.rsqrt(ms + EPS) * g_ref[...]


def _merge(x, mixa, yb, sgb, yc, sgc, wo, g_final):
    n = x.shape[0]
    row = lambda w: pl.BlockSpec((TM, w), lambda i: (i, 0))
    return pl.pallas_call(
        _merge_kernel, grid=(n // TM,),
        in_specs=[row(D_MODEL), row(A_WIDTH), row(B_WIDTH), row(B_WIDTH), row(C_WIDTH), row(C_WIDTH),
                  pl.BlockSpec(wo.shape, lambda i: (0, 0)), pl.BlockSpec(g_final.shape, lambda i: (0, 0))],
        out_specs=row(D_MODEL),
        out_shape=jax.ShapeDtypeStruct((n, D_MODEL), F32),
        compiler_params=_cparams(("parallel",)), name="merge",
    )(x, mixa, yb, sgb, yc, sgc, wo, g_final)


def _rope_tables(pos):
    inv = ROPE_THETA ** (-jnp.arange(0, HEAD_DIM, 2, dtype=F32) / HEAD_DIM)
    ang = pos.astype(F32)[:, None] * inv[None, :]
    c = jnp.cos(ang)
    s = jnp.sin(ang)
    return jnp.concatenate([c, c, c, c], axis=1), jnp.concatenate([-s, s, -s, s], axis=1)


def _heads_last(t, heads):
    b, _, n = t.shape
    return t.reshape(b, heads, HEAD_DIM, n).transpose(0, 3, 1, 2)


def _channels_first(t):
    p, n, heads, d = t.shape
    return t.transpose(0, 2, 3, 1).reshape(p, heads * d, n)


def kernel(x_prompt, x_sample, mem_prompt, cache_k, cache_v, cache_idx_k, cache_mem_k, cache_mem_v,
           page_table, g_norm, w_in, w_spatial, b_spatial, g_v, w_mem_kv, g_mem, w_out, g_final):
    batch, seq, _ = x_prompt.shape
    db, t_new, _ = x_sample.shape
    n_pages = page_table.shape[1]
    past = n_pages * PAGE_SIZE

    ki_cols = w_in[:, 3328:3392]
    wi_cols = w_in[:, 3392:3400]
    w_r = jnp.concatenate(
        [w_in[:, :3328], w_in[:, 3400:3912], ki_cols, ki_cols, wi_cols,
         jnp.zeros((D_MODEL, LANES - IDX_HEADS), F32)], axis=1).astype(BF16)
    g_norm2 = g_norm.reshape(1, D_MODEL)
    g_v2 = g_v.reshape(1, A_WIDTH)
    bs_prompt = jnp.repeat(b_spatial.T, HEAD_DIM, axis=1)
    reps = CHUNK // t_new
    ws_sample = jnp.tile(w_spatial[:, :t_new, :t_new], (1, reps, reps))
    bs_sample = jnp.tile(jnp.repeat(b_spatial[:, :t_new].T, HEAD_DIM, axis=1), (reps, 1))
    wo_b = w_out.astype(BF16)
    g_final2 = g_final.reshape(1, D_MODEL)

    cos_p, sin_p = _rope_tables(jnp.arange(seq, dtype=jnp.int32))
    pos_s = past + jnp.arange(t_new, dtype=jnp.int32)
    cos_s, sin_s = _rope_tables(jnp.tile(pos_s, TM // t_new))

    xp = x_prompt.reshape(batch * seq, D_MODEL)
    (kt, vt, kit, kbf, vtb, ki2, qz, qiz, wt, sgb, mixa, qc, sgc) = _proj(
        xp, g_norm2, w_r, cos_p, sin_p, w_spatial, bs_prompt, g_v2, CHUNK, seq // TM, False)
    mkt, mvt = _memkv(mem_prompt.reshape(batch * N_MEM, D_MODEL), g_mem.reshape(1, D_MODEL),
                      w_mem_kv.astype(BF16))
    yc = _memattn(qc.reshape(batch * seq // TM, TM, C_WIDTH), mkt, mvt, seq // TM).reshape(batch * seq, C_WIDTH)
    yb = _dsa_prompt(qz, qiz, wt, ki2, kbf, vtb, batch, seq)
    y_prompt = _merge(xp, mixa, yb, sgb, yc, sgc, wo_b, g_final2).reshape(batch, seq, D_MODEL)

    ns = db * t_new
    xs = x_sample.reshape(ns, D_MODEL)
    (q_s, k_s, v_s, qi_s, ki_s, wi_s, sgb_s, mixa_s, qc_s, sgc_s, vn_s) = _proj(
        xs, g_norm2, w_r, cos_s, sin_s, ws_sample, bs_sample, g_v2, t_new, 1, True)
    sc, scn = _idx_sample(page_table, qi_s.reshape(db, t_new * IDX_HEADS, IDX_DIM),
                          wi_s.reshape(db, t_new * IDX_HEADS, 1), ki_s.reshape(db, t_new, IDX_DIM),
                          cache_idx_k.transpose(0, 2, 1))
    topk_s = min(TOPK_MAX, (past + t_new) // 4)
    bias, biasn = _thr_sample(sc.reshape(ns, past), scn.reshape(ns, LANES), topk_s)
    yb_pad = _attn_sample(page_table, q_s.reshape(db, t_new, B_WIDTH), bias.reshape(db, t_new, past),
                          biasn.reshape(db, t_new, LANES), k_s.reshape(db, t_new, B_WIDTH),
                          v_s.reshape(db, t_new, B_WIDTH), _channels_first(cache_k), _channels_first(cache_v))
    yb_s = yb_pad[:, ::B_HEADS, :].reshape(ns, B_WIDTH)
    yc_pad = _memattn_sample(qc_s.reshape(db, t_new, C_WIDTH),
                             _channels_first(cache_mem_k), _channels_first(cache_mem_v))
    yc_s = yc_pad[:, ::SUBLANES, :].reshape(ns, C_WIDTH)
    y_sample = _merge(xs, mixa_s, yb_s, sgb_s, yc_s, sgc_s, wo_b, g_final2).reshape(db, t_new, D_MODEL)

    return (y_prompt, y_sample,
            _heads_last(kt, B_HEADS), _heads_last(vt, B_HEADS), kit.transpose(0, 2, 1),
            _heads_last(mkt, C_HEADS), _heads_last(mvt, C_HEADS),
            k_s.reshape(db, t_new, B_HEADS, HEAD_DIM), v_s.reshape(db, t_new, B_HEADS, HEAD_DIM),
            ki_s.reshape(db, t_new, IDX_DIM), vn_s.reshape(db, t_new, A_WIDTH))
```

```python
import functools

import jax
import jax.numpy as jnp
from jax import lax
from jax.experimental import pallas as pl
from jax.experimental.pallas import tpu as pltpu

F32 = jnp.float32
BF16 = jnp.bfloat16

HEAD_DIM = 64
HALF = HEAD_DIM // 2
D_MODEL = 1024
A_GROUPS = 4
A_WIDTH = A_GROUPS * HEAD_DIM
CHUNK = 128
B_HEADS = 8
B_WIDTH = B_HEADS * HEAD_DIM
IDX_HEADS = 8
IDX_DIM = 64
TOPK_MAX = 256
N_MEM = 256
C_HEADS = 4
C_WIDTH = C_HEADS * HEAD_DIM
PAGE_SIZE = 128
ROPE_THETA = 10000.0
EPS = 1e-6
IDX_W_SCALE = (IDX_HEADS * IDX_DIM) ** -0.5
ATTN_SCALE = HEAD_DIM ** -0.5
LOG2E = 1.4426950408889634

LANES = 128
SUBLANES = 8
NEG = -1e30
FLT_MAX = 3.4028234663852886e38
INT_MIN = -2 ** 31

TM = 256
QB = 256
KC = 256
PAGES_PER_STEP = 8
THR_ROWS = 64
VMEM_LIMIT = 52 * 1024 * 1024

OFF_U, OFF_VA, OFF_GA = 0, 256, 512
OFF_Q, OFF_K, OFF_V, OFF_GB, OFF_QI = 768, 1280, 1792, 2304, 2816
OFF_QC, OFF_GC, OFF_KI2, OFF_WI, W_TOTAL = 3328, 3584, 3840, 3968, 4096


def _cparams(sem):
    return pltpu.CompilerParams(dimension_semantics=sem, vmem_limit_bytes=VMEM_LIMIT)


def _iota(shape, dim):
    return lax.broadcasted_iota(jnp.int32, shape, dim)


def _silu(x):
    return x / (1.0 + jnp.exp(-x))


def _dot(a, b):
    return jnp.dot(a, b, preferred_element_type=F32)


def _dot_nt(a, b):
    return lax.dot_general(a, b, (((1,), (1,)), ((), ())), preferred_element_type=F32)


def _key_to_float(k):
    bits = k ^ ((k >> 31) & jnp.int32(0x7FFFFFFF))
    return lax.bitcast_convert_type(bits, F32)


def _kth_largest(count_ge, kth, shape):
    cnt0 = count_ge(jnp.zeros(shape, F32))
    ok0 = cnt0 >= kth
    key = jnp.where(ok0, jnp.int32(0), jnp.int32(INT_MIN))
    cnt_key = jnp.where(ok0, cnt0, jnp.float32(2 ** 24))

    def bit_body(it, carry):
        key, cnt_key = carry
        trial = key + lax.shift_left(jnp.int32(1), 30 - it)
        cnt = count_ge(_key_to_float(trial))
        ok = cnt >= kth
        return jnp.where(ok, trial, key), jnp.where(ok, cnt, cnt_key)

    key, cnt_key = lax.fori_loop(0, 31, bit_body, (key, cnt_key))
    return _key_to_float(key), cnt_key


def _rope(t, cos, sin):
    outs = []
    first = (_iota((t.shape[0], LANES), 1) % HEAD_DIM) < HALF
    for j in range(t.shape[1] // LANES):
        x = t[:, j * LANES:(j + 1) * LANES]
        partner = jnp.where(first, pltpu.roll(x, LANES - HALF, 1), pltpu.roll(x, HALF, 1))
        outs.append(x * cos + partner * sin)
    return outs[0] if len(outs) == 1 else jnp.concatenate(outs, axis=1)


def _proj_kernel(chunk, sample, x_ref, g_ref, w_ref, cos_ref, sin_ref, ws_ref, bs_ref, gv_ref, *outs):
    if sample:
        q_out, k_out, v_out, qi_out, ki_out, wi_out, sgb_out, mixa_out, qc_out, sgc_out, vn_out = outs
    else:
        (kt_out, vt_out, kit_out, kbf_out, vtb_out, ki2_out, qz_out, qiz_out, wt_out,
         sgb_out, mixa_out, qc_out, sgc_out) = outs
    tm = x_ref.shape[0]
    x = x_ref[...]
    ms = jnp.mean(x * x, axis=-1, keepdims=True)
    h = (x * lax.rsqrt(ms + EPS) * g_ref[...]).astype(BF16)
    cos = cos_ref[...]
    sin = sin_ref[...]

    def seg(a, b):
        return _dot(h, w_ref[:, a:b])

    half_id = (_iota((tm, LANES), 1) // HEAD_DIM)

    def head_split(t, out):
        for hd in range(B_HEADS):
            pair = t[:, (hd // 2) * LANES:(hd // 2 + 1) * LANES]
            out[hd] = jnp.where(half_id == hd % 2, pair, 0.0).astype(BF16)

    q = _rope(seg(OFF_Q, OFF_K), cos, sin)
    k = _rope(seg(OFF_K, OFF_V), cos, sin)
    v = seg(OFF_V, OFF_GB)
    qi = _rope(seg(OFF_QI, OFF_QC), cos, sin)
    ki2 = _rope(seg(OFF_KI2, OFF_WI), cos, sin)
    wi = seg(OFF_WI, W_TOTAL) * IDX_W_SCALE
    qc = seg(OFF_QC, OFF_GC) * ATTN_SCALE
    sgb_out[...] = _silu(seg(OFF_GB, OFF_QI)).astype(BF16)
    sgc_out[...] = _silu(seg(OFF_GC, OFF_KI2)).astype(BF16)
    qc_out[...] = qc.astype(qc_out.dtype)
    if sample:
        q_out[...] = q
        k_out[...] = k
        v_out[...] = v
        qi_out[...] = qi
        ki_out[...] = ki2[:, :IDX_DIM]
        wi_out[...] = wi[:, :IDX_HEADS]
    else:
        head_split(q * (ATTN_SCALE * LOG2E), qz_out)
        head_split(qi, qiz_out)
        kt_out[0] = k.T
        kbf_out[...] = k.astype(BF16)
        vt = v.T
        vt_out[0] = vt
        vtb_out[0] = vt.astype(BF16)
        kit_out[0] = ki2.T[:IDX_DIM, :]
        ki2_out[...] = ki2.astype(BF16)
        wt_out[...] = wi.T[:IDX_HEADS, :]

    u = seg(OFF_U, OFF_VA)
    va = seg(OFF_VA, OFF_GA)
    ga = seg(OFF_GA, OFF_Q)
    grp_r = _iota((A_WIDTH, A_WIDTH), 0) // HEAD_DIM
    grp_c = _iota((A_WIDTH, A_WIDTH), 1) // HEAD_DIM
    gmat = jnp.where(grp_r == grp_c, 1.0 / HEAD_DIM, 0.0).astype(BF16)

    def group_mean(t):
        hi = t.astype(BF16)
        lo = (t - hi.astype(F32)).astype(BF16)
        return _dot(hi, gmat) + _dot(lo, gmat)

    mu = group_mean(va)
    d = va - mu
    var = group_mean(d * d)
    vn = d * lax.rsqrt(var + EPS) * gv_ref[...]
    if sample:
        vn_out[...] = vn
    vnb = vn.astype(BF16)
    row = _iota((CHUNK, CHUNK), 0)
    col = _iota((CHUNK, CHUNK), 1)
    causal = (row >= col) & ((row // chunk) == (col // chunk))
    lane_grp = _iota((CHUNK, A_WIDTH), 1) // HEAD_DIM
    wms = [jnp.where(causal, ws_ref[g], 0.0).astype(BF16) for g in range(A_GROUPS)]
    mixes = []
    for j in range(tm // CHUNK):
        vc = vnb[j * CHUNK:(j + 1) * CHUNK, :]
        m = bs_ref[...]
        for g in range(A_GROUPS):
            m = m + jnp.where(lane_grp == g, _dot(wms[g], vc), 0.0)
        mixes.append(m)
    mix = jnp.concatenate(mixes, axis=0)
    mixa_out[...] = (_silu(ga) * u * mix).astype(BF16)


def _proj(x, g_norm, w_r, cos_t, sin_t, ws, bs, g_v, chunk, seq_blocks, sample):
    n = x.shape[0]
    nt = n // TM
    nb = nt // seq_blocks
    row = lambda w: pl.BlockSpec((TM, w), lambda i: (i, 0))
    const2 = lambda a: pl.BlockSpec(a.shape, lambda i: (0, 0))
    sds = jax.ShapeDtypeStruct
    in_specs = [
        row(D_MODEL), const2(g_norm), const2(w_r),
        pl.BlockSpec((TM, LANES), lambda i: (i % seq_blocks, 0)),
        pl.BlockSpec((TM, LANES), lambda i: (i % seq_blocks, 0)),
        pl.BlockSpec(ws.shape, lambda i: (0, 0, 0)), const2(bs), const2(g_v),
    ]
    if sample:
        outs = [
            (sds((n, B_WIDTH), F32), row(B_WIDTH)),
            (sds((n, B_WIDTH), F32), row(B_WIDTH)),
            (sds((n, B_WIDTH), F32), row(B_WIDTH)),
            (sds((n, IDX_HEADS * IDX_DIM), F32), row(IDX_HEADS * IDX_DIM)),
            (sds((n, IDX_DIM), F32), row(IDX_DIM)),
            (sds((n, IDX_HEADS), F32), row(IDX_HEADS)),
            (sds((n, B_WIDTH), BF16), row(B_WIDTH)),
            (sds((n, A_WIDTH), BF16), row(A_WIDTH)),
            (sds((n, C_WIDTH), F32), row(C_WIDTH)),
            (sds((n, C_WIDTH), BF16), row(C_WIDTH)),
            (sds((n, A_WIDTH), F32), row(A_WIDTH)),
        ]
    else:
        seq = seq_blocks * TM
        tspec = lambda w: pl.BlockSpec((1, w, TM), lambda i: (i // seq_blocks, 0, i % seq_blocks))
        hspec = pl.BlockSpec((B_HEADS, TM, LANES), lambda i: (0, i, 0))
        outs = [
            (sds((nb, B_WIDTH, seq), F32), tspec(B_WIDTH)),
            (sds((nb, B_WIDTH, seq), F32), tspec(B_WIDTH)),
            (sds((nb, IDX_DIM, seq), F32), tspec(IDX_DIM)),
            (sds((n, B_WIDTH), BF16), row(B_WIDTH)),
            (sds((nt, B_WIDTH, TM), BF16), pl.BlockSpec((1, B_WIDTH, TM), lambda i: (i, 0, 0))),
            (sds((n, LANES), BF16), row(LANES)),
            (sds((B_HEADS, n, LANES), BF16), hspec),
            (sds((IDX_HEADS, n, LANES), BF16), hspec),
            (sds((IDX_HEADS, n), F32), pl.BlockSpec((IDX_HEADS, TM), lambda i: (0, i))),
            (sds((n, B_WIDTH), BF16), row(B_WIDTH)),
            (sds((n, A_WIDTH), BF16), row(A_WIDTH)),
            (sds((n, C_WIDTH), BF16), row(C_WIDTH)),
            (sds((n, C_WIDTH), BF16), row(C_WIDTH)),
        ]
    return pl.pallas_call(
        functools.partial(_proj_kernel, chunk, sample),
        grid=(nt,), in_specs=in_specs, out_specs=[o[1] for o in outs], out_shape=[o[0] for o in outs],
        compiler_params=_cparams(("parallel",)), name="proj",
    )(x, g_norm, w_r, cos_t, sin_t, ws, bs, g_v)


def _memkv_kernel(m_ref, g_ref, w_ref, mkt_out, mvt_out):
    x = m_ref[...]
    ms = jnp.mean(x * x, axis=-1, keepdims=True)
    h = (x * lax.rsqrt(ms + EPS) * g_ref[...]).astype(BF16)
    kv = _dot(h, w_ref[...])
    mkt_out[0] = kv[:, :C_WIDTH].T
    mvt_out[0] = kv[:, C_WIDTH:].T


def _memkv(mem, g_mem, w_kv):
    n = mem.shape[0]
    nb = n // N_MEM
    return pl.pallas_call(
        _memkv_kernel, grid=(nb,),
        in_specs=[pl.BlockSpec((N_MEM, D_MODEL), lambda i: (i, 0)),
                  pl.BlockSpec(g_mem.shape, lambda i: (0, 0)),
                  pl.BlockSpec(w_kv.shape, lambda i: (0, 0))],
        out_specs=[pl.BlockSpec((1, C_WIDTH, N_MEM), lambda i: (i, 0, 0))] * 2,
        out_shape=[jax.ShapeDtypeStruct((nb, C_WIDTH, N_MEM), F32)] * 2,
        compiler_params=_cparams(("parallel",)), name="memkv",
    )(mem, g_mem, w_kv)


def _memattn_kernel(q_ref, mkt_ref, mvt_ref, o_ref):
    q = q_ref[0]
    mkt = mkt_ref[0].astype(BF16)
    mvt = mvt_ref[0].astype(BF16)
    t = q.shape[0]
    lane_head = _iota((t, C_WIDTH), 1) // HEAD_DIM
    out = jnp.zeros((t, C_WIDTH), F32)
    for hd in range(C_HEADS):
        qh = jnp.where(lane_head == hd, q, 0.0).astype(BF16)
        s = _dot(qh, mkt)
        p = jnp.exp(s - jnp.max(s, axis=-1, keepdims=True))
        o = _dot_nt(p.astype(BF16), mvt) / jnp.sum(p, axis=-1, keepdims=True)
        out = out + jnp.where(lane_head == hd, o, 0.0)
    o_ref[0] = out.astype(o_ref.dtype)


def _memattn(q3, mkt3, mvt3, tiles_per_mem):
    g, t, _ = q3.shape
    mem_spec = pl.BlockSpec((1, C_WIDTH, N_MEM), lambda i: (i // tiles_per_mem, 0, 0))
    return pl.pallas_call(
        _memattn_kernel, grid=(g,),
        in_specs=[pl.BlockSpec((1, t, C_WIDTH), lambda i: (i, 0, 0)), mem_spec, mem_spec],
        out_specs=pl.BlockSpec((1, t, C_WIDTH), lambda i: (i, 0, 0)),
        out_shape=jax.ShapeDtypeStruct((g, t, C_WIDTH), BF16),
        compiler_params=_cparams(("parallel",)), name="memattn",
    )(q3, mkt3, mvt3)


def _memattn_sample_kernel(q_ref, mkt_ref, mvt_ref, o_ref):
    q4 = q_ref[0]
    mkt = mkt_ref[0].astype(BF16)
    mvt = mvt_ref[0].astype(BF16)
    t = q4.shape[0]
    mask8 = (_iota((SUBLANES, C_WIDTH), 1) // HEAD_DIM) == _iota((SUBLANES, C_WIDTH), 0)
    qbd = jnp.concatenate(
        [jnp.where(mask8, jnp.broadcast_to(q4[i:i + 1, :], (SUBLANES, C_WIDTH)), 0.0) for i in range(t)], axis=0)
    s = _dot(qbd.astype(BF16), mkt)
    p = jnp.exp(s - jnp.max(s, axis=-1, keepdims=True))
    o = _dot_nt(p.astype(BF16), mvt) / jnp.sum(p, axis=-1, keepdims=True)
    for i in range(t):
        slab = jnp.where(mask8, o[i * SUBLANES:(i + 1) * SUBLANES, :], 0.0)
        o_ref[0, i * SUBLANES:(i + 1) * SUBLANES, :] = jnp.broadcast_to(
            jnp.sum(slab, axis=0, keepdims=True), (SUBLANES, C_WIDTH))


def _memattn_sample(q3, mkt3, mvt3):
    g, t, _ = q3.shape
    mem_spec = pl.BlockSpec((1, C_WIDTH, N_MEM), lambda i: (i, 0, 0))
    return pl.pallas_call(
        _memattn_sample_kernel, grid=(g,),
        in_specs=[pl.BlockSpec((1, t, C_WIDTH), lambda i: (i, 0, 0)), mem_spec, mem_spec],
        out_specs=pl.BlockSpec((1, t * SUBLANES, C_WIDTH), lambda i: (i, 0, 0)),
        out_shape=jax.ShapeDtypeStruct((g, t * SUBLANES, C_WIDTH), F32),
        compiler_params=_cparams(("parallel",)), name="memattn_sample",
    )(q3, mkt3, mvt3)


def _dsa_prompt_kernel(topk, qz_ref, qiz_ref, wt_ref, ki2_ref, kbf_ref, vt_ref, o_ref,
                       sc_ref, acc_ref, m_ref, l_ref, s_ref):
    i = pl.program_id(1)
    nk = i + 1
    npair = (nk + 1) // 2
    kth = jnp.float32(topk)

    def sum_rows(x):
        return jnp.sum(x.reshape(KC // SUBLANES, SUBLANES, QB), axis=0)

    def max_rows(x):
        return jnp.max(x.reshape(KC // SUBLANES, SUBLANES, QB), axis=0)

    def score_chunk(c, _):
        keys = ki2_ref[0, c]
        acc = jnp.zeros((KC, QB), F32)
        for hd in range(IDX_HEADS):
            y = _dot_nt(keys, qiz_ref[hd])
            acc = acc + jnp.maximum(y, 0.0) * wt_ref[hd:hd + 1, :]
        kpos = c * KC + _iota((KC, QB), 0)
        qpos = i * QB + _iota((KC, QB), 1)
        sc_ref[c] = jnp.where(kpos <= qpos, acc, -jnp.inf)
        return 0

    lax.fori_loop(0, nk, score_chunk, 0)

    @pl.when(nk % 2 == 1)
    def _():
        sc_ref[jnp.minimum(nk, sc_ref.shape[0] - 1)] = jnp.full((KC, QB), -jnp.inf, F32)

    def count(pred):
        def body(j, cnt):
            return (cnt + sum_rows(pred(sc_ref[2 * j]).astype(F32))
                    + sum_rows(pred(sc_ref[2 * j + 1]).astype(F32)))
        return jnp.sum(lax.fori_loop(0, npair, body, jnp.zeros((SUBLANES, QB), F32)), axis=0, keepdims=True)

    def write_ge(thr):
        def body(j, _):
            sc_ref[2 * j] = jnp.where(sc_ref[2 * j] >= thr, 0.0, NEG)
            sc_ref[2 * j + 1] = jnp.where(sc_ref[2 * j + 1] >= thr, 0.0, NEG)
            return 0
        lax.fori_loop(0, npair, body, 0)

    @pl.when(nk * KC <= topk)
    def _():
        write_ge(jnp.full((1, QB), -FLT_MAX, F32))

    @pl.when(nk * KC > topk)
    def _():
        thr, cnt_thr = _kth_largest(lambda t: count(lambda x: x >= t), kth, (1, QB))
        has_tie = jnp.max(cnt_thr) > kth

        @pl.when(jnp.logical_not(has_tie))
        def _():
            write_ge(thr)

        @pl.when(has_tie)
        def _():
            need = kth - count(lambda x: x > thr)
            ltri = (_iota((KC, KC), 1) < _iota((KC, KC), 0)).astype(BF16)

            def body(c, carry):
                x = sc_ref[c]
                eq = x == thr
                before = _dot(ltri, eq.astype(BF16)) + carry
                sel = (x > thr) | (eq & (before < need))
                sc_ref[c] = jnp.where(sel, 0.0, NEG)
                return carry + jnp.sum(sum_rows(eq.astype(F32)), axis=0, keepdims=True)

            lax.fori_loop(0, nk, body, jnp.zeros((1, QB), F32))

    m_ref[...] = jnp.full((B_HEADS, QB), NEG, F32)
    l_ref[...] = jnp.zeros((B_HEADS, QB), F32)
    acc_ref[...] = jnp.zeros((B_WIDTH, QB), F32)

    def biased_scores(c, hd):
        pair = hd // 2
        kc = kbf_ref[0, c, :, pair * LANES:(pair + 1) * LANES]
        return _dot_nt(kc, qz_ref[hd]) + sc_ref[c]

    for hd in range(B_HEADS):
        s_ref[hd] = biased_scores(0, hd)

    def attend(c, _):
        nxt = jnp.minimum(c + 1, nk - 1)
        for hd in range(B_HEADS):
            rows = slice(hd * HEAD_DIM, (hd + 1) * HEAD_DIM)
            s_next = biased_scores(nxt, hd)
            s = s_ref[hd]
            m_old = m_ref[hd:hd + 1, :]
            m_new = jnp.maximum(m_old, jnp.max(max_rows(s), axis=0, keepdims=True))
            alpha = jnp.exp2(m_old - m_new)
            p = jnp.exp2(s - m_new)
            l_ref[hd:hd + 1, :] = alpha * l_ref[hd:hd + 1, :] + jnp.sum(sum_rows(p), axis=0, keepdims=True)
            acc_ref[rows, :] = alpha * acc_ref[rows, :] + _dot(vt_ref[0, c, rows, :], p.astype(BF16))
            m_ref[hd:hd + 1, :] = m_new
            s_ref[hd] = s_next
        return 0

    lax.fori_loop(0, nk, attend, 0)
    for hd in range(B_HEADS):
        rows = slice(hd * HEAD_DIM, (hd + 1) * HEAD_DIM)
        acc_ref[rows, :] = acc_ref[rows, :] / l_ref[hd:hd + 1, :]
    o_ref[...] = acc_ref[...].T.astype(o_ref.dtype)


def _dsa_prompt(qz, qiz, wt, ki2, kbf, vtb, batch, seq):
    nq = seq // QB
    nc = seq // KC
    topk = min(TOPK_MAX, seq // 4)
    n = batch * seq
    qspec = pl.BlockSpec((B_HEADS, QB, LANES), lambda b, i: (0, b * nq + i, 0))
    return pl.pallas_call(
        functools.partial(_dsa_prompt_kernel, topk),
        grid=(batch, nq),
        in_specs=[qspec, qspec,
                  pl.BlockSpec((IDX_HEADS, QB), lambda b, i: (0, b * nq + i)),
                  pl.BlockSpec((1, nc, KC, LANES), lambda b, i: (b, 0, 0, 0)),
                  pl.BlockSpec((1, nc, KC, B_WIDTH), lambda b, i: (b, 0, 0, 0)),
                  pl.BlockSpec((1, nc, B_WIDTH, KC), lambda b, i: (b, 0, 0, 0))],
        out_specs=pl.BlockSpec((QB, B_WIDTH), lambda b, i: (b * nq + i, 0)),
        out_shape=jax.ShapeDtypeStruct((n, B_WIDTH), BF16),
        scratch_shapes=[pltpu.VMEM((nc, KC, QB), F32), pltpu.VMEM((B_WIDTH, QB), F32),
                        pltpu.VMEM((B_HEADS, QB), F32), pltpu.VMEM((B_HEADS, QB), F32),
                        pltpu.VMEM((B_HEADS, KC, QB), F32)],
        compiler_params=_cparams(("parallel", "arbitrary")), name="dsa_prompt",
    )(qz, qiz, wt, ki2.reshape(batch, nc, KC, LANES), kbf.reshape(batch, nc, KC, B_WIDTH),
      vtb.reshape(batch, nc, B_WIDTH, KC))


def _page_specs(block, n_pages_per_step):
    def make(r):
        return pl.BlockSpec(block, lambda b, j, pt: (pt[b, j * n_pages_per_step + r], 0, 0))
    return [make(r) for r in range(n_pages_per_step)]


def _idx_sample_kernel(t_new, pt_ref, qi_ref, w_ref, kin_ref, *rest):
    pages = rest[:PAGES_PER_STEP]
    sc_out, scn_out = rest[PAGES_PER_STEP:]
    j = pl.program_id(1)
    qi = qi_ref[0]
    w = w_ref[0]
    qib = qi.astype(BF16)

    def token_scores(y):
        z = jnp.maximum(y, 0.0) * w
        return jnp.sum(z.reshape(t_new, IDX_HEADS, y.shape[1]), axis=1)

    for r in range(PAGES_PER_STEP):
        y = _dot(qib, pages[r][0].astype(BF16))
        sc_out[0, :, r * PAGE_SIZE:(r + 1) * PAGE_SIZE] = token_scores(y)

    @pl.when(j == 0)
    def _():
        qf = qib.astype(F32)
        lane = _iota((t_new, LANES), 1)
        tok = _iota((t_new, LANES), 0)
        new = jnp.full((t_new, LANES), -jnp.inf, F32)
        for tk in range(t_new):
            kr = kin_ref[0, tk:tk + 1, :].astype(BF16).astype(F32)
            col = token_scores(jnp.sum(qf * kr, axis=-1, keepdims=True))
            new = jnp.where((lane == tk) & (tok >= tk), col, new)
        scn_out[0] = new


def _idx_sample(page_table, qi32, w32, ki_new, cache_idx_kt):
    db, rows, _ = qi32.shape
    t_new = rows // IDX_HEADS
    n_pages = page_table.shape[1]
    steps = n_pages // PAGES_PER_STEP
    span = PAGES_PER_STEP * PAGE_SIZE
    grid_spec = pltpu.PrefetchScalarGridSpec(
        num_scalar_prefetch=1, grid=(db, steps),
        in_specs=[pl.BlockSpec((1, rows, IDX_DIM), lambda b, j, pt: (b, 0, 0)),
                  pl.BlockSpec((1, rows, 1), lambda b, j, pt: (b, 0, 0)),
                  pl.BlockSpec((1, t_new, IDX_DIM), lambda b, j, pt: (b, 0, 0))]
        + _page_specs((1, IDX_DIM, PAGE_SIZE), PAGES_PER_STEP),
        out_specs=[pl.BlockSpec((1, t_new, span), lambda b, j, pt: (b, 0, j)),
                   pl.BlockSpec((1, t_new, LANES), lambda b, j, pt: (b, 0, 0))])
    return pl.pallas_call(
        functools.partial(_idx_sample_kernel, t_new), grid_spec=grid_spec,
        out_shape=[jax.ShapeDtypeStruct((db, t_new, n_pages * PAGE_SIZE), F32),
                   jax.ShapeDtypeStruct((db, t_new, LANES), F32)],
        compiler_params=_cparams(("parallel", "arbitrary")), name="idx_sample",
    )(page_table, qi32, w32, ki_new, *([cache_idx_kt] * PAGES_PER_STEP))


def _thr_sample_kernel(topk, sc_ref, scn_ref, b_out, bn_out):
    kth = jnp.float32(topk)
    n_chunks = sc_ref.shape[1] // LANES

    def count(pred):
        return (jnp.sum(pred(sc_ref[...]).astype(F32), axis=-1, keepdims=True)
                + jnp.sum(pred(scn_ref[...]).astype(F32), axis=-1, keepdims=True))

    thr, cnt_thr = _kth_largest(lambda t: count(lambda x: x >= t), kth, (THR_ROWS, 1))
    has_tie = jnp.max(cnt_thr) > kth

    @pl.when(jnp.logical_not(has_tie))
    def _():
        b_out[...] = jnp.where(sc_ref[...] >= thr, 0.0, NEG)
        bn_out[...] = jnp.where(scn_ref[...] >= thr, 0.0, NEG)

    @pl.when(has_tie)
    def _():
        need = kth - count(lambda x: x > thr)
        utri = (_iota((LANES, LANES), 0) < _iota((LANES, LANES), 1)).astype(BF16)
        carry = jnp.zeros((THR_ROWS, 1), F32)
        for c in range(n_chunks + 1):
            x = sc_ref[:, c * LANES:(c + 1) * LANES] if c < n_chunks else scn_ref[...]
            eq = x == thr
            before = _dot(eq.astype(BF16), utri) + carry
            sel = (x > thr) | (eq & (before < need))
            bias = jnp.where(sel, 0.0, NEG)
            if c < n_chunks:
                b_out[:, c * LANES:(c + 1) * LANES] = bias
            else:
                bn_out[...] = bias
            carry = carry + jnp.sum(eq.astype(F32), axis=-1, keepdims=True)


def _thr_sample(sc, scn, topk):
    n, width = sc.shape
    return pl.pallas_call(
        functools.partial(_thr_sample_kernel, topk), grid=(n // THR_ROWS,),
        in_specs=[pl.BlockSpec((THR_ROWS, width), lambda i: (i, 0)),
                  pl.BlockSpec((THR_ROWS, LANES), lambda i: (i, 0))],
        out_specs=[pl.BlockSpec((THR_ROWS, width), lambda i: (i, 0)),
                   pl.BlockSpec((THR_ROWS, LANES), lambda i: (i, 0))],
        out_shape=[jax.ShapeDtypeStruct((n, width), F32), jax.ShapeDtypeStruct((n, LANES), F32)],
        compiler_params=_cparams(("parallel",)), name="thr_sample",
    )(sc, scn)


def _attn_sample_kernel(t_new, pt_ref, q_ref, bias_ref, biasn_ref, kn_ref, vn_ref, *rest):
    kpages = rest[:PAGES_PER_STEP]
    vpages = rest[PAGES_PER_STEP:2 * PAGES_PER_STEP]
    o_ref, m_ref, l_ref, acc_ref, qbd_ref = rest[2 * PAGES_PER_STEP:]
    j = pl.program_id(1)
    rows = t_new * B_HEADS
    mask8 = (_iota((SUBLANES, B_WIDTH), 1) // HEAD_DIM) == _iota((SUBLANES, B_WIDTH), 0)

    def per_token(ref, width):
        return jnp.concatenate(
            [jnp.broadcast_to(ref[0, i:i + 1, :], (B_HEADS, width)) for i in range(t_new)], axis=0)

    @pl.when(j == 0)
    def _():
        m_ref[...] = jnp.full((rows, 1), NEG, F32)
        l_ref[...] = jnp.zeros((rows, 1), F32)
        acc_ref[...] = jnp.zeros((rows, B_WIDTH), F32)
        qbd_ref[...] = jnp.concatenate(
            [jnp.where(mask8, jnp.broadcast_to(q_ref[0, i:i + 1, :] * ATTN_SCALE, (B_HEADS, B_WIDTH)), 0.0)
             for i in range(t_new)], axis=0).astype(BF16)

    qbd = qbd_ref[...]
    span = PAGES_PER_STEP * PAGE_SIZE
    s = jnp.concatenate([_dot(qbd, kpages[r][0].astype(BF16)) for r in range(PAGES_PER_STEP)], axis=1)
    s = s + per_token(bias_ref, span)
    m_old = m_ref[...]
    m_new = jnp.maximum(m_old, jnp.max(s, axis=-1, keepdims=True))
    alpha = jnp.exp(m_old - m_new)
    p = jnp.exp(s - m_new)
    l_ref[...] = alpha * l_ref[...] + jnp.sum(p, axis=-1, keepdims=True)
    pb = p.astype(BF16)
    pv = jnp.zeros((rows, B_WIDTH), F32)
    for r in range(PAGES_PER_STEP):
        pv = pv + _dot_nt(pb[:, r * PAGE_SIZE:(r + 1) * PAGE_SIZE], vpages[r][0].astype(BF16))
    acc_ref[...] = alpha * acc_ref[...] + pv
    m_ref[...] = m_new

    @pl.when(j == pl.num_programs(1) - 1)
    def _():
        qf = qbd.astype(F32)
        bn = per_token(biasn_ref, LANES)
        cols = []
        for tk in range(t_new):
            kr = kn_ref[0, tk:tk + 1, :].astype(BF16).astype(F32)
            cols.append(jnp.sum(qf * kr, axis=-1, keepdims=True) + bn[:, tk:tk + 1])
        m_old = m_ref[...]
        m_new = m_old
        for cval in cols:
            m_new = jnp.maximum(m_new, cval)
        alpha = jnp.exp(m_old - m_new)
        l = alpha * l_ref[...]
        acc = alpha * acc_ref[...]
        for tk in range(t_new):
            pk = jnp.exp(cols[tk] - m_new)
            l = l + pk
            acc = acc + pk.astype(BF16).astype(F32) * vn_ref[0, tk:tk + 1, :].astype(BF16).astype(F32)
        o = acc / l
        for i in range(t_new):
            slab = jnp.where(mask8, o[i * B_HEADS:(i + 1) * B_HEADS, :], 0.0)
            o_ref[0, i * B_HEADS:(i + 1) * B_HEADS, :] = jnp.broadcast_to(
                jnp.sum(slab, axis=0, keepdims=True), (B_HEADS, B_WIDTH))


def _attn_sample(page_table, q3, bias3, biasn3, k_new, v_new, cache_kt, cache_vt):
    db, t_new, _ = q3.shape
    n_pages = page_table.shape[1]
    steps = n_pages // PAGES_PER_STEP
    span = PAGES_PER_STEP * PAGE_SIZE
    rows = t_new * B_HEADS
    per_b = lambda w: pl.BlockSpec((1, t_new, w), lambda b, j, pt: (b, 0, 0))
    grid_spec = pltpu.PrefetchScalarGridSpec(
        num_scalar_prefetch=1, grid=(db, steps),
        in_specs=[per_b(B_WIDTH),
                  pl.BlockSpec((1, t_new, span), lambda b, j, pt: (b, 0, j)),
                  per_b(LANES), per_b(B_WIDTH), per_b(B_WIDTH)]
        + _page_specs((1, B_WIDTH, PAGE_SIZE), PAGES_PER_STEP)
        + _page_specs((1, B_WIDTH, PAGE_SIZE), PAGES_PER_STEP),
        out_specs=pl.BlockSpec((1, rows, B_WIDTH), lambda b, j, pt: (b, 0, 0)),
        scratch_shapes=[pltpu.VMEM((rows, 1), F32), pltpu.VMEM((rows, 1), F32),
                        pltpu.VMEM((rows, B_WIDTH), F32), pltpu.VMEM((rows, B_WIDTH), BF16)])
    return pl.pallas_call(
        functools.partial(_attn_sample_kernel, t_new), grid_spec=grid_spec,
        out_shape=jax.ShapeDtypeStruct((db, rows, B_WIDTH), F32),
        compiler_params=_cparams(("parallel", "arbitrary")), name="attn_sample",
    )(page_table, q3, bias3, biasn3, k_new, v_new,
      *([cache_kt] * PAGES_PER_STEP), *([cache_vt] * PAGES_PER_STEP))


def _merge_kernel(x_ref, ma_ref, yb_ref, sgb_ref, yc_ref, sgc_ref, wo_ref, g_ref, y_out):
    mb = (yb_ref[...].astype(F32) * sgb_ref[...].astype(F32)).astype(BF16)
    mc = (yc_ref[...].astype(F32) * sgc_ref[...].astype(F32)).astype(BF16)
    o = (_dot(ma_ref[...], wo_ref[0:A_WIDTH, :])
         + _dot(mb, wo_ref[A_WIDTH:A_WIDTH + B_WIDTH, :])
         + _dot(mc, wo_ref[A_WIDTH + B_WIDTH:, :]))
    z = x_ref[...] + o
    ms = jnp.mean(z * z, axis=-1, keepdims=True)
    y_out[...] = z * lax.rsqrt(ms + EPS) * g_ref[...]


def _merge(x, mixa, yb, sgb, yc, sgc, wo, g_final):
    n = x.shape[0]
    row = lambda w: pl.BlockSpec((TM, w), lambda i: (i, 0))
    return pl.pallas_call(
        _merge_kernel, grid=(n // TM,),
        in_specs=[row(D_MODEL), row(A_WIDTH), row(B_WIDTH), row(B_WIDTH), row(C_WIDTH), row(C_WIDTH),
                  pl.BlockSpec(wo.shape, lambda i: (0, 0)), pl.BlockSpec(g_final.shape, lambda i: (0, 0))],
        out_specs=row(D_MODEL),
        out_shape=jax.ShapeDtypeStruct((n, D_MODEL), F32),
        compiler_params=_cparams(("parallel",)), name="merge",
    )(x, mixa, yb, sgb, yc, sgc, wo, g_final)


def _rope_tables(pos):
    inv = ROPE_THETA ** (-jnp.arange(0, HEAD_DIM, 2, dtype=F32) / HEAD_DIM)
    ang = pos.astype(F32)[:, None] * inv[None, :]
    c = jnp.cos(ang)
    s = jnp.sin(ang)
    return jnp.concatenate([c, c, c, c], axis=1), jnp.concatenate([-s, s, -s, s], axis=1)


def _heads_last(t, heads):
    b, _, n = t.shape
    return t.reshape(b, heads, HEAD_DIM, n).transpose(0, 3, 1, 2)


def _channels_first(t):
    p, n, heads, d = t.shape
    return t.transpose(0, 2, 3, 1).reshape(p, heads * d, n)


def kernel(x_prompt, x_sample, mem_prompt, cache_k, cache_v, cache_idx_k, cache_mem_k, cache_mem_v,
           page_table, g_norm, w_in, w_spatial, b_spatial, g_v, w_mem_kv, g_mem, w_out, g_final):
    batch, seq, _ = x_prompt.shape
    db, t_new, _ = x_sample.shape
    n_pages = page_table.shape[1]
    past = n_pages * PAGE_SIZE

    ki_cols = w_in[:, 3328:3392]
    wi_cols = w_in[:, 3392:3400]
    w_r = jnp.concatenate(
        [w_in[:, :3328], w_in[:, 3400:3912], ki_cols, ki_cols, wi_cols,
         jnp.zeros((D_MODEL, LANES - IDX_HEADS), F32)], axis=1).astype(BF16)
    g_norm2 = g_norm.reshape(1, D_MODEL)
    g_v2 = g_v.reshape(1, A_WIDTH)
    bs_prompt = jnp.repeat(b_spatial.T, HEAD_DIM, axis=1)
    reps = CHUNK // t_new
    ws_sample = jnp.tile(w_spatial[:, :t_new, :t_new], (1, reps, reps))
    bs_sample = jnp.tile(jnp.repeat(b_spatial[:, :t_new].T, HEAD_DIM, axis=1), (reps, 1))
    wo_b = w_out.astype(BF16)
    g_final2 = g_final.reshape(1, D_MODEL)

    cos_p, sin_p = _rope_tables(jnp.arange(seq, dtype=jnp.int32))
    pos_s = past + jnp.arange(t_new, dtype=jnp.int32)
    cos_s, sin_s = _rope_tables(jnp.tile(pos_s, TM // t_new))

    xp = x_prompt.reshape(batch * seq, D_MODEL)
    (kt, vt, kit, kbf, vtb, ki2, qz, qiz, wt, sgb, mixa, qc, sgc) = _proj(
        xp, g_norm2, w_r, cos_p, sin_p, w_spatial, bs_prompt, g_v2, CHUNK, seq // TM, False)
    mkt, mvt = _memkv(mem_prompt.reshape(batch * N_MEM, D_MODEL), g_mem.reshape(1, D_MODEL),
                      w_mem_kv.astype(BF16))
    yc = _memattn(qc.reshape(batch * seq // TM, TM, C_WIDTH), mkt, mvt, seq // TM).reshape(batch * seq, C_WIDTH)
    yb = _dsa_prompt(qz, qiz, wt, ki2, kbf, vtb, batch, seq)
    y_prompt = _merge(xp, mixa, yb, sgb, yc, sgc, wo_b, g_final2).reshape(batch, seq, D_MODEL)

    ns = db * t_new
    xs = x_sample.reshape(ns, D_MODEL)
    (q_s, k_s, v_s, qi_s, ki_s, wi_s, sgb_s, mixa_s, qc_s, sgc_s, vn_s) = _proj(
        xs, g_norm2, w_r, cos_s, sin_s, ws_sample, bs_sample, g_v2, t_new, 1, True)
    sc, scn = _idx_sample(page_table, qi_s.reshape(db, t_new * IDX_HEADS, IDX_DIM),
                          wi_s.reshape(db, t_new * IDX_HEADS, 1), ki_s.reshape(db, t_new, IDX_DIM),
                          cache_idx_k.transpose(0, 2, 1))
    topk_s = min(TOPK_MAX, (past + t_new) // 4)
    bias, biasn = _thr_sample(sc.reshape(ns, past), scn.reshape(ns, LANES), topk_s)
    yb_pad = _attn_sample(page_table, q_s.reshape(db, t_new, B_WIDTH), bias.reshape(db, t_new, past),
                          biasn.reshape(db, t_new, LANES), k_s.reshape(db, t_new, B_WIDTH),
                          v_s.reshape(db, t_new, B_WIDTH), _channels_first(cache_k), _channels_first(cache_v))
    yb_s = yb_pad[:, ::B_HEADS, :].reshape(ns, B_WIDTH)
    yc_pad = _memattn_sample(qc_s.reshape(db, t_new, C_WIDTH),
                             _channels_first(cache_mem_k), _channels_first(cache_mem_v))
    yc_s = yc_pad[:, ::SUBLANES, :].reshape(ns, C_WIDTH)
    y_sample = _merge(xs, mixa_s, yb_s, sgb_s, yc_s, sgc_s, wo_b, g_final2).reshape(db, t_new, D_MODEL)

    return (y_prompt, y_sample,
            _heads_last(kt, B_HEADS), _heads_last(vt, B_HEADS), kit.transpose(0, 2, 1),
            _heads_last(mkt, C_HEADS), _heads_last(mvt, C_HEADS),
            k_s.reshape(db, t_new, B_HEADS, HEAD_DIM), v_s.reshape(db, t_new, B_HEADS, HEAD_DIM),
            ki_s.reshape(db, t_new, IDX_DIM), vn_s.reshape(db, t_new, A_WIDTH))
```

```python
import functools

import jax
import jax.numpy as jnp
from jax import lax
from jax.experimental import pallas as pl
from jax.experimental.pallas import tpu as pltpu

F32 = jnp.float32
BF16 = jnp.bfloat16

HEAD_DIM = 64
HALF = HEAD_DIM // 2
D_MODEL = 1024
A_GROUPS = 4
A_WIDTH = A_GROUPS * HEAD_DIM
CHUNK = 128
B_HEADS = 8
B_WIDTH = B_HEADS * HEAD_DIM
IDX_HEADS = 8
IDX_DIM = 64
TOPK_MAX = 256
N_MEM = 256
C_HEADS = 4
C_WIDTH = C_HEADS * HEAD_DIM
PAGE_SIZE = 128
ROPE_THETA = 10000.0
EPS = 1e-6
IDX_W_SCALE = (IDX_HEADS * IDX_DIM) ** -0.5
ATTN_SCALE = HEAD_DIM ** -0.5
LOG2E = 1.4426950408889634

LANES = 128
SUBLANES = 8
NEG = -1e30
FLT_MAX = 3.4028234663852886e38
INT_MIN = -2 ** 31

TM = 256
QB = 256
KC = 256
V_ROWS = HEAD_DIM + 16
IDX_PAGES_PER_STEP = 32
ATTN_PAGES_PER_STEP = 16
THR_ROWS = 64
VMEM_LIMIT = 52 * 1024 * 1024

OFF_U, OFF_VA, OFF_GA = 0, 256, 512
OFF_Q, OFF_K, OFF_V, OFF_GB, OFF_QI = 768, 1280, 1792, 2304, 2816
OFF_QC, OFF_GC, OFF_KI2, OFF_WI, W_TOTAL = 3328, 3584, 3840, 3968, 4096


def _cparams(sem):
    return pltpu.CompilerParams(dimension_semantics=sem, vmem_limit_bytes=VMEM_LIMIT)


def _iota(shape, dim):
    return lax.broadcasted_iota(jnp.int32, shape, dim)


def _silu(x):
    return x / (1.0 + jnp.exp(-x))


def _dot(a, b):
    return jnp.dot(a, b, preferred_element_type=F32)


def _dot_nt(a, b):
    return lax.dot_general(a, b, (((1,), (1,)), ((), ())), preferred_element_type=F32)


def _key_to_float(k):
    bits = k ^ ((k >> 31) & jnp.int32(0x7FFFFFFF))
    return lax.bitcast_convert_type(bits, F32)


def _kth_largest(count_ge, kth, shape):
    cnt0 = count_ge(jnp.zeros(shape, F32))
    ok0 = cnt0 >= kth
    key = jnp.where(ok0, jnp.int32(0), jnp.int32(INT_MIN))
    cnt_key = jnp.where(ok0, cnt0, jnp.float32(2 ** 24))

    def bit_body(it, carry):
        key, cnt_key = carry
        trial = key + lax.shift_left(jnp.int32(1), 30 - it)
        cnt = count_ge(_key_to_float(trial))
        ok = cnt >= kth
        return jnp.where(ok, trial, key), jnp.where(ok, cnt, cnt_key)

    key, cnt_key = lax.fori_loop(0, 31, bit_body, (key, cnt_key))
    return _key_to_float(key), cnt_key


def _rope(t, cos, sin):
    outs = []
    first = (_iota((t.shape[0], LANES), 1) % HEAD_DIM) < HALF
    for j in range(t.shape[1] // LANES):
        x = t[:, j * LANES:(j + 1) * LANES]
        partner = jnp.where(first, pltpu.roll(x, LANES - HALF, 1), pltpu.roll(x, HALF, 1))
        outs.append(x * cos + partner * sin)
    return outs[0] if len(outs) == 1 else jnp.concatenate(outs, axis=1)


def _proj_kernel(chunk, sample, x_ref, g_ref, w_ref, cos_ref, sin_ref, ws_ref, bs_ref, gv_ref, *outs):
    if sample:
        q_out, k_out, v_out, qi_out, ki_out, wi_out, sgb_out, mixa_out, qc_out, sgc_out, vn_out = outs
    else:
        (kt_out, vt_out, kit_out, kbf_out, vtb_out, ki2_out, qz_out, qiz_out, wt_out,
         sgb_out, mixa_out, qc_out, sgc_out) = outs
    tm = x_ref.shape[0]
    x = x_ref[...]
    ms = jnp.mean(x * x, axis=-1, keepdims=True)
    h = (x * lax.rsqrt(ms + EPS) * g_ref[...]).astype(BF16)
    cos = cos_ref[...]
    sin = sin_ref[...]

    def seg(a, b):
        return _dot(h, w_ref[:, a:b])

    half_id = (_iota((tm, LANES), 1) // HEAD_DIM)

    def head_split(t, out):
        for hd in range(B_HEADS):
            pair = t[:, (hd // 2) * LANES:(hd // 2 + 1) * LANES]
            out[hd] = jnp.where(half_id == hd % 2, pair, 0.0).astype(BF16)

    q = _rope(seg(OFF_Q, OFF_K), cos, sin)
    k = _rope(seg(OFF_K, OFF_V), cos, sin)
    v = seg(OFF_V, OFF_GB)
    qi = _rope(seg(OFF_QI, OFF_QC), cos, sin)
    ki2 = _rope(seg(OFF_KI2, OFF_WI), cos, sin)
    wi = seg(OFF_WI, W_TOTAL) * IDX_W_SCALE
    qc = seg(OFF_QC, OFF_GC) * ATTN_SCALE
    sgb_out[...] = _silu(seg(OFF_GB, OFF_QI)).astype(BF16)
    sgc_out[...] = _silu(seg(OFF_GC, OFF_KI2)).astype(BF16)
    qc_out[...] = qc.astype(qc_out.dtype)
    if sample:
        q_out[...] = q
        k_out[...] = k
        v_out[...] = v
        qi_out[...] = qi
        ki_out[...] = ki2[:, :IDX_DIM]
        wi_out[...] = wi[:, :IDX_HEADS]
    else:
        head_split(q * (ATTN_SCALE * LOG2E), qz_out)
        head_split(qi, qiz_out)
        kt_out[0] = k.T
        kbf_out[...] = k.astype(BF16)
        vt = v.T
        vt_out[0] = vt
        vtb = vt.astype(BF16)
        ones = jnp.ones((V_ROWS - HEAD_DIM, tm), BF16)
        for hd in range(B_HEADS):
            vtb_out[0, hd * V_ROWS:hd * V_ROWS + HEAD_DIM, :] = vtb[hd * HEAD_DIM:(hd + 1) * HEAD_DIM, :]
            vtb_out[0, hd * V_ROWS + HEAD_DIM:(hd + 1) * V_ROWS, :] = ones
        kit_out[0] = ki2.T[:IDX_DIM, :]
        ki2_out[...] = ki2.astype(BF16)
        wt_out[...] = wi.T[:IDX_HEADS, :]

    u = seg(OFF_U, OFF_VA)
    va = seg(OFF_VA, OFF_GA)
    ga = seg(OFF_GA, OFF_Q)
    grp_r = _iota((A_WIDTH, A_WIDTH), 0) // HEAD_DIM
    grp_c = _iota((A_WIDTH, A_WIDTH), 1) // HEAD_DIM
    gmat = jnp.where(grp_r == grp_c, 1.0 / HEAD_DIM, 0.0).astype(BF16)

    def group_mean(t):
        hi = t.astype(BF16)
        lo = (t - hi.astype(F32)).astype(BF16)
        return _dot(hi, gmat) + _dot(lo, gmat)

    mu = group_mean(va)
    d = va - mu
    var = group_mean(d * d)
    vn = d * lax.rsqrt(var + EPS) * gv_ref[...]
    if sample:
        vn_out[...] = vn
    vnb = vn.astype(BF16)
    row = _iota((CHUNK, CHUNK), 0)
    col = _iota((CHUNK, CHUNK), 1)
    causal = (row >= col) & ((row // chunk) == (col // chunk))
    lane_grp = _iota((CHUNK, A_WIDTH), 1) // HEAD_DIM
    wms = [jnp.where(causal, ws_ref[g], 0.0).astype(BF16) for g in range(A_GROUPS)]
    mixes = []
    for j in range(tm // CHUNK):
        vc = vnb[j * CHUNK:(j + 1) * CHUNK, :]
        m = bs_ref[...]
        for g in range(A_GROUPS):
            m = m + jnp.where(lane_grp == g, _dot(wms[g], vc), 0.0)
        mixes.append(m)
    mix = jnp.concatenate(mixes, axis=0)
    mixa_out[...] = (_silu(ga) * u * mix).astype(BF16)


def _proj(x, g_norm, w_r, cos_t, sin_t, ws, bs, g_v, chunk, seq_blocks, sample):
    n = x.shape[0]
    nt = n // TM
    nb = nt // seq_blocks
    row = lambda w: pl.BlockSpec((TM, w), lambda i: (i, 0))
    const2 = lambda a: pl.BlockSpec(a.shape, lambda i: (0, 0))
    sds = jax.ShapeDtypeStruct
    in_specs = [
        row(D_MODEL), const2(g_norm), const2(w_r),
        pl.BlockSpec((TM, LANES), lambda i: (i % seq_blocks, 0)),
        pl.BlockSpec((TM, LANES), lambda i: (i % seq_blocks, 0)),
        pl.BlockSpec(ws.shape, lambda i: (0, 0, 0)), const2(bs), const2(g_v),
    ]
    if sample:
        outs = [
            (sds((n, B_WIDTH), F32), row(B_WIDTH)),
            (sds((n, B_WIDTH), F32), row(B_WIDTH)),
            (sds((n, B_WIDTH), F32), row(B_WIDTH)),
            (sds((n, IDX_HEADS * IDX_DIM), F32), row(IDX_HEADS * IDX_DIM)),
            (sds((n, IDX_DIM), F32), row(IDX_DIM)),
            (sds((n, IDX_HEADS), F32), row(IDX_HEADS)),
            (sds((n, B_WIDTH), BF16), row(B_WIDTH)),
            (sds((n, A_WIDTH), BF16), row(A_WIDTH)),
            (sds((n, C_WIDTH), F32), row(C_WIDTH)),
            (sds((n, C_WIDTH), BF16), row(C_WIDTH)),
            (sds((n, A_WIDTH), F32), row(A_WIDTH)),
        ]
    else:
        seq = seq_blocks * TM
        tspec = lambda w: pl.BlockSpec((1, w, TM), lambda i: (i // seq_blocks, 0, i % seq_blocks))
        hspec = pl.BlockSpec((B_HEADS, TM, LANES), lambda i: (0, i, 0))
        outs = [
            (sds((nb, B_WIDTH, seq), F32), tspec(B_WIDTH)),
            (sds((nb, B_WIDTH, seq), F32), tspec(B_WIDTH)),
            (sds((nb, IDX_DIM, seq), F32), tspec(IDX_DIM)),
            (sds((n, B_WIDTH), BF16), row(B_WIDTH)),
            (sds((nt, B_HEADS * V_ROWS, TM), BF16),
             pl.BlockSpec((1, B_HEADS * V_ROWS, TM), lambda i: (i, 0, 0))),
            (sds((n, LANES), BF16), row(LANES)),
            (sds((B_HEADS, n, LANES), BF16), hspec),
            (sds((IDX_HEADS, n, LANES), BF16), hspec),
            (sds((IDX_HEADS, n), F32), pl.BlockSpec((IDX_HEADS, TM), lambda i: (0, i))),
            (sds((n, B_WIDTH), BF16), row(B_WIDTH)),
            (sds((n, A_WIDTH), BF16), row(A_WIDTH)),
            (sds((n, C_WIDTH), BF16), row(C_WIDTH)),
            (sds((n, C_WIDTH), BF16), row(C_WIDTH)),
        ]
    return pl.pallas_call(
        functools.partial(_proj_kernel, chunk, sample),
        grid=(nt,), in_specs=in_specs, out_specs=[o[1] for o in outs], out_shape=[o[0] for o in outs],
        compiler_params=_cparams(("parallel",)), name="proj",
    )(x, g_norm, w_r, cos_t, sin_t, ws, bs, g_v)


def _memkv_kernel(m_ref, g_ref, w_ref, mkt_out, mvt_out):
    x = m_ref[...]
    ms = jnp.mean(x * x, axis=-1, keepdims=True)
    h = (x * lax.rsqrt(ms + EPS) * g_ref[...]).astype(BF16)
    kv = _dot(h, w_ref[...])
    mkt_out[0] = kv[:, :C_WIDTH].T
    mvt_out[0] = kv[:, C_WIDTH:].T


def _memkv(mem, g_mem, w_kv):
    n = mem.shape[0]
    nb = n // N_MEM
    return pl.pallas_call(
        _memkv_kernel, grid=(nb,),
        in_specs=[pl.BlockSpec((N_MEM, D_MODEL), lambda i: (i, 0)),
                  pl.BlockSpec(g_mem.shape, lambda i: (0, 0)),
                  pl.BlockSpec(w_kv.shape, lambda i: (0, 0))],
        out_specs=[pl.BlockSpec((1, C_WIDTH, N_MEM), lambda i: (i, 0, 0))] * 2,
        out_shape=[jax.ShapeDtypeStruct((nb, C_WIDTH, N_MEM), F32)] * 2,
        compiler_params=_cparams(("parallel",)), name="memkv",
    )(mem, g_mem, w_kv)


def _memattn_kernel(q_ref, mkt_ref, mvt_ref, o_ref):
    q = q_ref[0]
    mkt = mkt_ref[0].astype(BF16)
    mvt = mvt_ref[0].astype(BF16)
    t = q.shape[0]
    lane_head = _iota((t, C_WIDTH), 1) // HEAD_DIM
    out = jnp.zeros((t, C_WIDTH), F32)
    for hd in range(C_HEADS):
        qh = jnp.where(lane_head == hd, q, 0.0).astype(BF16)
        s = _dot(qh, mkt)
        p = jnp.exp(s - jnp.max(s, axis=-1, keepdims=True))
        o = _dot_nt(p.astype(BF16), mvt) / jnp.sum(p, axis=-1, keepdims=True)
        out = out + jnp.where(lane_head == hd, o, 0.0)
    o_ref[0] = out.astype(o_ref.dtype)


def _memattn(q3, mkt3, mvt3, tiles_per_mem):
    g, t, _ = q3.shape
    mem_spec = pl.BlockSpec((1, C_WIDTH, N_MEM), lambda i: (i // tiles_per_mem, 0, 0))
    return pl.pallas_call(
        _memattn_kernel, grid=(g,),
        in_specs=[pl.BlockSpec((1, t, C_WIDTH), lambda i: (i, 0, 0)), mem_spec, mem_spec],
        out_specs=pl.BlockSpec((1, t, C_WIDTH), lambda i: (i, 0, 0)),
        out_shape=jax.ShapeDtypeStruct((g, t, C_WIDTH), BF16),
        compiler_params=_cparams(("parallel",)), name="memattn",
    )(q3, mkt3, mvt3)


def _memattn_sample_kernel(q_ref, mkt_ref, mvt_ref, o_ref):
    q4 = q_ref[0]
    mkt = mkt_ref[0].astype(BF16)
    mvt = mvt_ref[0].astype(BF16)
    t = q4.shape[0]
    mask8 = (_iota((SUBLANES, C_WIDTH), 1) // HEAD_DIM) == _iota((SUBLANES, C_WIDTH), 0)
    qbd = jnp.concatenate(
        [jnp.where(mask8, jnp.broadcast_to(q4[i:i + 1, :], (SUBLANES, C_WIDTH)), 0.0) for i in range(t)], axis=0)
    s = _dot(qbd.astype(BF16), mkt)
    p = jnp.exp(s - jnp.max(s, axis=-1, keepdims=True))
    o = _dot_nt(p.astype(BF16), mvt) / jnp.sum(p, axis=-1, keepdims=True)
    for i in range(t):
        slab = jnp.where(mask8, o[i * SUBLANES:(i + 1) * SUBLANES, :], 0.0)
        o_ref[0, i * SUBLANES:(i + 1) * SUBLANES, :] = jnp.broadcast_to(
            jnp.sum(slab, axis=0, keepdims=True), (SUBLANES, C_WIDTH))


def _memattn_sample(q3, mkt3, mvt3):
    g, t, _ = q3.shape
    mem_spec = pl.BlockSpec((1, C_WIDTH, N_MEM), lambda i: (i, 0, 0))
    return pl.pallas_call(
        _memattn_sample_kernel, grid=(g,),
        in_specs=[pl.BlockSpec((1, t, C_WIDTH), lambda i: (i, 0, 0)), mem_spec, mem_spec],
        out_specs=pl.BlockSpec((1, t * SUBLANES, C_WIDTH), lambda i: (i, 0, 0)),
        out_shape=jax.ShapeDtypeStruct((g, t * SUBLANES, C_WIDTH), F32),
        compiler_params=_cparams(("parallel",)), name="memattn_sample",
    )(q3, mkt3, mvt3)


def _dsa_prompt_kernel(topk, qz_ref, qiz_ref, wt_ref, ki2_ref, kbf_ref, vt_ref, o_ref,
                       sc_ref, acc_ref, m_ref, s_ref):
    i = pl.program_id(1)
    nk = i + 1
    npair = (nk + 1) // 2
    kth = jnp.float32(topk)

    def sum_rows(x):
        return jnp.sum(x.reshape(KC // SUBLANES, SUBLANES, QB), axis=0)

    def max_rows(x):
        return jnp.max(x.reshape(KC // SUBLANES, SUBLANES, QB), axis=0)

    def score_chunk(c, _):
        keys = ki2_ref[0, c]
        acc = jnp.zeros((KC, QB), F32)
        for hd in range(IDX_HEADS):
            y = _dot_nt(keys, qiz_ref[hd])
            acc = acc + jnp.maximum(y, 0.0) * wt_ref[hd:hd + 1, :]
        kpos = c * KC + _iota((KC, QB), 0)
        qpos = i * QB + _iota((KC, QB), 1)
        sc_ref[c] = jnp.where(kpos <= qpos, acc, -jnp.inf)
        return 0

    lax.fori_loop(0, nk, score_chunk, 0)

    @pl.when(nk % 2 == 1)
    def _():
        sc_ref[jnp.minimum(nk, sc_ref.shape[0] - 1)] = jnp.full((KC, QB), -jnp.inf, F32)

    def count(pred):
        def body(j, cnt):
            return (cnt + sum_rows(pred(sc_ref[2 * j]).astype(F32))
                    + sum_rows(pred(sc_ref[2 * j + 1]).astype(F32)))
        return jnp.sum(lax.fori_loop(0, npair, body, jnp.zeros((SUBLANES, QB), F32)), axis=0, keepdims=True)

    def write_ge(thr):
        def body(j, _):
            sc_ref[2 * j] = jnp.where(sc_ref[2 * j] >= thr, 0.0, NEG)
            sc_ref[2 * j + 1] = jnp.where(sc_ref[2 * j + 1] >= thr, 0.0, NEG)
            return 0
        lax.fori_loop(0, npair, body, 0)

    @pl.when(nk * KC <= topk)
    def _():
        write_ge(jnp.full((1, QB), -FLT_MAX, F32))

    @pl.when(nk * KC > topk)
    def _():
        thr, cnt_thr = _kth_largest(lambda t: count(lambda x: x >= t), kth, (1, QB))
        few = (i * QB + 1 + _iota((1, QB), 1)).astype(F32) <= kth
        thr = jnp.where(few, -FLT_MAX, thr)
        cnt_thr = jnp.where(few, kth, cnt_thr)
        has_tie = jnp.max(cnt_thr) > kth

        @pl.when(jnp.logical_not(has_tie))
        def _():
            write_ge(thr)

        @pl.when(has_tie)
        def _():
            need = kth - count(lambda x: x > thr)
            ltri = (_iota((KC, KC), 1) < _iota((KC, KC), 0)).astype(BF16)

            def body(c, carry):
                x = sc_ref[c]
                eq = x == thr
                before = _dot(ltri, eq.astype(BF16)) + carry
                sel = (x > thr) | (eq & (before < need))
                sc_ref[c] = jnp.where(sel, 0.0, NEG)
                return carry + jnp.sum(sum_rows(eq.astype(F32)), axis=0, keepdims=True)

            lax.fori_loop(0, nk, body, jnp.zeros((1, QB), F32))

    m_ref[...] = jnp.full((B_HEADS, QB), NEG, F32)
    acc_ref[...] = jnp.zeros((B_HEADS * V_ROWS, QB), F32)

    def biased_scores(c, hd):
        pair = hd // 2
        kc = kbf_ref[0, c, :, pair * LANES:(pair + 1) * LANES]
        return _dot_nt(kc, qz_ref[hd]) + sc_ref[c]

    for hd in range(B_HEADS):
        s_ref[hd] = biased_scores(0, hd)

    def attend(c, _):
        nxt = jnp.minimum(c + 1, nk - 1)
        for hd in range(B_HEADS):
            rows = slice(hd * V_ROWS, (hd + 1) * V_ROWS)
            s_next = biased_scores(nxt, hd)
            s = s_ref[hd]
            m_old = m_ref[hd:hd + 1, :]
            m_new = jnp.maximum(m_old, jnp.max(max_rows(s), axis=0, keepdims=True))
            alpha = jnp.exp2(m_old - m_new)
            p = jnp.exp2(s - m_new)
            acc_ref[rows, :] = alpha * acc_ref[rows, :] + _dot(vt_ref[0, c, rows, :], p.astype(BF16))
            m_ref[hd:hd + 1, :] = m_new
            s_ref[hd] = s_next
        return 0

    lax.fori_loop(0, nk, attend, 0)
    outs = []
    for hd in range(B_HEADS):
        base = hd * V_ROWS
        outs.append(acc_ref[base:base + HEAD_DIM, :] / acc_ref[base + HEAD_DIM:base + HEAD_DIM + 1, :])
    o_ref[...] = jnp.concatenate(outs, axis=0).T.astype(o_ref.dtype)


def _dsa_prompt(qz, qiz, wt, ki2, kbf, vtb, batch, seq):
    nq = seq // QB
    nc = seq // KC
    topk = min(TOPK_MAX, seq // 4)
    n = batch * seq
    qspec = pl.BlockSpec((B_HEADS, QB, LANES), lambda b, i: (0, b * nq + i, 0))
    return pl.pallas_call(
        functools.partial(_dsa_prompt_kernel, topk),
        grid=(batch, nq),
        in_specs=[qspec, qspec,
                  pl.BlockSpec((IDX_HEADS, QB), lambda b, i: (0, b * nq + i)),
                  pl.BlockSpec((1, nc, KC, LANES), lambda b, i: (b, 0, 0, 0)),
                  pl.BlockSpec((1, nc, KC, B_WIDTH), lambda b, i: (b, 0, 0, 0)),
                  pl.BlockSpec((1, nc, B_HEADS * V_ROWS, KC), lambda b, i: (b, 0, 0, 0))],
        out_specs=pl.BlockSpec((QB, B_WIDTH), lambda b, i: (b * nq + i, 0)),
        out_shape=jax.ShapeDtypeStruct((n, B_WIDTH), BF16),
        scratch_shapes=[pltpu.VMEM((nc, KC, QB), F32), pltpu.VMEM((B_HEADS * V_ROWS, QB), F32),
                        pltpu.VMEM((B_HEADS, QB), F32), pltpu.VMEM((B_HEADS, KC, QB), F32)],
        compiler_params=_cparams(("parallel", "arbitrary")), name="dsa_prompt",
    )(qz, qiz, wt, ki2.reshape(batch, nc, KC, LANES), kbf.reshape(batch, nc, KC, B_WIDTH),
      vtb.reshape(batch, nc, B_HEADS * V_ROWS, KC))


def _page_specs(block, n_pages_per_step):
    def make(r):
        return pl.BlockSpec(block, lambda b, j, pt: (pt[b, j * n_pages_per_step + r], 0, 0))
    return [make(r) for r in range(n_pages_per_step)]


def _idx_sample_kernel(t_new, pps, pt_ref, qi_ref, w_ref, kin_ref, *rest):
    pages = rest[:pps]
    sc_out, scn_out = rest[pps:]
    j = pl.program_id(1)
    qi = qi_ref[0]
    w = w_ref[0]
    qib = qi.astype(BF16)

    def token_scores(y):
        z = jnp.maximum(y, 0.0) * w
        return jnp.sum(z.reshape(t_new, IDX_HEADS, y.shape[1]), axis=1)

    for r in range(pps):
        y = _dot(qib, pages[r][0].astype(BF16))
        sc_out[0, :, r * PAGE_SIZE:(r + 1) * PAGE_SIZE] = token_scores(y)

    @pl.when(j == 0)
    def _():
        qf = qib.astype(F32)
        lane = _iota((t_new, LANES), 1)
        tok = _iota((t_new, LANES), 0)
        new = jnp.full((t_new, LANES), -jnp.inf, F32)
        for tk in range(t_new):
            kr = kin_ref[0, tk:tk + 1, :].astype(BF16).astype(F32)
            col = token_scores(jnp.sum(qf * kr, axis=-1, keepdims=True))
            new = jnp.where((lane == tk) & (tok >= tk), col, new)
        scn_out[0] = new


def _idx_sample(page_table, qi32, w32, ki_new, cache_idx_kt):
    db, rows, _ = qi32.shape
    t_new = rows // IDX_HEADS
    n_pages = page_table.shape[1]
    pps = min(IDX_PAGES_PER_STEP, n_pages)
    steps = n_pages // pps
    span = pps * PAGE_SIZE
    grid_spec = pltpu.PrefetchScalarGridSpec(
        num_scalar_prefetch=1, grid=(db, steps),
        in_specs=[pl.BlockSpec((1, rows, IDX_DIM), lambda b, j, pt: (b, 0, 0)),
                  pl.BlockSpec((1, rows, 1), lambda b, j, pt: (b, 0, 0)),
                  pl.BlockSpec((1, t_new, IDX_DIM), lambda b, j, pt: (b, 0, 0))]
        + _page_specs((1, IDX_DIM, PAGE_SIZE), pps),
        out_specs=[pl.BlockSpec((1, t_new, span), lambda b, j, pt: (b, 0, j)),
                   pl.BlockSpec((1, t_new, LANES), lambda b, j, pt: (b, 0, 0))])
    return pl.pallas_call(
        functools.partial(_idx_sample_kernel, t_new, pps), grid_spec=grid_spec,
        out_shape=[jax.ShapeDtypeStruct((db, t_new, n_pages * PAGE_SIZE), F32),
                   jax.ShapeDtypeStruct((db, t_new, LANES), F32)],
        compiler_params=_cparams(("parallel", "arbitrary")), name="idx_sample",
    )(page_table, qi32, w32, ki_new, *([cache_idx_kt] * pps))


def _thr_sample_kernel(topk, sc_ref, scn_ref, b_out, bn_out):
    kth = jnp.float32(topk)
    n_chunks = sc_ref.shape[1] // LANES

    def count(pred):
        return (jnp.sum(pred(sc_ref[...]).astype(F32), axis=-1, keepdims=True)
                + jnp.sum(pred(scn_ref[...]).astype(F32), axis=-1, keepdims=True))

    thr, cnt_thr = _kth_largest(lambda t: count(lambda x: x >= t), kth, (THR_ROWS, 1))
    has_tie = jnp.max(cnt_thr) > kth

    @pl.when(jnp.logical_not(has_tie))
    def _():
        b_out[...] = jnp.where(sc_ref[...] >= thr, 0.0, NEG)
        bn_out[...] = jnp.where(scn_ref[...] >= thr, 0.0, NEG)

    @pl.when(has_tie)
    def _():
        need = kth - count(lambda x: x > thr)
        utri = (_iota((LANES, LANES), 0) < _iota((LANES, LANES), 1)).astype(BF16)
        carry = jnp.zeros((THR_ROWS, 1), F32)
        for c in range(n_chunks + 1):
            x = sc_ref[:, c * LANES:(c + 1) * LANES] if c < n_chunks else scn_ref[...]
            eq = x == thr
            before = _dot(eq.astype(BF16), utri) + carry
            sel = (x > thr) | (eq & (before < need))
            bias = jnp.where(sel, 0.0, NEG)
            if c < n_chunks:
                b_out[:, c * LANES:(c + 1) * LANES] = bias
            else:
                bn_out[...] = bias
            carry = carry + jnp.sum(eq.astype(F32), axis=-1, keepdims=True)


def _thr_sample(sc, scn, topk):
    n, width = sc.shape
    return pl.pallas_call(
        functools.partial(_thr_sample_kernel, topk), grid=(n // THR_ROWS,),
        in_specs=[pl.BlockSpec((THR_ROWS, width), lambda i: (i, 0)),
                  pl.BlockSpec((THR_ROWS, LANES), lambda i: (i, 0))],
        out_specs=[pl.BlockSpec((THR_ROWS, width), lambda i: (i, 0)),
                   pl.BlockSpec((THR_ROWS, LANES), lambda i: (i, 0))],
        out_shape=[jax.ShapeDtypeStruct((n, width), F32), jax.ShapeDtypeStruct((n, LANES), F32)],
        compiler_params=_cparams(("parallel",)), name="thr_sample",
    )(sc, scn)


def _attn_sample_kernel(t_new, pps, pt_ref, q_ref, bias_ref, biasn_ref, kn_ref, vn_ref, *rest):
    kpages = rest[:pps]
    vpages = rest[pps:2 * pps]
    o_ref, m_ref, l_ref, acc_ref, qbd_ref = rest[2 * pps:]
    j = pl.program_id(1)
    rows = t_new * B_HEADS
    mask8 = (_iota((SUBLANES, B_WIDTH), 1) // HEAD_DIM) == _iota((SUBLANES, B_WIDTH), 0)

    def per_token(ref, width):
        return jnp.concatenate(
            [jnp.broadcast_to(ref[0, i:i + 1, :], (B_HEADS, width)) for i in range(t_new)], axis=0)

    @pl.when(j == 0)
    def _():
        m_ref[...] = jnp.full((rows, 1), NEG, F32)
        l_ref[...] = jnp.zeros((rows, 1), F32)
        acc_ref[...] = jnp.zeros((rows, B_WIDTH), F32)
        qbd_ref[...] = jnp.concatenate(
            [jnp.where(mask8, jnp.broadcast_to(q_ref[0, i:i + 1, :] * ATTN_SCALE, (B_HEADS, B_WIDTH)), 0.0)
             for i in range(t_new)], axis=0).astype(BF16)

    qbd = qbd_ref[...]
    span = pps * PAGE_SIZE
    s = jnp.concatenate([_dot(qbd, kpages[r][0].astype(BF16)) for r in range(pps)], axis=1)
    s = s + per_token(bias_ref, span)
    m_old = m_ref[...]
    m_new = jnp.maximum(m_old, jnp.max(s, axis=-1, keepdims=True))
    alpha = jnp.exp(m_old - m_new)
    p = jnp.exp(s - m_new)
    l_ref[...] = alpha * l_ref[...] + jnp.sum(p, axis=-1, keepdims=True)
    pb = p.astype(BF16)
    pv = jnp.zeros((rows, B_WIDTH), F32)
    for r in range(pps):
        pv = pv + _dot_nt(pb[:, r * PAGE_SIZE:(r + 1) * PAGE_SIZE], vpages[r][0].astype(BF16))
    acc_ref[...] = alpha * acc_ref[...] + pv
    m_ref[...] = m_new

    @pl.when(j == pl.num_programs(1) - 1)
    def _():
        qf = qbd.astype(F32)
        bn = per_token(biasn_ref, LANES)
        cols = []
        for tk in range(t_new):
            kr = kn_ref[0, tk:tk + 1, :].astype(BF16).astype(F32)
            cols.append(jnp.sum(qf * kr, axis=-1, keepdims=True) + bn[:, tk:tk + 1])
        m_old = m_ref[...]
        m_new = m_old
        for cval in cols:
            m_new = jnp.maximum(m_new, cval)
        alpha = jnp.exp(m_old - m_new)
        l = alpha * l_ref[...]
        acc = alpha * acc_ref[...]
        for tk in range(t_new):
            pk = jnp.exp(cols[tk] - m_new)
            l = l + pk
            acc = acc + pk.astype(BF16).astype(F32) * vn_ref[0, tk:tk + 1, :].astype(BF16).astype(F32)
        o = acc / l
        for i in range(t_new):
            slab = jnp.where(mask8, o[i * B_HEADS:(i + 1) * B_HEADS, :], 0.0)
            o_ref[0, i * B_HEADS:(i + 1) * B_HEADS, :] = jnp.broadcast_to(
                jnp.sum(slab, axis=0, keepdims=True), (B_HEADS, B_WIDTH))


def _attn_sample(page_table, q3, bias3, biasn3, k_new, v_new, cache_kt, cache_vt):
    db, t_new, _ = q3.shape
    n_pages = page_table.shape[1]
    pps = min(ATTN_PAGES_PER_STEP, n_pages)
    steps = n_pages // pps
    span = pps * PAGE_SIZE
    rows = t_new * B_HEADS
    per_b = lambda w: pl.BlockSpec((1, t_new, w), lambda b, j, pt: (b, 0, 0))
    grid_spec = pltpu.PrefetchScalarGridSpec(
        num_scalar_prefetch=1, grid=(db, steps),
        in_specs=[per_b(B_WIDTH),
                  pl.BlockSpec((1, t_new, span), lambda b, j, pt: (b, 0, j)),
                  per_b(LANES), per_b(B_WIDTH), per_b(B_WIDTH)]
        + _page_specs((1, B_WIDTH, PAGE_SIZE), pps)
        + _page_specs((1, B_WIDTH, PAGE_SIZE), pps),
        out_specs=pl.BlockSpec((1, rows, B_WIDTH), lambda b, j, pt: (b, 0, 0)),
        scratch_shapes=[pltpu.VMEM((rows, 1), F32), pltpu.VMEM((rows, 1), F32),
                        pltpu.VMEM((rows, B_WIDTH), F32), pltpu.VMEM((rows, B_WIDTH), BF16)])
    return pl.pallas_call(
        functools.partial(_attn_sample_kernel, t_new, pps), grid_spec=grid_spec,
        out_shape=jax.ShapeDtypeStruct((db, rows, B_WIDTH), F32),
        compiler_params=_cparams(("parallel", "arbitrary")), name="attn_sample",
    )(page_table, q3, bias3, biasn3, k_new, v_new, *([cache_kt] * pps), *([cache_vt] * pps))


def _merge_kernel(x_ref, ma_ref, yb_ref, sgb_ref, yc_ref, sgc_ref, wo_ref, g_ref, y_out):
    mb = (yb_ref[...].astype(F32) * sgb_ref[...].astype(F32)).astype(BF16)
    mc = (yc_ref[...].astype(F32) * sgc_ref[...].astype(F32)).astype(BF16)
    o = (_dot(ma_ref[...], wo_ref[0:A_WIDTH, :])
         + _dot(mb, wo_ref[A_WIDTH:A_WIDTH + B_WIDTH, :])
         + _dot(mc, wo_ref[A_WIDTH + B_WIDTH:, :]))
    z = x_ref[...] + o
    ms = jnp.mean(z * z, axis=-1, keepdims=True)
    y_out[...] = z * lax.rsqrt(ms + EPS) * g_ref[...]


def _merge(x, mixa, yb, sgb, yc, sgc, wo, g_final):
    n = x.shape[0]
    row = lambda w: pl.BlockSpec((TM, w), lambda i: (i, 0))
    return pl.pallas_call(
        _merge_kernel, grid=(n // TM,),
        in_specs=[row(D_MODEL), row(A_WIDTH), row(B_WIDTH), row(B_WIDTH), row(C_WIDTH), row(C_WIDTH),
                  pl.BlockSpec(wo.shape, lambda i: (0, 0)), pl.BlockSpec(g_final.shape, lambda i: (0, 0))],
        out_specs=row(D_MODEL),
        out_shape=jax.ShapeDtypeStruct((n, D_MODEL), F32),
        compiler_params=_cparams(("parallel",)), name="merge",
    )(x, mixa, yb, sgb, yc, sgc, wo, g_final)


def _rope_tables(pos):
    inv = ROPE_THETA ** (-jnp.arange(0, HEAD_DIM, 2, dtype=F32) / HEAD_DIM)
    ang = pos.astype(F32)[:, None] * inv[None, :]
    c = jnp.cos(ang)
    s = jnp.sin(ang)
    return jnp.concatenate([c, c, c, c], axis=1), jnp.concatenate([-s, s, -s, s], axis=1)


def _heads_last(t, heads):
    b, _, n = t.shape
    return t.reshape(b, heads, HEAD_DIM, n).transpose(0, 3, 1, 2)


def _channels_first(t):
    p, n, heads, d = t.shape
    return t.transpose(0, 2, 3, 1).reshape(p, heads * d, n)


def kernel(x_prompt, x_sample, mem_prompt, cache_k, cache_v, cache_idx_k, cache_mem_k, cache_mem_v,
           page_table, g_norm, w_in, w_spatial, b_spatial, g_v, w_mem_kv, g_mem, w_out, g_final):
    batch, seq, _ = x_prompt.shape
    db, t_new, _ = x_sample.shape
    n_pages = page_table.shape[1]
    past = n_pages * PAGE_SIZE

    ki_cols = w_in[:, 3328:3392]
    wi_cols = w_in[:, 3392:3400]
    w_r = jnp.concatenate(
        [w_in[:, :3328], w_in[:, 3400:3912], ki_cols, ki_cols, wi_cols,
         jnp.zeros((D_MODEL, LANES - IDX_HEADS), F32)], axis=1).astype(BF16)
    g_norm2 = g_norm.reshape(1, D_MODEL)
    g_v2 = g_v.reshape(1, A_WIDTH)
    bs_prompt = jnp.repeat(b_spatial.T, HEAD_DIM, axis=1)
    reps = CHUNK // t_new
    ws_sample = jnp.tile(w_spatial[:, :t_new, :t_new], (1, reps, reps))
    bs_sample = jnp.tile(jnp.repeat(b_spatial[:, :t_new].T, HEAD_DIM, axis=1), (reps, 1))
    wo_b = w_out.astype(BF16)
    g_final2 = g_final.reshape(1, D_MODEL)

    cos_p, sin_p = _rope_tables(jnp.arange(seq, dtype=jnp.int32))
    pos_s = past + jnp.arange(t_new, dtype=jnp.int32)
    cos_s, sin_s = _rope_tables(jnp.tile(pos_s, TM // t_new))

    xp = x_prompt.reshape(batch * seq, D_MODEL)
    (kt, vt, kit, kbf, vtb, ki2, qz, qiz, wt, sgb, mixa, qc, sgc) = _proj(
        xp, g_norm2, w_r, cos_p, sin_p, w_spatial, bs_prompt, g_v2, CHUNK, seq // TM, False)
    mkt, mvt = _memkv(mem_prompt.reshape(batch * N_MEM, D_MODEL), g_mem.reshape(1, D_MODEL),
                      w_mem_kv.astype(BF16))
    yc = _memattn(qc.reshape(batch * seq // TM, TM, C_WIDTH), mkt, mvt, seq // TM).reshape(batch * seq, C_WIDTH)
    yb = _dsa_prompt(qz, qiz, wt, ki2, kbf, vtb, batch, seq)
    y_prompt = _merge(xp, mixa, yb, sgb, yc, sgc, wo_b, g_final2).reshape(batch, seq, D_MODEL)

    ns = db * t_new
    xs = x_sample.reshape(ns, D_MODEL)
    (q_s, k_s, v_s, qi_s, ki_s, wi_s, sgb_s, mixa_s, qc_s, sgc_s, vn_s) = _proj(
        xs, g_norm2, w_r, cos_s, sin_s, ws_sample, bs_sample, g_v2, t_new, 1, True)
    sc, scn = _idx_sample(page_table, qi_s.reshape(db, t_new * IDX_HEADS, IDX_DIM),
                          wi_s.reshape(db, t_new * IDX_HEADS, 1), ki_s.reshape(db, t_new, IDX_DIM),
                          cache_idx_k.transpose(0, 2, 1))
    topk_s = min(TOPK_MAX, (past + t_new) // 4)
    bias, biasn = _thr_sample(sc.reshape(ns, past), scn.reshape(ns, LANES), topk_s)
    yb_pad = _attn_sample(page_table, q_s.reshape(db, t_new, B_WIDTH), bias.reshape(db, t_new, past),
                          biasn.reshape(db, t_new, LANES), k_s.reshape(db, t_new, B_WIDTH),
                          v_s.reshape(db, t_new, B_WIDTH), _channels_first(cache_k), _channels_first(cache_v))
    yb_s = yb_pad[:, ::B_HEADS, :].reshape(ns, B_WIDTH)
    yc_pad = _memattn_sample(qc_s.reshape(db, t_new, C_WIDTH),
                             _channels_first(cache_mem_k), _channels_first(cache_mem_v))
    yc_s = yc_pad[:, ::SUBLANES, :].reshape(ns, C_WIDTH)
    y_sample = _merge(xs, mixa_s, yb_s, sgb_s, yc_s, sgc_s, wo_b, g_final2).reshape(db, t_new, D_MODEL)

    return (y_prompt, y_sample,
            _heads_last(kt, B_HEADS), _heads_last(vt, B_HEADS), kit.transpose(0, 2, 1),
            _heads_last(mkt, C_HEADS), _heads_last(mvt, C_HEADS),
            k_s.reshape(db, t_new, B_HEADS, HEAD_DIM), v_s.reshape(db, t_new, B_HEADS, HEAD_DIM),
            ki_s.reshape(db, t_new, IDX_DIM), vn_s.reshape(db, t_new, A_WIDTH))
```

```python
import functools

import jax
import jax.numpy as jnp
from jax import lax
from jax.experimental import pallas as pl
from jax.experimental.pallas import tpu as pltpu

F32 = jnp.float32
BF16 = jnp.bfloat16

HEAD_DIM = 64
HALF = HEAD_DIM // 2
D_MODEL = 1024
A_GROUPS = 4
A_WIDTH = A_GROUPS * HEAD_DIM
CHUNK = 128
B_HEADS = 8
B_WIDTH = B_HEADS * HEAD_DIM
IDX_HEADS = 8
IDX_DIM = 64
TOPK_MAX = 256
N_MEM = 256
C_HEADS = 4
C_WIDTH = C_HEADS * HEAD_DIM
PAGE_SIZE = 128
ROPE_THETA = 10000.0
EPS = 1e-6
IDX_W_SCALE = (IDX_HEADS * IDX_DIM) ** -0.5
ATTN_SCALE = HEAD_DIM ** -0.5
LOG2E = 1.4426950408889634

LANES = 128
SUBLANES = 8
NEG = -1e30
FLT_MAX = 3.4028234663852886e38
I16_MIN = -2 ** 15
I16_ROWS = 16

TM = 256
QB = 256
KC = 256
V_ROWS = HEAD_DIM + 16
IDX_PAGES_PER_STEP = 32
ATTN_PAGES_PER_STEP = 32
THR_ROWS = 64
VMEM_LIMIT = 52 * 1024 * 1024

OFF_U, OFF_VA, OFF_GA = 0, 256, 512
OFF_Q, OFF_K, OFF_V, OFF_GB, OFF_QI = 768, 1280, 1792, 2304, 2816
OFF_QC, OFF_GC, OFF_KI2, OFF_WI, W_TOTAL = 3328, 3584, 3840, 3968, 4096


def _cparams(sem):
    return pltpu.CompilerParams(dimension_semantics=sem, vmem_limit_bytes=VMEM_LIMIT)


def _iota(shape, dim):
    return lax.broadcasted_iota(jnp.int32, shape, dim)


def _silu(x):
    return x / (1.0 + jnp.exp(-x))


def _dot(a, b):
    return jnp.dot(a, b, preferred_element_type=F32)


def _dot_nt(a, b):
    return lax.dot_general(a, b, (((1,), (1,)), ((), ())), preferred_element_type=F32)


def _key_to_float(k):
    bits = k ^ ((k >> 31) & jnp.int32(0x7FFFFFFF))
    return lax.bitcast_convert_type(bits, F32)


def _bitwise_kth(count_ge, kth, shape, bits):
    cnt0 = count_ge(jnp.zeros(shape, jnp.int32))
    ok0 = cnt0 >= kth
    key = jnp.where(ok0, jnp.int32(0), jnp.int32(-2 ** (bits - 1)))
    cnt_key = jnp.where(ok0, cnt0, jnp.float32(2 ** 24))

    def bit_body(it, carry):
        key, cnt_key = carry
        trial = key + lax.shift_left(jnp.int32(1), bits - 2 - it)
        cnt = count_ge(trial)
        ok = cnt >= kth
        return jnp.where(ok, trial, key), jnp.where(ok, cnt, cnt_key)

    return lax.fori_loop(0, bits - 1, bit_body, (key, cnt_key))


def _kth_largest(count_ge, kth, shape):
    key, cnt_key = _bitwise_kth(lambda k: count_ge(_key_to_float(k)), kth, shape, 32)
    return _key_to_float(key), cnt_key


def _rope(t, cos, sin):
    outs = []
    first = (_iota((t.shape[0], LANES), 1) % HEAD_DIM) < HALF
    for j in range(t.shape[1] // LANES):
        x = t[:, j * LANES:(j + 1) * LANES]
        partner = jnp.where(first, pltpu.roll(x, LANES - HALF, 1), pltpu.roll(x, HALF, 1))
        outs.append(x * cos + partner * sin)
    return outs[0] if len(outs) == 1 else jnp.concatenate(outs, axis=1)


def _proj_kernel(chunk, sample, x_ref, g_ref, w_ref, cos_ref, sin_ref, ws_ref, bs_ref, gv_ref, *outs):
    if sample:
        q_out, k_out, v_out, qi_out, ki_out, wi_out, sgb_out, mixa_out, qc_out, sgc_out, vn_out = outs
    else:
        (kt_out, vt_out, kit_out, kbf_out, vtb_out, ki2_out, qz_out, qiz_out, wt_out,
         sgb_out, mixa_out, qc_out, sgc_out) = outs
    tm = x_ref.shape[0]
    x = x_ref[...]
    ms = jnp.mean(x * x, axis=-1, keepdims=True)
    h = (x * lax.rsqrt(ms + EPS) * g_ref[...]).astype(BF16)
    cos = cos_ref[...]
    sin = sin_ref[...]

    def seg(a, b):
        return _dot(h, w_ref[:, a:b])

    half_id = (_iota((tm, LANES), 1) // HEAD_DIM)

    def head_split(t, out):
        for hd in range(B_HEADS):
            pair = t[:, (hd // 2) * LANES:(hd // 2 + 1) * LANES]
            out[hd] = jnp.where(half_id == hd % 2, pair, 0.0).astype(BF16)

    q = _rope(seg(OFF_Q, OFF_K), cos, sin)
    k = _rope(seg(OFF_K, OFF_V), cos, sin)
    v = seg(OFF_V, OFF_GB)
    qi = _rope(seg(OFF_QI, OFF_QC), cos, sin)
    ki2 = _rope(seg(OFF_KI2, OFF_WI), cos, sin)
    wi = seg(OFF_WI, W_TOTAL) * IDX_W_SCALE
    qc = seg(OFF_QC, OFF_GC) * ATTN_SCALE
    sgb_out[...] = _silu(seg(OFF_GB, OFF_QI)).astype(BF16)
    sgc_out[...] = _silu(seg(OFF_GC, OFF_KI2)).astype(BF16)
    qc_out[...] = qc.astype(qc_out.dtype)
    if sample:
        q_out[...] = q
        k_out[...] = k
        v_out[...] = v
        qi_out[...] = qi
        ki_out[...] = ki2[:, :IDX_DIM]
        wi_out[...] = wi[:, :IDX_HEADS]
    else:
        head_split(q * (ATTN_SCALE * LOG2E), qz_out)
        head_split(qi, qiz_out)
        kt_out[0] = k.T
        kbf_out[...] = k.astype(BF16)
        vt = v.T
        vt_out[0] = vt
        vtb = vt.astype(BF16)
        ones = jnp.ones((V_ROWS - HEAD_DIM, tm), BF16)
        for hd in range(B_HEADS):
            vtb_out[0, hd * V_ROWS:hd * V_ROWS + HEAD_DIM, :] = vtb[hd * HEAD_DIM:(hd + 1) * HEAD_DIM, :]
            vtb_out[0, hd * V_ROWS + HEAD_DIM:(hd + 1) * V_ROWS, :] = ones
        kit_out[0] = ki2.T[:IDX_DIM, :]
        ki2_out[...] = ki2.astype(BF16)
        wt_out[...] = wi.T[:IDX_HEADS, :]

    u = seg(OFF_U, OFF_VA)
    va = seg(OFF_VA, OFF_GA)
    ga = seg(OFF_GA, OFF_Q)
    grp_r = _iota((A_WIDTH, A_WIDTH), 0) // HEAD_DIM
    grp_c = _iota((A_WIDTH, A_WIDTH), 1) // HEAD_DIM
    gmat = jnp.where(grp_r == grp_c, 1.0 / HEAD_DIM, 0.0).astype(BF16)

    def group_mean(t):
        hi = t.astype(BF16)
        lo = (t - hi.astype(F32)).astype(BF16)
        return _dot(hi, gmat) + _dot(lo, gmat)

    mu = group_mean(va)
    d = va - mu
    var = group_mean(d * d)
    vn = d * lax.rsqrt(var + EPS) * gv_ref[...]
    if sample:
        vn_out[...] = vn
    vnb = vn.astype(BF16)
    row = _iota((CHUNK, CHUNK), 0)
    col = _iota((CHUNK, CHUNK), 1)
    causal = (row >= col) & ((row // chunk) == (col // chunk))
    lane_grp = _iota((CHUNK, A_WIDTH), 1) // HEAD_DIM
    wms = [jnp.where(causal, ws_ref[g], 0.0).astype(BF16) for g in range(A_GROUPS)]
    mixes = []
    for j in range(tm // CHUNK):
        vc = vnb[j * CHUNK:(j + 1) * CHUNK, :]
        m = bs_ref[...]
        for g in range(A_GROUPS):
            m = m + jnp.where(lane_grp == g, _dot(wms[g], vc), 0.0)
        mixes.append(m)
    mix = jnp.concatenate(mixes, axis=0)
    mixa_out[...] = (_silu(ga) * u * mix).astype(BF16)


def _proj(x, g_norm, w_r, cos_t, sin_t, ws, bs, g_v, chunk, seq_blocks, sample):
    n = x.shape[0]
    nt = n // TM
    nb = nt // seq_blocks
    row = lambda w: pl.BlockSpec((TM, w), lambda i: (i, 0))
    const2 = lambda a: pl.BlockSpec(a.shape, lambda i: (0, 0))
    sds = jax.ShapeDtypeStruct
    in_specs = [
        row(D_MODEL), const2(g_norm), const2(w_r),
        pl.BlockSpec((TM, LANES), lambda i: (i % seq_blocks, 0)),
        pl.BlockSpec((TM, LANES), lambda i: (i % seq_blocks, 0)),
        pl.BlockSpec(ws.shape, lambda i: (0, 0, 0)), const2(bs), const2(g_v),
    ]
    if sample:
        outs = [
            (sds((n, B_WIDTH), F32), row(B_WIDTH)),
            (sds((n, B_WIDTH), F32), row(B_WIDTH)),
            (sds((n, B_WIDTH), F32), row(B_WIDTH)),
            (sds((n, IDX_HEADS * IDX_DIM), F32), row(IDX_HEADS * IDX_DIM)),
            (sds((n, IDX_DIM), F32), row(IDX_DIM)),
            (sds((n, IDX_HEADS), F32), row(IDX_HEADS)),
            (sds((n, B_WIDTH), BF16), row(B_WIDTH)),
            (sds((n, A_WIDTH), BF16), row(A_WIDTH)),
            (sds((n, C_WIDTH), F32), row(C_WIDTH)),
            (sds((n, C_WIDTH), BF16), row(C_WIDTH)),
            (sds((n, A_WIDTH), F32), row(A_WIDTH)),
        ]
    else:
        seq = seq_blocks * TM
        tspec = lambda w: pl.BlockSpec((1, w, TM), lambda i: (i // seq_blocks, 0, i % seq_blocks))
        hspec = pl.BlockSpec((B_HEADS, TM, LANES), lambda i: (0, i, 0))
        outs = [
            (sds((nb, B_WIDTH, seq), F32), tspec(B_WIDTH)),
            (sds((nb, B_WIDTH, seq), F32), tspec(B_WIDTH)),
            (sds((nb, IDX_DIM, seq), F32), tspec(IDX_DIM)),
            (sds((n, B_WIDTH), BF16), row(B_WIDTH)),
            (sds((nt, B_HEADS * V_ROWS, TM), BF16),
             pl.BlockSpec((1, B_HEADS * V_ROWS, TM), lambda i: (i, 0, 0))),
            (sds((n, LANES), BF16), row(LANES)),
            (sds((B_HEADS, n, LANES), BF16), hspec),
            (sds((IDX_HEADS, n, LANES), BF16), hspec),
            (sds((IDX_HEADS, n), F32), pl.BlockSpec((IDX_HEADS, TM), lambda i: (0, i))),
            (sds((n, B_WIDTH), BF16), row(B_WIDTH)),
            (sds((n, A_WIDTH), BF16), row(A_WIDTH)),
            (sds((n, C_WIDTH), BF16), row(C_WIDTH)),
            (sds((n, C_WIDTH), BF16), row(C_WIDTH)),
        ]
    return pl.pallas_call(
        functools.partial(_proj_kernel, chunk, sample),
        grid=(nt,), in_specs=in_specs, out_specs=[o[1] for o in outs], out_shape=[o[0] for o in outs],
        compiler_params=_cparams(("parallel",)), name="proj",
    )(x, g_norm, w_r, cos_t, sin_t, ws, bs, g_v)


def _memkv_kernel(m_ref, g_ref, w_ref, mkt_out, mvt_out):
    x = m_ref[...]
    ms = jnp.mean(x * x, axis=-1, keepdims=True)
    h = (x * lax.rsqrt(ms + EPS) * g_ref[...]).astype(BF16)
    kv = _dot(h, w_ref[...])
    mkt_out[0] = kv[:, :C_WIDTH].T
    mvt_out[0] = kv[:, C_WIDTH:].T


def _memkv(mem, g_mem, w_kv):
    n = mem.shape[0]
    nb = n // N_MEM
    return pl.pallas_call(
        _memkv_kernel, grid=(nb,),
        in_specs=[pl.BlockSpec((N_MEM, D_MODEL), lambda i: (i, 0)),
                  pl.BlockSpec(g_mem.shape, lambda i: (0, 0)),
                  pl.BlockSpec(w_kv.shape, lambda i: (0, 0))],
        out_specs=[pl.BlockSpec((1, C_WIDTH, N_MEM), lambda i: (i, 0, 0))] * 2,
        out_shape=[jax.ShapeDtypeStruct((nb, C_WIDTH, N_MEM), F32)] * 2,
        compiler_params=_cparams(("parallel",)), name="memkv",
    )(mem, g_mem, w_kv)


def _memattn_kernel(q_ref, mkt_ref, mvt_ref, o_ref):
    q = q_ref[0]
    mkt = mkt_ref[0].astype(BF16)
    mvt = mvt_ref[0].astype(BF16)
    t = q.shape[0]
    lane_head = _iota((t, C_WIDTH), 1) // HEAD_DIM
    out = jnp.zeros((t, C_WIDTH), F32)
    for hd in range(C_HEADS):
        qh = jnp.where(lane_head == hd, q, 0.0).astype(BF16)
        s = _dot(qh, mkt)
        p = jnp.exp(s - jnp.max(s, axis=-1, keepdims=True))
        o = _dot_nt(p.astype(BF16), mvt) / jnp.sum(p, axis=-1, keepdims=True)
        out = out + jnp.where(lane_head == hd, o, 0.0)
    o_ref[0] = out.astype(o_ref.dtype)


def _memattn(q3, mkt3, mvt3, tiles_per_mem):
    g, t, _ = q3.shape
    mem_spec = pl.BlockSpec((1, C_WIDTH, N_MEM), lambda i: (i // tiles_per_mem, 0, 0))
    return pl.pallas_call(
        _memattn_kernel, grid=(g,),
        in_specs=[pl.BlockSpec((1, t, C_WIDTH), lambda i: (i, 0, 0)), mem_spec, mem_spec],
        out_specs=pl.BlockSpec((1, t, C_WIDTH), lambda i: (i, 0, 0)),
        out_shape=jax.ShapeDtypeStruct((g, t, C_WIDTH), BF16),
        compiler_params=_cparams(("parallel",)), name="memattn",
    )(q3, mkt3, mvt3)


def _memattn_sample_kernel(q_ref, mkt_ref, mvt_ref, o_ref):
    q4 = q_ref[0]
    mkt = mkt_ref[0].astype(BF16)
    mvt = mvt_ref[0].astype(BF16)
    t = q4.shape[0]
    mask8 = (_iota((SUBLANES, C_WIDTH), 1) // HEAD_DIM) == _iota((SUBLANES, C_WIDTH), 0)
    qbd = jnp.concatenate(
        [jnp.where(mask8, jnp.broadcast_to(q4[i:i + 1, :], (SUBLANES, C_WIDTH)), 0.0) for i in range(t)], axis=0)
    s = _dot(qbd.astype(BF16), mkt)
    p = jnp.exp(s - jnp.max(s, axis=-1, keepdims=True))
    o = _dot_nt(p.astype(BF16), mvt) / jnp.sum(p, axis=-1, keepdims=True)
    for i in range(t):
        slab = jnp.where(mask8, o[i * SUBLANES:(i + 1) * SUBLANES, :], 0.0)
        o_ref[0, i * SUBLANES:(i + 1) * SUBLANES, :] = jnp.broadcast_to(
            jnp.sum(slab, axis=0, keepdims=True), (SUBLANES, C_WIDTH))


def _memattn_sample(q3, mkt3, mvt3):
    g, t, _ = q3.shape
    mem_spec = pl.BlockSpec((1, C_WIDTH, N_MEM), lambda i: (i, 0, 0))
    return pl.pallas_call(
        _memattn_sample_kernel, grid=(g,),
        in_specs=[pl.BlockSpec((1, t, C_WIDTH), lambda i: (i, 0, 0)), mem_spec, mem_spec],
        out_specs=pl.BlockSpec((1, t * SUBLANES, C_WIDTH), lambda i: (i, 0, 0)),
        out_shape=jax.ShapeDtypeStruct((g, t * SUBLANES, C_WIDTH), F32),
        compiler_params=_cparams(("parallel",)), name="memattn_sample",
    )(q3, mkt3, mvt3)


def _dsa_prompt_kernel(topk, qz_ref, qiz_ref, wt_ref, ki2_ref, kbf_ref, vt_ref, o_ref,
                       sc_ref, acc_ref, m_ref, s_ref, hi_ref, lo_ref):
    i = pl.program_id(1)
    nk = i + 1
    npair = (nk + 1) // 2
    kth = jnp.float32(topk)

    def sum_rows(x):
        return jnp.sum(x.reshape(KC // SUBLANES, SUBLANES, QB), axis=0)

    def max_rows(x):
        return jnp.max(x.reshape(KC // SUBLANES, SUBLANES, QB), axis=0)

    def score_chunk(c, _):
        keys = ki2_ref[0, c]
        acc = jnp.zeros((KC, QB), F32)
        for hd in range(IDX_HEADS):
            y = _dot_nt(keys, qiz_ref[hd])
            acc = acc + jnp.maximum(y, 0.0) * wt_ref[hd:hd + 1, :]
        kpos = c * KC + _iota((KC, QB), 0)
        qpos = i * QB + _iota((KC, QB), 1)
        masked = jnp.where(kpos <= qpos, acc, -jnp.inf)
        sc_ref[c] = masked
        bits = lax.bitcast_convert_type(masked, jnp.int32)
        key = bits ^ ((bits >> 31) & jnp.int32(0x7FFFFFFF))
        hi_ref[c] = (key >> 16).astype(jnp.int16)
        lo_ref[c] = ((key & jnp.int32(0xFFFF)) + jnp.int32(I16_MIN)).astype(jnp.int16)
        return 0

    lax.fori_loop(0, nk, score_chunk, 0)

    @pl.when(nk % 2 == 1)
    def _():
        pad = jnp.minimum(nk, sc_ref.shape[0] - 1)
        sc_ref[pad] = jnp.full((KC, QB), -jnp.inf, F32)
        hi_ref[pad] = jnp.full((KC, QB), I16_MIN, jnp.int16)
        lo_ref[pad] = jnp.full((KC, QB), I16_MIN, jnp.int16)

    def count(pred):
        def body(j, cnt):
            return (cnt + sum_rows(pred(sc_ref[2 * j]).astype(F32))
                    + sum_rows(pred(sc_ref[2 * j + 1]).astype(F32)))
        return jnp.sum(lax.fori_loop(0, npair, body, jnp.zeros((SUBLANES, QB), F32)), axis=0, keepdims=True)

    def count16(ref, pred):
        def slab_count(c):
            ones = jnp.where(pred(ref[c]), jnp.int16(1), jnp.int16(0))
            acc = ones[0:I16_ROWS]
            for r in range(1, KC // I16_ROWS):
                acc = acc + ones[r * I16_ROWS:(r + 1) * I16_ROWS]
            return acc

        def body(j, cnt):
            return cnt + slab_count(2 * j) + slab_count(2 * j + 1)

        cnt16 = lax.fori_loop(0, npair, body, jnp.zeros((I16_ROWS, QB), jnp.int16))
        return jnp.sum(cnt16.astype(F32), axis=0, keepdims=True)

    def write_ge(thr):
        def body(j, _):
            sc_ref[2 * j] = jnp.where(sc_ref[2 * j] >= thr, 0.0, NEG)
            sc_ref[2 * j + 1] = jnp.where(sc_ref[2 * j + 1] >= thr, 0.0, NEG)
            return 0
        lax.fori_loop(0, npair, body, 0)

    @pl.when(nk * KC <= topk)
    def _():
        write_ge(jnp.full((1, QB), -FLT_MAX, F32))

    @pl.when(nk * KC > topk)
    def _():
        hi_k, _ = _bitwise_kth(lambda t: count16(hi_ref, lambda h: h >= t.astype(jnp.int16)), kth, (1, QB), 16)
        hi_k16 = hi_k.astype(jnp.int16)
        kth_lo = kth - count16(hi_ref, lambda h: h > hi_k16)

        def keep_matching(j, _):
            for c in (2 * j, 2 * j + 1):
                lo_ref[c] = jnp.where(hi_ref[c] == hi_k16, lo_ref[c], jnp.int16(I16_MIN))
            return 0

        lax.fori_loop(0, npair, keep_matching, 0)
        lo_k, _ = _bitwise_kth(lambda t: count16(lo_ref, lambda v: v >= t.astype(jnp.int16)), kth_lo, (1, QB), 16)
        thr = _key_to_float(hi_k * jnp.int32(2 ** 16) + (lo_k - I16_MIN))
        cnt_thr = count(lambda x: x >= thr)
        few = (i * QB + 1 + _iota((1, QB), 1)).astype(F32) <= kth
        thr = jnp.where(few, -FLT_MAX, thr)
        cnt_thr = jnp.where(few, kth, cnt_thr)
        has_tie = jnp.max(cnt_thr) > kth

        @pl.when(jnp.logical_not(has_tie))
        def _():
            write_ge(thr)

        @pl.when(has_tie)
        def _():
            need = kth - count(lambda x: x > thr)
            ltri = (_iota((KC, KC), 1) < _iota((KC, KC), 0)).astype(BF16)

            def body(c, carry):
                x = sc_ref[c]
                eq = x == thr
                before = _dot(ltri, eq.astype(BF16)) + carry
                sel = (x > thr) | (eq & (before < need))
                sc_ref[c] = jnp.where(sel, 0.0, NEG)
                return carry + jnp.sum(sum_rows(eq.astype(F32)), axis=0, keepdims=True)

            lax.fori_loop(0, nk, body, jnp.zeros((1, QB), F32))

    m_ref[...] = jnp.full((B_HEADS, QB), NEG, F32)
    acc_ref[...] = jnp.zeros((B_HEADS * V_ROWS, QB), F32)

    def biased_scores(c, hd):
        pair = hd // 2
        kc = kbf_ref[0, c, :, pair * LANES:(pair + 1) * LANES]
        return _dot_nt(kc, qz_ref[hd]) + sc_ref[c]

    for hd in range(B_HEADS):
        s_ref[hd] = biased_scores(0, hd)

    def attend(c, _):
        nxt = jnp.minimum(c + 1, nk - 1)
        for hd in range(B_HEADS):
            rows = slice(hd * V_ROWS, (hd + 1) * V_ROWS)
            s_next = biased_scores(nxt, hd)
            s = s_ref[hd]
            m_old = m_ref[hd:hd + 1, :]
            m_new = jnp.maximum(m_old, jnp.max(max_rows(s), axis=0, keepdims=True))
            alpha = jnp.exp2(m_old - m_new)
            p = jnp.exp2(s - m_new)
            acc_ref[rows, :] = alpha * acc_ref[rows, :] + _dot(vt_ref[0, c, rows, :], p.astype(BF16))
            m_ref[hd:hd + 1, :] = m_new
            s_ref[hd] = s_next
        return 0

    lax.fori_loop(0, nk, attend, 0)
    outs = []
    for hd in range(B_HEADS):
        base = hd * V_ROWS
        outs.append(acc_ref[base:base + HEAD_DIM, :] / acc_ref[base + HEAD_DIM:base + HEAD_DIM + 1, :])
    o_ref[...] = jnp.concatenate(outs, axis=0).T.astype(o_ref.dtype)


def _dsa_prompt(qz, qiz, wt, ki2, kbf, vtb, batch, seq):
    nq = seq // QB
    nc = seq // KC
    topk = min(TOPK_MAX, seq // 4)
    n = batch * seq
    qspec = pl.BlockSpec((B_HEADS, QB, LANES), lambda b, i: (0, b * nq + i, 0))
    return pl.pallas_call(
        functools.partial(_dsa_prompt_kernel, topk),
        grid=(batch, nq),
        in_specs=[qspec, qspec,
                  pl.BlockSpec((IDX_HEADS, QB), lambda b, i: (0, b * nq + i)),
                  pl.BlockSpec((1, nc, KC, LANES), lambda b, i: (b, 0, 0, 0)),
                  pl.BlockSpec((1, nc, KC, B_WIDTH), lambda b, i: (b, 0, 0, 0)),
                  pl.BlockSpec((1, nc, B_HEADS * V_ROWS, KC), lambda b, i: (b, 0, 0, 0))],
        out_specs=pl.BlockSpec((QB, B_WIDTH), lambda b, i: (b * nq + i, 0)),
        out_shape=jax.ShapeDtypeStruct((n, B_WIDTH), BF16),
        scratch_shapes=[pltpu.VMEM((nc, KC, QB), F32), pltpu.VMEM((B_HEADS * V_ROWS, QB), F32),
                        pltpu.VMEM((B_HEADS, QB), F32), pltpu.VMEM((B_HEADS, KC, QB), F32),
                        pltpu.VMEM((nc, KC, QB), jnp.int16), pltpu.VMEM((nc, KC, QB), jnp.int16)],
        compiler_params=_cparams(("parallel", "arbitrary")), name="dsa_prompt",
    )(qz, qiz, wt, ki2.reshape(batch, nc, KC, LANES), kbf.reshape(batch, nc, KC, B_WIDTH),
      vtb.reshape(batch, nc, B_HEADS * V_ROWS, KC))


def _page_specs(block, n_pages_per_step):
    def make(r):
        return pl.BlockSpec(block, lambda b, j, pt: (pt[b, j * n_pages_per_step + r], 0, 0))
    return [make(r) for r in range(n_pages_per_step)]


def _idx_sample_kernel(t_new, pps, pt_ref, qi_ref, w_ref, kin_ref, *rest):
    pages = rest[:pps]
    sc_out, scn_out = rest[pps:]
    j = pl.program_id(1)
    qi = qi_ref[0]
    w = w_ref[0]
    qib = qi.astype(BF16)

    def token_scores(y):
        z = jnp.maximum(y, 0.0) * w
        return jnp.sum(z.reshape(t_new, IDX_HEADS, y.shape[1]), axis=1)

    for r in range(pps):
        y = _dot(qib, pages[r][0].astype(BF16))
        sc_out[0, :, r * PAGE_SIZE:(r + 1) * PAGE_SIZE] = token_scores(y)

    @pl.when(j == 0)
    def _():
        qf = qib.astype(F32)
        lane = _iota((t_new, LANES), 1)
        tok = _iota((t_new, LANES), 0)
        new = jnp.full((t_new, LANES), -jnp.inf, F32)
        for tk in range(t_new):
            kr = kin_ref[0, tk:tk + 1, :].astype(BF16).astype(F32)
            col = token_scores(jnp.sum(qf * kr, axis=-1, keepdims=True))
            new = jnp.where((lane == tk) & (tok >= tk), col, new)
        scn_out[0] = new


def _idx_sample(page_table, qi32, w32, ki_new, cache_idx_kt):
    db, rows, _ = qi32.shape
    t_new = rows // IDX_HEADS
    n_pages = page_table.shape[1]
    pps = min(IDX_PAGES_PER_STEP, n_pages)
    steps = n_pages // pps
    span = pps * PAGE_SIZE
    grid_spec = pltpu.PrefetchScalarGridSpec(
        num_scalar_prefetch=1, grid=(db, steps),
        in_specs=[pl.BlockSpec((1, rows, IDX_DIM), lambda b, j, pt: (b, 0, 0)),
                  pl.BlockSpec((1, rows, 1), lambda b, j, pt: (b, 0, 0)),
                  pl.BlockSpec((1, t_new, IDX_DIM), lambda b, j, pt: (b, 0, 0))]
        + _page_specs((1, IDX_DIM, PAGE_SIZE), pps),
        out_specs=[pl.BlockSpec((1, t_new, span), lambda b, j, pt: (b, 0, j)),
                   pl.BlockSpec((1, t_new, LANES), lambda b, j, pt: (b, 0, 0))])
    return pl.pallas_call(
        functools.partial(_idx_sample_kernel, t_new, pps), grid_spec=grid_spec,
        out_shape=[jax.ShapeDtypeStruct((db, t_new, n_pages * PAGE_SIZE), F32),
                   jax.ShapeDtypeStruct((db, t_new, LANES), F32)],
        compiler_params=_cparams(("parallel", "arbitrary")), name="idx_sample",
    )(page_table, qi32, w32, ki_new, *([cache_idx_kt] * pps))


def _thr_sample_kernel(topk, sc_ref, scn_ref, b_out, bn_out):
    kth = jnp.float32(topk)
    n_chunks = sc_ref.shape[1] // LANES

    def count(pred):
        return (jnp.sum(pred(sc_ref[...]).astype(F32), axis=-1, keepdims=True)
                + jnp.sum(pred(scn_ref[...]).astype(F32), axis=-1, keepdims=True))

    thr, cnt_thr = _kth_largest(lambda t: count(lambda x: x >= t), kth, (THR_ROWS, 1))
    has_tie = jnp.max(cnt_thr) > kth

    @pl.when(jnp.logical_not(has_tie))
    def _():
        b_out[...] = jnp.where(sc_ref[...] >= thr, 0.0, NEG)
        bn_out[...] = jnp.where(scn_ref[...] >= thr, 0.0, NEG)

    @pl.when(has_tie)
    def _():
        need = kth - count(lambda x: x > thr)
        utri = (_iota((LANES, LANES), 0) < _iota((LANES, LANES), 1)).astype(BF16)
        carry = jnp.zeros((THR_ROWS, 1), F32)
        for c in range(n_chunks + 1):
            x = sc_ref[:, c * LANES:(c + 1) * LANES] if c < n_chunks else scn_ref[...]
            eq = x == thr
            before = _dot(eq.astype(BF16), utri) + carry
            sel = (x > thr) | (eq & (before < need))
            bias = jnp.where(sel, 0.0, NEG)
            if c < n_chunks:
                b_out[:, c * LANES:(c + 1) * LANES] = bias
            else:
                bn_out[...] = bias
            carry = carry + jnp.sum(eq.astype(F32), axis=-1, keepdims=True)


def _thr_sample(sc, scn, topk):
    n, width = sc.shape
    return pl.pallas_call(
        functools.partial(_thr_sample_kernel, topk), grid=(n // THR_ROWS,),
        in_specs=[pl.BlockSpec((THR_ROWS, width), lambda i: (i, 0)),
                  pl.BlockSpec((THR_ROWS, LANES), lambda i: (i, 0))],
        out_specs=[pl.BlockSpec((THR_ROWS, width), lambda i: (i, 0)),
                   pl.BlockSpec((THR_ROWS, LANES), lambda i: (i, 0))],
        out_shape=[jax.ShapeDtypeStruct((n, width), F32), jax.ShapeDtypeStruct((n, LANES), F32)],
        compiler_params=_cparams(("parallel",)), name="thr_sample",
    )(sc, scn)


def _attn_sample_kernel(t_new, pps, pt_ref, q_ref, bias_ref, biasn_ref, kn_ref, vn_ref, *rest):
    kpages = rest[:pps]
    vpages = rest[pps:2 * pps]
    o_ref, m_ref, l_ref, acc_ref, qbd_ref = rest[2 * pps:]
    j = pl.program_id(1)
    rows = t_new * B_HEADS
    mask8 = (_iota((SUBLANES, B_WIDTH), 1) // HEAD_DIM) == _iota((SUBLANES, B_WIDTH), 0)

    def per_token(ref, width):
        return jnp.concatenate(
            [jnp.broadcast_to(ref[0, i:i + 1, :], (B_HEADS, width)) for i in range(t_new)], axis=0)

    @pl.when(j == 0)
    def _():
        m_ref[...] = jnp.full((rows, 1), NEG, F32)
        l_ref[...] = jnp.zeros((rows, 1), F32)
        acc_ref[...] = jnp.zeros((rows, B_WIDTH), F32)
        qbd_ref[...] = jnp.concatenate(
            [jnp.where(mask8, jnp.broadcast_to(q_ref[0, i:i + 1, :] * ATTN_SCALE, (B_HEADS, B_WIDTH)), 0.0)
             for i in range(t_new)], axis=0).astype(BF16)

    qbd = qbd_ref[...]
    span = pps * PAGE_SIZE
    s = jnp.concatenate([_dot(qbd, kpages[r][0].astype(BF16)) for r in range(pps)], axis=1)
    s = s + per_token(bias_ref, span)
    m_old = m_ref[...]
    m_new = jnp.maximum(m_old, jnp.max(s, axis=-1, keepdims=True))
    alpha = jnp.exp(m_old - m_new)
    p = jnp.exp(s - m_new)
    l_ref[...] = alpha * l_ref[...] + jnp.sum(p, axis=-1, keepdims=True)
    pb = p.astype(BF16)
    pv = jnp.zeros((rows, B_WIDTH), F32)
    for r in range(pps):
        pv = pv + _dot_nt(pb[:, r * PAGE_SIZE:(r + 1) * PAGE_SIZE], vpages[r][0].astype(BF16))
    acc_ref[...] = alpha * acc_ref[...] + pv
    m_ref[...] = m_new

    @pl.when(j == pl.num_programs(1) - 1)
    def _():
        qf = qbd.astype(F32)
        bn = per_token(biasn_ref, LANES)
        cols = []
        for tk in range(t_new):
            kr = kn_ref[0, tk:tk + 1, :].astype(BF16).astype(F32)
            cols.append(jnp.sum(qf * kr, axis=-1, keepdims=True) + bn[:, tk:tk + 1])
        m_old = m_ref[...]
        m_new = m_old
        for cval in cols:
            m_new = jnp.maximum(m_new, cval)
        alpha = jnp.exp(m_old - m_new)
        l = alpha * l_ref[...]
        acc = alpha * acc_ref[...]
        for tk in range(t_new):
            pk = jnp.exp(cols[tk] - m_new)
            l = l + pk
            acc = acc + pk.astype(BF16).astype(F32) * vn_ref[0, tk:tk + 1, :].astype(BF16).astype(F32)
        o = acc / l
        for i in range(t_new):
            slab = jnp.where(mask8, o[i * B_HEADS:(i + 1) * B_HEADS, :], 0.0)
            o_ref[0, i * B_HEADS:(i + 1) * B_HEADS, :] = jnp.broadcast_to(
                jnp.sum(slab, axis=0, keepdims=True), (B_HEADS, B_WIDTH))


def _attn_sample(page_table, q3, bias3, biasn3, k_new, v_new, cache_kt, cache_vt):
    db, t_new, _ = q3.shape
    n_pages = page_table.shape[1]
    pps = min(ATTN_PAGES_PER_STEP, n_pages)
    steps = n_pages // pps
    span = pps * PAGE_SIZE
    rows = t_new * B_HEADS
    per_b = lambda w: pl.BlockSpec((1, t_new, w), lambda b, j, pt: (b, 0, 0))
    grid_spec = pltpu.PrefetchScalarGridSpec(
        num_scalar_prefetch=1, grid=(db, steps),
        in_specs=[per_b(B_WIDTH),
                  pl.BlockSpec((1, t_new, span), lambda b, j, pt: (b, 0, j)),
                  per_b(LANES), per_b(B_WIDTH), per_b(B_WIDTH)]
        + _page_specs((1, B_WIDTH, PAGE_SIZE), pps)
        + _page_specs((1, B_WIDTH, PAGE_SIZE), pps),
        out_specs=pl.BlockSpec((1, rows, B_WIDTH), lambda b, j, pt: (b, 0, 0)),
        scratch_shapes=[pltpu.VMEM((rows, 1), F32), pltpu.VMEM((rows, 1), F32),
                        pltpu.VMEM((rows, B_WIDTH), F32), pltpu.VMEM((rows, B_WIDTH), BF16)])
    return pl.pallas_call(
        functools.partial(_attn_sample_kernel, t_new, pps), grid_spec=grid_spec,
        out_shape=jax.ShapeDtypeStruct((db, rows, B_WIDTH), F32),
        compiler_params=_cparams(("parallel", "arbitrary")), name="attn_sample",
    )(page_table, q3, bias3, biasn3, k_new, v_new, *([cache_kt] * pps), *([cache_vt] * pps))


def _merge_kernel(x_ref, ma_ref, yb_ref, sgb_ref, yc_ref, sgc_ref, wo_ref, g_ref, y_out):
    mb = (yb_ref[...].astype(F32) * sgb_ref[...].astype(F32)).astype(BF16)
    mc = (yc_ref[...].astype(F32) * sgc_ref[...].astype(F32)).astype(BF16)
    o = (_dot(ma_ref[...], wo_ref[0:A_WIDTH, :])
         + _dot(mb, wo_ref[A_WIDTH:A_WIDTH + B_WIDTH, :])
         + _dot(mc, wo_ref[A_WIDTH + B_WIDTH:, :]))
    z = x_ref[...] + o
    ms = jnp.mean(z * z, axis=-1, keepdims=True)
    y_out[...] = z * lax.rsqrt(ms + EPS) * g_ref[...]


def _merge(x, mixa, yb, sgb, yc, sgc, wo, g_final):
    n = x.shape[0]
    row = lambda w: pl.BlockSpec((TM, w), lambda i: (i, 0))
    return pl.pallas_call(
        _merge_kernel, grid=(n // TM,),
        in_specs=[row(D_MODEL), row(A_WIDTH), row(B_WIDTH), row(B_WIDTH), row(C_WIDTH), row(C_WIDTH),
                  pl.BlockSpec(wo.shape, lambda i: (0, 0)), pl.BlockSpec(g_final.shape, lambda i: (0, 0))],
        out_specs=row(D_MODEL),
        out_shape=jax.ShapeDtypeStruct((n, D_MODEL), F32),
        compiler_params=_cparams(("parallel",)), name="merge",
    )(x, mixa, yb, sgb, yc, sgc, wo, g_final)


def _rope_tables(pos):
    inv = ROPE_THETA ** (-jnp.arange(0, HEAD_DIM, 2, dtype=F32) / HEAD_DIM)
    ang = pos.astype(F32)[:, None] * inv[None, :]
    c = jnp.cos(ang)
    s = jnp.sin(ang)
    return jnp.concatenate([c, c, c, c], axis=1), jnp.concatenate([-s, s, -s, s], axis=1)


def _heads_last(t, heads):
    b, _, n = t.shape
    return t.reshape(b, heads, HEAD_DIM, n).transpose(0, 3, 1, 2)


def _channels_first(t):
    p, n, heads, d = t.shape
    return t.transpose(0, 2, 3, 1).reshape(p, heads * d, n)


def kernel(x_prompt, x_sample, mem_prompt, cache_k, cache_v, cache_idx_k, cache_mem_k, cache_mem_v,
           page_table, g_norm, w_in, w_spatial, b_spatial, g_v, w_mem_kv, g_mem, w_out, g_final):
    batch, seq, _ = x_prompt.shape
    db, t_new, _ = x_sample.shape
    n_pages = page_table.shape[1]
    past = n_pages * PAGE_SIZE

    ki_cols = w_in[:, 3328:3392]
    wi_cols = w_in[:, 3392:3400]
    w_r = jnp.concatenate(
        [w_in[:, :3328], w_in[:, 3400:3912], ki_cols, ki_cols, wi_cols,
         jnp.zeros((D_MODEL, LANES - IDX_HEADS), F32)], axis=1).astype(BF16)
    g_norm2 = g_norm.reshape(1, D_MODEL)
    g_v2 = g_v.reshape(1, A_WIDTH)
    bs_prompt = jnp.repeat(b_spatial.T, HEAD_DIM, axis=1)
    reps = CHUNK // t_new
    ws_sample = jnp.tile(w_spatial[:, :t_new, :t_new], (1, reps, reps))
    bs_sample = jnp.tile(jnp.repeat(b_spatial[:, :t_new].T, HEAD_DIM, axis=1), (reps, 1))
    wo_b = w_out.astype(BF16)
    g_final2 = g_final.reshape(1, D_MODEL)

    cos_p, sin_p = _rope_tables(jnp.arange(seq, dtype=jnp.int32))
    pos_s = past + jnp.arange(t_new, dtype=jnp.int32)
    cos_s, sin_s = _rope_tables(jnp.tile(pos_s, TM // t_new))

    xp = x_prompt.reshape(batch * seq, D_MODEL)
    (kt, vt, kit, kbf, vtb, ki2, qz, qiz, wt, sgb, mixa, qc, sgc) = _proj(
        xp, g_norm2, w_r, cos_p, sin_p, w_spatial, bs_prompt, g_v2, CHUNK, seq // TM, False)
    mkt, mvt = _memkv(mem_prompt.reshape(batch * N_MEM, D_MODEL), g_mem.reshape(1, D_MODEL),
                      w_mem_kv.astype(BF16))
    yc = _memattn(qc.reshape(batch * seq // TM, TM, C_WIDTH), mkt, mvt, seq // TM).reshape(batch * seq, C_WIDTH)
    yb = _dsa_prompt(qz, qiz, wt, ki2, kbf, vtb, batch, seq)
    y_prompt = _merge(xp, mixa, yb, sgb, yc, sgc, wo_b, g_final2).reshape(batch, seq, D_MODEL)

    ns = db * t_new
    xs = x_sample.reshape(ns, D_MODEL)
    (q_s, k_s, v_s, qi_s, ki_s, wi_s, sgb_s, mixa_s, qc_s, sgc_s, vn_s) = _proj(
        xs, g_norm2, w_r, cos_s, sin_s, ws_sample, bs_sample, g_v2, t_new, 1, True)
    sc, scn = _idx_sample(page_table, qi_s.reshape(db, t_new * IDX_HEADS, IDX_DIM),
                          wi_s.reshape(db, t_new * IDX_HEADS, 1), ki_s.reshape(db, t_new, IDX_DIM),
                          cache_idx_k.transpose(0, 2, 1))
    topk_s = min(TOPK_MAX, (past + t_new) // 4)
    bias, biasn = _thr_sample(sc.reshape(ns, past), scn.reshape(ns, LANES), topk_s)
    yb_pad = _attn_sample(page_table, q_s.reshape(db, t_new, B_WIDTH), bias.reshape(db, t_new, past),
                          biasn.reshape(db, t_new, LANES), k_s.reshape(db, t_new, B_WIDTH),
                          v_s.reshape(db, t_new, B_WIDTH), _channels_first(cache_k), _channels_first(cache_v))
    yb_s = yb_pad[:, ::B_HEADS, :].reshape(ns, B_WIDTH)
    yc_pad = _memattn_sample(qc_s.reshape(db, t_new, C_WIDTH),
                             _channels_first(cache_mem_k), _channels_first(cache_mem_v))
    yc_s = yc_pad[:, ::SUBLANES, :].reshape(ns, C_WIDTH)
    y_sample = _merge(xs, mixa_s, yb_s, sgb_s, yc_s, sgc_s, wo_b, g_final2).reshape(db, t_new, D_MODEL)

    return (y_prompt, y_sample,
            _heads_last(kt, B_HEADS), _heads_last(vt, B_HEADS), kit.transpose(0, 2, 1),
            _heads_last(mkt, C_HEADS), _heads_last(mvt, C_HEADS),
            k_s.reshape(db, t_new, B_HEADS, HEAD_DIM), v_s.reshape(db, t_new, B_HEADS, HEAD_DIM),
            ki_s.reshape(db, t_new, IDX_DIM), vn_s.reshape(db, t_new, A_WIDTH))
```

```python
import functools

import jax
import jax.numpy as jnp
from jax import lax
from jax.experimental import pallas as pl
from jax.experimental.pallas import tpu as pltpu

F32 = jnp.float32
BF16 = jnp.bfloat16

HEAD_DIM = 64
HALF = HEAD_DIM // 2
D_MODEL = 1024
A_GROUPS = 4
A_WIDTH = A_GROUPS * HEAD_DIM
CHUNK = 128
B_HEADS = 8
B_WIDTH = B_HEADS * HEAD_DIM
IDX_HEADS = 8
IDX_DIM = 64
TOPK_MAX = 256
N_MEM = 256
C_HEADS = 4
C_WIDTH = C_HEADS * HEAD_DIM
PAGE_SIZE = 128
ROPE_THETA = 10000.0
EPS = 1e-6
IDX_W_SCALE = (IDX_HEADS * IDX_DIM) ** -0.5
ATTN_SCALE = HEAD_DIM ** -0.5
LOG2E = 1.4426950408889634

LANES = 128
SUBLANES = 8
NEG = -1e30
FLT_MAX = 3.4028234663852886e38
PACKED_ROWS = 16

TM = 256
QB = 256
KC = 256
V_ROWS = HEAD_DIM + 16
IDX_PAGES_PER_STEP = 32
ATTN_PAGES_PER_STEP = 32
THR_ROWS = 64
VMEM_LIMIT = 52 * 1024 * 1024

OFF_U, OFF_VA, OFF_GA = 0, 256, 512
OFF_Q, OFF_K, OFF_V, OFF_GB, OFF_QI = 768, 1280, 1792, 2304, 2816
OFF_QC, OFF_GC, OFF_KI2, OFF_WI, W_TOTAL = 3328, 3584, 3840, 3968, 4096


def _cparams(sem):
    return pltpu.CompilerParams(dimension_semantics=sem, vmem_limit_bytes=VMEM_LIMIT)


def _iota(shape, dim):
    return lax.broadcasted_iota(jnp.int32, shape, dim)


def _silu(x):
    return x / (1.0 + jnp.exp(-x))


def _dot(a, b):
    return jnp.dot(a, b, preferred_element_type=F32)


def _dot_nt(a, b):
    return lax.dot_general(a, b, (((1,), (1,)), ((), ())), preferred_element_type=F32)


def _key_to_float(k):
    bits = k ^ ((k >> 31) & jnp.int32(0x7FFFFFFF))
    return lax.bitcast_convert_type(bits, F32)


def _bitwise_kth(count_ge, kth, shape, bits):
    cnt0 = count_ge(jnp.zeros(shape, jnp.int32))
    ok0 = cnt0 >= kth
    key = jnp.where(ok0, jnp.int32(0), jnp.int32(-2 ** (bits - 1)))
    cnt_key = jnp.where(ok0, cnt0, jnp.float32(2 ** 24))

    def bit_body(it, carry):
        key, cnt_key = carry
        trial = key + lax.shift_left(jnp.int32(1), bits - 2 - it)
        cnt = count_ge(trial)
        ok = cnt >= kth
        return jnp.where(ok, trial, key), jnp.where(ok, cnt, cnt_key)

    return lax.fori_loop(0, bits - 1, bit_body, (key, cnt_key))


def _kth_largest(count_ge, kth, shape):
    key, cnt_key = _bitwise_kth(lambda k: count_ge(_key_to_float(k)), kth, shape, 32)
    return _key_to_float(key), cnt_key


def _rope(t, cos, sin):
    outs = []
    first = (_iota((t.shape[0], LANES), 1) % HEAD_DIM) < HALF
    for j in range(t.shape[1] // LANES):
        x = t[:, j * LANES:(j + 1) * LANES]
        partner = jnp.where(first, pltpu.roll(x, LANES - HALF, 1), pltpu.roll(x, HALF, 1))
        outs.append(x * cos + partner * sin)
    return outs[0] if len(outs) == 1 else jnp.concatenate(outs, axis=1)


def _proj_kernel(chunk, sample, x_ref, g_ref, w_ref, cos_ref, sin_ref, ws_ref, bs_ref, gv_ref, *outs):
    if sample:
        q_out, k_out, v_out, qi_out, ki_out, wi_out, sgb_out, mixa_out, qc_out, sgc_out, vn_out = outs
    else:
        (kt_out, vt_out, kit_out, kbf_out, vtb_out, ki2_out, qz_out, qiz_out, wt_out,
         sgb_out, mixa_out, qc_out, sgc_out) = outs
    tm = x_ref.shape[0]
    x = x_ref[...]
    ms = jnp.mean(x * x, axis=-1, keepdims=True)
    h = (x * lax.rsqrt(ms + EPS) * g_ref[...]).astype(BF16)
    cos = cos_ref[...]
    sin = sin_ref[...]

    def seg(a, b):
        return _dot(h, w_ref[:, a:b])

    half_id = (_iota((tm, LANES), 1) // HEAD_DIM)

    def head_split(t, out):
        for hd in range(B_HEADS):
            pair = t[:, (hd // 2) * LANES:(hd // 2 + 1) * LANES]
            out[hd] = jnp.where(half_id == hd % 2, pair, 0.0).astype(BF16)

    q = _rope(seg(OFF_Q, OFF_K), cos, sin)
    k = _rope(seg(OFF_K, OFF_V), cos, sin)
    v = seg(OFF_V, OFF_GB)
    qi = _rope(seg(OFF_QI, OFF_QC), cos, sin)
    ki2 = _rope(seg(OFF_KI2, OFF_WI), cos, sin)
    wi = seg(OFF_WI, W_TOTAL) * IDX_W_SCALE
    qc = seg(OFF_QC, OFF_GC) * ATTN_SCALE
    sgb_out[...] = _silu(seg(OFF_GB, OFF_QI)).astype(BF16)
    sgc_out[...] = _silu(seg(OFF_GC, OFF_KI2)).astype(BF16)
    qc_out[...] = qc.astype(qc_out.dtype)
    if sample:
        q_out[...] = q
        k_out[...] = k
        v_out[...] = v
        qi_out[...] = qi
        ki_out[...] = ki2[:, :IDX_DIM]
        wi_out[...] = wi[:, :IDX_HEADS]
    else:
        head_split(q * (ATTN_SCALE * LOG2E), qz_out)
        head_split(qi, qiz_out)
        kt_out[0] = k.T
        kbf_out[...] = k.astype(BF16)
        vt = v.T
        vt_out[0] = vt
        vtb = vt.astype(BF16)
        ones = jnp.ones((V_ROWS - HEAD_DIM, tm), BF16)
        for hd in range(B_HEADS):
            vtb_out[0, hd * V_ROWS:hd * V_ROWS + HEAD_DIM, :] = vtb[hd * HEAD_DIM:(hd + 1) * HEAD_DIM, :]
            vtb_out[0, hd * V_ROWS + HEAD_DIM:(hd + 1) * V_ROWS, :] = ones
        kit_out[0] = ki2.T[:IDX_DIM, :]
        ki2_out[...] = ki2.astype(BF16)
        wt_out[...] = wi.T[:IDX_HEADS, :]

    u = seg(OFF_U, OFF_VA)
    va = seg(OFF_VA, OFF_GA)
    ga = seg(OFF_GA, OFF_Q)
    grp_r = _iota((A_WIDTH, A_WIDTH), 0) // HEAD_DIM
    grp_c = _iota((A_WIDTH, A_WIDTH), 1) // HEAD_DIM
    gmat = jnp.where(grp_r == grp_c, 1.0 / HEAD_DIM, 0.0).astype(BF16)

    def group_mean(t):
        hi = t.astype(BF16)
        lo = (t - hi.astype(F32)).astype(BF16)
        return _dot(hi, gmat) + _dot(lo, gmat)

    mu = group_mean(va)
    d = va - mu
    var = group_mean(d * d)
    vn = d * lax.rsqrt(var + EPS) * gv_ref[...]
    if sample:
        vn_out[...] = vn
    vnb = vn.astype(BF16)
    row = _iota((CHUNK, CHUNK), 0)
    col = _iota((CHUNK, CHUNK), 1)
    causal = (row >= col) & ((row // chunk) == (col // chunk))
    lane_grp = _iota((CHUNK, A_WIDTH), 1) // HEAD_DIM
    wms = [jnp.where(causal, ws_ref[g], 0.0).astype(BF16) for g in range(A_GROUPS)]
    mixes = []
    for j in range(tm // CHUNK):
        vc = vnb[j * CHUNK:(j + 1) * CHUNK, :]
        m = bs_ref[...]
        for g in range(A_GROUPS):
            m = m + jnp.where(lane_grp == g, _dot(wms[g], vc), 0.0)
        mixes.append(m)
    mix = jnp.concatenate(mixes, axis=0)
    mixa_out[...] = (_silu(ga) * u * mix).astype(BF16)


def _proj(x, g_norm, w_r, cos_t, sin_t, ws, bs, g_v, chunk, seq_blocks, sample):
    n = x.shape[0]
    nt = n // TM
    nb = nt // seq_blocks
    row = lambda w: pl.BlockSpec((TM, w), lambda i: (i, 0))
    const2 = lambda a: pl.BlockSpec(a.shape, lambda i: (0, 0))
    sds = jax.ShapeDtypeStruct
    in_specs = [
        row(D_MODEL), const2(g_norm), const2(w_r),
        pl.BlockSpec((TM, LANES), lambda i: (i % seq_blocks, 0)),
        pl.BlockSpec((TM, LANES), lambda i: (i % seq_blocks, 0)),
        pl.BlockSpec(ws.shape, lambda i: (0, 0, 0)), const2(bs), const2(g_v),
    ]
    if sample:
        outs = [
            (sds((n, B_WIDTH), F32), row(B_WIDTH)),
            (sds((n, B_WIDTH), F32), row(B_WIDTH)),
            (sds((n, B_WIDTH), F32), row(B_WIDTH)),
            (sds((n, IDX_HEADS * IDX_DIM), F32), row(IDX_HEADS * IDX_DIM)),
            (sds((n, IDX_DIM), F32), row(IDX_DIM)),
            (sds((n, IDX_HEADS), F32), row(IDX_HEADS)),
            (sds((n, B_WIDTH), BF16), row(B_WIDTH)),
            (sds((n, A_WIDTH), BF16), row(A_WIDTH)),
            (sds((n, C_WIDTH), F32), row(C_WIDTH)),
            (sds((n, C_WIDTH), BF16), row(C_WIDTH)),
            (sds((n, A_WIDTH), F32), row(A_WIDTH)),
        ]
    else:
        seq = seq_blocks * TM
        tspec = lambda w: pl.BlockSpec((1, w, TM), lambda i: (i // seq_blocks, 0, i % seq_blocks))
        hspec = pl.BlockSpec((B_HEADS, TM, LANES), lambda i: (0, i, 0))
        outs = [
            (sds((nb, B_WIDTH, seq), F32), tspec(B_WIDTH)),
            (sds((nb, B_WIDTH, seq), F32), tspec(B_WIDTH)),
            (sds((nb, IDX_DIM, seq), F32), tspec(IDX_DIM)),
            (sds((n, B_WIDTH), BF16), row(B_WIDTH)),
            (sds((nt, B_HEADS * V_ROWS, TM), BF16),
             pl.BlockSpec((1, B_HEADS * V_ROWS, TM), lambda i: (i, 0, 0))),
            (sds((n, LANES), BF16), row(LANES)),
            (sds((B_HEADS, n, LANES), BF16), hspec),
            (sds((IDX_HEADS, n, LANES), BF16), hspec),
            (sds((IDX_HEADS, n), F32), pl.BlockSpec((IDX_HEADS, TM), lambda i: (0, i))),
            (sds((n, B_WIDTH), BF16), row(B_WIDTH)),
            (sds((n, A_WIDTH), BF16), row(A_WIDTH)),
            (sds((n, C_WIDTH), BF16), row(C_WIDTH)),
            (sds((n, C_WIDTH), BF16), row(C_WIDTH)),
        ]
    return pl.pallas_call(
        functools.partial(_proj_kernel, chunk, sample),
        grid=(nt,), in_specs=in_specs, out_specs=[o[1] for o in outs], out_shape=[o[0] for o in outs],
        compiler_params=_cparams(("parallel",)), name="proj",
    )(x, g_norm, w_r, cos_t, sin_t, ws, bs, g_v)


def _memkv_kernel(m_ref, g_ref, w_ref, mkt_out, mvt_out):
    x = m_ref[...]
    ms = jnp.mean(x * x, axis=-1, keepdims=True)
    h = (x * lax.rsqrt(ms + EPS) * g_ref[...]).astype(BF16)
    kv = _dot(h, w_ref[...])
    mkt_out[0] = kv[:, :C_WIDTH].T
    mvt_out[0] = kv[:, C_WIDTH:].T


def _memkv(mem, g_mem, w_kv):
    n = mem.shape[0]
    nb = n // N_MEM
    return pl.pallas_call(
        _memkv_kernel, grid=(nb,),
        in_specs=[pl.BlockSpec((N_MEM, D_MODEL), lambda i: (i, 0)),
                  pl.BlockSpec(g_mem.shape, lambda i: (0, 0)),
                  pl.BlockSpec(w_kv.shape, lambda i: (0, 0))],
        out_specs=[pl.BlockSpec((1, C_WIDTH, N_MEM), lambda i: (i, 0, 0))] * 2,
        out_shape=[jax.ShapeDtypeStruct((nb, C_WIDTH, N_MEM), F32)] * 2,
        compiler_params=_cparams(("parallel",)), name="memkv",
    )(mem, g_mem, w_kv)


def _memattn_kernel(q_ref, mkt_ref, mvt_ref, o_ref):
    q = q_ref[0]
    mkt = mkt_ref[0].astype(BF16)
    mvt = mvt_ref[0].astype(BF16)
    t = q.shape[0]
    lane_head = _iota((t, C_WIDTH), 1) // HEAD_DIM
    out = jnp.zeros((t, C_WIDTH), F32)
    for hd in range(C_HEADS):
        qh = jnp.where(lane_head == hd, q, 0.0).astype(BF16)
        s = _dot(qh, mkt)
        p = jnp.exp(s - jnp.max(s, axis=-1, keepdims=True))
        o = _dot_nt(p.astype(BF16), mvt) / jnp.sum(p, axis=-1, keepdims=True)
        out = out + jnp.where(lane_head == hd, o, 0.0)
    o_ref[0] = out.astype(o_ref.dtype)


def _memattn(q3, mkt3, mvt3, tiles_per_mem):
    g, t, _ = q3.shape
    mem_spec = pl.BlockSpec((1, C_WIDTH, N_MEM), lambda i: (i // tiles_per_mem, 0, 0))
    return pl.pallas_call(
        _memattn_kernel, grid=(g,),
        in_specs=[pl.BlockSpec((1, t, C_WIDTH), lambda i: (i, 0, 0)), mem_spec, mem_spec],
        out_specs=pl.BlockSpec((1, t, C_WIDTH), lambda i: (i, 0, 0)),
        out_shape=jax.ShapeDtypeStruct((g, t, C_WIDTH), BF16),
        compiler_params=_cparams(("parallel",)), name="memattn",
    )(q3, mkt3, mvt3)


def _memattn_sample_kernel(q_ref, mkt_ref, mvt_ref, o_ref):
    q4 = q_ref[0]
    mkt = mkt_ref[0].astype(BF16)
    mvt = mvt_ref[0].astype(BF16)
    t = q4.shape[0]
    mask8 = (_iota((SUBLANES, C_WIDTH), 1) // HEAD_DIM) == _iota((SUBLANES, C_WIDTH), 0)
    qbd = jnp.concatenate(
        [jnp.where(mask8, jnp.broadcast_to(q4[i:i + 1, :], (SUBLANES, C_WIDTH)), 0.0) for i in range(t)], axis=0)
    s = _dot(qbd.astype(BF16), mkt)
    p = jnp.exp(s - jnp.max(s, axis=-1, keepdims=True))
    o = _dot_nt(p.astype(BF16), mvt) / jnp.sum(p, axis=-1, keepdims=True)
    for i in range(t):
        slab = jnp.where(mask8, o[i * SUBLANES:(i + 1) * SUBLANES, :], 0.0)
        o_ref[0, i * SUBLANES:(i + 1) * SUBLANES, :] = jnp.broadcast_to(
            jnp.sum(slab, axis=0, keepdims=True), (SUBLANES, C_WIDTH))


def _memattn_sample(q3, mkt3, mvt3):
    g, t, _ = q3.shape
    mem_spec = pl.BlockSpec((1, C_WIDTH, N_MEM), lambda i: (i, 0, 0))
    return pl.pallas_call(
        _memattn_sample_kernel, grid=(g,),
        in_specs=[pl.BlockSpec((1, t, C_WIDTH), lambda i: (i, 0, 0)), mem_spec, mem_spec],
        out_specs=pl.BlockSpec((1, t * SUBLANES, C_WIDTH), lambda i: (i, 0, 0)),
        out_shape=jax.ShapeDtypeStruct((g, t * SUBLANES, C_WIDTH), F32),
        compiler_params=_cparams(("parallel",)), name="memattn_sample",
    )(q3, mkt3, mvt3)


def _dsa_prompt_kernel(topk, qz_ref, qiz_ref, wt_ref, ki2_ref, kbf_ref, vt_ref, o_ref,
                       sc_ref, acc_ref, m_ref, s_ref, sb_ref):
    i = pl.program_id(1)
    nk = i + 1
    npair = (nk + 1) // 2
    kth = jnp.float32(topk)

    def sum_rows(x):
        return jnp.sum(x.reshape(KC // SUBLANES, SUBLANES, QB), axis=0)

    def max_rows(x):
        return jnp.max(x.reshape(KC // SUBLANES, SUBLANES, QB), axis=0)

    def score_chunk(c, _):
        keys = ki2_ref[0, c]
        acc = jnp.zeros((KC, QB), F32)
        for hd in range(IDX_HEADS):
            y = _dot_nt(keys, qiz_ref[hd])
            acc = acc + jnp.maximum(y, 0.0) * wt_ref[hd:hd + 1, :]
        kpos = c * KC + _iota((KC, QB), 0)
        qpos = i * QB + _iota((KC, QB), 1)
        masked = jnp.where(kpos <= qpos, acc, -jnp.inf)
        sc_ref[c] = masked
        sb_ref[c] = masked.astype(BF16)
        return 0

    lax.fori_loop(0, nk, score_chunk, 0)

    @pl.when(nk % 2 == 1)
    def _():
        pad = jnp.minimum(nk, sc_ref.shape[0] - 1)
        sc_ref[pad] = jnp.full((KC, QB), -jnp.inf, F32)
        sb_ref[pad] = jnp.full((KC, QB), -jnp.inf, BF16)

    def count(pred):
        def body(j, cnt):
            return (cnt + sum_rows(pred(sc_ref[2 * j]).astype(F32))
                    + sum_rows(pred(sc_ref[2 * j + 1]).astype(F32)))
        return jnp.sum(lax.fori_loop(0, npair, body, jnp.zeros((SUBLANES, QB), F32)), axis=0, keepdims=True)

    def count_ge_packed(t):
        tb = t.astype(BF16)
        def slab_count(c):
            ones = jnp.where(sb_ref[c] >= tb, jnp.ones((), BF16), jnp.zeros((), BF16))
            acc = ones[0:PACKED_ROWS]
            for r in range(1, KC // PACKED_ROWS):
                acc = acc + ones[r * PACKED_ROWS:(r + 1) * PACKED_ROWS]
            return acc

        def body(j, cnt):
            return cnt + slab_count(2 * j) + slab_count(2 * j + 1)

        cnt16 = lax.fori_loop(0, npair, body, jnp.zeros((PACKED_ROWS, QB), BF16))
        return jnp.sum(cnt16.astype(F32), axis=0, keepdims=True)

    def write_ge(thr):
        def body(j, _):
            sc_ref[2 * j] = jnp.where(sc_ref[2 * j] >= thr, 0.0, NEG)
            sc_ref[2 * j + 1] = jnp.where(sc_ref[2 * j + 1] >= thr, 0.0, NEG)
            return 0
        lax.fori_loop(0, npair, body, 0)

    @pl.when(nk * KC <= topk)
    def _():
        write_ge(jnp.full((1, QB), -FLT_MAX, F32))

    @pl.when(nk * KC > topk)
    def _():
        def bf16_key_to_float_key(k16):
            return k16 * jnp.int32(2 ** 16) + jnp.where(k16 < 0, jnp.int32(2 ** 16 - 1), jnp.int32(0))

        tau_k16, _ = _bitwise_kth(
            lambda k16: count_ge_packed(_key_to_float(bf16_key_to_float_key(k16))), kth, (1, QB), 16)
        base = bf16_key_to_float_key(tau_k16) - jnp.int32(2 ** 16)

        def fine_bit(it, carry):
            off, cnt_off = carry
            cand = off + lax.shift_left(jnp.int32(1), 16 - it)
            cnt = count(lambda x: x >= _key_to_float(base + cand))
            ok = cnt >= kth
            return jnp.where(ok, cand, off), jnp.where(ok, cnt, cnt_off)

        off, cnt_thr = lax.fori_loop(0, 17, fine_bit,
                                     (jnp.zeros((1, QB), jnp.int32), jnp.full((1, QB), 2.0 ** 24, F32)))
        thr = _key_to_float(base + off)
        few = (i * QB + 1 + _iota((1, QB), 1)).astype(F32) <= kth
        thr = jnp.where(few, -FLT_MAX, thr)
        cnt_thr = jnp.where(few, kth, cnt_thr)
        has_tie = jnp.max(cnt_thr) > kth

        @pl.when(jnp.logical_not(has_tie))
        def _():
            write_ge(thr)

        @pl.when(has_tie)
        def _():
            need = kth - count(lambda x: x > thr)
            ltri = (_iota((KC, KC), 1) < _iota((KC, KC), 0)).astype(BF16)

            def body(c, carry):
                x = sc_ref[c]
                eq = x == thr
                before = _dot(ltri, eq.astype(BF16)) + carry
                sel = (x > thr) | (eq & (before < need))
                sc_ref[c] = jnp.where(sel, 0.0, NEG)
                return carry + jnp.sum(sum_rows(eq.astype(F32)), axis=0, keepdims=True)

            lax.fori_loop(0, nk, body, jnp.zeros((1, QB), F32))

    m_ref[...] = jnp.full((B_HEADS, QB), NEG, F32)
    acc_ref[...] = jnp.zeros((B_HEADS * V_ROWS, QB), F32)

    def biased_scores(c, hd):
        pair = hd // 2
        kc = kbf_ref[0, c, :, pair * LANES:(pair + 1) * LANES]
        return _dot_nt(kc, qz_ref[hd]) + sc_ref[c]

    for hd in range(B_HEADS):
        s_ref[hd] = biased_scores(0, hd)

    def attend(c, _):
        nxt = jnp.minimum(c + 1, nk - 1)
        for hd in range(B_HEADS):
            rows = slice(hd * V_ROWS, (hd + 1) * V_ROWS)
            s_next = biased_scores(nxt, hd)
            s = s_ref[hd]
            m_old = m_ref[hd:hd + 1, :]
            m_new = jnp.maximum(m_old, jnp.max(max_rows(s), axis=0, keepdims=True))
            alpha = jnp.exp2(m_old - m_new)
            p = jnp.exp2(s - m_new)
            acc_ref[rows, :] = alpha * acc_ref[rows, :] + _dot(vt_ref[0, c, rows, :], p.astype(BF16))
            m_ref[hd:hd + 1, :] = m_new
            s_ref[hd] = s_next
        return 0

    lax.fori_loop(0, nk, attend, 0)
    outs = []
    for hd in range(B_HEADS):
        base = hd * V_ROWS
        outs.append(acc_ref[base:base + HEAD_DIM, :] / acc_ref[base + HEAD_DIM:base + HEAD_DIM + 1, :])
    o_ref[...] = jnp.concatenate(outs, axis=0).T.astype(o_ref.dtype)


def _dsa_prompt(qz, qiz, wt, ki2, kbf, vtb, batch, seq):
    nq = seq // QB
    nc = seq // KC
    topk = min(TOPK_MAX, seq // 4)
    n = batch * seq
    qspec = pl.BlockSpec((B_HEADS, QB, LANES), lambda b, i: (0, b * nq + i, 0))
    return pl.pallas_call(
        functools.partial(_dsa_prompt_kernel, topk),
        grid=(batch, nq),
        in_specs=[qspec, qspec,
                  pl.BlockSpec((IDX_HEADS, QB), lambda b, i: (0, b * nq + i)),
                  pl.BlockSpec((1, nc, KC, LANES), lambda b, i: (b, 0, 0, 0)),
                  pl.BlockSpec((1, nc, KC, B_WIDTH), lambda b, i: (b, 0, 0, 0)),
                  pl.BlockSpec((1, nc, B_HEADS * V_ROWS, KC), lambda b, i: (b, 0, 0, 0))],
        out_specs=pl.BlockSpec((QB, B_WIDTH), lambda b, i: (b * nq + i, 0)),
        out_shape=jax.ShapeDtypeStruct((n, B_WIDTH), BF16),
        scratch_shapes=[pltpu.VMEM((nc, KC, QB), F32), pltpu.VMEM((B_HEADS * V_ROWS, QB), F32),
                        pltpu.VMEM((B_HEADS, QB), F32), pltpu.VMEM((B_HEADS, KC, QB), F32),
                        pltpu.VMEM((nc, KC, QB), BF16)],
        compiler_params=_cparams(("parallel", "arbitrary")), name="dsa_prompt",
    )(qz, qiz, wt, ki2.reshape(batch, nc, KC, LANES), kbf.reshape(batch, nc, KC, B_WIDTH),
      vtb.reshape(batch, nc, B_HEADS * V_ROWS, KC))


def _page_specs(block, n_pages_per_step):
    def make(r):
        return pl.BlockSpec(block, lambda b, j, pt: (pt[b, j * n_pages_per_step + r], 0, 0))
    return [make(r) for r in range(n_pages_per_step)]


def _idx_sample_kernel(t_new, pps, pt_ref, qi_ref, w_ref, kin_ref, *rest):
    pages = rest[:pps]
    sc_out, scn_out = rest[pps:]
    j = pl.program_id(1)
    qi = qi_ref[0]
    w = w_ref[0]
    qib = qi.astype(BF16)

    def token_scores(y):
        z = jnp.maximum(y, 0.0) * w
        return jnp.sum(z.reshape(t_new, IDX_HEADS, y.shape[1]), axis=1)

    for r in range(pps):
        y = _dot(qib, pages[r][0].astype(BF16))
        sc_out[0, :, r * PAGE_SIZE:(r + 1) * PAGE_SIZE] = token_scores(y)

    @pl.when(j == 0)
    def _():
        qf = qib.astype(F32)
        lane = _iota((t_new, LANES), 1)
        tok = _iota((t_new, LANES), 0)
        new = jnp.full((t_new, LANES), -jnp.inf, F32)
        for tk in range(t_new):
            kr = kin_ref[0, tk:tk + 1, :].astype(BF16).astype(F32)
            col = token_scores(jnp.sum(qf * kr, axis=-1, keepdims=True))
            new = jnp.where((lane == tk) & (tok >= tk), col, new)
        scn_out[0] = new


def _idx_sample(page_table, qi32, w32, ki_new, cache_idx_kt):
    db, rows, _ = qi32.shape
    t_new = rows // IDX_HEADS
    n_pages = page_table.shape[1]
    pps = min(IDX_PAGES_PER_STEP, n_pages)
    steps = n_pages // pps
    span = pps * PAGE_SIZE
    grid_spec = pltpu.PrefetchScalarGridSpec(
        num_scalar_prefetch=1, grid=(db, steps),
        in_specs=[pl.BlockSpec((1, rows, IDX_DIM), lambda b, j, pt: (b, 0, 0)),
                  pl.BlockSpec((1, rows, 1), lambda b, j, pt: (b, 0, 0)),
                  pl.BlockSpec((1, t_new, IDX_DIM), lambda b, j, pt: (b, 0, 0))]
        + _page_specs((1, IDX_DIM, PAGE_SIZE), pps),
        out_specs=[pl.BlockSpec((1, t_new, span), lambda b, j, pt: (b, 0, j)),
                   pl.BlockSpec((1, t_new, LANES), lambda b, j, pt: (b, 0, 0))])
    return pl.pallas_call(
        functools.partial(_idx_sample_kernel, t_new, pps), grid_spec=grid_spec,
        out_shape=[jax.ShapeDtypeStruct((db, t_new, n_pages * PAGE_SIZE), F32),
                   jax.ShapeDtypeStruct((db, t_new, LANES), F32)],
        compiler_params=_cparams(("parallel", "arbitrary")), name="idx_sample",
    )(page_table, qi32, w32, ki_new, *([cache_idx_kt] * pps))


def _thr_sample_kernel(topk, sc_ref, scn_ref, b_out, bn_out):
    kth = jnp.float32(topk)
    n_chunks = sc_ref.shape[1] // LANES

    def count(pred):
        return (jnp.sum(pred(sc_ref[...]).astype(F32), axis=-1, keepdims=True)
                + jnp.sum(pred(scn_ref[...]).astype(F32), axis=-1, keepdims=True))

    thr, cnt_thr = _kth_largest(lambda t: count(lambda x: x >= t), kth, (THR_ROWS, 1))
    has_tie = jnp.max(cnt_thr) > kth

    @pl.when(jnp.logical_not(has_tie))
    def _():
        b_out[...] = jnp.where(sc_ref[...] >= thr, 0.0, NEG)
        bn_out[...] = jnp.where(scn_ref[...] >= thr, 0.0, NEG)

    @pl.when(has_tie)
    def _():
        need = kth - count(lambda x: x > thr)
        utri = (_iota((LANES, LANES), 0) < _iota((LANES, LANES), 1)).astype(BF16)
        carry = jnp.zeros((THR_ROWS, 1), F32)
        for c in range(n_chunks + 1):
            x = sc_ref[:, c * LANES:(c + 1) * LANES] if c < n_chunks else scn_ref[...]
            eq = x == thr
            before = _dot(eq.astype(BF16), utri) + carry
            sel = (x > thr) | (eq & (before < need))
            bias = jnp.where(sel, 0.0, NEG)
            if c < n_chunks:
                b_out[:, c * LANES:(c + 1) * LANES] = bias
            else:
                bn_out[...] = bias
            carry = carry + jnp.sum(eq.astype(F32), axis=-1, keepdims=True)


def _thr_sample(sc, scn, topk):
    n, width = sc.shape
    return pl.pallas_call(
        functools.partial(_thr_sample_kernel, topk), grid=(n // THR_ROWS,),
        in_specs=[pl.BlockSpec((THR_ROWS, width), lambda i: (i, 0)),
                  pl.BlockSpec((THR_ROWS, LANES), lambda i: (i, 0))],
        out_specs=[pl.BlockSpec((THR_ROWS, width), lambda i: (i, 0)),
                   pl.BlockSpec((THR_ROWS, LANES), lambda i: (i, 0))],
        out_shape=[jax.ShapeDtypeStruct((n, width), F32), jax.ShapeDtypeStruct((n, LANES), F32)],
        compiler_params=_cparams(("parallel",)), name="thr_sample",
    )(sc, scn)


def _attn_sample_kernel(t_new, pps, pt_ref, q_ref, bias_ref, biasn_ref, kn_ref, vn_ref, *rest):
    kpages = rest[:pps]
    vpages = rest[pps:2 * pps]
    o_ref, m_ref, l_ref, acc_ref, qbd_ref = rest[2 * pps:]
    j = pl.program_id(1)
    rows = t_new * B_HEADS
    mask8 = (_iota((SUBLANES, B_WIDTH), 1) // HEAD_DIM) == _iota((SUBLANES, B_WIDTH), 0)

    def per_token(ref, width):
        return jnp.concatenate(
            [jnp.broadcast_to(ref[0, i:i + 1, :], (B_HEADS, width)) for i in range(t_new)], axis=0)

    @pl.when(j == 0)
    def _():
        m_ref[...] = jnp.full((rows, 1), NEG, F32)
        l_ref[...] = jnp.zeros((rows, 1), F32)
        acc_ref[...] = jnp.zeros((rows, B_WIDTH), F32)
        qbd_ref[...] = jnp.concatenate(
            [jnp.where(mask8, jnp.broadcast_to(q_ref[0, i:i + 1, :] * ATTN_SCALE, (B_HEADS, B_WIDTH)), 0.0)
             for i in range(t_new)], axis=0).astype(BF16)

    qbd = qbd_ref[...]
    span = pps * PAGE_SIZE
    s = jnp.concatenate([_dot(qbd, kpages[r][0].astype(BF16)) for r in range(pps)], axis=1)
    s = s + per_token(bias_ref, span)
    m_old = m_ref[...]
    m_new = jnp.maximum(m_old, jnp.max(s, axis=-1, keepdims=True))
    alpha = jnp.exp(m_old - m_new)
    p = jnp.exp(s - m_new)
    l_ref[...] = alpha * l_ref[...] + jnp.sum(p, axis=-1, keepdims=True)
    pb = p.astype(BF16)
    pv = jnp.zeros((rows, B_WIDTH), F32)
    for r in range(pps):
        pv = pv + _dot_nt(pb[:, r * PAGE_SIZE:(r + 1) * PAGE_SIZE], vpages[r][0].astype(BF16))
    acc_ref[...] = alpha * acc_ref[...] + pv
    m_ref[...] = m_new

    @pl.when(j == pl.num_programs(1) - 1)
    def _():
        qf = qbd.astype(F32)
        bn = per_token(biasn_ref, LANES)
        cols = []
        for tk in range(t_new):
            kr = kn_ref[0, tk:tk + 1, :].astype(BF16).astype(F32)
            cols.append(jnp.sum(qf * kr, axis=-1, keepdims=True) + bn[:, tk:tk + 1])
        m_old = m_ref[...]
        m_new = m_old
        for cval in cols:
            m_new = jnp.maximum(m_new, cval)
        alpha = jnp.exp(m_old - m_new)
        l = alpha * l_ref[...]
        acc = alpha * acc_ref[...]
        for tk in range(t_new):
            pk = jnp.exp(cols[tk] - m_new)
            l = l + pk
            acc = acc + pk.astype(BF16).astype(F32) * vn_ref[0, tk:tk + 1, :].astype(BF16).astype(F32)
        o = acc / l
        for i in range(t_new):
            slab = jnp.where(mask8, o[i * B_HEADS:(i + 1) * B_HEADS, :], 0.0)
            o_ref[0, i * B_HEADS:(i + 1) * B_HEADS, :] = jnp.broadcast_to(
                jnp.sum(slab, axis=0, keepdims=True), (B_HEADS, B_WIDTH))


def _attn_sample(page_table, q3, bias3, biasn3, k_new, v_new, cache_kt, cache_vt):
    db, t_new, _ = q3.shape
    n_pages = page_table.shape[1]
    pps = min(ATTN_PAGES_PER_STEP, n_pages)
    steps = n_pages // pps
    span = pps * PAGE_SIZE
    rows = t_new * B_HEADS
    per_b = lambda w: pl.BlockSpec((1, t_new, w), lambda b, j, pt: (b, 0, 0))
    grid_spec = pltpu.PrefetchScalarGridSpec(
        num_scalar_prefetch=1, grid=(db, steps),
        in_specs=[per_b(B_WIDTH),
                  pl.BlockSpec((1, t_new, span), lambda b, j, pt: (b, 0, j)),
                  per_b(LANES), per_b(B_WIDTH), per_b(B_WIDTH)]
        + _page_specs((1, B_WIDTH, PAGE_SIZE), pps)
        + _page_specs((1, B_WIDTH, PAGE_SIZE), pps),
        out_specs=pl.BlockSpec((1, rows, B_WIDTH), lambda b, j, pt: (b, 0, 0)),
        scratch_shapes=[pltpu.VMEM((rows, 1), F32), pltpu.VMEM((rows, 1), F32),
                        pltpu.VMEM((rows, B_WIDTH), F32), pltpu.VMEM((rows, B_WIDTH), BF16)])
    return pl.pallas_call(
        functools.partial(_attn_sample_kernel, t_new, pps), grid_spec=grid_spec,
        out_shape=jax.ShapeDtypeStruct((db, rows, B_WIDTH), F32),
        compiler_params=_cparams(("parallel", "arbitrary")), name="attn_sample",
    )(page_table, q3, bias3, biasn3, k_new, v_new, *([cache_kt] * pps), *([cache_vt] * pps))


def _merge_kernel(x_ref, ma_ref, yb_ref, sgb_ref, yc_ref, sgc_ref, wo_ref, g_ref, y_out):
    mb = (yb_ref[...].astype(F32) * sgb_ref[...].astype(F32)).astype(BF16)
    mc = (yc_ref[...].astype(F32) * sgc_ref[...].astype(F32)).astype(BF16)
    o = (_dot(ma_ref[...], wo_ref[0:A_WIDTH, :])
         + _dot(mb, wo_ref[A_WIDTH:A_WIDTH + B_WIDTH, :])
         + _dot(mc, wo_ref[A_WIDTH + B_WIDTH:, :]))
    z = x_ref[...] + o
    ms = jnp.mean(z * z, axis=-1, keepdims=True)
    y_out[...] = z * lax.rsqrt(ms + EPS) * g_ref[...]


def _merge(x, mixa, yb, sgb, yc, sgc, wo, g_final):
    n = x.shape[0]
    row = lambda w: pl.BlockSpec((TM, w), lambda i: (i, 0))
    return pl.pallas_call(
        _merge_kernel, grid=(n // TM,),
        in_specs=[row(D_MODEL), row(A_WIDTH), row(B_WIDTH), row(B_WIDTH), row(C_WIDTH), row(C_WIDTH),
                  pl.BlockSpec(wo.shape, lambda i: (0, 0)), pl.BlockSpec(g_final.shape, lambda i: (0, 0))],
        out_specs=row(D_MODEL),
        out_shape=jax.ShapeDtypeStruct((n, D_MODEL), F32),
        compiler_params=_cparams(("parallel",)), name="merge",
    )(x, mixa, yb, sgb, yc, sgc, wo, g_final)


def _rope_tables(pos):
    inv = ROPE_THETA ** (-jnp.arange(0, HEAD_DIM, 2, dtype=F32) / HEAD_DIM)
    ang = pos.astype(F32)[:, None] * inv[None, :]
    c = jnp.cos(ang)
    s = jnp.sin(ang)
    return jnp.concatenate([c, c, c, c], axis=1), jnp.concatenate([-s, s, -s, s], axis=1)


def _heads_last(t, heads):
    b, _, n = t.shape
    return t.reshape(b, heads, HEAD_DIM, n).transpose(0, 3, 1, 2)


def _channels_first(t):
    p, n, heads, d = t.shape
    return t.transpose(0, 2, 3, 1).reshape(p, heads * d, n)


def kernel(x_prompt, x_sample, mem_prompt, cache_k, cache_v, cache_idx_k, cache_mem_k, cache_mem_v,
           page_table, g_norm, w_in, w_spatial, b_spatial, g_v, w_mem_kv, g_mem, w_out, g_final):
    batch, seq, _ = x_prompt.shape
    db, t_new, _ = x_sample.shape
    n_pages = page_table.shape[1]
    past = n_pages * PAGE_SIZE

    ki_cols = w_in[:, 3328:3392]
    wi_cols = w_in[:, 3392:3400]
    w_r = jnp.concatenate(
        [w_in[:, :3328], w_in[:, 3400:3912], ki_cols, ki_cols, wi_cols,
         jnp.zeros((D_MODEL, LANES - IDX_HEADS), F32)], axis=1).astype(BF16)
    g_norm2 = g_norm.reshape(1, D_MODEL)
    g_v2 = g_v.reshape(1, A_WIDTH)
    bs_prompt = jnp.repeat(b_spatial.T, HEAD_DIM, axis=1)
    reps = CHUNK // t_new
    ws_sample = jnp.tile(w_spatial[:, :t_new, :t_new], (1, reps, reps))
    bs_sample = jnp.tile(jnp.repeat(b_spatial[:, :t_new].T, HEAD_DIM, axis=1), (reps, 1))
    wo_b = w_out.astype(BF16)
    g_final2 = g_final.reshape(1, D_MODEL)

    cos_p, sin_p = _rope_tables(jnp.arange(seq, dtype=jnp.int32))
    pos_s = past + jnp.arange(t_new, dtype=jnp.int32)
    cos_s, sin_s = _rope_tables(jnp.tile(pos_s, TM // t_new))

    xp = x_prompt.reshape(batch * seq, D_MODEL)
    (kt, vt, kit, kbf, vtb, ki2, qz, qiz, wt, sgb, mixa, qc, sgc) = _proj(
        xp, g_norm2, w_r, cos_p, sin_p, w_spatial, bs_prompt, g_v2, CHUNK, seq // TM, False)
    mkt, mvt = _memkv(mem_prompt.reshape(batch * N_MEM, D_MODEL), g_mem.reshape(1, D_MODEL),
                      w_mem_kv.astype(BF16))
    yc = _memattn(qc.reshape(batch * seq // TM, TM, C_WIDTH), mkt, mvt, seq // TM).reshape(batch * seq, C_WIDTH)
    yb = _dsa_prompt(qz, qiz, wt, ki2, kbf, vtb, batch, seq)
    y_prompt = _merge(xp, mixa, yb, sgb, yc, sgc, wo_b, g_final2).reshape(batch, seq, D_MODEL)

    ns = db * t_new
    xs = x_sample.reshape(ns, D_MODEL)
    (q_s, k_s, v_s, qi_s, ki_s, wi_s, sgb_s, mixa_s, qc_s, sgc_s, vn_s) = _proj(
        xs, g_norm2, w_r, cos_s, sin_s, ws_sample, bs_sample, g_v2, t_new, 1, True)
    sc, scn = _idx_sample(page_table, qi_s.reshape(db, t_new * IDX_HEADS, IDX_DIM),
                          wi_s.reshape(db, t_new * IDX_HEADS, 1), ki_s.reshape(db, t_new, IDX_DIM),
                          cache_idx_k.transpose(0, 2, 1))
    topk_s = min(TOPK_MAX, (past + t_new) // 4)
    bias, biasn = _thr_sample(sc.reshape(ns, past), scn.reshape(ns, LANES), topk_s)
    yb_pad = _attn_sample(page_table, q_s.reshape(db, t_new, B_WIDTH), bias.reshape(db, t_new, past),
                          biasn.reshape(db, t_new, LANES), k_s.reshape(db, t_new, B_WIDTH),
                          v_s.reshape(db, t_new, B_WIDTH), _channels_first(cache_k), _channels_first(cache_v))
    yb_s = yb_pad[:, ::B_HEADS, :].reshape(ns, B_WIDTH)
    yc_pad = _memattn_sample(qc_s.reshape(db, t_new, C_WIDTH),
                             _channels_first(cache_mem_k), _channels_first(cache_mem_v))
    yc_s = yc_pad[:, ::SUBLANES, :].reshape(ns, C_WIDTH)
    y_sample = _merge(xs, mixa_s, yb_s, sgb_s, yc_s, sgc_s, wo_b, g_final2).reshape(db, t_new, D_MODEL)

    return (y_prompt, y_sample,
            _heads_last(kt, B_HEADS), _heads_last(vt, B_HEADS), kit.transpose(0, 2, 1),
            _heads_last(mkt, C_HEADS), _heads_last(mvt, C_HEADS),
            k_s.reshape(db, t_new, B_HEADS, HEAD_DIM), v_s.reshape(db, t_new, B_HEADS, HEAD_DIM),
            ki_s.reshape(db, t_new, IDX_DIM), vn_s.reshape(db, t_new, A_WIDTH))
```

```python
import functools

import jax
import jax.numpy as jnp
from jax import lax
from jax.experimental import pallas as pl
from jax.experimental.pallas import tpu as pltpu

F32 = jnp.float32
BF16 = jnp.bfloat16

HEAD_DIM = 64
HALF = HEAD_DIM // 2
D_MODEL = 1024
A_GROUPS = 4
A_WIDTH = A_GROUPS * HEAD_DIM
CHUNK = 128
B_HEADS = 8
B_WIDTH = B_HEADS * HEAD_DIM
IDX_HEADS = 8
IDX_DIM = 64
TOPK_MAX = 256
N_MEM = 256
C_HEADS = 4
C_WIDTH = C_HEADS * HEAD_DIM
PAGE_SIZE = 128
ROPE_THETA = 10000.0
EPS = 1e-6
IDX_W_SCALE = (IDX_HEADS * IDX_DIM) ** -0.5
ATTN_SCALE = HEAD_DIM ** -0.5
LOG2E = 1.4426950408889634

LANES = 128
SUBLANES = 8
NEG = -1e30
FLT_MAX = 3.4028234663852886e38
PACKED_ROWS = 16

TM = 256
MERGE_TM = 512
QB = 256
KC = 256
V_ROWS = HEAD_DIM + 16
QK_LEAD = 3
IDX_PAGES_PER_STEP = 32
ATTN_PAGES_PER_STEP = 32
THR_ROWS = 64
VMEM_LIMIT = 52 * 1024 * 1024

OFF_U, OFF_VA, OFF_GA = 0, 256, 512
OFF_Q, OFF_K, OFF_V, OFF_GB, OFF_QI = 768, 1280, 1792, 2304, 2816
OFF_QC, OFF_GC, OFF_KI2, OFF_WI, W_TOTAL = 3328, 3584, 3840, 3968, 4096


def _cparams(sem):
    return pltpu.CompilerParams(dimension_semantics=sem, vmem_limit_bytes=VMEM_LIMIT)


def _iota(shape, dim):
    return lax.broadcasted_iota(jnp.int32, shape, dim)


def _silu(x):
    return x / (1.0 + jnp.exp(-x))


def _dot(a, b):
    return jnp.dot(a, b, preferred_element_type=F32)


def _dot_nt(a, b):
    return lax.dot_general(a, b, (((1,), (1,)), ((), ())), preferred_element_type=F32)


def _key_to_float(k):
    bits = k ^ ((k >> 31) & jnp.int32(0x7FFFFFFF))
    return lax.bitcast_convert_type(bits, F32)


def _bitwise_kth(count_ge, kth, shape, bits):
    cnt0 = count_ge(jnp.zeros(shape, jnp.int32))
    ok0 = cnt0 >= kth
    key = jnp.where(ok0, jnp.int32(0), jnp.int32(-2 ** (bits - 1)))
    cnt_key = jnp.where(ok0, cnt0, jnp.float32(2 ** 24))

    def bit_body(it, carry):
        key, cnt_key = carry
        trial = key + lax.shift_left(jnp.int32(1), bits - 2 - it)
        cnt = count_ge(trial)
        ok = cnt >= kth
        return jnp.where(ok, trial, key), jnp.where(ok, cnt, cnt_key)

    return lax.fori_loop(0, bits - 1, bit_body, (key, cnt_key))


def _kth_largest(count_ge, kth, shape):
    key, cnt_key = _bitwise_kth(lambda k: count_ge(_key_to_float(k)), kth, shape, 32)
    return _key_to_float(key), cnt_key


def _rope(t, cos, sin):
    outs = []
    first = (_iota((t.shape[0], LANES), 1) % HEAD_DIM) < HALF
    for j in range(t.shape[1] // LANES):
        x = t[:, j * LANES:(j + 1) * LANES]
        partner = jnp.where(first, pltpu.roll(x, LANES - HALF, 1), pltpu.roll(x, HALF, 1))
        outs.append(x * cos + partner * sin)
    return outs[0] if len(outs) == 1 else jnp.concatenate(outs, axis=1)


def _proj_kernel(chunk, sample, x_ref, g_ref, w_ref, cos_ref, sin_ref, ws_ref, bs_ref, gv_ref, *outs):
    if sample:
        q_out, k_out, v_out, qi_out, ki_out, wi_out, sgb_out, mixa_out, qc_out, sgc_out, vn_out = outs
    else:
        (kt_out, vt_out, kit_out, kbf_out, vtb_out, ki2_out, qz_out, qiz_out, wt_out,
         sgb_out, mixa_out, qc_out, sgc_out) = outs
    tm = x_ref.shape[0]
    x = x_ref[...]
    ms = jnp.mean(x * x, axis=-1, keepdims=True)
    h = (x * lax.rsqrt(ms + EPS) * g_ref[...]).astype(BF16)
    cos = cos_ref[...]
    sin = sin_ref[...]

    def seg(a, b):
        return _dot(h, w_ref[:, a:b])

    half_id = (_iota((tm, LANES), 1) // HEAD_DIM)

    def head_split(t, out):
        for hd in range(B_HEADS):
            pair = t[:, (hd // 2) * LANES:(hd // 2 + 1) * LANES]
            out[hd] = jnp.where(half_id == hd % 2, pair, 0.0).astype(BF16)

    q = _rope(seg(OFF_Q, OFF_K), cos, sin)
    k = _rope(seg(OFF_K, OFF_V), cos, sin)
    v = seg(OFF_V, OFF_GB)
    qi = _rope(seg(OFF_QI, OFF_QC), cos, sin)
    ki2 = _rope(seg(OFF_KI2, OFF_WI), cos, sin)
    wi = seg(OFF_WI, W_TOTAL) * IDX_W_SCALE
    qc = seg(OFF_QC, OFF_GC) * ATTN_SCALE
    sgb_out[...] = _silu(seg(OFF_GB, OFF_QI)).astype(BF16)
    sgc_out[...] = _silu(seg(OFF_GC, OFF_KI2)).astype(BF16)
    qc_out[...] = qc.astype(qc_out.dtype)
    if sample:
        q_out[...] = q
        k_out[...] = k
        v_out[...] = v
        qi_out[...] = qi
        ki_out[...] = ki2[:, :IDX_DIM]
        wi_out[...] = wi[:, :IDX_HEADS]
    else:
        head_split(q * (ATTN_SCALE * LOG2E), qz_out)
        head_split(qi, qiz_out)
        kt_out[0] = k.T
        kbf_out[...] = k.astype(BF16)
        vt = v.T
        vt_out[0] = vt
        vtb = vt.astype(BF16)
        ones = jnp.ones((V_ROWS - HEAD_DIM, tm), BF16)
        for hd in range(B_HEADS):
            vtb_out[0, hd * V_ROWS:hd * V_ROWS + HEAD_DIM, :] = vtb[hd * HEAD_DIM:(hd + 1) * HEAD_DIM, :]
            vtb_out[0, hd * V_ROWS + HEAD_DIM:(hd + 1) * V_ROWS, :] = ones
        kit_out[0] = ki2.T[:IDX_DIM, :]
        ki2_out[...] = ki2.astype(BF16)
        wt_out[...] = wi.T[:IDX_HEADS, :]

    u = seg(OFF_U, OFF_VA)
    va = seg(OFF_VA, OFF_GA)
    ga = seg(OFF_GA, OFF_Q)
    grp_r = _iota((A_WIDTH, A_WIDTH), 0) // HEAD_DIM
    grp_c = _iota((A_WIDTH, A_WIDTH), 1) // HEAD_DIM
    gmat = jnp.where(grp_r == grp_c, 1.0 / HEAD_DIM, 0.0).astype(BF16)

    def group_mean(t):
        hi = t.astype(BF16)
        lo = (t - hi.astype(F32)).astype(BF16)
        return _dot(hi, gmat) + _dot(lo, gmat)

    mu = group_mean(va)
    d = va - mu
    var = group_mean(d * d)
    vn = d * lax.rsqrt(var + EPS) * gv_ref[...]
    if sample:
        vn_out[...] = vn
    vnb = vn.astype(BF16)
    row = _iota((CHUNK, CHUNK), 0)
    col = _iota((CHUNK, CHUNK), 1)
    causal = (row >= col) & ((row // chunk) == (col // chunk))
    lane_grp = _iota((CHUNK, A_WIDTH), 1) // HEAD_DIM
    wms = [jnp.where(causal, ws_ref[g], 0.0).astype(BF16) for g in range(A_GROUPS)]
    mixes = []
    for j in range(tm // CHUNK):
        vc = vnb[j * CHUNK:(j + 1) * CHUNK, :]
        m = bs_ref[...]
        for g in range(A_GROUPS):
            m = m + jnp.where(lane_grp == g, _dot(wms[g], vc), 0.0)
        mixes.append(m)
    mix = jnp.concatenate(mixes, axis=0)
    mixa_out[...] = (_silu(ga) * u * mix).astype(BF16)


def _proj(x, g_norm, w_r, cos_t, sin_t, ws, bs, g_v, chunk, seq_blocks, sample):
    n = x.shape[0]
    nt = n // TM
    nb = nt // seq_blocks
    row = lambda w: pl.BlockSpec((TM, w), lambda i: (i, 0))
    const2 = lambda a: pl.BlockSpec(a.shape, lambda i: (0, 0))
    sds = jax.ShapeDtypeStruct
    in_specs = [
        row(D_MODEL), const2(g_norm), const2(w_r),
        pl.BlockSpec((TM, LANES), lambda i: (i % seq_blocks, 0)),
        pl.BlockSpec((TM, LANES), lambda i: (i % seq_blocks, 0)),
        pl.BlockSpec(ws.shape, lambda i: (0, 0, 0)), const2(bs), const2(g_v),
    ]
    if sample:
        outs = [
            (sds((n, B_WIDTH), F32), row(B_WIDTH)),
            (sds((n, B_WIDTH), F32), row(B_WIDTH)),
            (sds((n, B_WIDTH), F32), row(B_WIDTH)),
            (sds((n, IDX_HEADS * IDX_DIM), F32), row(IDX_HEADS * IDX_DIM)),
            (sds((n, IDX_DIM), F32), row(IDX_DIM)),
            (sds((n, IDX_HEADS), F32), row(IDX_HEADS)),
            (sds((n, B_WIDTH), BF16), row(B_WIDTH)),
            (sds((n, A_WIDTH), BF16), row(A_WIDTH)),
            (sds((n, C_WIDTH), F32), row(C_WIDTH)),
            (sds((n, C_WIDTH), BF16), row(C_WIDTH)),
            (sds((n, A_WIDTH), F32), row(A_WIDTH)),
        ]
    else:
        seq = seq_blocks * TM
        tspec = lambda w: pl.BlockSpec((1, w, TM), lambda i: (i // seq_blocks, 0, i % seq_blocks))
        hspec = pl.BlockSpec((B_HEADS, TM, LANES), lambda i: (0, i, 0))
        outs = [
            (sds((nb, B_WIDTH, seq), F32), tspec(B_WIDTH)),
            (sds((nb, B_WIDTH, seq), F32), tspec(B_WIDTH)),
            (sds((nb, IDX_DIM, seq), F32), tspec(IDX_DIM)),
            (sds((n, B_WIDTH), BF16), row(B_WIDTH)),
            (sds((nt, B_HEADS * V_ROWS, TM), BF16),
             pl.BlockSpec((1, B_HEADS * V_ROWS, TM), lambda i: (i, 0, 0))),
            (sds((n, LANES), BF16), row(LANES)),
            (sds((B_HEADS, n, LANES), BF16), hspec),
            (sds((IDX_HEADS, n, LANES), BF16), hspec),
            (sds((IDX_HEADS, n), F32), pl.BlockSpec((IDX_HEADS, TM), lambda i: (0, i))),
            (sds((n, B_WIDTH), BF16), row(B_WIDTH)),
            (sds((n, A_WIDTH), BF16), row(A_WIDTH)),
            (sds((n, C_WIDTH), BF16), row(C_WIDTH)),
            (sds((n, C_WIDTH), BF16), row(C_WIDTH)),
        ]
    return pl.pallas_call(
        functools.partial(_proj_kernel, chunk, sample),
        grid=(nt,), in_specs=in_specs, out_specs=[o[1] for o in outs], out_shape=[o[0] for o in outs],
        compiler_params=_cparams(("parallel",)), name="proj",
    )(x, g_norm, w_r, cos_t, sin_t, ws, bs, g_v)


def _memkv_kernel(m_ref, g_ref, w_ref, mkt_out, mvt_out):
    x = m_ref[...]
    ms = jnp.mean(x * x, axis=-1, keepdims=True)
    h = (x * lax.rsqrt(ms + EPS) * g_ref[...]).astype(BF16)
    kv = _dot(h, w_ref[...])
    mkt_out[0] = kv[:, :C_WIDTH].T
    mvt_out[0] = kv[:, C_WIDTH:].T


def _memkv(mem, g_mem, w_kv):
    n = mem.shape[0]
    nb = n // N_MEM
    return pl.pallas_call(
        _memkv_kernel, grid=(nb,),
        in_specs=[pl.BlockSpec((N_MEM, D_MODEL), lambda i: (i, 0)),
                  pl.BlockSpec(g_mem.shape, lambda i: (0, 0)),
                  pl.BlockSpec(w_kv.shape, lambda i: (0, 0))],
        out_specs=[pl.BlockSpec((1, C_WIDTH, N_MEM), lambda i: (i, 0, 0))] * 2,
        out_shape=[jax.ShapeDtypeStruct((nb, C_WIDTH, N_MEM), F32)] * 2,
        compiler_params=_cparams(("parallel",)), name="memkv",
    )(mem, g_mem, w_kv)


def _memattn_kernel(q_ref, mkt_ref, mvt_ref, o_ref):
    q = q_ref[0]
    mkt = mkt_ref[0].astype(BF16)
    mvt = mvt_ref[0].astype(BF16)
    t = q.shape[0]
    lane_head = _iota((t, C_WIDTH), 1) // HEAD_DIM
    out = jnp.zeros((t, C_WIDTH), F32)
    for hd in range(C_HEADS):
        qh = jnp.where(lane_head == hd, q, 0.0).astype(BF16)
        s = _dot(qh, mkt)
        p = jnp.exp(s - jnp.max(s, axis=-1, keepdims=True))
        o = _dot_nt(p.astype(BF16), mvt) / jnp.sum(p, axis=-1, keepdims=True)
        out = out + jnp.where(lane_head == hd, o, 0.0)
    o_ref[0] = out.astype(o_ref.dtype)


def _memattn(q3, mkt3, mvt3, tiles_per_mem):
    g, t, _ = q3.shape
    mem_spec = pl.BlockSpec((1, C_WIDTH, N_MEM), lambda i: (i // tiles_per_mem, 0, 0))
    return pl.pallas_call(
        _memattn_kernel, grid=(g,),
        in_specs=[pl.BlockSpec((1, t, C_WIDTH), lambda i: (i, 0, 0)), mem_spec, mem_spec],
        out_specs=pl.BlockSpec((1, t, C_WIDTH), lambda i: (i, 0, 0)),
        out_shape=jax.ShapeDtypeStruct((g, t, C_WIDTH), BF16),
        compiler_params=_cparams(("parallel",)), name="memattn",
    )(q3, mkt3, mvt3)


def _memattn_sample_kernel(q_ref, mkt_ref, mvt_ref, o_ref):
    q4 = q_ref[0]
    mkt = mkt_ref[0].astype(BF16)
    mvt = mvt_ref[0].astype(BF16)
    t = q4.shape[0]
    mask8 = (_iota((SUBLANES, C_WIDTH), 1) // HEAD_DIM) == _iota((SUBLANES, C_WIDTH), 0)
    qbd = jnp.concatenate(
        [jnp.where(mask8, jnp.broadcast_to(q4[i:i + 1, :], (SUBLANES, C_WIDTH)), 0.0) for i in range(t)], axis=0)
    s = _dot(qbd.astype(BF16), mkt)
    p = jnp.exp(s - jnp.max(s, axis=-1, keepdims=True))
    o = _dot_nt(p.astype(BF16), mvt) / jnp.sum(p, axis=-1, keepdims=True)
    for i in range(t):
        slab = jnp.where(mask8, o[i * SUBLANES:(i + 1) * SUBLANES, :], 0.0)
        o_ref[0, i * SUBLANES:(i + 1) * SUBLANES, :] = jnp.broadcast_to(
            jnp.sum(slab, axis=0, keepdims=True), (SUBLANES, C_WIDTH))


def _memattn_sample(q3, mkt3, mvt3):
    g, t, _ = q3.shape
    mem_spec = pl.BlockSpec((1, C_WIDTH, N_MEM), lambda i: (i, 0, 0))
    return pl.pallas_call(
        _memattn_sample_kernel, grid=(g,),
        in_specs=[pl.BlockSpec((1, t, C_WIDTH), lambda i: (i, 0, 0)), mem_spec, mem_spec],
        out_specs=pl.BlockSpec((1, t * SUBLANES, C_WIDTH), lambda i: (i, 0, 0)),
        out_shape=jax.ShapeDtypeStruct((g, t * SUBLANES, C_WIDTH), F32),
        compiler_params=_cparams(("parallel",)), name="memattn_sample",
    )(q3, mkt3, mvt3)


def _dsa_prompt_kernel(topk, qz_ref, qiz_ref, wt_ref, ki2_ref, kbf_ref, vt_ref, o_ref,
                       sc_ref, acc_ref, m_ref, s_ref, sb_ref):
    i = pl.program_id(1)
    nk = i + 1
    npair = (nk + 1) // 2
    kth = jnp.float32(topk)

    def sum_rows(x):
        return jnp.sum(x.reshape(KC // SUBLANES, SUBLANES, QB), axis=0)

    def max_rows(x):
        return jnp.max(x.reshape(KC // SUBLANES, SUBLANES, QB), axis=0)

    def score_chunk(c, _):
        keys = ki2_ref[0, c]
        acc = jnp.zeros((KC, QB), F32)
        for hd in range(IDX_HEADS):
            y = _dot_nt(keys, qiz_ref[hd])
            acc = acc + jnp.maximum(y, 0.0) * wt_ref[hd:hd + 1, :]
        kpos = c * KC + _iota((KC, QB), 0)
        qpos = i * QB + _iota((KC, QB), 1)
        masked = jnp.where(kpos <= qpos, acc, -jnp.inf)
        sc_ref[c] = masked
        sb_ref[c] = masked.astype(BF16)
        return 0

    lax.fori_loop(0, nk, score_chunk, 0)

    @pl.when(nk % 2 == 1)
    def _():
        pad = jnp.minimum(nk, sc_ref.shape[0] - 1)
        sc_ref[pad] = jnp.full((KC, QB), -jnp.inf, F32)
        sb_ref[pad] = jnp.full((KC, QB), -jnp.inf, BF16)

    def count(pred):
        def body(j, cnt):
            return (cnt + sum_rows(pred(sc_ref[2 * j]).astype(F32))
                    + sum_rows(pred(sc_ref[2 * j + 1]).astype(F32)))
        return jnp.sum(lax.fori_loop(0, npair, body, jnp.zeros((SUBLANES, QB), F32)), axis=0, keepdims=True)

    def count_ge_packed(t):
        tb = t.astype(BF16)
        def slab_count(c):
            ones = jnp.where(sb_ref[c] >= tb, jnp.ones((), BF16), jnp.zeros((), BF16))
            acc = ones[0:PACKED_ROWS]
            for r in range(1, KC // PACKED_ROWS):
                acc = acc + ones[r * PACKED_ROWS:(r + 1) * PACKED_ROWS]
            return acc

        def body(j, cnt):
            return cnt + slab_count(2 * j) + slab_count(2 * j + 1)

        cnt16 = lax.fori_loop(0, npair, body, jnp.zeros((PACKED_ROWS, QB), BF16))
        return jnp.sum(cnt16.astype(F32), axis=0, keepdims=True)

    def write_ge(thr):
        def body(j, _):
            sc_ref[2 * j] = jnp.where(sc_ref[2 * j] >= thr, 0.0, NEG)
            sc_ref[2 * j + 1] = jnp.where(sc_ref[2 * j + 1] >= thr, 0.0, NEG)
            return 0
        lax.fori_loop(0, npair, body, 0)

    @pl.when(nk * KC <= topk)
    def _():
        write_ge(jnp.full((1, QB), -FLT_MAX, F32))

    @pl.when(nk * KC > topk)
    def _():
        def bf16_key_to_float_key(k16):
            return k16 * jnp.int32(2 ** 16) + jnp.where(k16 < 0, jnp.int32(2 ** 16 - 1), jnp.int32(0))

        tau_k16, _ = _bitwise_kth(
            lambda k16: count_ge_packed(_key_to_float(bf16_key_to_float_key(k16))), kth, (1, QB), 16)
        base = bf16_key_to_float_key(tau_k16) - jnp.int32(2 ** 16)

        def fine_bit(it, carry):
            off, cnt_off = carry
            cand = off + lax.shift_left(jnp.int32(1), 16 - it)
            cnt = count(lambda x: x >= _key_to_float(base + cand))
            ok = cnt >= kth
            return jnp.where(ok, cand, off), jnp.where(ok, cnt, cnt_off)

        off, cnt_thr = lax.fori_loop(0, 17, fine_bit,
                                     (jnp.zeros((1, QB), jnp.int32), jnp.full((1, QB), 2.0 ** 24, F32)))
        thr = _key_to_float(base + off)
        few = (i * QB + 1 + _iota((1, QB), 1)).astype(F32) <= kth
        thr = jnp.where(few, -FLT_MAX, thr)
        cnt_thr = jnp.where(few, kth, cnt_thr)
        has_tie = jnp.max(cnt_thr) > kth

        @pl.when(jnp.logical_not(has_tie))
        def _():
            write_ge(thr)

        @pl.when(has_tie)
        def _():
            need = kth - count(lambda x: x > thr)
            ltri = (_iota((KC, KC), 1) < _iota((KC, KC), 0)).astype(BF16)

            def body(c, carry):
                x = sc_ref[c]
                eq = x == thr
                before = _dot(ltri, eq.astype(BF16)) + carry
                sel = (x > thr) | (eq & (before < need))
                sc_ref[c] = jnp.where(sel, 0.0, NEG)
                return carry + jnp.sum(sum_rows(eq.astype(F32)), axis=0, keepdims=True)

            lax.fori_loop(0, nk, body, jnp.zeros((1, QB), F32))

    m_ref[...] = jnp.full((B_HEADS, QB), NEG, F32)
    acc_ref[...] = jnp.zeros((B_HEADS * V_ROWS, QB), F32)

    def biased_scores(c, hd):
        pair = hd // 2
        kc = kbf_ref[0, c, :, pair * LANES:(pair + 1) * LANES]
        return _dot_nt(kc, qz_ref[hd]) + sc_ref[c]

    for hd in range(B_HEADS):
        s_ref[hd] = biased_scores(0, hd)

    def attend(c, _):
        nxt = jnp.minimum(c + 1, nk - 1)
        ahead = [biased_scores(nxt, hd) for hd in range(QK_LEAD)]
        for hd in range(B_HEADS):
            rows = slice(hd * V_ROWS, (hd + 1) * V_ROWS)
            s = s_ref[hd]
            m_old = m_ref[hd:hd + 1, :]
            m_new = jnp.maximum(m_old, jnp.max(max_rows(s), axis=0, keepdims=True))
            alpha = jnp.exp2(m_old - m_new)
            p = jnp.exp2(s - m_new)
            acc_ref[rows, :] = alpha * acc_ref[rows, :] + _dot(vt_ref[0, c, rows, :], p.astype(BF16))
            m_ref[hd:hd + 1, :] = m_new
            s_ref[hd] = ahead[hd]
            if hd + QK_LEAD < B_HEADS:
                ahead.append(biased_scores(nxt, hd + QK_LEAD))
        return 0

    lax.fori_loop(0, nk, attend, 0)
    outs = []
    for hd in range(B_HEADS):
        base = hd * V_ROWS
        outs.append(acc_ref[base:base + HEAD_DIM, :] / acc_ref[base + HEAD_DIM:base + HEAD_DIM + 1, :])
    o_ref[...] = jnp.concatenate(outs, axis=0).T.astype(o_ref.dtype)


def _dsa_prompt(qz, qiz, wt, ki2, kbf, vtb, batch, seq):
    nq = seq // QB
    nc = seq // KC
    topk = min(TOPK_MAX, seq // 4)
    n = batch * seq
    qspec = pl.BlockSpec((B_HEADS, QB, LANES), lambda b, i: (0, b * nq + i, 0))
    return pl.pallas_call(
        functools.partial(_dsa_prompt_kernel, topk),
        grid=(batch, nq),
        in_specs=[qspec, qspec,
                  pl.BlockSpec((IDX_HEADS, QB), lambda b, i: (0, b * nq + i)),
                  pl.BlockSpec((1, nc, KC, LANES), lambda b, i: (b, 0, 0, 0)),
                  pl.BlockSpec((1, nc, KC, B_WIDTH), lambda b, i: (b, 0, 0, 0)),
                  pl.BlockSpec((1, nc, B_HEADS * V_ROWS, KC), lambda b, i: (b, 0, 0, 0))],
        out_specs=pl.BlockSpec((QB, B_WIDTH), lambda b, i: (b * nq + i, 0)),
        out_shape=jax.ShapeDtypeStruct((n, B_WIDTH), BF16),
        scratch_shapes=[pltpu.VMEM((nc, KC, QB), F32), pltpu.VMEM((B_HEADS * V_ROWS, QB), F32),
                        pltpu.VMEM((B_HEADS, QB), F32), pltpu.VMEM((B_HEADS, KC, QB), F32),
                        pltpu.VMEM((nc, KC, QB), BF16)],
        compiler_params=_cparams(("parallel", "arbitrary")), name="dsa_prompt",
    )(qz, qiz, wt, ki2.reshape(batch, nc, KC, LANES), kbf.reshape(batch, nc, KC, B_WIDTH),
      vtb.reshape(batch, nc, B_HEADS * V_ROWS, KC))


def _page_specs(block, n_pages_per_step):
    def make(r):
        return pl.BlockSpec(block, lambda b, j, pt: (pt[b, j * n_pages_per_step + r], 0, 0))
    return [make(r) for r in range(n_pages_per_step)]


def _idx_sample_kernel(t_new, pps, pt_ref, qi_ref, w_ref, kin_ref, *rest):
    pages = rest[:pps]
    sc_out, scn_out = rest[pps:]
    j = pl.program_id(1)
    qi = qi_ref[0]
    w = w_ref[0]
    qib = qi.astype(BF16)

    def token_scores(y):
        z = jnp.maximum(y, 0.0) * w
        return jnp.sum(z.reshape(t_new, IDX_HEADS, y.shape[1]), axis=1)

    for r in range(pps):
        y = _dot(qib, pages[r][0].astype(BF16))
        sc_out[0, :, r * PAGE_SIZE:(r + 1) * PAGE_SIZE] = token_scores(y)

    @pl.when(j == 0)
    def _():
        qf = qib.astype(F32)
        lane = _iota((t_new, LANES), 1)
        tok = _iota((t_new, LANES), 0)
        new = jnp.full((t_new, LANES), -jnp.inf, F32)
        for tk in range(t_new):
            kr = kin_ref[0, tk:tk + 1, :].astype(BF16).astype(F32)
            col = token_scores(jnp.sum(qf * kr, axis=-1, keepdims=True))
            new = jnp.where((lane == tk) & (tok >= tk), col, new)
        scn_out[0] = new


def _idx_sample(page_table, qi32, w32, ki_new, cache_idx_kt):
    db, rows, _ = qi32.shape
    t_new = rows // IDX_HEADS
    n_pages = page_table.shape[1]
    pps = min(IDX_PAGES_PER_STEP, n_pages)
    steps = n_pages // pps
    span = pps * PAGE_SIZE
    grid_spec = pltpu.PrefetchScalarGridSpec(
        num_scalar_prefetch=1, grid=(db, steps),
        in_specs=[pl.BlockSpec((1, rows, IDX_DIM), lambda b, j, pt: (b, 0, 0)),
                  pl.BlockSpec((1, rows, 1), lambda b, j, pt: (b, 0, 0)),
                  pl.BlockSpec((1, t_new, IDX_DIM), lambda b, j, pt: (b, 0, 0))]
        + _page_specs((1, IDX_DIM, PAGE_SIZE), pps),
        out_specs=[pl.BlockSpec((1, t_new, span), lambda b, j, pt: (b, 0, j)),
                   pl.BlockSpec((1, t_new, LANES), lambda b, j, pt: (b, 0, 0))])
    return pl.pallas_call(
        functools.partial(_idx_sample_kernel, t_new, pps), grid_spec=grid_spec,
        out_shape=[jax.ShapeDtypeStruct((db, t_new, n_pages * PAGE_SIZE), F32),
                   jax.ShapeDtypeStruct((db, t_new, LANES), F32)],
        compiler_params=_cparams(("parallel", "arbitrary")), name="idx_sample",
    )(page_table, qi32, w32, ki_new, *([cache_idx_kt] * pps))


def _thr_sample_kernel(topk, sc_ref, scn_ref, b_out, bn_out):
    kth = jnp.float32(topk)
    n_chunks = sc_ref.shape[1] // LANES

    def count(pred):
        return (jnp.sum(pred(sc_ref[...]).astype(F32), axis=-1, keepdims=True)
                + jnp.sum(pred(scn_ref[...]).astype(F32), axis=-1, keepdims=True))

    thr, cnt_thr = _kth_largest(lambda t: count(lambda x: x >= t), kth, (THR_ROWS, 1))
    has_tie = jnp.max(cnt_thr) > kth

    @pl.when(jnp.logical_not(has_tie))
    def _():
        b_out[...] = jnp.where(sc_ref[...] >= thr, 0.0, NEG)
        bn_out[...] = jnp.where(scn_ref[...] >= thr, 0.0, NEG)

    @pl.when(has_tie)
    def _():
        need = kth - count(lambda x: x > thr)
        utri = (_iota((LANES, LANES), 0) < _iota((LANES, LANES), 1)).astype(BF16)
        carry = jnp.zeros((THR_ROWS, 1), F32)
        for c in range(n_chunks + 1):
            x = sc_ref[:, c * LANES:(c + 1) * LANES] if c < n_chunks else scn_ref[...]
            eq = x == thr
            before = _dot(eq.astype(BF16), utri) + carry
            sel = (x > thr) | (eq & (before < need))
            bias = jnp.where(sel, 0.0, NEG)
            if c < n_chunks:
                b_out[:, c * LANES:(c + 1) * LANES] = bias
            else:
                bn_out[...] = bias
            carry = carry + jnp.sum(eq.astype(F32), axis=-1, keepdims=True)


def _thr_sample(sc, scn, topk):
    n, width = sc.shape
    return pl.pallas_call(
        functools.partial(_thr_sample_kernel, topk), grid=(n // THR_ROWS,),
        in_specs=[pl.BlockSpec((THR_ROWS, width), lambda i: (i, 0)),
                  pl.BlockSpec((THR_ROWS, LANES), lambda i: (i, 0))],
        out_specs=[pl.BlockSpec((THR_ROWS, width), lambda i: (i, 0)),
                   pl.BlockSpec((THR_ROWS, LANES), lambda i: (i, 0))],
        out_shape=[jax.ShapeDtypeStruct((n, width), F32), jax.ShapeDtypeStruct((n, LANES), F32)],
        compiler_params=_cparams(("parallel",)), name="thr_sample",
    )(sc, scn)


def _attn_sample_kernel(t_new, pps, pt_ref, q_ref, bias_ref, biasn_ref, kn_ref, vn_ref, *rest):
    kpages = rest[:pps]
    vpages = rest[pps:2 * pps]
    o_ref, m_ref, l_ref, acc_ref, qbd_ref = rest[2 * pps:]
    j = pl.program_id(1)
    rows = t_new * B_HEADS
    mask8 = (_iota((SUBLANES, B_WIDTH), 1) // HEAD_DIM) == _iota((SUBLANES, B_WIDTH), 0)

    def per_token(ref, width):
        return jnp.concatenate(
            [jnp.broadcast_to(ref[0, i:i + 1, :], (B_HEADS, width)) for i in range(t_new)], axis=0)

    @pl.when(j == 0)
    def _():
        m_ref[...] = jnp.full((rows, 1), NEG, F32)
        l_ref[...] = jnp.zeros((rows, 1), F32)
        acc_ref[...] = jnp.zeros((rows, B_WIDTH), F32)
        qbd_ref[...] = jnp.concatenate(
            [jnp.where(mask8, jnp.broadcast_to(q_ref[0, i:i + 1, :] * ATTN_SCALE, (B_HEADS, B_WIDTH)), 0.0)
             for i in range(t_new)], axis=0).astype(BF16)

    qbd = qbd_ref[...]
    span = pps * PAGE_SIZE
    s = jnp.concatenate([_dot(qbd, kpages[r][0].astype(BF16)) for r in range(pps)], axis=1)
    s = s + per_token(bias_ref, span)
    m_old = m_ref[...]
    m_new = jnp.maximum(m_old, jnp.max(s, axis=-1, keepdims=True))
    alpha = jnp.exp(m_old - m_new)
    p = jnp.exp(s - m_new)
    l_ref[...] = alpha * l_ref[...] + jnp.sum(p, axis=-1, keepdims=True)
    pb = p.astype(BF16)
    pv = jnp.zeros((rows, B_WIDTH), F32)
    for r in range(pps):
        pv = pv + _dot_nt(pb[:, r * PAGE_SIZE:(r + 1) * PAGE_SIZE], vpages[r][0].astype(BF16))
    acc_ref[...] = alpha * acc_ref[...] + pv
    m_ref[...] = m_new

    @pl.when(j == pl.num_programs(1) - 1)
    def _():
        qf = qbd.astype(F32)
        bn = per_token(biasn_ref, LANES)
        cols = []
        for tk in range(t_new):
            kr = kn_ref[0, tk:tk + 1, :].astype(BF16).astype(F32)
            cols.append(jnp.sum(qf * kr, axis=-1, keepdims=True) + bn[:, tk:tk + 1])
        m_old = m_ref[...]
        m_new = m_old
        for cval in cols:
            m_new = jnp.maximum(m_new, cval)
        alpha = jnp.exp(m_old - m_new)
        l = alpha * l_ref[...]
        acc = alpha * acc_ref[...]
        for tk in range(t_new):
            pk = jnp.exp(cols[tk] - m_new)
            l = l + pk
            acc = acc + pk.astype(BF16).astype(F32) * vn_ref[0, tk:tk + 1, :].astype(BF16).astype(F32)
        o = acc / l
        for i in range(t_new):
            slab = jnp.where(mask8, o[i * B_HEADS:(i + 1) * B_HEADS, :], 0.0)
            o_ref[0, i * B_HEADS:(i + 1) * B_HEADS, :] = jnp.broadcast_to(
                jnp.sum(slab, axis=0, keepdims=True), (B_HEADS, B_WIDTH))


def _attn_sample(page_table, q3, bias3, biasn3, k_new, v_new, cache_kt, cache_vt):
    db, t_new, _ = q3.shape
    n_pages = page_table.shape[1]
    pps = min(ATTN_PAGES_PER_STEP, n_pages)
    steps = n_pages // pps
    span = pps * PAGE_SIZE
    rows = t_new * B_HEADS
    per_b = lambda w: pl.BlockSpec((1, t_new, w), lambda b, j, pt: (b, 0, 0))
    grid_spec = pltpu.PrefetchScalarGridSpec(
        num_scalar_prefetch=1, grid=(db, steps),
        in_specs=[per_b(B_WIDTH),
                  pl.BlockSpec((1, t_new, span), lambda b, j, pt: (b, 0, j)),
                  per_b(LANES), per_b(B_WIDTH), per_b(B_WIDTH)]
        + _page_specs((1, B_WIDTH, PAGE_SIZE), pps)
        + _page_specs((1, B_WIDTH, PAGE_SIZE), pps),
        out_specs=pl.BlockSpec((1, rows, B_WIDTH), lambda b, j, pt: (b, 0, 0)),
        scratch_shapes=[pltpu.VMEM((rows, 1), F32), pltpu.VMEM((rows, 1), F32),
                        pltpu.VMEM((rows, B_WIDTH), F32), pltpu.VMEM((rows, B_WIDTH), BF16)])
    return pl.pallas_call(
        functools.partial(_attn_sample_kernel, t_new, pps), grid_spec=grid_spec,
        out_shape=jax.ShapeDtypeStruct((db, rows, B_WIDTH), F32),
        compiler_params=_cparams(("parallel", "arbitrary")), name="attn_sample",
    )(page_table, q3, bias3, biasn3, k_new, v_new, *([cache_kt] * pps), *([cache_vt] * pps))


def _merge_kernel(x_ref, ma_ref, yb_ref, sgb_ref, yc_ref, sgc_ref, wo_ref, g_ref, y_out):
    mb = (yb_ref[...].astype(F32) * sgb_ref[...].astype(F32)).astype(BF16)
    mc = (yc_ref[...].astype(F32) * sgc_ref[...].astype(F32)).astype(BF16)
    o = (_dot(ma_ref[...], wo_ref[0:A_WIDTH, :])
         + _dot(mb, wo_ref[A_WIDTH:A_WIDTH + B_WIDTH, :])
         + _dot(mc, wo_ref[A_WIDTH + B_WIDTH:, :]))
    z = x_ref[...] + o
    ms = jnp.mean(z * z, axis=-1, keepdims=True)
    y_out[...] = z * lax.rsqrt(ms + EPS) * g_ref[...]


def _merge(x, mixa, yb, sgb, yc, sgc, wo, g_final):
    n = x.shape[0]
    row = lambda w: pl.BlockSpec((MERGE_TM, w), lambda i: (i, 0))
    return pl.pallas_call(
        _merge_kernel, grid=(n // MERGE_TM,),
        in_specs=[row(D_MODEL), row(A_WIDTH), row(B_WIDTH), row(B_WIDTH), row(C_WIDTH), row(C_WIDTH),
                  pl.BlockSpec(wo.shape, lambda i: (0, 0)), pl.BlockSpec(g_final.shape, lambda i: (0, 0))],
        out_specs=row(D_MODEL),
        out_shape=jax.ShapeDtypeStruct((n, D_MODEL), F32),
        compiler_params=_cparams(("parallel",)), name="merge",
    )(x, mixa, yb, sgb, yc, sgc, wo, g_final)


def _rope_tables(pos):
    inv = ROPE_THETA ** (-jnp.arange(0, HEAD_DIM, 2, dtype=F32) / HEAD_DIM)
    ang = pos.astype(F32)[:, None] * inv[None, :]
    c = jnp.cos(ang)
    s = jnp.sin(ang)
    return jnp.concatenate([c, c, c, c], axis=1), jnp.concatenate([-s, s, -s, s], axis=1)


def _heads_last(t, heads):
    b, _, n = t.shape
    return t.reshape(b, heads, HEAD_DIM, n).transpose(0, 3, 1, 2)


def _channels_first(t):
    p, n, heads, d = t.shape
    return t.transpose(0, 2, 3, 1).reshape(p, heads * d, n)


def kernel(x_prompt, x_sample, mem_prompt, cache_k, cache_v, cache_idx_k, cache_mem_k, cache_mem_v,
           page_table, g_norm, w_in, w_spatial, b_spatial, g_v, w_mem_kv, g_mem, w_out, g_final):
    batch, seq, _ = x_prompt.shape
    db, t_new, _ = x_sample.shape
    n_pages = page_table.shape[1]
    past = n_pages * PAGE_SIZE

    ki_cols = w_in[:, 3328:3392]
    wi_cols = w_in[:, 3392:3400]
    w_r = jnp.concatenate(
        [w_in[:, :3328], w_in[:, 3400:3912], ki_cols, ki_cols, wi_cols,
         jnp.zeros((D_MODEL, LANES - IDX_HEADS), F32)], axis=1).astype(BF16)
    g_norm2 = g_norm.reshape(1, D_MODEL)
    g_v2 = g_v.reshape(1, A_WIDTH)
    bs_prompt = jnp.repeat(b_spatial.T, HEAD_DIM, axis=1)
    reps = CHUNK // t_new
    ws_sample = jnp.tile(w_spatial[:, :t_new, :t_new], (1, reps, reps))
    bs_sample = jnp.tile(jnp.repeat(b_spatial[:, :t_new].T, HEAD_DIM, axis=1), (reps, 1))
    wo_b = w_out.astype(BF16)
    g_final2 = g_final.reshape(1, D_MODEL)

    cos_p, sin_p = _rope_tables(jnp.arange(seq, dtype=jnp.int32))
    pos_s = past + jnp.arange(t_new, dtype=jnp.int32)
    cos_s, sin_s = _rope_tables(jnp.tile(pos_s, TM // t_new))

    xp = x_prompt.reshape(batch * seq, D_MODEL)
    (kt, vt, kit, kbf, vtb, ki2, qz, qiz, wt, sgb, mixa, qc, sgc) = _proj(
        xp, g_norm2, w_r, cos_p, sin_p, w_spatial, bs_prompt, g_v2, CHUNK, seq // TM, False)
    mkt, mvt = _memkv(mem_prompt.reshape(batch * N_MEM, D_MODEL), g_mem.reshape(1, D_MODEL),
                      w_mem_kv.astype(BF16))
    yc = _memattn(qc.reshape(batch * seq // TM, TM, C_WIDTH), mkt, mvt, seq // TM).reshape(batch * seq, C_WIDTH)
    yb = _dsa_prompt(qz, qiz, wt, ki2, kbf, vtb, batch, seq)
    y_prompt = _merge(xp, mixa, yb, sgb, yc, sgc, wo_b, g_final2).reshape(batch, seq, D_MODEL)

    ns = db * t_new
    xs = x_sample.reshape(ns, D_MODEL)
    (q_s, k_s, v_s, qi_s, ki_s, wi_s, sgb_s, mixa_s, qc_s, sgc_s, vn_s) = _proj(
        xs, g_norm2, w_r, cos_s, sin_s, ws_sample, bs_sample, g_v2, t_new, 1, True)
    sc, scn = _idx_sample(page_table, qi_s.reshape(db, t_new * IDX_HEADS, IDX_DIM),
                          wi_s.reshape(db, t_new * IDX_HEADS, 1), ki_s.reshape(db, t_new, IDX_DIM),
                          cache_idx_k.transpose(0, 2, 1))
    topk_s = min(TOPK_MAX, (past + t_new) // 4)
    bias, biasn = _thr_sample(sc.reshape(ns, past), scn.reshape(ns, LANES), topk_s)
    yb_pad = _attn_sample(page_table, q_s.reshape(db, t_new, B_WIDTH), bias.reshape(db, t_new, past),
                          biasn.reshape(db, t_new, LANES), k_s.reshape(db, t_new, B_WIDTH),
                          v_s.reshape(db, t_new, B_WIDTH), _channels_first(cache_k), _channels_first(cache_v))
    yb_s = yb_pad[:, ::B_HEADS, :].reshape(ns, B_WIDTH)
    yc_pad = _memattn_sample(qc_s.reshape(db, t_new, C_WIDTH),
                             _channels_first(cache_mem_k), _channels_first(cache_mem_v))
    yc_s = yc_pad[:, ::SUBLANES, :].reshape(ns, C_WIDTH)
    y_sample = _merge(xs, mixa_s, yb_s, sgb_s, yc_s, sgc_s, wo_b, g_final2).reshape(db, t_new, D_MODEL)

    return (y_prompt, y_sample,
            _heads_last(kt, B_HEADS), _heads_last(vt, B_HEADS), kit.transpose(0, 2, 1),
            _heads_last(mkt, C_HEADS), _heads_last(mvt, C_HEADS),
            k_s.reshape(db, t_new, B_HEADS, HEAD_DIM), v_s.reshape(db, t_new, B_HEADS, HEAD_DIM),
            ki_s.reshape(db, t_new, IDX_DIM), vn_s.reshape(db, t_new, A_WIDTH))
```

```python
import functools

import jax
import jax.numpy as jnp
from jax import lax
from jax.experimental import pallas as pl
from jax.experimental.pallas import tpu as pltpu

F32 = jnp.float32
BF16 = jnp.bfloat16

HEAD_DIM = 64
HALF = HEAD_DIM // 2
D_MODEL = 1024
A_GROUPS = 4
A_WIDTH = A_GROUPS * HEAD_DIM
CHUNK = 128
B_HEADS = 8
B_WIDTH = B_HEADS * HEAD_DIM
IDX_HEADS = 8
IDX_DIM = 64
TOPK_MAX = 256
N_MEM = 256
C_HEADS = 4
C_WIDTH = C_HEADS * HEAD_DIM
PAGE_SIZE = 128
ROPE_THETA = 10000.0
EPS = 1e-6
IDX_W_SCALE = (IDX_HEADS * IDX_DIM) ** -0.5
ATTN_SCALE = HEAD_DIM ** -0.5
LOG2E = 1.4426950408889634

LANES = 128
SUBLANES = 8
NEG = -1e30
FLT_MAX = 3.4028234663852886e38
PACKED_ROWS = 16

TM = 256
MEMATTN_TM = 1024
MEMATTN_SEQS_PER_STEP = 4
MERGE_TM = 1024
QB = 256
KC = 256
V_ROWS = HEAD_DIM + 16
QK_LEAD = 3
IDX_PAGES_PER_STEP = 32
ATTN_PAGES_PER_STEP = 32
THR_ROWS = 64
VMEM_LIMIT = 52 * 1024 * 1024

OFF_U, OFF_VA, OFF_GA = 0, 256, 512
OFF_Q, OFF_K, OFF_V, OFF_GB, OFF_QI = 768, 1280, 1792, 2304, 2816
OFF_QC, OFF_GC, OFF_KI2, OFF_WI, W_TOTAL = 3328, 3584, 3840, 3968, 4096


def _cparams(sem):
    return pltpu.CompilerParams(dimension_semantics=sem, vmem_limit_bytes=VMEM_LIMIT)


def _iota(shape, dim):
    return lax.broadcasted_iota(jnp.int32, shape, dim)


def _silu(x):
    return x / (1.0 + jnp.exp(-x))


def _dot(a, b):
    return jnp.dot(a, b, preferred_element_type=F32)


def _dot_nt(a, b):
    return lax.dot_general(a, b, (((1,), (1,)), ((), ())), preferred_element_type=F32)


def _key_to_float(k):
    bits = k ^ ((k >> 31) & jnp.int32(0x7FFFFFFF))
    return lax.bitcast_convert_type(bits, F32)


def _bitwise_kth(count_ge, kth, shape, bits):
    cnt0 = count_ge(jnp.zeros(shape, jnp.int32))
    ok0 = cnt0 >= kth
    key = jnp.where(ok0, jnp.int32(0), jnp.int32(-2 ** (bits - 1)))
    cnt_key = jnp.where(ok0, cnt0, jnp.float32(2 ** 24))

    def bit_body(it, carry):
        key, cnt_key = carry
        trial = key + lax.shift_left(jnp.int32(1), bits - 2 - it)
        cnt = count_ge(trial)
        ok = cnt >= kth
        return jnp.where(ok, trial, key), jnp.where(ok, cnt, cnt_key)

    return lax.fori_loop(0, bits - 1, bit_body, (key, cnt_key))


def _kth_largest(count_ge, kth, shape):
    key, cnt_key = _bitwise_kth(lambda k: count_ge(_key_to_float(k)), kth, shape, 32)
    return _key_to_float(key), cnt_key


def _rope(t, cos, sin):
    outs = []
    first = (_iota((t.shape[0], LANES), 1) % HEAD_DIM) < HALF
    for j in range(t.shape[1] // LANES):
        x = t[:, j * LANES:(j + 1) * LANES]
        partner = jnp.where(first, pltpu.roll(x, LANES - HALF, 1), pltpu.roll(x, HALF, 1))
        outs.append(x * cos + partner * sin)
    return outs[0] if len(outs) == 1 else jnp.concatenate(outs, axis=1)


def _proj_kernel(chunk, sample, x_ref, g_ref, w_ref, cos_ref, sin_ref, ws_ref, bs_ref, gv_ref, *outs):
    if sample:
        q_out, k_out, v_out, qi_out, ki_out, wi_out, sgb_out, mixa_out, qc_out, sgc_out, vn_out = outs
    else:
        (kt_out, vt_out, kit_out, kbf_out, vtb_out, ki2_out, qz_out, qiz_out, wt_out,
         sgb_out, mixa_out, qc_out, sgc_out) = outs
    tm = x_ref.shape[0]
    x = x_ref[...]
    ms = jnp.mean(x * x, axis=-1, keepdims=True)
    h = (x * lax.rsqrt(ms + EPS) * g_ref[...]).astype(BF16)
    cos = cos_ref[...]
    sin = sin_ref[...]

    def seg(a, b):
        return _dot(h, w_ref[:, a:b])

    half_id = (_iota((tm, LANES), 1) // HEAD_DIM)

    def head_split(t, out):
        for hd in range(B_HEADS):
            pair = t[:, (hd // 2) * LANES:(hd // 2 + 1) * LANES]
            out[hd] = jnp.where(half_id == hd % 2, pair, 0.0).astype(BF16)

    q = _rope(seg(OFF_Q, OFF_K), cos, sin)
    k = _rope(seg(OFF_K, OFF_V), cos, sin)
    v = seg(OFF_V, OFF_GB)
    qi = _rope(seg(OFF_QI, OFF_QC), cos, sin)
    ki2 = _rope(seg(OFF_KI2, OFF_WI), cos, sin)
    wi = seg(OFF_WI, W_TOTAL) * IDX_W_SCALE
    qc = seg(OFF_QC, OFF_GC) * ATTN_SCALE
    sgb_out[...] = _silu(seg(OFF_GB, OFF_QI)).astype(BF16)
    sgc_out[...] = _silu(seg(OFF_GC, OFF_KI2)).astype(BF16)
    qc_out[...] = qc.astype(qc_out.dtype)
    if sample:
        q_out[...] = q
        k_out[...] = k
        v_out[...] = v
        qi_out[...] = qi
        ki_out[...] = ki2[:, :IDX_DIM]
        wi_out[...] = wi[:, :IDX_HEADS]
    else:
        head_split(q * (ATTN_SCALE * LOG2E), qz_out)
        head_split(qi, qiz_out)
        kt_out[0] = k.T
        kbf_out[...] = k.astype(BF16)
        vt = v.T
        vt_out[0] = vt
        vtb = vt.astype(BF16)
        ones = jnp.ones((V_ROWS - HEAD_DIM, tm), BF16)
        for hd in range(B_HEADS):
            vtb_out[0, hd * V_ROWS:hd * V_ROWS + HEAD_DIM, :] = vtb[hd * HEAD_DIM:(hd + 1) * HEAD_DIM, :]
            vtb_out[0, hd * V_ROWS + HEAD_DIM:(hd + 1) * V_ROWS, :] = ones
        kit_out[0] = ki2.T[:IDX_DIM, :]
        ki2_out[...] = ki2.astype(BF16)
        wt_out[...] = wi.T[:IDX_HEADS, :]

    u = seg(OFF_U, OFF_VA)
    va = seg(OFF_VA, OFF_GA)
    ga = seg(OFF_GA, OFF_Q)
    grp_r = _iota((A_WIDTH, A_WIDTH), 0) // HEAD_DIM
    grp_c = _iota((A_WIDTH, A_WIDTH), 1) // HEAD_DIM
    gmat = jnp.where(grp_r == grp_c, 1.0 / HEAD_DIM, 0.0).astype(BF16)

    def group_mean(t):
        hi = t.astype(BF16)
        lo = (t - hi.astype(F32)).astype(BF16)
        return _dot(hi, gmat) + _dot(lo, gmat)

    mu = group_mean(va)
    d = va - mu
    var = group_mean(d * d)
    vn = d * lax.rsqrt(var + EPS) * gv_ref[...]
    if sample:
        vn_out[...] = vn
    vnb = vn.astype(BF16)
    row = _iota((CHUNK, CHUNK), 0)
    col = _iota((CHUNK, CHUNK), 1)
    causal = (row >= col) & ((row // chunk) == (col // chunk))
    lane_grp = _iota((CHUNK, A_WIDTH), 1) // HEAD_DIM
    wms = [jnp.where(causal, ws_ref[g], 0.0).astype(BF16) for g in range(A_GROUPS)]
    mixes = []
    for j in range(tm // CHUNK):
        vc = vnb[j * CHUNK:(j + 1) * CHUNK, :]
        m = bs_ref[...]
        for g in range(A_GROUPS):
            m = m + jnp.where(lane_grp == g, _dot(wms[g], vc), 0.0)
        mixes.append(m)
    mix = jnp.concatenate(mixes, axis=0)
    mixa_out[...] = (_silu(ga) * u * mix).astype(BF16)


def _proj(x, g_norm, w_r, cos_t, sin_t, ws, bs, g_v, chunk, seq_blocks, sample):
    n = x.shape[0]
    nt = n // TM
    nb = nt // seq_blocks
    row = lambda w: pl.BlockSpec((TM, w), lambda i: (i, 0))
    const2 = lambda a: pl.BlockSpec(a.shape, lambda i: (0, 0))
    sds = jax.ShapeDtypeStruct
    in_specs = [
        row(D_MODEL), const2(g_norm), const2(w_r),
        pl.BlockSpec((TM, LANES), lambda i: (i % seq_blocks, 0)),
        pl.BlockSpec((TM, LANES), lambda i: (i % seq_blocks, 0)),
        pl.BlockSpec(ws.shape, lambda i: (0, 0, 0)), const2(bs), const2(g_v),
    ]
    if sample:
        outs = [
            (sds((n, B_WIDTH), F32), row(B_WIDTH)),
            (sds((n, B_WIDTH), F32), row(B_WIDTH)),
            (sds((n, B_WIDTH), F32), row(B_WIDTH)),
            (sds((n, IDX_HEADS * IDX_DIM), F32), row(IDX_HEADS * IDX_DIM)),
            (sds((n, IDX_DIM), F32), row(IDX_DIM)),
            (sds((n, IDX_HEADS), F32), row(IDX_HEADS)),
            (sds((n, B_WIDTH), BF16), row(B_WIDTH)),
            (sds((n, A_WIDTH), BF16), row(A_WIDTH)),
            (sds((n, C_WIDTH), F32), row(C_WIDTH)),
            (sds((n, C_WIDTH), BF16), row(C_WIDTH)),
            (sds((n, A_WIDTH), F32), row(A_WIDTH)),
        ]
    else:
        seq = seq_blocks * TM
        tspec = lambda w: pl.BlockSpec((1, w, TM), lambda i: (i // seq_blocks, 0, i % seq_blocks))
        hspec = pl.BlockSpec((B_HEADS, TM, LANES), lambda i: (0, i, 0))
        outs = [
            (sds((nb, B_WIDTH, seq), F32), tspec(B_WIDTH)),
            (sds((nb, B_WIDTH, seq), F32), tspec(B_WIDTH)),
            (sds((nb, IDX_DIM, seq), F32), tspec(IDX_DIM)),
            (sds((n, B_WIDTH), BF16), row(B_WIDTH)),
            (sds((nt, B_HEADS * V_ROWS, TM), BF16),
             pl.BlockSpec((1, B_HEADS * V_ROWS, TM), lambda i: (i, 0, 0))),
            (sds((n, LANES), BF16), row(LANES)),
            (sds((B_HEADS, n, LANES), BF16), hspec),
            (sds((IDX_HEADS, n, LANES), BF16), hspec),
            (sds((IDX_HEADS, n), F32), pl.BlockSpec((IDX_HEADS, TM), lambda i: (0, i))),
            (sds((n, B_WIDTH), BF16), row(B_WIDTH)),
            (sds((n, A_WIDTH), BF16), row(A_WIDTH)),
            (sds((n, C_WIDTH), BF16), row(C_WIDTH)),
            (sds((n, C_WIDTH), BF16), row(C_WIDTH)),
        ]
    return pl.pallas_call(
        functools.partial(_proj_kernel, chunk, sample),
        grid=(nt,), in_specs=in_specs, out_specs=[o[1] for o in outs], out_shape=[o[0] for o in outs],
        compiler_params=_cparams(("parallel",)), name="proj",
    )(x, g_norm, w_r, cos_t, sin_t, ws, bs, g_v)


def _memkv_kernel(m_ref, g_ref, w_ref, mkt_out, mvt_out):
    x = m_ref[...]
    ms = jnp.mean(x * x, axis=-1, keepdims=True)
    h = (x * lax.rsqrt(ms + EPS) * g_ref[...]).astype(BF16)
    kv = _dot(h, w_ref[...])
    mkt_out[0] = kv[:, :C_WIDTH].T
    mvt_out[0] = kv[:, C_WIDTH:].T


def _memkv(mem, g_mem, w_kv):
    n = mem.shape[0]
    nb = n // N_MEM
    return pl.pallas_call(
        _memkv_kernel, grid=(nb,),
        in_specs=[pl.BlockSpec((N_MEM, D_MODEL), lambda i: (i, 0)),
                  pl.BlockSpec(g_mem.shape, lambda i: (0, 0)),
                  pl.BlockSpec(w_kv.shape, lambda i: (0, 0))],
        out_specs=[pl.BlockSpec((1, C_WIDTH, N_MEM), lambda i: (i, 0, 0))] * 2,
        out_shape=[jax.ShapeDtypeStruct((nb, C_WIDTH, N_MEM), F32)] * 2,
        compiler_params=_cparams(("parallel",)), name="memkv",
    )(mem, g_mem, w_kv)


def _memattn_kernel(q_ref, mkt_ref, mvt_ref, o_ref):
    q = q_ref[0]
    mkt = mkt_ref[0].astype(BF16)
    mvt = mvt_ref[0].astype(BF16)
    t = q.shape[0]
    lane_head = _iota((t, C_WIDTH), 1) // HEAD_DIM
    out = jnp.zeros((t, C_WIDTH), F32)
    for hd in range(C_HEADS):
        qh = jnp.where(lane_head == hd, q, 0.0).astype(BF16)
        s = _dot(qh, mkt)
        p = jnp.exp(s - jnp.max(s, axis=-1, keepdims=True))
        o = _dot_nt(p.astype(BF16), mvt) / jnp.sum(p, axis=-1, keepdims=True)
        out = out + jnp.where(lane_head == hd, o, 0.0)
    o_ref[0] = out.astype(o_ref.dtype)


def _memattn(q3, mkt3, mvt3, tiles_per_mem):
    g, t, _ = q3.shape
    mem_spec = pl.BlockSpec((1, C_WIDTH, N_MEM), lambda i: (i // tiles_per_mem, 0, 0))
    return pl.pallas_call(
        _memattn_kernel, grid=(g,),
        in_specs=[pl.BlockSpec((1, t, C_WIDTH), lambda i: (i, 0, 0)), mem_spec, mem_spec],
        out_specs=pl.BlockSpec((1, t, C_WIDTH), lambda i: (i, 0, 0)),
        out_shape=jax.ShapeDtypeStruct((g, t, C_WIDTH), BF16),
        compiler_params=_cparams(("parallel",)), name="memattn",
    )(q3, mkt3, mvt3)


def _memattn_sample_kernel(q_ref, mkt_ref, mvt_ref, o_ref):
    t = q_ref.shape[1]
    mask8 = (_iota((SUBLANES, C_WIDTH), 1) // HEAD_DIM) == _iota((SUBLANES, C_WIDTH), 0)
    for g in range(q_ref.shape[0]):
        q4 = q_ref[g]
        qbd = jnp.concatenate(
            [jnp.where(mask8, jnp.broadcast_to(q4[i:i + 1, :], (SUBLANES, C_WIDTH)), 0.0) for i in range(t)], axis=0)
        s = _dot(qbd.astype(BF16), mkt_ref[g].astype(BF16))
        p = jnp.exp(s - jnp.max(s, axis=-1, keepdims=True))
        o = _dot_nt(p.astype(BF16), mvt_ref[g].astype(BF16)) / jnp.sum(p, axis=-1, keepdims=True)
        for i in range(t):
            slab = jnp.where(mask8, o[i * SUBLANES:(i + 1) * SUBLANES, :], 0.0)
            o_ref[g, i * SUBLANES:(i + 1) * SUBLANES, :] = jnp.broadcast_to(
                jnp.sum(slab, axis=0, keepdims=True), (SUBLANES, C_WIDTH))


def _memattn_sample(q3, mkt3, mvt3):
    n, t, _ = q3.shape
    g = min(MEMATTN_SEQS_PER_STEP, n)
    mem_spec = pl.BlockSpec((g, C_WIDTH, N_MEM), lambda i: (i, 0, 0))
    return pl.pallas_call(
        _memattn_sample_kernel, grid=(n // g,),
        in_specs=[pl.BlockSpec((g, t, C_WIDTH), lambda i: (i, 0, 0)), mem_spec, mem_spec],
        out_specs=pl.BlockSpec((g, t * SUBLANES, C_WIDTH), lambda i: (i, 0, 0)),
        out_shape=jax.ShapeDtypeStruct((n, t * SUBLANES, C_WIDTH), F32),
        compiler_params=_cparams(("parallel",)), name="memattn_sample",
    )(q3, mkt3, mvt3)


def _dsa_prompt_kernel(topk, qz_ref, qiz_ref, wt_ref, ki2_ref, kbf_ref, vt_ref, o_ref,
                       sc_ref, acc_ref, m_ref, s_ref, sb_ref):
    i = pl.program_id(1)
    nk = i + 1
    npair = (nk + 1) // 2
    kth = jnp.float32(topk)

    def sum_rows(x):
        return jnp.sum(x.reshape(KC // SUBLANES, SUBLANES, QB), axis=0)

    def max_rows(x):
        return jnp.max(x.reshape(KC // SUBLANES, SUBLANES, QB), axis=0)

    def score_chunk(c, _):
        keys = ki2_ref[0, c]
        acc = jnp.zeros((KC, QB), F32)
        for hd in range(IDX_HEADS):
            y = _dot_nt(keys, qiz_ref[hd])
            acc = acc + jnp.maximum(y, 0.0) * wt_ref[hd:hd + 1, :]
        kpos = c * KC + _iota((KC, QB), 0)
        qpos = i * QB + _iota((KC, QB), 1)
        masked = jnp.where(kpos <= qpos, acc, -jnp.inf)
        sc_ref[c] = masked
        sb_ref[c] = masked.astype(BF16)
        return 0

    lax.fori_loop(0, nk, score_chunk, 0)

    @pl.when(nk % 2 == 1)
    def _():
        pad = jnp.minimum(nk, sc_ref.shape[0] - 1)
        sc_ref[pad] = jnp.full((KC, QB), -jnp.inf, F32)
        sb_ref[pad] = jnp.full((KC, QB), -jnp.inf, BF16)

    def count(pred):
        def body(j, cnt):
            return (cnt + sum_rows(pred(sc_ref[2 * j]).astype(F32))
                    + sum_rows(pred(sc_ref[2 * j + 1]).astype(F32)))
        return jnp.sum(lax.fori_loop(0, npair, body, jnp.zeros((SUBLANES, QB), F32)), axis=0, keepdims=True)

    def count_ge_packed(t):
        tb = t.astype(BF16)
        def slab_count(c):
            ones = jnp.where(sb_ref[c] >= tb, jnp.ones((), BF16), jnp.zeros((), BF16))
            acc = ones[0:PACKED_ROWS]
            for r in range(1, KC // PACKED_ROWS):
                acc = acc + ones[r * PACKED_ROWS:(r + 1) * PACKED_ROWS]
            return acc

        def body(j, cnt):
            return cnt + slab_count(2 * j) + slab_count(2 * j + 1)

        cnt16 = lax.fori_loop(0, npair, body, jnp.zeros((PACKED_ROWS, QB), BF16))
        return jnp.sum(cnt16.astype(F32), axis=0, keepdims=True)

    def write_ge(thr):
        def body(j, _):
            sc_ref[2 * j] = jnp.where(sc_ref[2 * j] >= thr, 0.0, NEG)
            sc_ref[2 * j + 1] = jnp.where(sc_ref[2 * j + 1] >= thr, 0.0, NEG)
            return 0
        lax.fori_loop(0, npair, body, 0)

    @pl.when(nk * KC <= topk)
    def _():
        write_ge(jnp.full((1, QB), -FLT_MAX, F32))

    @pl.when(nk * KC > topk)
    def _():
        def bf16_key_to_float_key(k16):
            return k16 * jnp.int32(2 ** 16) + jnp.where(k16 < 0, jnp.int32(2 ** 16 - 1), jnp.int32(0))

        tau_k16, _ = _bitwise_kth(
            lambda k16: count_ge_packed(_key_to_float(bf16_key_to_float_key(k16))), kth, (1, QB), 16)
        base = bf16_key_to_float_key(tau_k16) - jnp.int32(2 ** 16)

        def fine_bit(it, carry):
            off, cnt_off = carry
            cand = off + lax.shift_left(jnp.int32(1), 16 - it)
            cnt = count(lambda x: x >= _key_to_float(base + cand))
            ok = cnt >= kth
            return jnp.where(ok, cand, off), jnp.where(ok, cnt, cnt_off)

        off, cnt_thr = lax.fori_loop(0, 17, fine_bit,
                                     (jnp.zeros((1, QB), jnp.int32), jnp.full((1, QB), 2.0 ** 24, F32)))
        thr = _key_to_float(base + off)
        few = (i * QB + 1 + _iota((1, QB), 1)).astype(F32) <= kth
        thr = jnp.where(few, -FLT_MAX, thr)
        cnt_thr = jnp.where(few, kth, cnt_thr)
        has_tie = jnp.max(cnt_thr) > kth

        @pl.when(jnp.logical_not(has_tie))
        def _():
            write_ge(thr)

        @pl.when(has_tie)
        def _():
            need = kth - count(lambda x: x > thr)
            ltri = (_iota((KC, KC), 1) < _iota((KC, KC), 0)).astype(BF16)

            def body(c, carry):
                x = sc_ref[c]
                eq = x == thr
                before = _dot(ltri, eq.astype(BF16)) + carry
                sel = (x > thr) | (eq & (before < need))
                sc_ref[c] = jnp.where(sel, 0.0, NEG)
                return carry + jnp.sum(sum_rows(eq.astype(F32)), axis=0, keepdims=True)

            lax.fori_loop(0, nk, body, jnp.zeros((1, QB), F32))

    m_ref[...] = jnp.full((B_HEADS, QB), NEG, F32)
    acc_ref[...] = jnp.zeros((B_HEADS * V_ROWS, QB), F32)

    def biased_scores(c, hd):
        pair = hd // 2
        kc = kbf_ref[0, c, :, pair * LANES:(pair + 1) * LANES]
        return _dot_nt(kc, qz_ref[hd]) + sc_ref[c]

    for hd in range(B_HEADS):
        s_ref[hd] = biased_scores(0, hd)

    def attend(c, _):
        nxt = jnp.minimum(c + 1, nk - 1)
        ahead = [biased_scores(nxt, hd) for hd in range(QK_LEAD)]
        for hd in range(B_HEADS):
            rows = slice(hd * V_ROWS, (hd + 1) * V_ROWS)
            s = s_ref[hd]
            m_old = m_ref[hd:hd + 1, :]
            m_new = jnp.maximum(m_old, jnp.max(max_rows(s), axis=0, keepdims=True))
            alpha = jnp.exp2(m_old - m_new)
            p = jnp.exp2(s - m_new)
            acc_ref[rows, :] = alpha * acc_ref[rows, :] + _dot(vt_ref[0, c, rows, :], p.astype(BF16))
            m_ref[hd:hd + 1, :] = m_new
            s_ref[hd] = ahead[hd]
            if hd + QK_LEAD < B_HEADS:
                ahead.append(biased_scores(nxt, hd + QK_LEAD))
        return 0

    lax.fori_loop(0, nk, attend, 0)
    outs = []
    for hd in range(B_HEADS):
        base = hd * V_ROWS
        outs.append(acc_ref[base:base + HEAD_DIM, :] / acc_ref[base + HEAD_DIM:base + HEAD_DIM + 1, :])
    o_ref[...] = jnp.concatenate(outs, axis=0).T.astype(o_ref.dtype)


def _dsa_prompt(qz, qiz, wt, ki2, kbf, vtb, batch, seq):
    nq = seq // QB
    nc = seq // KC
    topk = min(TOPK_MAX, seq // 4)
    n = batch * seq
    qspec = pl.BlockSpec((B_HEADS, QB, LANES), lambda b, i: (0, b * nq + i, 0))
    return pl.pallas_call(
        functools.partial(_dsa_prompt_kernel, topk),
        grid=(batch, nq),
        in_specs=[qspec, qspec,
                  pl.BlockSpec((IDX_HEADS, QB), lambda b, i: (0, b * nq + i)),
                  pl.BlockSpec((1, nc, KC, LANES), lambda b, i: (b, 0, 0, 0)),
                  pl.BlockSpec((1, nc, KC, B_WIDTH), lambda b, i: (b, 0, 0, 0)),
                  pl.BlockSpec((1, nc, B_HEADS * V_ROWS, KC), lambda b, i: (b, 0, 0, 0))],
        out_specs=pl.BlockSpec((QB, B_WIDTH), lambda b, i: (b * nq + i, 0)),
        out_shape=jax.ShapeDtypeStruct((n, B_WIDTH), BF16),
        scratch_shapes=[pltpu.VMEM((nc, KC, QB), F32), pltpu.VMEM((B_HEADS * V_ROWS, QB), F32),
                        pltpu.VMEM((B_HEADS, QB), F32), pltpu.VMEM((B_HEADS, KC, QB), F32),
                        pltpu.VMEM((nc, KC, QB), BF16)],
        compiler_params=_cparams(("parallel", "arbitrary")), name="dsa_prompt",
    )(qz, qiz, wt, ki2.reshape(batch, nc, KC, LANES), kbf.reshape(batch, nc, KC, B_WIDTH),
      vtb.reshape(batch, nc, B_HEADS * V_ROWS, KC))


def _page_specs(block, n_pages_per_step):
    def make(r):
        return pl.BlockSpec(block, lambda b, j, pt: (pt[b, j * n_pages_per_step + r], 0, 0))
    return [make(r) for r in range(n_pages_per_step)]


def _idx_sample_kernel(t_new, pps, pt_ref, qi_ref, w_ref, kin_ref, *rest):
    pages = rest[:pps]
    sc_out, scn_out = rest[pps:]
    j = pl.program_id(1)
    qi = qi_ref[0]
    w = w_ref[0]
    qib = qi.astype(BF16)

    def token_scores(y):
        z = jnp.maximum(y, 0.0) * w
        return jnp.sum(z.reshape(t_new, IDX_HEADS, y.shape[1]), axis=1)

    for r in range(pps):
        y = _dot(qib, pages[r][0].astype(BF16))
        sc_out[0, :, r * PAGE_SIZE:(r + 1) * PAGE_SIZE] = token_scores(y)

    @pl.when(j == 0)
    def _():
        qf = qib.astype(F32)
        lane = _iota((t_new, LANES), 1)
        tok = _iota((t_new, LANES), 0)
        new = jnp.full((t_new, LANES), -jnp.inf, F32)
        for tk in range(t_new):
            kr = kin_ref[0, tk:tk + 1, :].astype(BF16).astype(F32)
            col = token_scores(jnp.sum(qf * kr, axis=-1, keepdims=True))
            new = jnp.where((lane == tk) & (tok >= tk), col, new)
        scn_out[0] = new


def _idx_sample(page_table, qi32, w32, ki_new, cache_idx_kt):
    db, rows, _ = qi32.shape
    t_new = rows // IDX_HEADS
    n_pages = page_table.shape[1]
    pps = min(IDX_PAGES_PER_STEP, n_pages)
    steps = n_pages // pps
    span = pps * PAGE_SIZE
    grid_spec = pltpu.PrefetchScalarGridSpec(
        num_scalar_prefetch=1, grid=(db, steps),
        in_specs=[pl.BlockSpec((1, rows, IDX_DIM), lambda b, j, pt: (b, 0, 0)),
                  pl.BlockSpec((1, rows, 1), lambda b, j, pt: (b, 0, 0)),
                  pl.BlockSpec((1, t_new, IDX_DIM), lambda b, j, pt: (b, 0, 0))]
        + _page_specs((1, IDX_DIM, PAGE_SIZE), pps),
        out_specs=[pl.BlockSpec((1, t_new, span), lambda b, j, pt: (b, 0, j)),
                   pl.BlockSpec((1, t_new, LANES), lambda b, j, pt: (b, 0, 0))])
    return pl.pallas_call(
        functools.partial(_idx_sample_kernel, t_new, pps), grid_spec=grid_spec,
        out_shape=[jax.ShapeDtypeStruct((db, t_new, n_pages * PAGE_SIZE), F32),
                   jax.ShapeDtypeStruct((db, t_new, LANES), F32)],
        compiler_params=_cparams(("parallel", "arbitrary")), name="idx_sample",
    )(page_table, qi32, w32, ki_new, *([cache_idx_kt] * pps))


def _thr_sample_kernel(topk, sc_ref, scn_ref, b_out, bn_out):
    kth = jnp.float32(topk)
    n_chunks = sc_ref.shape[1] // LANES

    def count(pred):
        return (jnp.sum(pred(sc_ref[...]).astype(F32), axis=-1, keepdims=True)
                + jnp.sum(pred(scn_ref[...]).astype(F32), axis=-1, keepdims=True))

    thr, cnt_thr = _kth_largest(lambda t: count(lambda x: x >= t), kth, (THR_ROWS, 1))
    has_tie = jnp.max(cnt_thr) > kth

    @pl.when(jnp.logical_not(has_tie))
    def _():
        b_out[...] = jnp.where(sc_ref[...] >= thr, 0.0, NEG)
        bn_out[...] = jnp.where(scn_ref[...] >= thr, 0.0, NEG)

    @pl.when(has_tie)
    def _():
        need = kth - count(lambda x: x > thr)
        utri = (_iota((LANES, LANES), 0) < _iota((LANES, LANES), 1)).astype(BF16)
        carry = jnp.zeros((THR_ROWS, 1), F32)
        for c in range(n_chunks + 1):
            x = sc_ref[:, c * LANES:(c + 1) * LANES] if c < n_chunks else scn_ref[...]
            eq = x == thr
            before = _dot(eq.astype(BF16), utri) + carry
            sel = (x > thr) | (eq & (before < need))
            bias = jnp.where(sel, 0.0, NEG)
            if c < n_chunks:
                b_out[:, c * LANES:(c + 1) * LANES] = bias
            else:
                bn_out[...] = bias
            carry = carry + jnp.sum(eq.astype(F32), axis=-1, keepdims=True)


def _thr_sample(sc, scn, topk):
    n, width = sc.shape
    return pl.pallas_call(
        functools.partial(_thr_sample_kernel, topk), grid=(n // THR_ROWS,),
        in_specs=[pl.BlockSpec((THR_ROWS, width), lambda i: (i, 0)),
                  pl.BlockSpec((THR_ROWS, LANES), lambda i: (i, 0))],
        out_specs=[pl.BlockSpec((THR_ROWS, width), lambda i: (i, 0)),
                   pl.BlockSpec((THR_ROWS, LANES), lambda i: (i, 0))],
        out_shape=[jax.ShapeDtypeStruct((n, width), F32), jax.ShapeDtypeStruct((n, LANES), F32)],
        compiler_params=_cparams(("parallel",)), name="thr_sample",
    )(sc, scn)


def _attn_sample_kernel(t_new, pps, pt_ref, q_ref, bias_ref, biasn_ref, kn_ref, vn_ref, *rest):
    kpages = rest[:pps]
    vpages = rest[pps:2 * pps]
    o_ref, m_ref, l_ref, acc_ref, qbd_ref = rest[2 * pps:]
    j = pl.program_id(1)
    rows = t_new * B_HEADS
    mask8 = (_iota((SUBLANES, B_WIDTH), 1) // HEAD_DIM) == _iota((SUBLANES, B_WIDTH), 0)

    def per_token(ref, width):
        return jnp.concatenate(
            [jnp.broadcast_to(ref[0, i:i + 1, :], (B_HEADS, width)) for i in range(t_new)], axis=0)

    @pl.when(j == 0)
    def _():
        m_ref[...] = jnp.full((rows, 1), NEG, F32)
        l_ref[...] = jnp.zeros((rows, 1), F32)
        acc_ref[...] = jnp.zeros((rows, B_WIDTH), F32)
        qbd_ref[...] = jnp.concatenate(
            [jnp.where(mask8, jnp.broadcast_to(q_ref[0, i:i + 1, :] * ATTN_SCALE, (B_HEADS, B_WIDTH)), 0.0)
             for i in range(t_new)], axis=0).astype(BF16)

    qbd = qbd_ref[...]
    span = pps * PAGE_SIZE
    s = jnp.concatenate([_dot(qbd, kpages[r][0].astype(BF16)) for r in range(pps)], axis=1)
    s = s + per_token(bias_ref, span)
    m_old = m_ref[...]
    m_new = jnp.maximum(m_old, jnp.max(s, axis=-1, keepdims=True))
    alpha = jnp.exp(m_old - m_new)
    p = jnp.exp(s - m_new)
    l_ref[...] = alpha * l_ref[...] + jnp.sum(p, axis=-1, keepdims=True)
    pb = p.astype(BF16)
    pv = jnp.zeros((rows, B_WIDTH), F32)
    for r in range(pps):
        pv = pv + _dot_nt(pb[:, r * PAGE_SIZE:(r + 1) * PAGE_SIZE], vpages[r][0].astype(BF16))
    acc_ref[...] = alpha * acc_ref[...] + pv
    m_ref[...] = m_new

    @pl.when(j == pl.num_programs(1) - 1)
    def _():
        qf = qbd.astype(F32)
        bn = per_token(biasn_ref, LANES)
        cols = []
        for tk in range(t_new):
            kr = kn_ref[0, tk:tk + 1, :].astype(BF16).astype(F32)
            cols.append(jnp.sum(qf * kr, axis=-1, keepdims=True) + bn[:, tk:tk + 1])
        m_old = m_ref[...]
        m_new = m_old
        for cval in cols:
            m_new = jnp.maximum(m_new, cval)
        alpha = jnp.exp(m_old - m_new)
        l = alpha * l_ref[...]
        acc = alpha * acc_ref[...]
        for tk in range(t_new):
            pk = jnp.exp(cols[tk] - m_new)
            l = l + pk
            acc = acc + pk.astype(BF16).astype(F32) * vn_ref[0, tk:tk + 1, :].astype(BF16).astype(F32)
        o = acc / l
        for i in range(t_new):
            slab = jnp.where(mask8, o[i * B_HEADS:(i + 1) * B_HEADS, :], 0.0)
            o_ref[0, i * B_HEADS:(i + 1) * B_HEADS, :] = jnp.broadcast_to(
                jnp.sum(slab, axis=0, keepdims=True), (B_HEADS, B_WIDTH))


def _attn_sample(page_table, q3, bias3, biasn3, k_new, v_new, cache_kt, cache_vt):
    db, t_new, _ = q3.shape
    n_pages = page_table.shape[1]
    pps = min(ATTN_PAGES_PER_STEP, n_pages)
    steps = n_pages // pps
    span = pps * PAGE_SIZE
    rows = t_new * B_HEADS
    per_b = lambda w: pl.BlockSpec((1, t_new, w), lambda b, j, pt: (b, 0, 0))
    grid_spec = pltpu.PrefetchScalarGridSpec(
        num_scalar_prefetch=1, grid=(db, steps),
        in_specs=[per_b(B_WIDTH),
                  pl.BlockSpec((1, t_new, span), lambda b, j, pt: (b, 0, j)),
                  per_b(LANES), per_b(B_WIDTH), per_b(B_WIDTH)]
        + _page_specs((1, B_WIDTH, PAGE_SIZE), pps)
        + _page_specs((1, B_WIDTH, PAGE_SIZE), pps),
        out_specs=pl.BlockSpec((1, rows, B_WIDTH), lambda b, j, pt: (b, 0, 0)),
        scratch_shapes=[pltpu.VMEM((rows, 1), F32), pltpu.VMEM((rows, 1), F32),
                        pltpu.VMEM((rows, B_WIDTH), F32), pltpu.VMEM((rows, B_WIDTH), BF16)])
    return pl.pallas_call(
        functools.partial(_attn_sample_kernel, t_new, pps), grid_spec=grid_spec,
        out_shape=jax.ShapeDtypeStruct((db, rows, B_WIDTH), F32),
        compiler_params=_cparams(("parallel", "arbitrary")), name="attn_sample",
    )(page_table, q3, bias3, biasn3, k_new, v_new, *([cache_kt] * pps), *([cache_vt] * pps))


def _merge_kernel(x_ref, ma_ref, yb_ref, sgb_ref, yc_ref, sgc_ref, wo_ref, g_ref, y_out):
    mb = (yb_ref[...].astype(F32) * sgb_ref[...].astype(F32)).astype(BF16)
    mc = (yc_ref[...].astype(F32) * sgc_ref[...].astype(F32)).astype(BF16)
    o = (_dot(ma_ref[...], wo_ref[0:A_WIDTH, :])
         + _dot(mb, wo_ref[A_WIDTH:A_WIDTH + B_WIDTH, :])
         + _dot(mc, wo_ref[A_WIDTH + B_WIDTH:, :]))
    z = x_ref[...] + o
    ms = jnp.mean(z * z, axis=-1, keepdims=True)
    y_out[...] = z * lax.rsqrt(ms + EPS) * g_ref[...]


def _merge(x, mixa, yb, sgb, yc, sgc, wo, g_final):
    n = x.shape[0]
    tm = min(MERGE_TM, n)
    row = lambda w: pl.BlockSpec((tm, w), lambda i: (i, 0))
    return pl.pallas_call(
        _merge_kernel, grid=(n // tm,),
        in_specs=[row(D_MODEL), row(A_WIDTH), row(B_WIDTH), row(B_WIDTH), row(C_WIDTH), row(C_WIDTH),
                  pl.BlockSpec(wo.shape, lambda i: (0, 0)), pl.BlockSpec(g_final.shape, lambda i: (0, 0))],
        out_specs=row(D_MODEL),
        out_shape=jax.ShapeDtypeStruct((n, D_MODEL), F32),
        compiler_params=_cparams(("parallel",)), name="merge",
    )(x, mixa, yb, sgb, yc, sgc, wo, g_final)


def _rope_tables(pos):
    inv = ROPE_THETA ** (-jnp.arange(0, HEAD_DIM, 2, dtype=F32) / HEAD_DIM)
    ang = pos.astype(F32)[:, None] * inv[None, :]
    c = jnp.cos(ang)
    s = jnp.sin(ang)
    return jnp.concatenate([c, c, c, c], axis=1), jnp.concatenate([-s, s, -s, s], axis=1)


def _heads_last(t, heads):
    b, _, n = t.shape
    return t.reshape(b, heads, HEAD_DIM, n).transpose(0, 3, 1, 2)


def _channels_first(t):
    p, n, heads, d = t.shape
    return t.transpose(0, 2, 3, 1).reshape(p, heads * d, n)


def kernel(x_prompt, x_sample, mem_prompt, cache_k, cache_v, cache_idx_k, cache_mem_k, cache_mem_v,
           page_table, g_norm, w_in, w_spatial, b_spatial, g_v, w_mem_kv, g_mem, w_out, g_final):
    batch, seq, _ = x_prompt.shape
    db, t_new, _ = x_sample.shape
    n_pages = page_table.shape[1]
    past = n_pages * PAGE_SIZE

    ki_cols = w_in[:, 3328:3392]
    wi_cols = w_in[:, 3392:3400]
    w_r = jnp.concatenate(
        [w_in[:, :3328], w_in[:, 3400:3912], ki_cols, ki_cols, wi_cols,
         jnp.zeros((D_MODEL, LANES - IDX_HEADS), F32)], axis=1).astype(BF16)
    g_norm2 = g_norm.reshape(1, D_MODEL)
    g_v2 = g_v.reshape(1, A_WIDTH)
    bs_prompt = jnp.repeat(b_spatial.T, HEAD_DIM, axis=1)
    reps = CHUNK // t_new
    ws_sample = jnp.tile(w_spatial[:, :t_new, :t_new], (1, reps, reps))
    bs_sample = jnp.tile(jnp.repeat(b_spatial[:, :t_new].T, HEAD_DIM, axis=1), (reps, 1))
    wo_b = w_out.astype(BF16)
    g_final2 = g_final.reshape(1, D_MODEL)

    cos_p, sin_p = _rope_tables(jnp.arange(seq, dtype=jnp.int32))
    pos_s = past + jnp.arange(t_new, dtype=jnp.int32)
    cos_s, sin_s = _rope_tables(jnp.tile(pos_s, TM // t_new))

    xp = x_prompt.reshape(batch * seq, D_MODEL)
    (kt, vt, kit, kbf, vtb, ki2, qz, qiz, wt, sgb, mixa, qc, sgc) = _proj(
        xp, g_norm2, w_r, cos_p, sin_p, w_spatial, bs_prompt, g_v2, CHUNK, seq // TM, False)
    mkt, mvt = _memkv(mem_prompt.reshape(batch * N_MEM, D_MODEL), g_mem.reshape(1, D_MODEL),
                      w_mem_kv.astype(BF16))
    yc = _memattn(qc.reshape(batch * seq // MEMATTN_TM, MEMATTN_TM, C_WIDTH), mkt, mvt,
                  seq // MEMATTN_TM).reshape(batch * seq, C_WIDTH)
    yb = _dsa_prompt(qz, qiz, wt, ki2, kbf, vtb, batch, seq)
    y_prompt = _merge(xp, mixa, yb, sgb, yc, sgc, wo_b, g_final2).reshape(batch, seq, D_MODEL)

    ns = db * t_new
    xs = x_sample.reshape(ns, D_MODEL)
    (q_s, k_s, v_s, qi_s, ki_s, wi_s, sgb_s, mixa_s, qc_s, sgc_s, vn_s) = _proj(
        xs, g_norm2, w_r, cos_s, sin_s, ws_sample, bs_sample, g_v2, t_new, 1, True)
    sc, scn = _idx_sample(page_table, qi_s.reshape(db, t_new * IDX_HEADS, IDX_DIM),
                          wi_s.reshape(db, t_new * IDX_HEADS, 1), ki_s.reshape(db, t_new, IDX_DIM),
                          cache_idx_k.transpose(0, 2, 1))
    topk_s = min(TOPK_MAX, (past + t_new) // 4)
    bias, biasn = _thr_sample(sc.reshape(ns, past), scn.reshape(ns, LANES), topk_s)
    yb_pad = _attn_sample(page_table, q_s.reshape(db, t_new, B_WIDTH), bias.reshape(db, t_new, past),
                          biasn.reshape(db, t_new, LANES), k_s.reshape(db, t_new, B_WIDTH),
                          v_s.reshape(db, t_new, B_WIDTH), _channels_first(cache_k), _channels_first(cache_v))
    yb_s = yb_pad[:, ::B_HEADS, :].reshape(ns, B_WIDTH)
    yc_pad = _memattn_sample(qc_s.reshape(db, t_new, C_WIDTH),
                             _channels_first(cache_mem_k), _channels_first(cache_mem_v))
    yc_s = yc_pad[:, ::SUBLANES, :].reshape(ns, C_WIDTH)
    y_sample = _merge(xs, mixa_s, yb_s, sgb_s, yc_s, sgc_s, wo_b, g_final2).reshape(db, t_new, D_MODEL)

    return (y_prompt, y_sample,
            _heads_last(kt, B_HEADS), _heads_last(vt, B_HEADS), kit.transpose(0, 2, 1),
            _heads_last(mkt, C_HEADS), _heads_last(mvt, C_HEADS),
            k_s.reshape(db, t_new, B_HEADS, HEAD_DIM), v_s.reshape(db, t_new, B_HEADS, HEAD_DIM),
            ki_s.reshape(db, t_new, IDX_DIM), vn_s.reshape(db, t_new, A_WIDTH))
```

```python
import functools

import jax
import jax.numpy as jnp
from jax import lax
from jax.experimental import pallas as pl
from jax.experimental.pallas import tpu as pltpu

F32 = jnp.float32
BF16 = jnp.bfloat16

HEAD_DIM = 64
HALF = HEAD_DIM // 2
D_MODEL = 1024
A_GROUPS = 4
A_WIDTH = A_GROUPS * HEAD_DIM
CHUNK = 128
B_HEADS = 8
B_WIDTH = B_HEADS * HEAD_DIM
IDX_HEADS = 8
IDX_DIM = 64
TOPK_MAX = 256
N_MEM = 256
C_HEADS = 4
C_WIDTH = C_HEADS * HEAD_DIM
PAGE_SIZE = 128
ROPE_THETA = 10000.0
EPS = 1e-6
IDX_W_SCALE = (IDX_HEADS * IDX_DIM) ** -0.5
ATTN_SCALE = HEAD_DIM ** -0.5
LOG2E = 1.4426950408889634

LANES = 128
SUBLANES = 8
NEG = -1e30
FLT_MAX = 3.4028234663852886e38
PACKED_ROWS = 16

TM = 512
MEMATTN_TM = 1024
MEMATTN_SEQS_PER_STEP = 8
MERGE_TM = 1024
QB = 256
KC = 256
V_ROWS = HEAD_DIM + 16
QK_LEAD = 3
IDX_PAGES_PER_STEP = 64
ATTN_PAGES_PER_STEP = 32
THR_ROWS = 64
VMEM_LIMIT = 52 * 1024 * 1024

OFF_U, OFF_VA, OFF_GA = 0, 256, 512
OFF_Q, OFF_K, OFF_V, OFF_GB, OFF_QI = 768, 1280, 1792, 2304, 2816
OFF_QC, OFF_GC, OFF_KI2, OFF_WI, W_TOTAL = 3328, 3584, 3840, 3968, 4096


def _cparams(sem):
    return pltpu.CompilerParams(dimension_semantics=sem, vmem_limit_bytes=VMEM_LIMIT)


def _iota(shape, dim):
    return lax.broadcasted_iota(jnp.int32, shape, dim)


def _silu(x):
    return x / (1.0 + jnp.exp(-x))


def _dot(a, b):
    return jnp.dot(a, b, preferred_element_type=F32)


def _dot_nt(a, b):
    return lax.dot_general(a, b, (((1,), (1,)), ((), ())), preferred_element_type=F32)


def _key_to_float(k):
    bits = k ^ ((k >> 31) & jnp.int32(0x7FFFFFFF))
    return lax.bitcast_convert_type(bits, F32)


def _bitwise_kth(count_ge, kth, shape, bits):
    cnt0 = count_ge(jnp.zeros(shape, jnp.int32))
    ok0 = cnt0 >= kth
    key = jnp.where(ok0, jnp.int32(0), jnp.int32(-2 ** (bits - 1)))
    cnt_key = jnp.where(ok0, cnt0, jnp.float32(2 ** 24))

    def bit_body(it, carry):
        key, cnt_key = carry
        trial = key + lax.shift_left(jnp.int32(1), bits - 2 - it)
        cnt = count_ge(trial)
        ok = cnt >= kth
        return jnp.where(ok, trial, key), jnp.where(ok, cnt, cnt_key)

    return lax.fori_loop(0, bits - 1, bit_body, (key, cnt_key))


def _kth_largest(count_ge, kth, shape):
    key, cnt_key = _bitwise_kth(lambda k: count_ge(_key_to_float(k)), kth, shape, 32)
    return _key_to_float(key), cnt_key


def _rope(t, cos, sin):
    outs = []
    first = (_iota((t.shape[0], LANES), 1) % HEAD_DIM) < HALF
    for j in range(t.shape[1] // LANES):
        x = t[:, j * LANES:(j + 1) * LANES]
        partner = jnp.where(first, pltpu.roll(x, LANES - HALF, 1), pltpu.roll(x, HALF, 1))
        outs.append(x * cos + partner * sin)
    return outs[0] if len(outs) == 1 else jnp.concatenate(outs, axis=1)


def _proj_kernel(chunk, sample, x_ref, g_ref, w_ref, cos_ref, sin_ref, ws_ref, bs_ref, gv_ref, *outs):
    if sample:
        q_out, k_out, v_out, qi_out, ki_out, wi_out, sgb_out, mixa_out, qc_out, sgc_out, vn_out = outs
    else:
        (kt_out, vt_out, kit_out, kbf_out, vtb_out, ki2_out, qz_out, qiz_out, wt_out,
         sgb_out, mixa_out, qc_out, sgc_out) = outs
    tm = x_ref.shape[0]
    x = x_ref[...]
    ms = jnp.mean(x * x, axis=-1, keepdims=True)
    h = (x * lax.rsqrt(ms + EPS) * g_ref[...]).astype(BF16)
    cos = cos_ref[...]
    sin = sin_ref[...]

    def seg(a, b):
        return _dot(h, w_ref[:, a:b])

    half_id = (_iota((tm, LANES), 1) // HEAD_DIM)

    def head_split(t, out):
        for hd in range(B_HEADS):
            pair = t[:, (hd // 2) * LANES:(hd // 2 + 1) * LANES]
            out[hd] = jnp.where(half_id == hd % 2, pair, 0.0).astype(BF16)

    q = _rope(seg(OFF_Q, OFF_K), cos, sin)
    k = _rope(seg(OFF_K, OFF_V), cos, sin)
    v = seg(OFF_V, OFF_GB)
    qi = _rope(seg(OFF_QI, OFF_QC), cos, sin)
    ki2 = _rope(seg(OFF_KI2, OFF_WI), cos, sin)
    wi = seg(OFF_WI, W_TOTAL) * IDX_W_SCALE
    qc = seg(OFF_QC, OFF_GC) * ATTN_SCALE
    sgb_out[...] = _silu(seg(OFF_GB, OFF_QI)).astype(BF16)
    sgc_out[...] = _silu(seg(OFF_GC, OFF_KI2)).astype(BF16)
    qc_out[...] = qc.astype(qc_out.dtype)
    if sample:
        q_out[...] = q
        k_out[...] = k
        v_out[...] = v
        qi_out[...] = qi
        ki_out[...] = ki2[:, :IDX_DIM]
        wi_out[...] = wi[:, :IDX_HEADS]
    else:
        head_split(q * (ATTN_SCALE * LOG2E), qz_out)
        head_split(qi, qiz_out)
        kt_out[0] = k.T
        kbf_out[...] = k.astype(BF16)
        vt = v.T
        vt_out[0] = vt
        vtb = vt.astype(BF16)
        ones = jnp.ones((V_ROWS - HEAD_DIM, KC), BF16)
        for j in range(tm // KC):
            for hd in range(B_HEADS):
                vtb_out[j, hd * V_ROWS:hd * V_ROWS + HEAD_DIM, :] = (
                    vtb[hd * HEAD_DIM:(hd + 1) * HEAD_DIM, j * KC:(j + 1) * KC])
                vtb_out[j, hd * V_ROWS + HEAD_DIM:(hd + 1) * V_ROWS, :] = ones
        kit_out[0] = ki2.T[:IDX_DIM, :]
        ki2_out[...] = ki2.astype(BF16)
        wt_out[...] = wi.T[:IDX_HEADS, :]

    u = seg(OFF_U, OFF_VA)
    va = seg(OFF_VA, OFF_GA)
    ga = seg(OFF_GA, OFF_Q)
    grp_r = _iota((A_WIDTH, A_WIDTH), 0) // HEAD_DIM
    grp_c = _iota((A_WIDTH, A_WIDTH), 1) // HEAD_DIM
    gmat = jnp.where(grp_r == grp_c, 1.0 / HEAD_DIM, 0.0).astype(BF16)

    def group_mean(t):
        hi = t.astype(BF16)
        lo = (t - hi.astype(F32)).astype(BF16)
        return _dot(hi, gmat) + _dot(lo, gmat)

    mu = group_mean(va)
    d = va - mu
    var = group_mean(d * d)
    vn = d * lax.rsqrt(var + EPS) * gv_ref[...]
    if sample:
        vn_out[...] = vn
    vnb = vn.astype(BF16)
    row = _iota((CHUNK, CHUNK), 0)
    col = _iota((CHUNK, CHUNK), 1)
    causal = (row >= col) & ((row // chunk) == (col // chunk))
    lane_grp = _iota((CHUNK, A_WIDTH), 1) // HEAD_DIM
    wms = [jnp.where(causal, ws_ref[g], 0.0).astype(BF16) for g in range(A_GROUPS)]
    mixes = []
    for j in range(tm // CHUNK):
        vc = vnb[j * CHUNK:(j + 1) * CHUNK, :]
        m = bs_ref[...]
        for g in range(A_GROUPS):
            m = m + jnp.where(lane_grp == g, _dot(wms[g], vc), 0.0)
        mixes.append(m)
    mix = jnp.concatenate(mixes, axis=0)
    mixa_out[...] = (_silu(ga) * u * mix).astype(BF16)


def _proj(x, g_norm, w_r, cos_t, sin_t, ws, bs, g_v, chunk, seq_blocks, sample):
    n = x.shape[0]
    nt = n // TM
    nb = nt // seq_blocks
    row = lambda w: pl.BlockSpec((TM, w), lambda i: (i, 0))
    const2 = lambda a: pl.BlockSpec(a.shape, lambda i: (0, 0))
    sds = jax.ShapeDtypeStruct
    in_specs = [
        row(D_MODEL), const2(g_norm), const2(w_r),
        pl.BlockSpec((TM, LANES), lambda i: (i % seq_blocks, 0)),
        pl.BlockSpec((TM, LANES), lambda i: (i % seq_blocks, 0)),
        pl.BlockSpec(ws.shape, lambda i: (0, 0, 0)), const2(bs), const2(g_v),
    ]
    if sample:
        outs = [
            (sds((n, B_WIDTH), F32), row(B_WIDTH)),
            (sds((n, B_WIDTH), F32), row(B_WIDTH)),
            (sds((n, B_WIDTH), F32), row(B_WIDTH)),
            (sds((n, IDX_HEADS * IDX_DIM), F32), row(IDX_HEADS * IDX_DIM)),
            (sds((n, IDX_DIM), F32), row(IDX_DIM)),
            (sds((n, IDX_HEADS), F32), row(IDX_HEADS)),
            (sds((n, B_WIDTH), BF16), row(B_WIDTH)),
            (sds((n, A_WIDTH), BF16), row(A_WIDTH)),
            (sds((n, C_WIDTH), F32), row(C_WIDTH)),
            (sds((n, C_WIDTH), BF16), row(C_WIDTH)),
            (sds((n, A_WIDTH), F32), row(A_WIDTH)),
        ]
    else:
        seq = seq_blocks * TM
        tspec = lambda w: pl.BlockSpec((1, w, TM), lambda i: (i // seq_blocks, 0, i % seq_blocks))
        hspec = pl.BlockSpec((B_HEADS, TM, LANES), lambda i: (0, i, 0))
        outs = [
            (sds((nb, B_WIDTH, seq), F32), tspec(B_WIDTH)),
            (sds((nb, B_WIDTH, seq), F32), tspec(B_WIDTH)),
            (sds((nb, IDX_DIM, seq), F32), tspec(IDX_DIM)),
            (sds((n, B_WIDTH), BF16), row(B_WIDTH)),
            (sds((n // KC, B_HEADS * V_ROWS, KC), BF16),
             pl.BlockSpec((TM // KC, B_HEADS * V_ROWS, KC), lambda i: (i, 0, 0))),
            (sds((n, LANES), BF16), row(LANES)),
            (sds((B_HEADS, n, LANES), BF16), hspec),
            (sds((IDX_HEADS, n, LANES), BF16), hspec),
            (sds((IDX_HEADS, n), F32), pl.BlockSpec((IDX_HEADS, TM), lambda i: (0, i))),
            (sds((n, B_WIDTH), BF16), row(B_WIDTH)),
            (sds((n, A_WIDTH), BF16), row(A_WIDTH)),
            (sds((n, C_WIDTH), BF16), row(C_WIDTH)),
            (sds((n, C_WIDTH), BF16), row(C_WIDTH)),
        ]
    return pl.pallas_call(
        functools.partial(_proj_kernel, chunk, sample),
        grid=(nt,), in_specs=in_specs, out_specs=[o[1] for o in outs], out_shape=[o[0] for o in outs],
        compiler_params=_cparams(("parallel",)), name="proj",
    )(x, g_norm, w_r, cos_t, sin_t, ws, bs, g_v)


def _memkv_kernel(m_ref, g_ref, w_ref, mkt_out, mvt_out):
    x = m_ref[...]
    ms = jnp.mean(x * x, axis=-1, keepdims=True)
    h = (x * lax.rsqrt(ms + EPS) * g_ref[...]).astype(BF16)
    kv = _dot(h, w_ref[...])
    mkt_out[0] = kv[:, :C_WIDTH].T
    mvt_out[0] = kv[:, C_WIDTH:].T


def _memkv(mem, g_mem, w_kv):
    n = mem.shape[0]
    nb = n // N_MEM
    return pl.pallas_call(
        _memkv_kernel, grid=(nb,),
        in_specs=[pl.BlockSpec((N_MEM, D_MODEL), lambda i: (i, 0)),
                  pl.BlockSpec(g_mem.shape, lambda i: (0, 0)),
                  pl.BlockSpec(w_kv.shape, lambda i: (0, 0))],
        out_specs=[pl.BlockSpec((1, C_WIDTH, N_MEM), lambda i: (i, 0, 0))] * 2,
        out_shape=[jax.ShapeDtypeStruct((nb, C_WIDTH, N_MEM), F32)] * 2,
        compiler_params=_cparams(("parallel",)), name="memkv",
    )(mem, g_mem, w_kv)


def _memattn_kernel(q_ref, mkt_ref, mvt_ref, o_ref):
    q = q_ref[0]
    mkt = mkt_ref[0].astype(BF16)
    mvt = mvt_ref[0].astype(BF16)
    t = q.shape[0]
    lane_head = _iota((t, C_WIDTH), 1) // HEAD_DIM
    out = jnp.zeros((t, C_WIDTH), F32)
    for hd in range(C_HEADS):
        qh = jnp.where(lane_head == hd, q, 0.0).astype(BF16)
        s = _dot(qh, mkt)
        p = jnp.exp(s - jnp.max(s, axis=-1, keepdims=True))
        o = _dot_nt(p.astype(BF16), mvt) / jnp.sum(p, axis=-1, keepdims=True)
        out = out + jnp.where(lane_head == hd, o, 0.0)
    o_ref[0] = out.astype(o_ref.dtype)


def _memattn(q3, mkt3, mvt3, tiles_per_mem):
    g, t, _ = q3.shape
    mem_spec = pl.BlockSpec((1, C_WIDTH, N_MEM), lambda i: (i // tiles_per_mem, 0, 0))
    return pl.pallas_call(
        _memattn_kernel, grid=(g,),
        in_specs=[pl.BlockSpec((1, t, C_WIDTH), lambda i: (i, 0, 0)), mem_spec, mem_spec],
        out_specs=pl.BlockSpec((1, t, C_WIDTH), lambda i: (i, 0, 0)),
        out_shape=jax.ShapeDtypeStruct((g, t, C_WIDTH), BF16),
        compiler_params=_cparams(("parallel",)), name="memattn",
    )(q3, mkt3, mvt3)


def _memattn_sample_kernel(q_ref, mkt_ref, mvt_ref, o_ref):
    t = q_ref.shape[1]
    mask8 = (_iota((SUBLANES, C_WIDTH), 1) // HEAD_DIM) == _iota((SUBLANES, C_WIDTH), 0)
    for g in range(q_ref.shape[0]):
        q4 = q_ref[g]
        qbd = jnp.concatenate(
            [jnp.where(mask8, jnp.broadcast_to(q4[i:i + 1, :], (SUBLANES, C_WIDTH)), 0.0) for i in range(t)], axis=0)
        s = _dot(qbd.astype(BF16), mkt_ref[g].astype(BF16))
        p = jnp.exp(s - jnp.max(s, axis=-1, keepdims=True))
        o = _dot_nt(p.astype(BF16), mvt_ref[g].astype(BF16)) / jnp.sum(p, axis=-1, keepdims=True)
        for i in range(t):
            slab = jnp.where(mask8, o[i * SUBLANES:(i + 1) * SUBLANES, :], 0.0)
            o_ref[g, i * SUBLANES:(i + 1) * SUBLANES, :] = jnp.broadcast_to(
                jnp.sum(slab, axis=0, keepdims=True), (SUBLANES, C_WIDTH))


def _memattn_sample(q3, mkt3, mvt3):
    n, t, _ = q3.shape
    g = min(MEMATTN_SEQS_PER_STEP, n)
    mem_spec = pl.BlockSpec((g, C_WIDTH, N_MEM), lambda i: (i, 0, 0))
    return pl.pallas_call(
        _memattn_sample_kernel, grid=(n // g,),
        in_specs=[pl.BlockSpec((g, t, C_WIDTH), lambda i: (i, 0, 0)), mem_spec, mem_spec],
        out_specs=pl.BlockSpec((g, t * SUBLANES, C_WIDTH), lambda i: (i, 0, 0)),
        out_shape=jax.ShapeDtypeStruct((n, t * SUBLANES, C_WIDTH), F32),
        compiler_params=_cparams(("parallel",)), name="memattn_sample",
    )(q3, mkt3, mvt3)


def _dsa_prompt_kernel(topk, qz_ref, qiz_ref, wt_ref, ki2_ref, kbf_ref, vt_ref, o_ref,
                       sc_ref, acc_ref, m_ref, s_ref, sb_ref):
    i = pl.program_id(1)
    nk = i + 1
    npair = (nk + 1) // 2
    kth = jnp.float32(topk)

    def sum_rows(x):
        return jnp.sum(x.reshape(KC // SUBLANES, SUBLANES, QB), axis=0)

    def max_rows(x):
        return jnp.max(x.reshape(KC // SUBLANES, SUBLANES, QB), axis=0)

    def score_chunk(c, _):
        keys = ki2_ref[0, c]
        acc = jnp.zeros((KC, QB), F32)
        for hd in range(IDX_HEADS):
            y = _dot_nt(keys, qiz_ref[hd])
            acc = acc + jnp.maximum(y, 0.0) * wt_ref[hd:hd + 1, :]
        kpos = c * KC + _iota((KC, QB), 0)
        qpos = i * QB + _iota((KC, QB), 1)
        masked = jnp.where(kpos <= qpos, acc, -jnp.inf)
        sc_ref[c] = masked
        sb_ref[c] = masked.astype(BF16)
        return 0

    lax.fori_loop(0, nk, score_chunk, 0)

    @pl.when(nk % 2 == 1)
    def _():
        pad = jnp.minimum(nk, sc_ref.shape[0] - 1)
        sc_ref[pad] = jnp.full((KC, QB), -jnp.inf, F32)
        sb_ref[pad] = jnp.full((KC, QB), -jnp.inf, BF16)

    def count(pred):
        def body(j, cnt):
            return (cnt + sum_rows(pred(sc_ref[2 * j]).astype(F32))
                    + sum_rows(pred(sc_ref[2 * j + 1]).astype(F32)))
        return jnp.sum(lax.fori_loop(0, npair, body, jnp.zeros((SUBLANES, QB), F32)), axis=0, keepdims=True)

    def count_ge_packed(t):
        tb = t.astype(BF16)
        def slab_count(c):
            ones = jnp.where(sb_ref[c] >= tb, jnp.ones((), BF16), jnp.zeros((), BF16))
            acc = ones[0:PACKED_ROWS]
            for r in range(1, KC // PACKED_ROWS):
                acc = acc + ones[r * PACKED_ROWS:(r + 1) * PACKED_ROWS]
            return acc

        def body(j, cnt):
            return cnt + slab_count(2 * j) + slab_count(2 * j + 1)

        cnt16 = lax.fori_loop(0, npair, body, jnp.zeros((PACKED_ROWS, QB), BF16))
        return jnp.sum(cnt16.astype(F32), axis=0, keepdims=True)

    def write_ge(thr):
        def body(j, _):
            sc_ref[2 * j] = jnp.where(sc_ref[2 * j] >= thr, 0.0, NEG)
            sc_ref[2 * j + 1] = jnp.where(sc_ref[2 * j + 1] >= thr, 0.0, NEG)
            return 0
        lax.fori_loop(0, npair, body, 0)

    @pl.when(nk * KC <= topk)
    def _():
        write_ge(jnp.full((1, QB), -FLT_MAX, F32))

    @pl.when(nk * KC > topk)
    def _():
        def bf16_key_to_float_key(k16):
            return k16 * jnp.int32(2 ** 16) + jnp.where(k16 < 0, jnp.int32(2 ** 16 - 1), jnp.int32(0))

        tau_k16, _ = _bitwise_kth(
            lambda k16: count_ge_packed(_key_to_float(bf16_key_to_float_key(k16))), kth, (1, QB), 16)
        base = bf16_key_to_float_key(tau_k16) - jnp.int32(2 ** 16)

        def fine_bit(it, carry):
            off, cnt_off = carry
            cand = off + lax.shift_left(jnp.int32(1), 16 - it)
            cnt = count(lambda x: x >= _key_to_float(base + cand))
            ok = cnt >= kth
            return jnp.where(ok, cand, off), jnp.where(ok, cnt, cnt_off)

        off, cnt_thr = lax.fori_loop(0, 17, fine_bit,
                                     (jnp.zeros((1, QB), jnp.int32), jnp.full((1, QB), 2.0 ** 24, F32)))
        thr = _key_to_float(base + off)
        few = (i * QB + 1 + _iota((1, QB), 1)).astype(F32) <= kth
        thr = jnp.where(few, -FLT_MAX, thr)
        cnt_thr = jnp.where(few, kth, cnt_thr)
        has_tie = jnp.max(cnt_thr) > kth

        @pl.when(jnp.logical_not(has_tie))
        def _():
            write_ge(thr)

        @pl.when(has_tie)
        def _():
            need = kth - count(lambda x: x > thr)
            ltri = (_iota((KC, KC), 1) < _iota((KC, KC), 0)).astype(BF16)

            def body(c, carry):
                x = sc_ref[c]
                eq = x == thr
                before = _dot(ltri, eq.astype(BF16)) + carry
                sel = (x > thr) | (eq & (before < need))
                sc_ref[c] = jnp.where(sel, 0.0, NEG)
                return carry + jnp.sum(sum_rows(eq.astype(F32)), axis=0, keepdims=True)

            lax.fori_loop(0, nk, body, jnp.zeros((1, QB), F32))

    m_ref[...] = jnp.full((B_HEADS, QB), NEG, F32)
    acc_ref[...] = jnp.zeros((B_HEADS * V_ROWS, QB), F32)

    def biased_scores(c, hd):
        pair = hd // 2
        kc = kbf_ref[0, c, :, pair * LANES:(pair + 1) * LANES]
        return _dot_nt(kc, qz_ref[hd]) + sc_ref[c]

    for hd in range(B_HEADS):
        s_ref[hd] = biased_scores(0, hd)

    def attend(c, _):
        nxt = jnp.minimum(c + 1, nk - 1)
        ahead = [biased_scores(nxt, hd) for hd in range(QK_LEAD)]
        for hd in range(B_HEADS):
            rows = slice(hd * V_ROWS, (hd + 1) * V_ROWS)
            s = s_ref[hd]
            m_old = m_ref[hd:hd + 1, :]
            m_new = jnp.maximum(m_old, jnp.max(max_rows(s), axis=0, keepdims=True))
            alpha = jnp.exp2(m_old - m_new)
            p = jnp.exp2(s - m_new)
            acc_ref[rows, :] = alpha * acc_ref[rows, :] + _dot(vt_ref[0, c, rows, :], p.astype(BF16))
            m_ref[hd:hd + 1, :] = m_new
            s_ref[hd] = ahead[hd]
            if hd + QK_LEAD < B_HEADS:
                ahead.append(biased_scores(nxt, hd + QK_LEAD))
        return 0

    lax.fori_loop(0, nk, attend, 0)
    outs = []
    for hd in range(B_HEADS):
        base = hd * V_ROWS
        outs.append(acc_ref[base:base + HEAD_DIM, :] / acc_ref[base + HEAD_DIM:base + HEAD_DIM + 1, :])
    o_ref[...] = jnp.concatenate(outs, axis=0).T.astype(o_ref.dtype)


def _dsa_prompt(qz, qiz, wt, ki2, kbf, vtb, batch, seq):
    nq = seq // QB
    nc = seq // KC
    topk = min(TOPK_MAX, seq // 4)
    n = batch * seq
    qspec = pl.BlockSpec((B_HEADS, QB, LANES), lambda b, i: (0, b * nq + i, 0))
    return pl.pallas_call(
        functools.partial(_dsa_prompt_kernel, topk),
        grid=(batch, nq),
        in_specs=[qspec, qspec,
                  pl.BlockSpec((IDX_HEADS, QB), lambda b, i: (0, b * nq + i)),
                  pl.BlockSpec((1, nc, KC, LANES), lambda b, i: (b, 0, 0, 0)),
                  pl.BlockSpec((1, nc, KC, B_WIDTH), lambda b, i: (b, 0, 0, 0)),
                  pl.BlockSpec((1, nc, B_HEADS * V_ROWS, KC), lambda b, i: (b, 0, 0, 0))],
        out_specs=pl.BlockSpec((QB, B_WIDTH), lambda b, i: (b * nq + i, 0)),
        out_shape=jax.ShapeDtypeStruct((n, B_WIDTH), BF16),
        scratch_shapes=[pltpu.VMEM((nc, KC, QB), F32), pltpu.VMEM((B_HEADS * V_ROWS, QB), F32),
                        pltpu.VMEM((B_HEADS, QB), F32), pltpu.VMEM((B_HEADS, KC, QB), F32),
                        pltpu.VMEM((nc, KC, QB), BF16)],
        compiler_params=_cparams(("parallel", "arbitrary")), name="dsa_prompt",
    )(qz, qiz, wt, ki2.reshape(batch, nc, KC, LANES), kbf.reshape(batch, nc, KC, B_WIDTH),
      vtb.reshape(batch, nc, B_HEADS * V_ROWS, KC))


def _page_specs(block, n_pages_per_step):
    def make(r):
        return pl.BlockSpec(block, lambda b, j, pt: (pt[b, j * n_pages_per_step + r], 0, 0))
    return [make(r) for r in range(n_pages_per_step)]


def _idx_sample_kernel(t_new, pps, pt_ref, qi_ref, w_ref, kin_ref, *rest):
    pages = rest[:pps]
    sc_out, scn_out = rest[pps:]
    j = pl.program_id(1)
    qi = qi_ref[0]
    w = w_ref[0]
    qib = qi.astype(BF16)

    def token_scores(y):
        z = jnp.maximum(y, 0.0) * w
        return jnp.sum(z.reshape(t_new, IDX_HEADS, y.shape[1]), axis=1)

    for r in range(pps):
        y = _dot(qib, pages[r][0].astype(BF16))
        sc_out[0, :, r * PAGE_SIZE:(r + 1) * PAGE_SIZE] = token_scores(y)

    @pl.when(j == 0)
    def _():
        qf = qib.astype(F32)
        lane = _iota((t_new, LANES), 1)
        tok = _iota((t_new, LANES), 0)
        new = jnp.full((t_new, LANES), -jnp.inf, F32)
        for tk in range(t_new):
            kr = kin_ref[0, tk:tk + 1, :].astype(BF16).astype(F32)
            col = token_scores(jnp.sum(qf * kr, axis=-1, keepdims=True))
            new = jnp.where((lane == tk) & (tok >= tk), col, new)
        scn_out[0] = new


def _idx_sample(page_table, qi32, w32, ki_new, cache_idx_kt):
    db, rows, _ = qi32.shape
    t_new = rows // IDX_HEADS
    n_pages = page_table.shape[1]
    pps = min(IDX_PAGES_PER_STEP, n_pages)
    steps = n_pages // pps
    span = pps * PAGE_SIZE
    grid_spec = pltpu.PrefetchScalarGridSpec(
        num_scalar_prefetch=1, grid=(db, steps),
        in_specs=[pl.BlockSpec((1, rows, IDX_DIM), lambda b, j, pt: (b, 0, 0)),
                  pl.BlockSpec((1, rows, 1), lambda b, j, pt: (b, 0, 0)),
                  pl.BlockSpec((1, t_new, IDX_DIM), lambda b, j, pt: (b, 0, 0))]
        + _page_specs((1, IDX_DIM, PAGE_SIZE), pps),
        out_specs=[pl.BlockSpec((1, t_new, span), lambda b, j, pt: (b, 0, j)),
                   pl.BlockSpec((1, t_new, LANES), lambda b, j, pt: (b, 0, 0))])
    return pl.pallas_call(
        functools.partial(_idx_sample_kernel, t_new, pps), grid_spec=grid_spec,
        out_shape=[jax.ShapeDtypeStruct((db, t_new, n_pages * PAGE_SIZE), F32),
                   jax.ShapeDtypeStruct((db, t_new, LANES), F32)],
        compiler_params=_cparams(("parallel", "arbitrary")), name="idx_sample",
    )(page_table, qi32, w32, ki_new, *([cache_idx_kt] * pps))


def _thr_sample_kernel(topk, sc_ref, scn_ref, b_out, bn_out):
    kth = jnp.float32(topk)
    n_chunks = sc_ref.shape[1] // LANES

    def count(pred):
        return (jnp.sum(pred(sc_ref[...]).astype(F32), axis=-1, keepdims=True)
                + jnp.sum(pred(scn_ref[...]).astype(F32), axis=-1, keepdims=True))

    thr, cnt_thr = _kth_largest(lambda t: count(lambda x: x >= t), kth, (THR_ROWS, 1))
    has_tie = jnp.max(cnt_thr) > kth

    @pl.when(jnp.logical_not(has_tie))
    def _():
        b_out[...] = jnp.where(sc_ref[...] >= thr, 0.0, NEG)
        bn_out[...] = jnp.where(scn_ref[...] >= thr, 0.0, NEG)

    @pl.when(has_tie)
    def _():
        need = kth - count(lambda x: x > thr)
        utri = (_iota((LANES, LANES), 0) < _iota((LANES, LANES), 1)).astype(BF16)
        carry = jnp.zeros((THR_ROWS, 1), F32)
        for c in range(n_chunks + 1):
            x = sc_ref[:, c * LANES:(c + 1) * LANES] if c < n_chunks else scn_ref[...]
            eq = x == thr
            before = _dot(eq.astype(BF16), utri) + carry
            sel = (x > thr) | (eq & (before < need))
            bias = jnp.where(sel, 0.0, NEG)
            if c < n_chunks:
                b_out[:, c * LANES:(c + 1) * LANES] = bias
            else:
                bn_out[...] = bias
            carry = carry + jnp.sum(eq.astype(F32), axis=-1, keepdims=True)


def _thr_sample(sc, scn, topk):
    n, width = sc.shape
    return pl.pallas_call(
        functools.partial(_thr_sample_kernel, topk), grid=(n // THR_ROWS,),
        in_specs=[pl.BlockSpec((THR_ROWS, width), lambda i: (i, 0)),
                  pl.BlockSpec((THR_ROWS, LANES), lambda i: (i, 0))],
        out_specs=[pl.BlockSpec((THR_ROWS, width), lambda i: (i, 0)),
                   pl.BlockSpec((THR_ROWS, LANES), lambda i: (i, 0))],
        out_shape=[jax.ShapeDtypeStruct((n, width), F32), jax.ShapeDtypeStruct((n, LANES), F32)],
        compiler_params=_cparams(("parallel",)), name="thr_sample",
    )(sc, scn)


def _attn_sample_kernel(t_new, pps, pt_ref, q_ref, bias_ref, biasn_ref, kn_ref, vn_ref, *rest):
    kpages = rest[:pps]
    vpages = rest[pps:2 * pps]
    o_ref, m_ref, l_ref, acc_ref, qbd_ref = rest[2 * pps:]
    j = pl.program_id(1)
    rows = t_new * B_HEADS
    mask8 = (_iota((SUBLANES, B_WIDTH), 1) // HEAD_DIM) == _iota((SUBLANES, B_WIDTH), 0)

    def per_token(ref, width):
        return jnp.concatenate(
            [jnp.broadcast_to(ref[0, i:i + 1, :], (B_HEADS, width)) for i in range(t_new)], axis=0)

    @pl.when(j == 0)
    def _():
        m_ref[...] = jnp.full((rows, 1), NEG, F32)
        l_ref[...] = jnp.zeros((rows, 1), F32)
        acc_ref[...] = jnp.zeros((rows, B_WIDTH), F32)
        qbd_ref[...] = jnp.concatenate(
            [jnp.where(mask8, jnp.broadcast_to(q_ref[0, i:i + 1, :] * ATTN_SCALE, (B_HEADS, B_WIDTH)), 0.0)
             for i in range(t_new)], axis=0).astype(BF16)

    qbd = qbd_ref[...]
    span = pps * PAGE_SIZE
    s = jnp.concatenate([_dot(qbd, kpages[r][0].astype(BF16)) for r in range(pps)], axis=1)
    s = s + per_token(bias_ref, span)
    m_old = m_ref[...]
    m_new = jnp.maximum(m_old, jnp.max(s, axis=-1, keepdims=True))
    alpha = jnp.exp(m_old - m_new)
    p = jnp.exp(s - m_new)
    l_ref[...] = alpha * l_ref[...] + jnp.sum(p, axis=-1, keepdims=True)
    pb = p.astype(BF16)
    pv = jnp.zeros((rows, B_WIDTH), F32)
    for r in range(pps):
        pv = pv + _dot_nt(pb[:, r * PAGE_SIZE:(r + 1) * PAGE_SIZE], vpages[r][0].astype(BF16))
    acc_ref[...] = alpha * acc_ref[...] + pv
    m_ref[...] = m_new

    @pl.when(j == pl.num_programs(1) - 1)
    def _():
        qf = qbd.astype(F32)
        bn = per_token(biasn_ref, LANES)
        cols = []
        for tk in range(t_new):
            kr = kn_ref[0, tk:tk + 1, :].astype(BF16).astype(F32)
            cols.append(jnp.sum(qf * kr, axis=-1, keepdims=True) + bn[:, tk:tk + 1])
        m_old = m_ref[...]
        m_new = m_old
        for cval in cols:
            m_new = jnp.maximum(m_new, cval)
        alpha = jnp.exp(m_old - m_new)
        l = alpha * l_ref[...]
        acc = alpha * acc_ref[...]
        for tk in range(t_new):
            pk = jnp.exp(cols[tk] - m_new)
            l = l + pk
            acc = acc + pk.astype(BF16).astype(F32) * vn_ref[0, tk:tk + 1, :].astype(BF16).astype(F32)
        o = acc / l
        for i in range(t_new):
            slab = jnp.where(mask8, o[i * B_HEADS:(i + 1) * B_HEADS, :], 0.0)
            o_ref[0, i * B_HEADS:(i + 1) * B_HEADS, :] = jnp.broadcast_to(
                jnp.sum(slab, axis=0, keepdims=True), (B_HEADS, B_WIDTH))


def _attn_sample(page_table, q3, bias3, biasn3, k_new, v_new, cache_kt, cache_vt):
    db, t_new, _ = q3.shape
    n_pages = page_table.shape[1]
    pps = min(ATTN_PAGES_PER_STEP, n_pages)
    steps = n_pages // pps
    span = pps * PAGE_SIZE
    rows = t_new * B_HEADS
    per_b = lambda w: pl.BlockSpec((1, t_new, w), lambda b, j, pt: (b, 0, 0))
    grid_spec = pltpu.PrefetchScalarGridSpec(
        num_scalar_prefetch=1, grid=(db, steps),
        in_specs=[per_b(B_WIDTH),
                  pl.BlockSpec((1, t_new, span), lambda b, j, pt: (b, 0, j)),
                  per_b(LANES), per_b(B_WIDTH), per_b(B_WIDTH)]
        + _page_specs((1, B_WIDTH, PAGE_SIZE), pps)
        + _page_specs((1, B_WIDTH, PAGE_SIZE), pps),
        out_specs=pl.BlockSpec((1, rows, B_WIDTH), lambda b, j, pt: (b, 0, 0)),
        scratch_shapes=[pltpu.VMEM((rows, 1), F32), pltpu.VMEM((rows, 1), F32),
                        pltpu.VMEM((rows, B_WIDTH), F32), pltpu.VMEM((rows, B_WIDTH), BF16)])
    return pl.pallas_call(
        functools.partial(_attn_sample_kernel, t_new, pps), grid_spec=grid_spec,
        out_shape=jax.ShapeDtypeStruct((db, rows, B_WIDTH), F32),
        compiler_params=_cparams(("parallel", "arbitrary")), name="attn_sample",
    )(page_table, q3, bias3, biasn3, k_new, v_new, *([cache_kt] * pps), *([cache_vt] * pps))


def _merge_kernel(x_ref, ma_ref, yb_ref, sgb_ref, yc_ref, sgc_ref, wo_ref, g_ref, y_out):
    mb = (yb_ref[...].astype(F32) * sgb_ref[...].astype(F32)).astype(BF16)
    mc = (yc_ref[...].astype(F32) * sgc_ref[...].astype(F32)).astype(BF16)
    o = (_dot(ma_ref[...], wo_ref[0:A_WIDTH, :])
         + _dot(mb, wo_ref[A_WIDTH:A_WIDTH + B_WIDTH, :])
         + _dot(mc, wo_ref[A_WIDTH + B_WIDTH:, :]))
    z = x_ref[...] + o
    ms = jnp.mean(z * z, axis=-1, keepdims=True)
    y_out[...] = z * lax.rsqrt(ms + EPS) * g_ref[...]


def _merge(x, mixa, yb, sgb, yc, sgc, wo, g_final):
    n = x.shape[0]
    tm = min(MERGE_TM, n)
    row = lambda w: pl.BlockSpec((tm, w), lambda i: (i, 0))
    return pl.pallas_call(
        _merge_kernel, grid=(n // tm,),
        in_specs=[row(D_MODEL), row(A_WIDTH), row(B_WIDTH), row(B_WIDTH), row(C_WIDTH), row(C_WIDTH),
                  pl.BlockSpec(wo.shape, lambda i: (0, 0)), pl.BlockSpec(g_final.shape, lambda i: (0, 0))],
        out_specs=row(D_MODEL),
        out_shape=jax.ShapeDtypeStruct((n, D_MODEL), F32),
        compiler_params=_cparams(("parallel",)), name="merge",
    )(x, mixa, yb, sgb, yc, sgc, wo, g_final)


def _rope_tables(pos):
    inv = ROPE_THETA ** (-jnp.arange(0, HEAD_DIM, 2, dtype=F32) / HEAD_DIM)
    ang = pos.astype(F32)[:, None] * inv[None, :]
    c = jnp.cos(ang)
    s = jnp.sin(ang)
    return jnp.concatenate([c, c, c, c], axis=1), jnp.concatenate([-s, s, -s, s], axis=1)


def _heads_last(t, heads):
    b, _, n = t.shape
    return t.reshape(b, heads, HEAD_DIM, n).transpose(0, 3, 1, 2)


def _channels_first(t):
    p, n, heads, d = t.shape
    return t.transpose(0, 2, 3, 1).reshape(p, heads * d, n)


def kernel(x_prompt, x_sample, mem_prompt, cache_k, cache_v, cache_idx_k, cache_mem_k, cache_mem_v,
           page_table, g_norm, w_in, w_spatial, b_spatial, g_v, w_mem_kv, g_mem, w_out, g_final):
    batch, seq, _ = x_prompt.shape
    db, t_new, _ = x_sample.shape
    n_pages = page_table.shape[1]
    past = n_pages * PAGE_SIZE

    ki_cols = w_in[:, 3328:3392]
    wi_cols = w_in[:, 3392:3400]
    w_r = jnp.concatenate(
        [w_in[:, :3328], w_in[:, 3400:3912], ki_cols, ki_cols, wi_cols,
         jnp.zeros((D_MODEL, LANES - IDX_HEADS), F32)], axis=1).astype(BF16)
    g_norm2 = g_norm.reshape(1, D_MODEL)
    g_v2 = g_v.reshape(1, A_WIDTH)
    bs_prompt = jnp.repeat(b_spatial.T, HEAD_DIM, axis=1)
    reps = CHUNK // t_new
    ws_sample = jnp.tile(w_spatial[:, :t_new, :t_new], (1, reps, reps))
    bs_sample = jnp.tile(jnp.repeat(b_spatial[:, :t_new].T, HEAD_DIM, axis=1), (reps, 1))
    wo_b = w_out.astype(BF16)
    g_final2 = g_final.reshape(1, D_MODEL)

    cos_p, sin_p = _rope_tables(jnp.arange(seq, dtype=jnp.int32))
    pos_s = past + jnp.arange(t_new, dtype=jnp.int32)
    cos_s, sin_s = _rope_tables(jnp.tile(pos_s, TM // t_new))

    xp = x_prompt.reshape(batch * seq, D_MODEL)
    (kt, vt, kit, kbf, vtb, ki2, qz, qiz, wt, sgb, mixa, qc, sgc) = _proj(
        xp, g_norm2, w_r, cos_p, sin_p, w_spatial, bs_prompt, g_v2, CHUNK, seq // TM, False)
    mkt, mvt = _memkv(mem_prompt.reshape(batch * N_MEM, D_MODEL), g_mem.reshape(1, D_MODEL),
                      w_mem_kv.astype(BF16))
    yc = _memattn(qc.reshape(batch * seq // MEMATTN_TM, MEMATTN_TM, C_WIDTH), mkt, mvt,
                  seq // MEMATTN_TM).reshape(batch * seq, C_WIDTH)
    yb = _dsa_prompt(qz, qiz, wt, ki2, kbf, vtb, batch, seq)
    y_prompt = _merge(xp, mixa, yb, sgb, yc, sgc, wo_b, g_final2).reshape(batch, seq, D_MODEL)

    ns = db * t_new
    xs = x_sample.reshape(ns, D_MODEL)
    (q_s, k_s, v_s, qi_s, ki_s, wi_s, sgb_s, mixa_s, qc_s, sgc_s, vn_s) = _proj(
        xs, g_norm2, w_r, cos_s, sin_s, ws_sample, bs_sample, g_v2, t_new, 1, True)
    sc, scn = _idx_sample(page_table, qi_s.reshape(db, t_new * IDX_HEADS, IDX_DIM),
                          wi_s.reshape(db, t_new * IDX_HEADS, 1), ki_s.reshape(db, t_new, IDX_DIM),
                          cache_idx_k.transpose(0, 2, 1))
    topk_s = min(TOPK_MAX, (past + t_new) // 4)
    bias, biasn = _thr_sample(sc.reshape(ns, past), scn.reshape(ns, LANES), topk_s)
    yb_pad = _attn_sample(page_table, q_s.reshape(db, t_new, B_WIDTH), bias.reshape(db, t_new, past),
                          biasn.reshape(db, t_new, LANES), k_s.reshape(db, t_new, B_WIDTH),
                          v_s.reshape(db, t_new, B_WIDTH), _channels_first(cache_k), _channels_first(cache_v))
    yb_s = yb_pad[:, ::B_HEADS, :].reshape(ns, B_WIDTH)
    yc_pad = _memattn_sample(qc_s.reshape(db, t_new, C_WIDTH),
                             _channels_first(cache_mem_k), _channels_first(cache_mem_v))
    yc_s = yc_pad[:, ::SUBLANES, :].reshape(ns, C_WIDTH)
    y_sample = _merge(xs, mixa_s, yb_s, sgb_s, yc_s, sgc_s, wo_b, g_final2).reshape(db, t_new, D_MODEL)

    return (y_prompt, y_sample,
            _heads_last(kt, B_HEADS), _heads_last(vt, B_HEADS), kit.transpose(0, 2, 1),
            _heads_last(mkt, C_HEADS), _heads_last(mvt, C_HEADS),
            k_s.reshape(db, t_new, B_HEADS, HEAD_DIM), v_s.reshape(db, t_new, B_HEADS, HEAD_DIM),
            ki_s.reshape(db, t_new, IDX_DIM), vn_s.reshape(db, t_new, A_WIDTH))
```

```python
import functools

import jax
import jax.numpy as jnp
from jax import lax
from jax.experimental import pallas as pl
from jax.experimental.pallas import tpu as pltpu

F32 = jnp.float32
BF16 = jnp.bfloat16

HEAD_DIM = 64
HALF = HEAD_DIM // 2
D_MODEL = 1024
A_GROUPS = 4
A_WIDTH = A_GROUPS * HEAD_DIM
CHUNK = 128
B_HEADS = 8
B_WIDTH = B_HEADS * HEAD_DIM
IDX_HEADS = 8
IDX_DIM = 64
TOPK_MAX = 256
N_MEM = 256
C_HEADS = 4
C_WIDTH = C_HEADS * HEAD_DIM
PAGE_SIZE = 128
ROPE_THETA = 10000.0
EPS = 1e-6
IDX_W_SCALE = (IDX_HEADS * IDX_DIM) ** -0.5
ATTN_SCALE = HEAD_DIM ** -0.5
LOG2E = 1.4426950408889634

LANES = 128
SUBLANES = 8
NEG = -1e30
FLT_MAX = 3.4028234663852886e38
PACKED_ROWS = 16

TM = 512
MEMATTN_TM = 1024
MEMATTN_SEQS_PER_STEP = 16
MERGE_TM = 1024
QB = 256
KC = 256
V_ROWS = HEAD_DIM + PACKED_ROWS
QK_LEAD = 3
IDX_PAGES_PER_STEP = 64
ATTN_PAGES_PER_STEP = 32
THR_ROWS = 128
VMEM_LIMIT = 52 * 1024 * 1024

OFF_U, OFF_VA, OFF_GA = 0, 256, 512
OFF_Q, OFF_K, OFF_V, OFF_GB, OFF_QI = 768, 1280, 1792, 2304, 2816
OFF_QC, OFF_GC, OFF_KI2, OFF_WI, W_TOTAL = 3328, 3584, 3840, 3968, 4096


def _cparams(sem):
    return pltpu.CompilerParams(dimension_semantics=sem, vmem_limit_bytes=VMEM_LIMIT)


def _iota(shape, dim):
    return lax.broadcasted_iota(jnp.int32, shape, dim)


def _silu(x):
    return x / (1.0 + jnp.exp(-x))


def _dot(a, b):
    return jnp.dot(a, b, preferred_element_type=F32)


def _dot_nt(a, b):
    return lax.dot_general(a, b, (((1,), (1,)), ((), ())), preferred_element_type=F32)


def _key_to_float(k):
    bits = k ^ ((k >> 31) & jnp.int32(0x7FFFFFFF))
    return lax.bitcast_convert_type(bits, F32)


def _bitwise_kth(count_ge, kth, shape, bits):
    cnt0 = count_ge(jnp.zeros(shape, jnp.int32))
    ok0 = cnt0 >= kth
    key = jnp.where(ok0, jnp.int32(0), jnp.int32(-2 ** (bits - 1)))
    cnt_key = jnp.where(ok0, cnt0, jnp.float32(2 ** 24))

    def bit_body(it, carry):
        key, cnt_key = carry
        trial = key + lax.shift_left(jnp.int32(1), bits - 2 - it)
        cnt = count_ge(trial)
        ok = cnt >= kth
        return jnp.where(ok, trial, key), jnp.where(ok, cnt, cnt_key)

    return lax.fori_loop(0, bits - 1, bit_body, (key, cnt_key))


def _kth_largest(count_ge, kth, shape):
    key, cnt_key = _bitwise_kth(lambda k: count_ge(_key_to_float(k)), kth, shape, 32)
    return _key_to_float(key), cnt_key


def _rope(t, cos, sin):
    outs = []
    first = (_iota((t.shape[0], LANES), 1) % HEAD_DIM) < HALF
    for j in range(t.shape[1] // LANES):
        x = t[:, j * LANES:(j + 1) * LANES]
        partner = jnp.where(first, pltpu.roll(x, LANES - HALF, 1), pltpu.roll(x, HALF, 1))
        outs.append(x * cos + partner * sin)
    return outs[0] if len(outs) == 1 else jnp.concatenate(outs, axis=1)


def _proj_kernel(chunk, sample, x_ref, g_ref, w_ref, cos_ref, sin_ref, ws_ref, bs_ref, gv_ref, *outs):
    if sample:
        q_out, k_out, v_out, qi_out, ki_out, wi_out, sgb_out, mixa_out, qc_out, sgc_out, vn_out = outs
    else:
        (kt_out, vt_out, kit_out, kbf_out, vtb_out, ki2_out, qz_out, qiz_out, wt_out,
         sgb_out, mixa_out, qc_out, sgc_out) = outs
    tm = x_ref.shape[0]
    x = x_ref[...]
    ms = jnp.mean(x * x, axis=-1, keepdims=True)
    h = (x * lax.rsqrt(ms + EPS) * g_ref[...]).astype(BF16)
    cos = cos_ref[...]
    sin = sin_ref[...]

    def seg(a, b):
        return _dot(h, w_ref[:, a:b])

    half_id = (_iota((tm, LANES), 1) // HEAD_DIM)

    def head_split(t, out):
        for hd in range(B_HEADS):
            pair = t[:, (hd // 2) * LANES:(hd // 2 + 1) * LANES]
            out[hd] = jnp.where(half_id == hd % 2, pair, 0.0).astype(BF16)

    q = _rope(seg(OFF_Q, OFF_K), cos, sin)
    k = _rope(seg(OFF_K, OFF_V), cos, sin)
    v = seg(OFF_V, OFF_GB)
    qi = _rope(seg(OFF_QI, OFF_QC), cos, sin)
    ki2 = _rope(seg(OFF_KI2, OFF_WI), cos, sin)
    wi = seg(OFF_WI, W_TOTAL) * IDX_W_SCALE
    qc = seg(OFF_QC, OFF_GC) * ATTN_SCALE
    sgb_out[...] = _silu(seg(OFF_GB, OFF_QI)).astype(BF16)
    sgc_out[...] = _silu(seg(OFF_GC, OFF_KI2)).astype(BF16)
    qc_out[...] = qc.astype(qc_out.dtype)
    if sample:
        q_out[...] = q
        k_out[...] = k
        v_out[...] = v
        qi_out[...] = qi
        ki_out[...] = ki2[:, :IDX_DIM]
        wi_out[...] = wi[:, :IDX_HEADS]
    else:
        head_split(q * (ATTN_SCALE * LOG2E), qz_out)
        head_split(qi, qiz_out)
        kt_out[0] = k.T
        kbf_out[...] = k.astype(BF16)
        vt = v.T
        vt_out[0] = vt
        vtb = vt.astype(BF16)
        ones = jnp.ones((V_ROWS - HEAD_DIM, KC), BF16)
        for j in range(tm // KC):
            for hd in range(B_HEADS):
                vtb_out[j, hd * V_ROWS:hd * V_ROWS + HEAD_DIM, :] = (
                    vtb[hd * HEAD_DIM:(hd + 1) * HEAD_DIM, j * KC:(j + 1) * KC])
                vtb_out[j, hd * V_ROWS + HEAD_DIM:(hd + 1) * V_ROWS, :] = ones
        kit_out[0] = ki2.T[:IDX_DIM, :]
        ki2_out[...] = ki2.astype(BF16)
        wt_out[...] = wi.T[:IDX_HEADS, :]

    u = seg(OFF_U, OFF_VA)
    va = seg(OFF_VA, OFF_GA)
    ga = seg(OFF_GA, OFF_Q)
    grp_r = _iota((A_WIDTH, A_WIDTH), 0) // HEAD_DIM
    grp_c = _iota((A_WIDTH, A_WIDTH), 1) // HEAD_DIM
    gmat = jnp.where(grp_r == grp_c, 1.0 / HEAD_DIM, 0.0).astype(BF16)

    def group_mean(t):
        hi = t.astype(BF16)
        lo = (t - hi.astype(F32)).astype(BF16)
        return _dot(hi, gmat) + _dot(lo, gmat)

    mu = group_mean(va)
    d = va - mu
    var = group_mean(d * d)
    vn = d * lax.rsqrt(var + EPS) * gv_ref[...]
    if sample:
        vn_out[...] = vn
    vnb = vn.astype(BF16)
    row = _iota((CHUNK, CHUNK), 0)
    col = _iota((CHUNK, CHUNK), 1)
    causal = (row >= col) & ((row // chunk) == (col // chunk))
    lane_grp = _iota((CHUNK, A_WIDTH), 1) // HEAD_DIM
    wms = [jnp.where(causal, ws_ref[g], 0.0).astype(BF16) for g in range(A_GROUPS)]
    mixes = []
    for j in range(tm // CHUNK):
        vc = vnb[j * CHUNK:(j + 1) * CHUNK, :]
        m = bs_ref[...]
        for g in range(A_GROUPS):
            m = m + jnp.where(lane_grp == g, _dot(wms[g], vc), 0.0)
        mixes.append(m)
    mix = jnp.concatenate(mixes, axis=0)
    mixa_out[...] = (_silu(ga) * u * mix).astype(BF16)


def _proj(x, g_norm, w_r, cos_t, sin_t, ws, bs, g_v, chunk, seq_blocks, sample):
    n = x.shape[0]
    nt = n // TM
    nb = nt // seq_blocks
    row = lambda w: pl.BlockSpec((TM, w), lambda i: (i, 0))
    const2 = lambda a: pl.BlockSpec(a.shape, lambda i: (0, 0))
    sds = jax.ShapeDtypeStruct
    in_specs = [
        row(D_MODEL), const2(g_norm), const2(w_r),
        pl.BlockSpec((TM, LANES), lambda i: (i % seq_blocks, 0)),
        pl.BlockSpec((TM, LANES), lambda i: (i % seq_blocks, 0)),
        pl.BlockSpec(ws.shape, lambda i: (0, 0, 0)), const2(bs), const2(g_v),
    ]
    if sample:
        outs = [
            (sds((n, B_WIDTH), F32), row(B_WIDTH)),
            (sds((n, B_WIDTH), F32), row(B_WIDTH)),
            (sds((n, B_WIDTH), F32), row(B_WIDTH)),
            (sds((n, IDX_HEADS * IDX_DIM), F32), row(IDX_HEADS * IDX_DIM)),
            (sds((n, IDX_DIM), F32), row(IDX_DIM)),
            (sds((n, IDX_HEADS), F32), row(IDX_HEADS)),
            (sds((n, B_WIDTH), BF16), row(B_WIDTH)),
            (sds((n, A_WIDTH), BF16), row(A_WIDTH)),
            (sds((n, C_WIDTH), F32), row(C_WIDTH)),
            (sds((n, C_WIDTH), BF16), row(C_WIDTH)),
            (sds((n, A_WIDTH), F32), row(A_WIDTH)),
        ]
    else:
        seq = seq_blocks * TM
        tspec = lambda w: pl.BlockSpec((1, w, TM), lambda i: (i // seq_blocks, 0, i % seq_blocks))
        hspec = pl.BlockSpec((B_HEADS, TM, LANES), lambda i: (0, i, 0))
        outs = [
            (sds((nb, B_WIDTH, seq), F32), tspec(B_WIDTH)),
            (sds((nb, B_WIDTH, seq), F32), tspec(B_WIDTH)),
            (sds((nb, IDX_DIM, seq), F32), tspec(IDX_DIM)),
            (sds((n, B_WIDTH), BF16), row(B_WIDTH)),
            (sds((n // KC, B_HEADS * V_ROWS, KC), BF16),
             pl.BlockSpec((TM // KC, B_HEADS * V_ROWS, KC), lambda i: (i, 0, 0))),
            (sds((n, LANES), BF16), row(LANES)),
            (sds((B_HEADS, n, LANES), BF16), hspec),
            (sds((IDX_HEADS, n, LANES), BF16), hspec),
            (sds((IDX_HEADS, n), F32), pl.BlockSpec((IDX_HEADS, TM), lambda i: (0, i))),
            (sds((n, B_WIDTH), BF16), row(B_WIDTH)),
            (sds((n, A_WIDTH), BF16), row(A_WIDTH)),
            (sds((n, C_WIDTH), BF16), row(C_WIDTH)),
            (sds((n, C_WIDTH), BF16), row(C_WIDTH)),
        ]
    return pl.pallas_call(
        functools.partial(_proj_kernel, chunk, sample),
        grid=(nt,), in_specs=in_specs, out_specs=[o[1] for o in outs], out_shape=[o[0] for o in outs],
        compiler_params=_cparams(("parallel",)), name="proj",
    )(x, g_norm, w_r, cos_t, sin_t, ws, bs, g_v)


def _memkv_kernel(m_ref, g_ref, w_ref, mkt_out, mvt_out):
    x = m_ref[...]
    ms = jnp.mean(x * x, axis=-1, keepdims=True)
    h = (x * lax.rsqrt(ms + EPS) * g_ref[...]).astype(BF16)
    kv = _dot(h, w_ref[...])
    mkt_out[0] = kv[:, :C_WIDTH].T
    mvt_out[0] = kv[:, C_WIDTH:].T


def _memkv(mem, g_mem, w_kv):
    n = mem.shape[0]
    nb = n // N_MEM
    return pl.pallas_call(
        _memkv_kernel, grid=(nb,),
        in_specs=[pl.BlockSpec((N_MEM, D_MODEL), lambda i: (i, 0)),
                  pl.BlockSpec(g_mem.shape, lambda i: (0, 0)),
                  pl.BlockSpec(w_kv.shape, lambda i: (0, 0))],
        out_specs=[pl.BlockSpec((1, C_WIDTH, N_MEM), lambda i: (i, 0, 0))] * 2,
        out_shape=[jax.ShapeDtypeStruct((nb, C_WIDTH, N_MEM), F32)] * 2,
        compiler_params=_cparams(("parallel",)), name="memkv",
    )(mem, g_mem, w_kv)


def _memattn_kernel(q_ref, mkt_ref, mvt_ref, o_ref):
    q = q_ref[0]
    mkt = mkt_ref[0].astype(BF16)
    mvt = mvt_ref[0].astype(BF16)
    t = q.shape[0]
    lane_head = _iota((t, C_WIDTH), 1) // HEAD_DIM
    out = jnp.zeros((t, C_WIDTH), F32)
    for hd in range(C_HEADS):
        qh = jnp.where(lane_head == hd, q, 0.0).astype(BF16)
        s = _dot(qh, mkt)
        p = jnp.exp(s - jnp.max(s, axis=-1, keepdims=True))
        o = _dot_nt(p.astype(BF16), mvt) / jnp.sum(p, axis=-1, keepdims=True)
        out = out + jnp.where(lane_head == hd, o, 0.0)
    o_ref[0] = out.astype(o_ref.dtype)


def _memattn(q3, mkt3, mvt3, tiles_per_mem):
    g, t, _ = q3.shape
    mem_spec = pl.BlockSpec((1, C_WIDTH, N_MEM), lambda i: (i // tiles_per_mem, 0, 0))
    return pl.pallas_call(
        _memattn_kernel, grid=(g,),
        in_specs=[pl.BlockSpec((1, t, C_WIDTH), lambda i: (i, 0, 0)), mem_spec, mem_spec],
        out_specs=pl.BlockSpec((1, t, C_WIDTH), lambda i: (i, 0, 0)),
        out_shape=jax.ShapeDtypeStruct((g, t, C_WIDTH), BF16),
        compiler_params=_cparams(("parallel",)), name="memattn",
    )(q3, mkt3, mvt3)


def _memattn_sample_kernel(q_ref, mkt_ref, mvt_ref, o_ref):
    t = q_ref.shape[1]
    mask8 = (_iota((SUBLANES, C_WIDTH), 1) // HEAD_DIM) == _iota((SUBLANES, C_WIDTH), 0)
    for g in range(q_ref.shape[0]):
        q4 = q_ref[g]
        qbd = jnp.concatenate(
            [jnp.where(mask8, jnp.broadcast_to(q4[i:i + 1, :], (SUBLANES, C_WIDTH)), 0.0) for i in range(t)], axis=0)
        s = _dot(qbd.astype(BF16), mkt_ref[g].astype(BF16))
        p = jnp.exp(s - jnp.max(s, axis=-1, keepdims=True))
        o = _dot_nt(p.astype(BF16), mvt_ref[g].astype(BF16)) / jnp.sum(p, axis=-1, keepdims=True)
        for i in range(t):
            slab = jnp.where(mask8, o[i * SUBLANES:(i + 1) * SUBLANES, :], 0.0)
            o_ref[g, i * SUBLANES:(i + 1) * SUBLANES, :] = jnp.broadcast_to(
                jnp.sum(slab, axis=0, keepdims=True), (SUBLANES, C_WIDTH))


def _memattn_sample(q3, mkt3, mvt3):
    n, t, _ = q3.shape
    g = min(MEMATTN_SEQS_PER_STEP, n)
    mem_spec = pl.BlockSpec((g, C_WIDTH, N_MEM), lambda i: (i, 0, 0))
    return pl.pallas_call(
        _memattn_sample_kernel, grid=(n // g,),
        in_specs=[pl.BlockSpec((g, t, C_WIDTH), lambda i: (i, 0, 0)), mem_spec, mem_spec],
        out_specs=pl.BlockSpec((g, t * SUBLANES, C_WIDTH), lambda i: (i, 0, 0)),
        out_shape=jax.ShapeDtypeStruct((n, t * SUBLANES, C_WIDTH), F32),
        compiler_params=_cparams(("parallel",)), name="memattn_sample",
    )(q3, mkt3, mvt3)


def _dsa_prompt_kernel(topk, qz_ref, qiz_ref, wt_ref, ki2_ref, kbf_ref, vt_ref, o_ref,
                       sc_ref, acc_ref, m_ref, s_ref, sb_ref):
    i = pl.program_id(1)
    nk = i + 1
    npair = (nk + 1) // 2
    kth = jnp.float32(topk)

    def sum_rows(x):
        return jnp.sum(x.reshape(KC // SUBLANES, SUBLANES, QB), axis=0)

    def max_rows(x):
        return jnp.max(x.reshape(KC // SUBLANES, SUBLANES, QB), axis=0)

    def score_chunk(c, _):
        keys = ki2_ref[0, c]
        acc = jnp.zeros((KC, QB), F32)
        for hd in range(IDX_HEADS):
            y = _dot_nt(keys, qiz_ref[hd])
            acc = acc + jnp.maximum(y, 0.0) * wt_ref[hd:hd + 1, :]
        kpos = c * KC + _iota((KC, QB), 0)
        qpos = i * QB + _iota((KC, QB), 1)
        masked = jnp.where(kpos <= qpos, acc, -jnp.inf)
        sc_ref[c] = masked
        sb_ref[c] = masked.astype(BF16)
        return 0

    lax.fori_loop(0, nk, score_chunk, 0)

    @pl.when(nk % 2 == 1)
    def _():
        pad = jnp.minimum(nk, sc_ref.shape[0] - 1)
        sc_ref[pad] = jnp.full((KC, QB), -jnp.inf, F32)
        sb_ref[pad] = jnp.full((KC, QB), -jnp.inf, BF16)

    def count(pred):
        def body(j, cnt):
            return (cnt + sum_rows(pred(sc_ref[2 * j]).astype(F32))
                    + sum_rows(pred(sc_ref[2 * j + 1]).astype(F32)))
        return jnp.sum(lax.fori_loop(0, npair, body, jnp.zeros((SUBLANES, QB), F32)), axis=0, keepdims=True)

    def count_ge_packed(t):
        tb = t.astype(BF16)
        def slab_count(c):
            ones = jnp.where(sb_ref[c] >= tb, jnp.ones((), BF16), jnp.zeros((), BF16))
            acc = ones[0:PACKED_ROWS]
            for r in range(1, KC // PACKED_ROWS):
                acc = acc + ones[r * PACKED_ROWS:(r + 1) * PACKED_ROWS]
            return acc

        def body(j, cnt):
            return cnt + slab_count(2 * j) + slab_count(2 * j + 1)

        cnt16 = lax.fori_loop(0, npair, body, jnp.zeros((PACKED_ROWS, QB), BF16))
        return jnp.sum(cnt16.astype(F32), axis=0, keepdims=True)

    def write_ge(thr):
        def body(j, _):
            sc_ref[2 * j] = jnp.where(sc_ref[2 * j] >= thr, 0.0, NEG)
            sc_ref[2 * j + 1] = jnp.where(sc_ref[2 * j + 1] >= thr, 0.0, NEG)
            return 0
        lax.fori_loop(0, npair, body, 0)

    @pl.when(nk * KC <= topk)
    def _():
        write_ge(jnp.full((1, QB), -FLT_MAX, F32))

    @pl.when(nk * KC > topk)
    def _():
        def bf16_key_to_float_key(k16):
            return k16 * jnp.int32(2 ** 16) + jnp.where(k16 < 0, jnp.int32(2 ** 16 - 1), jnp.int32(0))

        tau_k16, _ = _bitwise_kth(
            lambda k16: count_ge_packed(_key_to_float(bf16_key_to_float_key(k16))), kth, (1, QB), 16)
        base = bf16_key_to_float_key(tau_k16) - jnp.int32(2 ** 16)

        def fine_bit(it, carry):
            off, cnt_off = carry
            cand = off + lax.shift_left(jnp.int32(1), 16 - it)
            cnt = count(lambda x: x >= _key_to_float(base + cand))
            ok = cnt >= kth
            return jnp.where(ok, cand, off), jnp.where(ok, cnt, cnt_off)

        off, cnt_thr = lax.fori_loop(0, 17, fine_bit,
                                     (jnp.zeros((1, QB), jnp.int32), jnp.full((1, QB), 2.0 ** 24, F32)))
        thr = _key_to_float(base + off)
        few = (i * QB + 1 + _iota((1, QB), 1)).astype(F32) <= kth
        thr = jnp.where(few, -FLT_MAX, thr)
        cnt_thr = jnp.where(few, kth, cnt_thr)
        has_tie = jnp.max(cnt_thr) > kth

        @pl.when(jnp.logical_not(has_tie))
        def _():
            write_ge(thr)

        @pl.when(has_tie)
        def _():
            need = kth - count(lambda x: x > thr)
            ltri = (_iota((KC, KC), 1) < _iota((KC, KC), 0)).astype(BF16)

            def body(c, carry):
                x = sc_ref[c]
                eq = x == thr
                before = _dot(ltri, eq.astype(BF16)) + carry
                sel = (x > thr) | (eq & (before < need))
                sc_ref[c] = jnp.where(sel, 0.0, NEG)
                return carry + jnp.sum(sum_rows(eq.astype(F32)), axis=0, keepdims=True)

            lax.fori_loop(0, nk, body, jnp.zeros((1, QB), F32))

    m_ref[...] = jnp.full((B_HEADS, QB), NEG, F32)
    acc_ref[...] = jnp.zeros((B_HEADS * V_ROWS, QB), F32)

    def biased_scores(c, hd):
        pair = hd // 2
        kc = kbf_ref[0, c, :, pair * LANES:(pair + 1) * LANES]
        return _dot_nt(kc, qz_ref[hd]) + sc_ref[c]

    for hd in range(B_HEADS):
        s_ref[hd] = biased_scores(0, hd)

    def attend(c, _):
        nxt = jnp.minimum(c + 1, nk - 1)
        ahead = [biased_scores(nxt, hd) for hd in range(QK_LEAD)]
        for hd in range(B_HEADS):
            rows = slice(hd * V_ROWS, (hd + 1) * V_ROWS)
            s = s_ref[hd]
            m_old = m_ref[hd:hd + 1, :]
            m_new = jnp.maximum(m_old, jnp.max(max_rows(s), axis=0, keepdims=True))
            alpha = jnp.exp2(m_old - m_new)
            p = jnp.exp2(s - m_new)
            acc_ref[rows, :] = alpha * acc_ref[rows, :] + _dot(vt_ref[0, c, rows, :], p.astype(BF16))
            m_ref[hd:hd + 1, :] = m_new
            s_ref[hd] = ahead[hd]
            if hd + QK_LEAD < B_HEADS:
                ahead.append(biased_scores(nxt, hd + QK_LEAD))
        return 0

    lax.fori_loop(0, nk, attend, 0)
    outs = []
    for hd in range(B_HEADS):
        base = hd * V_ROWS
        outs.append(acc_ref[base:base + HEAD_DIM, :] / acc_ref[base + HEAD_DIM:base + HEAD_DIM + 1, :])
    o_ref[...] = jnp.concatenate(outs, axis=0).T.astype(o_ref.dtype)


def _dsa_prompt(qz, qiz, wt, ki2, kbf, vtb, batch, seq):
    nq = seq // QB
    nc = seq // KC
    topk = min(TOPK_MAX, seq // 4)
    n = batch * seq
    qspec = pl.BlockSpec((B_HEADS, QB, LANES), lambda b, i: (0, b * nq + i, 0))
    return pl.pallas_call(
        functools.partial(_dsa_prompt_kernel, topk),
        grid=(batch, nq),
        in_specs=[qspec, qspec,
                  pl.BlockSpec((IDX_HEADS, QB), lambda b, i: (0, b * nq + i)),
                  pl.BlockSpec((1, nc, KC, LANES), lambda b, i: (b, 0, 0, 0)),
                  pl.BlockSpec((1, nc, KC, B_WIDTH), lambda b, i: (b, 0, 0, 0)),
                  pl.BlockSpec((1, nc, B_HEADS * V_ROWS, KC), lambda b, i: (b, 0, 0, 0))],
        out_specs=pl.BlockSpec((QB, B_WIDTH), lambda b, i: (b * nq + i, 0)),
        out_shape=jax.ShapeDtypeStruct((n, B_WIDTH), BF16),
        scratch_shapes=[pltpu.VMEM((nc, KC, QB), F32), pltpu.VMEM((B_HEADS * V_ROWS, QB), F32),
                        pltpu.VMEM((B_HEADS, QB), F32), pltpu.VMEM((B_HEADS, KC, QB), F32),
                        pltpu.VMEM((nc, KC, QB), BF16)],
        compiler_params=_cparams(("parallel", "arbitrary")), name="dsa_prompt",
    )(qz, qiz, wt, ki2.reshape(batch, nc, KC, LANES), kbf.reshape(batch, nc, KC, B_WIDTH),
      vtb.reshape(batch, nc, B_HEADS * V_ROWS, KC))


def _page_specs(block, n_pages_per_step):
    def make(r):
        return pl.BlockSpec(block, lambda b, j, pt: (pt[b, j * n_pages_per_step + r], 0, 0))
    return [make(r) for r in range(n_pages_per_step)]


def _idx_sample_kernel(t_new, pps, pt_ref, qi_ref, w_ref, kin_ref, *rest):
    pages = rest[:pps]
    sc_out, scn_out = rest[pps:]
    j = pl.program_id(1)
    qi = qi_ref[0]
    w = w_ref[0]
    qib = qi.astype(BF16)

    def token_scores(y):
        z = jnp.maximum(y, 0.0) * w
        return jnp.sum(z.reshape(t_new, IDX_HEADS, y.shape[1]), axis=1)

    for r in range(pps):
        y = _dot(qib, pages[r][0].astype(BF16))
        sc_out[0, :, r * PAGE_SIZE:(r + 1) * PAGE_SIZE] = token_scores(y)

    @pl.when(j == 0)
    def _():
        qf = qib.astype(F32)
        lane = _iota((t_new, LANES), 1)
        tok = _iota((t_new, LANES), 0)
        new = jnp.full((t_new, LANES), -jnp.inf, F32)
        for tk in range(t_new):
            kr = kin_ref[0, tk:tk + 1, :].astype(BF16).astype(F32)
            col = token_scores(jnp.sum(qf * kr, axis=-1, keepdims=True))
            new = jnp.where((lane == tk) & (tok >= tk), col, new)
        scn_out[0] = new


def _idx_sample(page_table, qi32, w32, ki_new, cache_idx_kt):
    db, rows, _ = qi32.shape
    t_new = rows // IDX_HEADS
    n_pages = page_table.shape[1]
    pps = min(IDX_PAGES_PER_STEP, n_pages)
    steps = n_pages // pps
    span = pps * PAGE_SIZE
    grid_spec = pltpu.PrefetchScalarGridSpec(
        num_scalar_prefetch=1, grid=(db, steps),
        in_specs=[pl.BlockSpec((1, rows, IDX_DIM), lambda b, j, pt: (b, 0, 0)),
                  pl.BlockSpec((1, rows, 1), lambda b, j, pt: (b, 0, 0)),
                  pl.BlockSpec((1, t_new, IDX_DIM), lambda b, j, pt: (b, 0, 0))]
        + _page_specs((1, IDX_DIM, PAGE_SIZE), pps),
        out_specs=[pl.BlockSpec((1, t_new, span), lambda b, j, pt: (b, 0, j)),
                   pl.BlockSpec((1, t_new, LANES), lambda b, j, pt: (b, 0, 0))])
    return pl.pallas_call(
        functools.partial(_idx_sample_kernel, t_new, pps), grid_spec=grid_spec,
        out_shape=[jax.ShapeDtypeStruct((db, t_new, n_pages * PAGE_SIZE), F32),
                   jax.ShapeDtypeStruct((db, t_new, LANES), F32)],
        compiler_params=_cparams(("parallel", "arbitrary")), name="idx_sample",
    )(page_table, qi32, w32, ki_new, *([cache_idx_kt] * pps))


def _thr_sample_kernel(topk, sc_ref, scn_ref, b_out, bn_out):
    kth = jnp.float32(topk)
    n_chunks = sc_ref.shape[1] // LANES

    def count(pred):
        return (jnp.sum(pred(sc_ref[...]).astype(F32), axis=-1, keepdims=True)
                + jnp.sum(pred(scn_ref[...]).astype(F32), axis=-1, keepdims=True))

    thr, cnt_thr = _kth_largest(lambda t: count(lambda x: x >= t), kth, (THR_ROWS, 1))
    has_tie = jnp.max(cnt_thr) > kth

    @pl.when(jnp.logical_not(has_tie))
    def _():
        b_out[...] = jnp.where(sc_ref[...] >= thr, 0.0, NEG)
        bn_out[...] = jnp.where(scn_ref[...] >= thr, 0.0, NEG)

    @pl.when(has_tie)
    def _():
        need = kth - count(lambda x: x > thr)
        utri = (_iota((LANES, LANES), 0) < _iota((LANES, LANES), 1)).astype(BF16)
        carry = jnp.zeros((THR_ROWS, 1), F32)
        for c in range(n_chunks + 1):
            x = sc_ref[:, c * LANES:(c + 1) * LANES] if c < n_chunks else scn_ref[...]
            eq = x == thr
            before = _dot(eq.astype(BF16), utri) + carry
            sel = (x > thr) | (eq & (before < need))
            bias = jnp.where(sel, 0.0, NEG)
            if c < n_chunks:
                b_out[:, c * LANES:(c + 1) * LANES] = bias
            else:
                bn_out[...] = bias
            carry = carry + jnp.sum(eq.astype(F32), axis=-1, keepdims=True)


def _thr_sample(sc, scn, topk):
    n, width = sc.shape
    return pl.pallas_call(
        functools.partial(_thr_sample_kernel, topk), grid=(n // THR_ROWS,),
        in_specs=[pl.BlockSpec((THR_ROWS, width), lambda i: (i, 0)),
                  pl.BlockSpec((THR_ROWS, LANES), lambda i: (i, 0))],
        out_specs=[pl.BlockSpec((THR_ROWS, width), lambda i: (i, 0)),
                   pl.BlockSpec((THR_ROWS, LANES), lambda i: (i, 0))],
        out_shape=[jax.ShapeDtypeStruct((n, width), F32), jax.ShapeDtypeStruct((n, LANES), F32)],
        compiler_params=_cparams(("parallel",)), name="thr_sample",
    )(sc, scn)


def _attn_sample_kernel(t_new, pps, pt_ref, q_ref, bias_ref, biasn_ref, kn_ref, vn_ref, *rest):
    kpages = rest[:pps]
    vpages = rest[pps:2 * pps]
    o_ref, m_ref, l_ref, acc_ref, qbd_ref = rest[2 * pps:]
    j = pl.program_id(1)
    rows = t_new * B_HEADS
    mask8 = (_iota((SUBLANES, B_WIDTH), 1) // HEAD_DIM) == _iota((SUBLANES, B_WIDTH), 0)

    def per_token(ref, width):
        return jnp.concatenate(
            [jnp.broadcast_to(ref[0, i:i + 1, :], (B_HEADS, width)) for i in range(t_new)], axis=0)

    @pl.when(j == 0)
    def _():
        m_ref[...] = jnp.full((rows, 1), NEG, F32)
        l_ref[...] = jnp.zeros((rows, 1), F32)
        acc_ref[...] = jnp.zeros((rows, B_WIDTH), F32)
        qbd_ref[...] = jnp.concatenate(
            [jnp.where(mask8, jnp.broadcast_to(q_ref[0, i:i + 1, :] * ATTN_SCALE, (B_HEADS, B_WIDTH)), 0.0)
             for i in range(t_new)], axis=0).astype(BF16)

    qbd = qbd_ref[...]
    span = pps * PAGE_SIZE
    s = jnp.concatenate([_dot(qbd, kpages[r][0].astype(BF16)) for r in range(pps)], axis=1)
    s = s + per_token(bias_ref, span)
    m_old = m_ref[...]
    m_new = jnp.maximum(m_old, jnp.max(s, axis=-1, keepdims=True))
    alpha = jnp.exp(m_old - m_new)
    p = jnp.exp(s - m_new)
    l_ref[...] = alpha * l_ref[...] + jnp.sum(p, axis=-1, keepdims=True)
    pb = p.astype(BF16)
    pv = jnp.zeros((rows, B_WIDTH), F32)
    for r in range(pps):
        pv = pv + _dot_nt(pb[:, r * PAGE_SIZE:(r + 1) * PAGE_SIZE], vpages[r][0].astype(BF16))
    acc_ref[...] = alpha * acc_ref[...] + pv
    m_ref[...] = m_new

    @pl.when(j == pl.num_programs(1) - 1)
    def _():
        qf = qbd.astype(F32)
        bn = per_token(biasn_ref, LANES)
        cols = []
        for tk in range(t_new):
            kr = kn_ref[0, tk:tk + 1, :].astype(BF16).astype(F32)
            cols.append(jnp.sum(qf * kr, axis=-1, keepdims=True) + bn[:, tk:tk + 1])
        m_old = m_ref[...]
        m_new = m_old
        for cval in cols:
            m_new = jnp.maximum(m_new, cval)
        alpha = jnp.exp(m_old - m_new)
        l = alpha * l_ref[...]
        acc = alpha * acc_ref[...]
        for tk in range(t_new):
            pk = jnp.exp(cols[tk] - m_new)
            l = l + pk
            acc = acc + pk.astype(BF16).astype(F32) * vn_ref[0, tk:tk + 1, :].astype(BF16).astype(F32)
        o = acc / l
        for i in range(t_new):
            slab = jnp.where(mask8, o[i * B_HEADS:(i + 1) * B_HEADS, :], 0.0)
            o_ref[0, i * B_HEADS:(i + 1) * B_HEADS, :] = jnp.broadcast_to(
                jnp.sum(slab, axis=0, keepdims=True), (B_HEADS, B_WIDTH))


def _attn_sample(page_table, q3, bias3, biasn3, k_new, v_new, cache_kt, cache_vt):
    db, t_new, _ = q3.shape
    n_pages = page_table.shape[1]
    pps = min(ATTN_PAGES_PER_STEP, n_pages)
    steps = n_pages // pps
    span = pps * PAGE_SIZE
    rows = t_new * B_HEADS
    per_b = lambda w: pl.BlockSpec((1, t_new, w), lambda b, j, pt: (b, 0, 0))
    grid_spec = pltpu.PrefetchScalarGridSpec(
        num_scalar_prefetch=1, grid=(db, steps),
        in_specs=[per_b(B_WIDTH),
                  pl.BlockSpec((1, t_new, span), lambda b, j, pt: (b, 0, j)),
                  per_b(LANES), per_b(B_WIDTH), per_b(B_WIDTH)]
        + _page_specs((1, B_WIDTH, PAGE_SIZE), pps)
        + _page_specs((1, B_WIDTH, PAGE_SIZE), pps),
        out_specs=pl.BlockSpec((1, rows, B_WIDTH), lambda b, j, pt: (b, 0, 0)),
        scratch_shapes=[pltpu.VMEM((rows, 1), F32), pltpu.VMEM((rows, 1), F32),
                        pltpu.VMEM((rows, B_WIDTH), F32), pltpu.VMEM((rows, B_WIDTH), BF16)])
    return pl.pallas_call(
        functools.partial(_attn_sample_kernel, t_new, pps), grid_spec=grid_spec,
        out_shape=jax.ShapeDtypeStruct((db, rows, B_WIDTH), F32),
        compiler_params=_cparams(("parallel", "arbitrary")), name="attn_sample",
    )(page_table, q3, bias3, biasn3, k_new, v_new, *([cache_kt] * pps), *([cache_vt] * pps))


def _merge_kernel(x_ref, ma_ref, yb_ref, sgb_ref, yc_ref, sgc_ref, wo_ref, g_ref, y_out):
    mb = (yb_ref[...].astype(F32) * sgb_ref[...].astype(F32)).astype(BF16)
    mc = (yc_ref[...].astype(F32) * sgc_ref[...].astype(F32)).astype(BF16)
    o = (_dot(ma_ref[...], wo_ref[0:A_WIDTH, :])
         + _dot(mb, wo_ref[A_WIDTH:A_WIDTH + B_WIDTH, :])
         + _dot(mc, wo_ref[A_WIDTH + B_WIDTH:, :]))
    z = x_ref[...] + o
    ms = jnp.mean(z * z, axis=-1, keepdims=True)
    y_out[...] = z * lax.rsqrt(ms + EPS) * g_ref[...]


def _merge(x, mixa, yb, sgb, yc, sgc, wo, g_final):
    n = x.shape[0]
    tm = min(MERGE_TM, n)
    row = lambda w: pl.BlockSpec((tm, w), lambda i: (i, 0))
    return pl.pallas_call(
        _merge_kernel, grid=(n // tm,),
        in_specs=[row(D_MODEL), row(A_WIDTH), row(B_WIDTH), row(B_WIDTH), row(C_WIDTH), row(C_WIDTH),
                  pl.BlockSpec(wo.shape, lambda i: (0, 0)), pl.BlockSpec(g_final.shape, lambda i: (0, 0))],
        out_specs=row(D_MODEL),
        out_shape=jax.ShapeDtypeStruct((n, D_MODEL), F32),
        compiler_params=_cparams(("parallel",)), name="merge",
    )(x, mixa, yb, sgb, yc, sgc, wo, g_final)


def _rope_tables(pos):
    inv = ROPE_THETA ** (-jnp.arange(0, HEAD_DIM, 2, dtype=F32) / HEAD_DIM)
    ang = pos.astype(F32)[:, None] * inv[None, :]
    c = jnp.cos(ang)
    s = jnp.sin(ang)
    return jnp.concatenate([c, c, c, c], axis=1), jnp.concatenate([-s, s, -s, s], axis=1)


def _heads_last(t, heads):
    b, _, n = t.shape
    return t.reshape(b, heads, HEAD_DIM, n).transpose(0, 3, 1, 2)


def _channels_first(t):
    p, n, heads, d = t.shape
    return t.transpose(0, 2, 3, 1).reshape(p, heads * d, n)


def kernel(x_prompt, x_sample, mem_prompt, cache_k, cache_v, cache_idx_k, cache_mem_k, cache_mem_v,
           page_table, g_norm, w_in, w_spatial, b_spatial, g_v, w_mem_kv, g_mem, w_out, g_final):
    batch, seq, _ = x_prompt.shape
    db, t_new, _ = x_sample.shape
    n_pages = page_table.shape[1]
    past = n_pages * PAGE_SIZE

    ki_cols = w_in[:, 3328:3392]
    wi_cols = w_in[:, 3392:3400]
    w_r = jnp.concatenate(
        [w_in[:, :3328], w_in[:, 3400:3912], ki_cols, ki_cols, wi_cols,
         jnp.zeros((D_MODEL, LANES - IDX_HEADS), F32)], axis=1).astype(BF16)
    g_norm2 = g_norm.reshape(1, D_MODEL)
    g_v2 = g_v.reshape(1, A_WIDTH)
    bs_prompt = jnp.repeat(b_spatial.T, HEAD_DIM, axis=1)
    reps = CHUNK // t_new
    ws_sample = jnp.tile(w_spatial[:, :t_new, :t_new], (1, reps, reps))
    bs_sample = jnp.tile(jnp.repeat(b_spatial[:, :t_new].T, HEAD_DIM, axis=1), (reps, 1))
    wo_b = w_out.astype(BF16)
    g_final2 = g_final.reshape(1, D_MODEL)

    cos_p, sin_p = _rope_tables(jnp.arange(seq, dtype=jnp.int32))
    pos_s = past + jnp.arange(t_new, dtype=jnp.int32)
    cos_s, sin_s = _rope_tables(jnp.tile(pos_s, TM // t_new))

    xp = x_prompt.reshape(batch * seq, D_MODEL)
    (kt, vt, kit, kbf, vtb, ki2, qz, qiz, wt, sgb, mixa, qc, sgc) = _proj(
        xp, g_norm2, w_r, cos_p, sin_p, w_spatial, bs_prompt, g_v2, CHUNK, seq // TM, False)
    mkt, mvt = _memkv(mem_prompt.reshape(batch * N_MEM, D_MODEL), g_mem.reshape(1, D_MODEL),
                      w_mem_kv.astype(BF16))
    yc = _memattn(qc.reshape(batch * seq // MEMATTN_TM, MEMATTN_TM, C_WIDTH), mkt, mvt,
                  seq // MEMATTN_TM).reshape(batch * seq, C_WIDTH)
    yb = _dsa_prompt(qz, qiz, wt, ki2, kbf, vtb, batch, seq)
    y_prompt = _merge(xp, mixa, yb, sgb, yc, sgc, wo_b, g_final2).reshape(batch, seq, D_MODEL)

    ns = db * t_new
    xs = x_sample.reshape(ns, D_MODEL)
    (q_s, k_s, v_s, qi_s, ki_s, wi_s, sgb_s, mixa_s, qc_s, sgc_s, vn_s) = _proj(
        xs, g_norm2, w_r, cos_s, sin_s, ws_sample, bs_sample, g_v2, t_new, 1, True)
    sc, scn = _idx_sample(page_table, qi_s.reshape(db, t_new * IDX_HEADS, IDX_DIM),
                          wi_s.reshape(db, t_new * IDX_HEADS, 1), ki_s.reshape(db, t_new, IDX_DIM),
                          cache_idx_k.transpose(0, 2, 1))
    topk_s = min(TOPK_MAX, (past + t_new) // 4)
    bias, biasn = _thr_sample(sc.reshape(ns, past), scn.reshape(ns, LANES), topk_s)
    yb_pad = _attn_sample(page_table, q_s.reshape(db, t_new, B_WIDTH), bias.reshape(db, t_new, past),
                          biasn.reshape(db, t_new, LANES), k_s.reshape(db, t_new, B_WIDTH),
                          v_s.reshape(db, t_new, B_WIDTH), _channels_first(cache_k), _channels_first(cache_v))
    yb_s = yb_pad[:, ::B_HEADS, :].reshape(ns, B_WIDTH)
    yc_pad = _memattn_sample(qc_s.reshape(db, t_new, C_WIDTH),
                             _channels_first(cache_mem_k), _channels_first(cache_mem_v))
    yc_s = yc_pad[:, ::SUBLANES, :].reshape(ns, C_WIDTH)
    y_sample = _merge(xs, mixa_s, yb_s, sgb_s, yc_s, sgc_s, wo_b, g_final2).reshape(db, t_new, D_MODEL)

    return (y_prompt, y_sample,
            _heads_last(kt, B_HEADS), _heads_last(vt, B_HEADS), kit.transpose(0, 2, 1),
            _heads_last(mkt, C_HEADS), _heads_last(mvt, C_HEADS),
            k_s.reshape(db, t_new, B_HEADS, HEAD_DIM), v_s.reshape(db, t_new, B_HEADS, HEAD_DIM),
            ki_s.reshape(db, t_new, IDX_DIM), vn_s.reshape(db, t_new, A_WIDTH))
```

```python
import functools

import jax
import jax.numpy as jnp
from jax import lax
from jax.experimental import pallas as pl
from jax.experimental.pallas import tpu as pltpu

F32 = jnp.float32
BF16 = jnp.bfloat16

HEAD_DIM = 64
HALF = HEAD_DIM // 2
D_MODEL = 1024
A_GROUPS = 4
A_WIDTH = A_GROUPS * HEAD_DIM
CHUNK = 128
B_HEADS = 8
B_WIDTH = B_HEADS * HEAD_DIM
IDX_HEADS = 8
IDX_DIM = 64
TOPK_MAX = 256
N_MEM = 256
C_HEADS = 4
C_WIDTH = C_HEADS * HEAD_DIM
PAGE_SIZE = 128
ROPE_THETA = 10000.0
EPS = 1e-6
IDX_W_SCALE = (IDX_HEADS * IDX_DIM) ** -0.5
ATTN_SCALE = HEAD_DIM ** -0.5
LOG2E = 1.4426950408889634

LANES = 128
SUBLANES = 8
NEG = -1e30
FLT_MAX = 3.4028234663852886e38
PACKED_ROWS = 16

TM = 512
MEMATTN_TM = 1024
MEMATTN_SEQS_PER_STEP = 16
MERGE_TM = 1024
QB = 256
KC = 256
V_ROWS = HEAD_DIM + PACKED_ROWS
QK_LEAD = 3
IDX_PAGES_PER_STEP = 64
ATTN_PAGES_PER_STEP = 32
THR_ROWS = 128
VMEM_LIMIT = 52 * 1024 * 1024

OFF_U, OFF_VA, OFF_GA = 0, 256, 512
OFF_Q, OFF_K, OFF_V, OFF_GB, OFF_QI = 768, 1280, 1792, 2304, 2816
OFF_QC, OFF_GC, OFF_KI2, OFF_WI, W_TOTAL = 3328, 3584, 3840, 3968, 4096


def _cparams(sem):
    return pltpu.CompilerParams(dimension_semantics=sem, vmem_limit_bytes=VMEM_LIMIT)


def _iota(shape, dim):
    return lax.broadcasted_iota(jnp.int32, shape, dim)


def _silu(x):
    return x / (1.0 + jnp.exp(-x))


def _dot(a, b):
    return jnp.dot(a, b, preferred_element_type=F32)


def _dot_nt(a, b):
    return lax.dot_general(a, b, (((1,), (1,)), ((), ())), preferred_element_type=F32)


def _key_to_float(k):
    bits = k ^ ((k >> 31) & jnp.int32(0x7FFFFFFF))
    return lax.bitcast_convert_type(bits, F32)


def _bitwise_kth(count_ge, kth, shape, bits):
    cnt0 = count_ge(jnp.zeros(shape, jnp.int32))
    ok0 = cnt0 >= kth
    key = jnp.where(ok0, jnp.int32(0), jnp.int32(-2 ** (bits - 1)))
    cnt_key = jnp.where(ok0, cnt0, jnp.float32(2 ** 24))

    def bit_body(it, carry):
        key, cnt_key = carry
        trial = key + lax.shift_left(jnp.int32(1), bits - 2 - it)
        cnt = count_ge(trial)
        ok = cnt >= kth
        return jnp.where(ok, trial, key), jnp.where(ok, cnt, cnt_key)

    return lax.fori_loop(0, bits - 1, bit_body, (key, cnt_key))


def _kth_largest(count_ge, kth, shape):
    key, cnt_key = _bitwise_kth(lambda k: count_ge(_key_to_float(k)), kth, shape, 32)
    return _key_to_float(key), cnt_key


def _rope(t, cos, sin):
    outs = []
    first = (_iota((t.shape[0], LANES), 1) % HEAD_DIM) < HALF
    for j in range(t.shape[1] // LANES):
        x = t[:, j * LANES:(j + 1) * LANES]
        partner = jnp.where(first, pltpu.roll(x, LANES - HALF, 1), pltpu.roll(x, HALF, 1))
        outs.append(x * cos + partner * sin)
    return outs[0] if len(outs) == 1 else jnp.concatenate(outs, axis=1)


def _proj_kernel(chunk, sample, x_ref, g_ref, w_ref, cos_ref, sin_ref, ws_ref, bs_ref, gv_ref, *outs):
    if sample:
        q_out, k_out, v_out, qi_out, ki_out, wi_out, sgb_out, mixa_out, qc_out, sgc_out, vn_out = outs
    else:
        (kt_out, vt_out, kit_out, kbf_out, vtb_out, ki2_out, qz_out, qiz_out, wt_out,
         sgb_out, mixa_out, qc_out, sgc_out) = outs
    tm = x_ref.shape[0]
    x = x_ref[...]
    ms = jnp.mean(x * x, axis=-1, keepdims=True)
    h = (x * lax.rsqrt(ms + EPS) * g_ref[...]).astype(BF16)
    cos = cos_ref[...]
    sin = sin_ref[...]

    def seg(a, b):
        return _dot(h, w_ref[:, a:b])

    half_id = (_iota((tm, LANES), 1) // HEAD_DIM)

    def head_split(t, out):
        for hd in range(B_HEADS):
            pair = t[:, (hd // 2) * LANES:(hd // 2 + 1) * LANES]
            out[hd] = jnp.where(half_id == hd % 2, pair, 0.0).astype(BF16)

    q = _rope(seg(OFF_Q, OFF_K), cos, sin)
    k = _rope(seg(OFF_K, OFF_V), cos, sin)
    v = seg(OFF_V, OFF_GB)
    qi = _rope(seg(OFF_QI, OFF_QC), cos, sin)
    ki2 = _rope(seg(OFF_KI2, OFF_WI), cos, sin)
    wi = seg(OFF_WI, W_TOTAL) * IDX_W_SCALE
    qc = seg(OFF_QC, OFF_GC) * ATTN_SCALE
    sgb_out[...] = _silu(seg(OFF_GB, OFF_QI)).astype(BF16)
    sgc_out[...] = _silu(seg(OFF_GC, OFF_KI2)).astype(BF16)
    qc_out[...] = qc.astype(qc_out.dtype)
    if sample:
        q_out[...] = q
        k_out[...] = k
        v_out[...] = v
        qi_out[...] = qi
        ki_out[...] = ki2[:, :IDX_DIM]
        wi_out[...] = wi[:, :IDX_HEADS]
    else:
        head_split(q * (ATTN_SCALE * LOG2E), qz_out)
        head_split(qi, qiz_out)
        kt_out[0] = k.T
        kbf_out[...] = k.astype(BF16)
        vt = v.T
        vt_out[0] = vt
        vtb = vt.astype(BF16)
        ones = jnp.ones((V_ROWS - HEAD_DIM, KC), BF16)
        for j in range(tm // KC):
            for hd in range(B_HEADS):
                vtb_out[j, hd * V_ROWS:hd * V_ROWS + HEAD_DIM, :] = (
                    vtb[hd * HEAD_DIM:(hd + 1) * HEAD_DIM, j * KC:(j + 1) * KC])
                vtb_out[j, hd * V_ROWS + HEAD_DIM:(hd + 1) * V_ROWS, :] = ones
        kit_out[0] = ki2.T[:IDX_DIM, :]
        ki2_out[...] = ki2.astype(BF16)
        wt_out[...] = wi.T[:IDX_HEADS, :]

    u = seg(OFF_U, OFF_VA)
    va = seg(OFF_VA, OFF_GA)
    ga = seg(OFF_GA, OFF_Q)
    grp_r = _iota((A_WIDTH, A_WIDTH), 0) // HEAD_DIM
    grp_c = _iota((A_WIDTH, A_WIDTH), 1) // HEAD_DIM
    gmat = jnp.where(grp_r == grp_c, 1.0 / HEAD_DIM, 0.0).astype(BF16)

    def group_mean(t):
        hi = t.astype(BF16)
        lo = (t - hi.astype(F32)).astype(BF16)
        return _dot(hi, gmat) + _dot(lo, gmat)

    mu = group_mean(va)
    d = va - mu
    var = group_mean(d * d)
    vn = d * lax.rsqrt(var + EPS) * gv_ref[...]
    if sample:
        vn_out[...] = vn
    vnb = vn.astype(BF16)
    row = _iota((CHUNK, CHUNK), 0)
    col = _iota((CHUNK, CHUNK), 1)
    causal = (row >= col) & ((row // chunk) == (col // chunk))
    lane_grp = _iota((CHUNK, A_WIDTH), 1) // HEAD_DIM
    wms = [jnp.where(causal, ws_ref[g], 0.0).astype(BF16) for g in range(A_GROUPS)]
    mixes = []
    for j in range(tm // CHUNK):
        vc = vnb[j * CHUNK:(j + 1) * CHUNK, :]
        m = bs_ref[...]
        for g in range(A_GROUPS):
            m = m + jnp.where(lane_grp == g, _dot(wms[g], vc), 0.0)
        mixes.append(m)
    mix = jnp.concatenate(mixes, axis=0)
    mixa_out[...] = (_silu(ga) * u * mix).astype(BF16)


def _proj(x, g_norm, w_r, cos_t, sin_t, ws, bs, g_v, chunk, seq_blocks, sample):
    n = x.shape[0]
    nt = n // TM
    nb = nt // seq_blocks
    row = lambda w: pl.BlockSpec((TM, w), lambda i: (i, 0))
    const2 = lambda a: pl.BlockSpec(a.shape, lambda i: (0, 0))
    sds = jax.ShapeDtypeStruct
    in_specs = [
        row(D_MODEL), const2(g_norm), const2(w_r),
        pl.BlockSpec((TM, LANES), lambda i: (i % seq_blocks, 0)),
        pl.BlockSpec((TM, LANES), lambda i: (i % seq_blocks, 0)),
        pl.BlockSpec(ws.shape, lambda i: (0, 0, 0)), const2(bs), const2(g_v),
    ]
    if sample:
        outs = [
            (sds((n, B_WIDTH), F32), row(B_WIDTH)),
            (sds((n, B_WIDTH), F32), row(B_WIDTH)),
            (sds((n, B_WIDTH), F32), row(B_WIDTH)),
            (sds((n, IDX_HEADS * IDX_DIM), F32), row(IDX_HEADS * IDX_DIM)),
            (sds((n, IDX_DIM), F32), row(IDX_DIM)),
            (sds((n, IDX_HEADS), F32), row(IDX_HEADS)),
            (sds((n, B_WIDTH), BF16), row(B_WIDTH)),
            (sds((n, A_WIDTH), BF16), row(A_WIDTH)),
            (sds((n, C_WIDTH), F32), row(C_WIDTH)),
            (sds((n, C_WIDTH), BF16), row(C_WIDTH)),
            (sds((n, A_WIDTH), F32), row(A_WIDTH)),
        ]
    else:
        seq = seq_blocks * TM
        tspec = lambda w: pl.BlockSpec((1, w, TM), lambda i: (i // seq_blocks, 0, i % seq_blocks))
        hspec = pl.BlockSpec((B_HEADS, TM, LANES), lambda i: (0, i, 0))
        outs = [
            (sds((nb, B_WIDTH, seq), F32), tspec(B_WIDTH)),
            (sds((nb, B_WIDTH, seq), F32), tspec(B_WIDTH)),
            (sds((nb, IDX_DIM, seq), F32), tspec(IDX_DIM)),
            (sds((n, B_WIDTH), BF16), row(B_WIDTH)),
            (sds((n // KC, B_HEADS * V_ROWS, KC), BF16),
             pl.BlockSpec((TM // KC, B_HEADS * V_ROWS, KC), lambda i: (i, 0, 0))),
            (sds((n, LANES), BF16), row(LANES)),
            (sds((B_HEADS, n, LANES), BF16), hspec),
            (sds((IDX_HEADS, n, LANES), BF16), hspec),
            (sds((IDX_HEADS, n), F32), pl.BlockSpec((IDX_HEADS, TM), lambda i: (0, i))),
            (sds((n, B_WIDTH), BF16), row(B_WIDTH)),
            (sds((n, A_WIDTH), BF16), row(A_WIDTH)),
            (sds((n, C_WIDTH), BF16), row(C_WIDTH)),
            (sds((n, C_WIDTH), BF16), row(C_WIDTH)),
        ]
    return pl.pallas_call(
        functools.partial(_proj_kernel, chunk, sample),
        grid=(nt,), in_specs=in_specs, out_specs=[o[1] for o in outs], out_shape=[o[0] for o in outs],
        compiler_params=_cparams(("parallel",)), name="proj",
    )(x, g_norm, w_r, cos_t, sin_t, ws, bs, g_v)


def _memkv_kernel(m_ref, g_ref, w_ref, mkt_out, mvt_out):
    x = m_ref[...]
    ms = jnp.mean(x * x, axis=-1, keepdims=True)
    h = (x * lax.rsqrt(ms + EPS) * g_ref[...]).astype(BF16)
    kv = _dot(h, w_ref[...])
    mkt_out[0] = kv[:, :C_WIDTH].T
    mvt_out[0] = kv[:, C_WIDTH:].T


def _memkv(mem, g_mem, w_kv):
    n = mem.shape[0]
    nb = n // N_MEM
    return pl.pallas_call(
        _memkv_kernel, grid=(nb,),
        in_specs=[pl.BlockSpec((N_MEM, D_MODEL), lambda i: (i, 0)),
                  pl.BlockSpec(g_mem.shape, lambda i: (0, 0)),
                  pl.BlockSpec(w_kv.shape, lambda i: (0, 0))],
        out_specs=[pl.BlockSpec((1, C_WIDTH, N_MEM), lambda i: (i, 0, 0))] * 2,
        out_shape=[jax.ShapeDtypeStruct((nb, C_WIDTH, N_MEM), F32)] * 2,
        compiler_params=_cparams(("parallel",)), name="memkv",
    )(mem, g_mem, w_kv)


def _memattn_kernel(q_ref, mkt_ref, mvt_ref, o_ref):
    q = q_ref[0]
    mkt = mkt_ref[0].astype(BF16)
    mvt = mvt_ref[0].astype(BF16)
    t = q.shape[0]
    lane_head = _iota((t, C_WIDTH), 1) // HEAD_DIM
    out = jnp.zeros((t, C_WIDTH), F32)
    for hd in range(C_HEADS):
        qh = jnp.where(lane_head == hd, q, 0.0).astype(BF16)
        s = _dot(qh, mkt)
        p = jnp.exp(s - jnp.max(s, axis=-1, keepdims=True))
        o = _dot_nt(p.astype(BF16), mvt) / jnp.sum(p, axis=-1, keepdims=True)
        out = out + jnp.where(lane_head == hd, o, 0.0)
    o_ref[0] = out.astype(o_ref.dtype)


def _memattn(q3, mkt3, mvt3, tiles_per_mem):
    g, t, _ = q3.shape
    mem_spec = pl.BlockSpec((1, C_WIDTH, N_MEM), lambda i: (i // tiles_per_mem, 0, 0))
    return pl.pallas_call(
        _memattn_kernel, grid=(g,),
        in_specs=[pl.BlockSpec((1, t, C_WIDTH), lambda i: (i, 0, 0)), mem_spec, mem_spec],
        out_specs=pl.BlockSpec((1, t, C_WIDTH), lambda i: (i, 0, 0)),
        out_shape=jax.ShapeDtypeStruct((g, t, C_WIDTH), BF16),
        compiler_params=_cparams(("parallel",)), name="memattn",
    )(q3, mkt3, mvt3)


def _memattn_sample_kernel(q_ref, mkt_ref, mvt_ref, o_ref):
    t = q_ref.shape[1]
    mask8 = (_iota((SUBLANES, C_WIDTH), 1) // HEAD_DIM) == _iota((SUBLANES, C_WIDTH), 0)
    for g in range(q_ref.shape[0]):
        q4 = q_ref[g]
        qbd = jnp.concatenate(
            [jnp.where(mask8, jnp.broadcast_to(q4[i:i + 1, :], (SUBLANES, C_WIDTH)), 0.0) for i in range(t)], axis=0)
        s = _dot(qbd.astype(BF16), mkt_ref[g].astype(BF16))
        p = jnp.exp(s - jnp.max(s, axis=-1, keepdims=True))
        o = _dot_nt(p.astype(BF16), mvt_ref[g].astype(BF16)) / jnp.sum(p, axis=-1, keepdims=True)
        for i in range(t):
            slab = jnp.where(mask8, o[i * SUBLANES:(i + 1) * SUBLANES, :], 0.0)
            o_ref[g, i * SUBLANES:(i + 1) * SUBLANES, :] = jnp.broadcast_to(
                jnp.sum(slab, axis=0, keepdims=True), (SUBLANES, C_WIDTH))


def _memattn_sample(q3, mkt3, mvt3):
    n, t, _ = q3.shape
    g = min(MEMATTN_SEQS_PER_STEP, n)
    mem_spec = pl.BlockSpec((g, C_WIDTH, N_MEM), lambda i: (i, 0, 0))
    return pl.pallas_call(
        _memattn_sample_kernel, grid=(n // g,),
        in_specs=[pl.BlockSpec((g, t, C_WIDTH), lambda i: (i, 0, 0)), mem_spec, mem_spec],
        out_specs=pl.BlockSpec((g, t * SUBLANES, C_WIDTH), lambda i: (i, 0, 0)),
        out_shape=jax.ShapeDtypeStruct((n, t * SUBLANES, C_WIDTH), F32),
        compiler_params=_cparams(("parallel",)), name="memattn_sample",
    )(q3, mkt3, mvt3)


def _dsa_prompt_kernel(topk, qz_ref, qiz_ref, wt_ref, ki2_ref, kbf_ref, vt_ref, o_ref,
                       sc_ref, acc_ref, m_ref, s_ref, sb_ref):
    i = pl.program_id(1)
    nk = i + 1
    npair = (nk + 1) // 2
    kth = jnp.float32(topk)

    def sum_rows(x):
        return jnp.sum(x.reshape(KC // SUBLANES, SUBLANES, QB), axis=0)

    def max_rows(x):
        return jnp.max(x.reshape(KC // SUBLANES, SUBLANES, QB), axis=0)

    def score_pair(j, _):
        for c in (2 * j, 2 * j + 1):
            keys = ki2_ref[0, c]
            acc = jnp.zeros((KC, QB), F32)
            for hd in range(IDX_HEADS):
                y = _dot_nt(keys, qiz_ref[hd])
                acc = acc + jnp.maximum(y, 0.0) * wt_ref[hd:hd + 1, :]
            kpos = c * KC + _iota((KC, QB), 0)
            qpos = i * QB + _iota((KC, QB), 1)
            masked = jnp.where(kpos <= qpos, acc, -jnp.inf)
            sc_ref[c] = masked
            sb_ref[c] = masked.astype(BF16)
        return 0

    lax.fori_loop(0, npair, score_pair, 0)

    def count(pred):
        def body(j, cnt):
            return (cnt + sum_rows(pred(sc_ref[2 * j]).astype(F32))
                    + sum_rows(pred(sc_ref[2 * j + 1]).astype(F32)))
        return jnp.sum(lax.fori_loop(0, npair, body, jnp.zeros((SUBLANES, QB), F32)), axis=0, keepdims=True)

    def count_ge_packed(t):
        tb = t.astype(BF16)
        def slab_count(c):
            ones = jnp.where(sb_ref[c] >= tb, jnp.ones((), BF16), jnp.zeros((), BF16))
            acc = ones[0:PACKED_ROWS]
            for r in range(1, KC // PACKED_ROWS):
                acc = acc + ones[r * PACKED_ROWS:(r + 1) * PACKED_ROWS]
            return acc

        def body(j, cnt):
            return cnt + slab_count(2 * j) + slab_count(2 * j + 1)

        cnt16 = lax.fori_loop(0, npair, body, jnp.zeros((PACKED_ROWS, QB), BF16))
        return jnp.sum(cnt16.astype(F32), axis=0, keepdims=True)

    def write_ge(thr):
        def body(j, _):
            sc_ref[2 * j] = jnp.where(sc_ref[2 * j] >= thr, 0.0, NEG)
            sc_ref[2 * j + 1] = jnp.where(sc_ref[2 * j + 1] >= thr, 0.0, NEG)
            return 0
        lax.fori_loop(0, npair, body, 0)

    @pl.when(nk * KC <= topk)
    def _():
        write_ge(jnp.full((1, QB), -FLT_MAX, F32))

    @pl.when(nk * KC > topk)
    def _():
        def bf16_key_to_float_key(k16):
            return k16 * jnp.int32(2 ** 16) + jnp.where(k16 < 0, jnp.int32(2 ** 16 - 1), jnp.int32(0))

        tau_k16, _ = _bitwise_kth(
            lambda k16: count_ge_packed(_key_to_float(bf16_key_to_float_key(k16))), kth, (1, QB), 16)
        base = bf16_key_to_float_key(tau_k16) - jnp.int32(2 ** 16)

        def fine_bit(it, carry):
            off, cnt_off = carry
            cand = off + lax.shift_left(jnp.int32(1), 16 - it)
            cnt = count(lambda x: x >= _key_to_float(base + cand))
            ok = cnt >= kth
            return jnp.where(ok, cand, off), jnp.where(ok, cnt, cnt_off)

        off, cnt_thr = lax.fori_loop(0, 17, fine_bit,
                                     (jnp.zeros((1, QB), jnp.int32), jnp.full((1, QB), 2.0 ** 24, F32)))
        thr = _key_to_float(base + off)
        few = (i * QB + 1 + _iota((1, QB), 1)).astype(F32) <= kth
        thr = jnp.where(few, -FLT_MAX, thr)
        cnt_thr = jnp.where(few, kth, cnt_thr)
        has_tie = jnp.max(cnt_thr) > kth

        @pl.when(jnp.logical_not(has_tie))
        def _():
            write_ge(thr)

        @pl.when(has_tie)
        def _():
            need = kth - count(lambda x: x > thr)
            ltri = (_iota((KC, KC), 1) < _iota((KC, KC), 0)).astype(BF16)

            def body(c, carry):
                x = sc_ref[c]
                eq = x == thr
                before = _dot(ltri, eq.astype(BF16)) + carry
                sel = (x > thr) | (eq & (before < need))
                sc_ref[c] = jnp.where(sel, 0.0, NEG)
                return carry + jnp.sum(sum_rows(eq.astype(F32)), axis=0, keepdims=True)

            lax.fori_loop(0, nk, body, jnp.zeros((1, QB), F32))

    m_ref[...] = jnp.full((B_HEADS, QB), NEG, F32)
    acc_ref[...] = jnp.zeros((B_HEADS * V_ROWS, QB), F32)

    def biased_scores(c, hd):
        pair = hd // 2
        kc = kbf_ref[0, c, :, pair * LANES:(pair + 1) * LANES]
        return _dot_nt(kc, qz_ref[hd]) + sc_ref[c]

    for hd in range(B_HEADS):
        s_ref[hd] = biased_scores(0, hd)

    def attend(c, _):
        nxt = jnp.minimum(c + 1, nk - 1)
        ahead = [biased_scores(nxt, hd) for hd in range(QK_LEAD)]
        for hd in range(B_HEADS):
            rows = slice(hd * V_ROWS, (hd + 1) * V_ROWS)
            s = s_ref[hd]
            m_old = m_ref[hd:hd + 1, :]
            m_new = jnp.maximum(m_old, jnp.max(max_rows(s), axis=0, keepdims=True))
            alpha = jnp.exp2(m_old - m_new)
            p = jnp.exp2(s - m_new)
            acc_ref[rows, :] = alpha * acc_ref[rows, :] + _dot(vt_ref[0, c, rows, :], p.astype(BF16))
            m_ref[hd:hd + 1, :] = m_new
            s_ref[hd] = ahead[hd]
            if hd + QK_LEAD < B_HEADS:
                ahead.append(biased_scores(nxt, hd + QK_LEAD))
        return 0

    lax.fori_loop(0, nk, attend, 0)
    outs = []
    for hd in range(B_HEADS):
        base = hd * V_ROWS
        outs.append(acc_ref[base:base + HEAD_DIM, :] / acc_ref[base + HEAD_DIM:base + HEAD_DIM + 1, :])
    o_ref[...] = jnp.concatenate(outs, axis=0).T.astype(o_ref.dtype)


def _dsa_prompt(qz, qiz, wt, ki2, kbf, vtb, batch, seq):
    nq = seq // QB
    nc = seq // KC
    topk = min(TOPK_MAX, seq // 4)
    n = batch * seq
    qspec = pl.BlockSpec((B_HEADS, QB, LANES), lambda b, i: (0, b * nq + i, 0))
    return pl.pallas_call(
        functools.partial(_dsa_prompt_kernel, topk),
        grid=(batch, nq),
        in_specs=[qspec, qspec,
                  pl.BlockSpec((IDX_HEADS, QB), lambda b, i: (0, b * nq + i)),
                  pl.BlockSpec((1, nc, KC, LANES), lambda b, i: (b, 0, 0, 0)),
                  pl.BlockSpec((1, nc, KC, B_WIDTH), lambda b, i: (b, 0, 0, 0)),
                  pl.BlockSpec((1, nc, B_HEADS * V_ROWS, KC), lambda b, i: (b, 0, 0, 0))],
        out_specs=pl.BlockSpec((QB, B_WIDTH), lambda b, i: (b * nq + i, 0)),
        out_shape=jax.ShapeDtypeStruct((n, B_WIDTH), BF16),
        scratch_shapes=[pltpu.VMEM((nc, KC, QB), F32), pltpu.VMEM((B_HEADS * V_ROWS, QB), F32),
                        pltpu.VMEM((B_HEADS, QB), F32), pltpu.VMEM((B_HEADS, KC, QB), F32),
                        pltpu.VMEM((nc, KC, QB), BF16)],
        compiler_params=_cparams(("parallel", "arbitrary")), name="dsa_prompt",
    )(qz, qiz, wt, ki2.reshape(batch, nc, KC, LANES), kbf.reshape(batch, nc, KC, B_WIDTH),
      vtb.reshape(batch, nc, B_HEADS * V_ROWS, KC))


def _page_specs(block, n_pages_per_step):
    def make(r):
        return pl.BlockSpec(block, lambda b, j, pt: (pt[b, j * n_pages_per_step + r], 0, 0))
    return [make(r) for r in range(n_pages_per_step)]


def _idx_sample_kernel(t_new, pps, pt_ref, qi_ref, w_ref, kin_ref, *rest):
    pages = rest[:pps]
    sc_out, scn_out = rest[pps:]
    j = pl.program_id(1)
    qi = qi_ref[0]
    w = w_ref[0]
    qib = qi.astype(BF16)

    def token_scores(y):
        z = jnp.maximum(y, 0.0) * w
        return jnp.sum(z.reshape(t_new, IDX_HEADS, y.shape[1]), axis=1)

    for r in range(pps):
        y = _dot(qib, pages[r][0].astype(BF16))
        sc_out[0, :, r * PAGE_SIZE:(r + 1) * PAGE_SIZE] = token_scores(y)

    @pl.when(j == 0)
    def _():
        qf = qib.astype(F32)
        lane = _iota((t_new, LANES), 1)
        tok = _iota((t_new, LANES), 0)
        new = jnp.full((t_new, LANES), -jnp.inf, F32)
        for tk in range(t_new):
            kr = kin_ref[0, tk:tk + 1, :].astype(BF16).astype(F32)
            col = token_scores(jnp.sum(qf * kr, axis=-1, keepdims=True))
            new = jnp.where((lane == tk) & (tok >= tk), col, new)
        scn_out[0] = new


def _idx_sample(page_table, qi32, w32, ki_new, cache_idx_kt):
    db, rows, _ = qi32.shape
    t_new = rows // IDX_HEADS
    n_pages = page_table.shape[1]
    pps = min(IDX_PAGES_PER_STEP, n_pages)
    steps = n_pages // pps
    span = pps * PAGE_SIZE
    grid_spec = pltpu.PrefetchScalarGridSpec(
        num_scalar_prefetch=1, grid=(db, steps),
        in_specs=[pl.BlockSpec((1, rows, IDX_DIM), lambda b, j, pt: (b, 0, 0)),
                  pl.BlockSpec((1, rows, 1), lambda b, j, pt: (b, 0, 0)),
                  pl.BlockSpec((1, t_new, IDX_DIM), lambda b, j, pt: (b, 0, 0))]
        + _page_specs((1, IDX_DIM, PAGE_SIZE), pps),
        out_specs=[pl.BlockSpec((1, t_new, span), lambda b, j, pt: (b, 0, j)),
                   pl.BlockSpec((1, t_new, LANES), lambda b, j, pt: (b, 0, 0))])
    return pl.pallas_call(
        functools.partial(_idx_sample_kernel, t_new, pps), grid_spec=grid_spec,
        out_shape=[jax.ShapeDtypeStruct((db, t_new, n_pages * PAGE_SIZE), F32),
                   jax.ShapeDtypeStruct((db, t_new, LANES), F32)],
        compiler_params=_cparams(("parallel", "arbitrary")), name="idx_sample",
    )(page_table, qi32, w32, ki_new, *([cache_idx_kt] * pps))


def _thr_sample_kernel(topk, sc_ref, scn_ref, b_out, bn_out):
    kth = jnp.float32(topk)
    n_chunks = sc_ref.shape[1] // LANES

    def count(pred):
        return (jnp.sum(pred(sc_ref[...]).astype(F32), axis=-1, keepdims=True)
                + jnp.sum(pred(scn_ref[...]).astype(F32), axis=-1, keepdims=True))

    thr, cnt_thr = _kth_largest(lambda t: count(lambda x: x >= t), kth, (THR_ROWS, 1))
    has_tie = jnp.max(cnt_thr) > kth

    @pl.when(jnp.logical_not(has_tie))
    def _():
        b_out[...] = jnp.where(sc_ref[...] >= thr, 0.0, NEG)
        bn_out[...] = jnp.where(scn_ref[...] >= thr, 0.0, NEG)

    @pl.when(has_tie)
    def _():
        need = kth - count(lambda x: x > thr)
        utri = (_iota((LANES, LANES), 0) < _iota((LANES, LANES), 1)).astype(BF16)
        carry = jnp.zeros((THR_ROWS, 1), F32)
        for c in range(n_chunks + 1):
            x = sc_ref[:, c * LANES:(c + 1) * LANES] if c < n_chunks else scn_ref[...]
            eq = x == thr
            before = _dot(eq.astype(BF16), utri) + carry
            sel = (x > thr) | (eq & (before < need))
            bias = jnp.where(sel, 0.0, NEG)
            if c < n_chunks:
                b_out[:, c * LANES:(c + 1) * LANES] = bias
            else:
                bn_out[...] = bias
            carry = carry + jnp.sum(eq.astype(F32), axis=-1, keepdims=True)


def _thr_sample(sc, scn, topk):
    n, width = sc.shape
    return pl.pallas_call(
        functools.partial(_thr_sample_kernel, topk), grid=(n // THR_ROWS,),
        in_specs=[pl.BlockSpec((THR_ROWS, width), lambda i: (i, 0)),
                  pl.BlockSpec((THR_ROWS, LANES), lambda i: (i, 0))],
        out_specs=[pl.BlockSpec((THR_ROWS, width), lambda i: (i, 0)),
                   pl.BlockSpec((THR_ROWS, LANES), lambda i: (i, 0))],
        out_shape=[jax.ShapeDtypeStruct((n, width), F32), jax.ShapeDtypeStruct((n, LANES), F32)],
        compiler_params=_cparams(("parallel",)), name="thr_sample",
    )(sc, scn)


def _attn_sample_kernel(t_new, pps, pt_ref, q_ref, bias_ref, biasn_ref, kn_ref, vn_ref, *rest):
    kpages = rest[:pps]
    vpages = rest[pps:2 * pps]
    o_ref, m_ref, l_ref, acc_ref, qbd_ref = rest[2 * pps:]
    j = pl.program_id(1)
    rows = t_new * B_HEADS
    mask8 = (_iota((SUBLANES, B_WIDTH), 1) // HEAD_DIM) == _iota((SUBLANES, B_WIDTH), 0)

    def per_token(ref, width):
        return jnp.concatenate(
            [jnp.broadcast_to(ref[0, i:i + 1, :], (B_HEADS, width)) for i in range(t_new)], axis=0)

    @pl.when(j == 0)
    def _():
        m_ref[...] = jnp.full((rows, 1), NEG, F32)
        l_ref[...] = jnp.zeros((rows, 1), F32)
        acc_ref[...] = jnp.zeros((rows, B_WIDTH), F32)
        qbd_ref[...] = jnp.concatenate(
            [jnp.where(mask8, jnp.broadcast_to(q_ref[0, i:i + 1, :] * ATTN_SCALE, (B_HEADS, B_WIDTH)), 0.0)
             for i in range(t_new)], axis=0).astype(BF16)

    qbd = qbd_ref[...]
    span = pps * PAGE_SIZE
    s = jnp.concatenate([_dot(qbd, kpages[r][0].astype(BF16)) for r in range(pps)], axis=1)
    s = s + per_token(bias_ref, span)
    m_old = m_ref[...]
    m_new = jnp.maximum(m_old, jnp.max(s, axis=-1, keepdims=True))
    alpha = jnp.exp(m_old - m_new)
    p = jnp.exp(s - m_new)
    l_ref[...] = alpha * l_ref[...] + jnp.sum(p, axis=-1, keepdims=True)
    pb = p.astype(BF16)
    pv = jnp.zeros((rows, B_WIDTH), F32)
    for r in range(pps):
        pv = pv + _dot_nt(pb[:, r * PAGE_SIZE:(r + 1) * PAGE_SIZE], vpages[r][0].astype(BF16))
    acc_ref[...] = alpha * acc_ref[...] + pv
    m_ref[...] = m_new

    @pl.when(j == pl.num_programs(1) - 1)
    def _():
        qf = qbd.astype(F32)
        bn = per_token(biasn_ref, LANES)
        cols = []
        for tk in range(t_new):
            kr = kn_ref[0, tk:tk + 1, :].astype(BF16).astype(F32)
            cols.append(jnp.sum(qf * kr, axis=-1, keepdims=True) + bn[:, tk:tk + 1])
        m_old = m_ref[...]
        m_new = m_old
        for cval in cols:
            m_new = jnp.maximum(m_new, cval)
        alpha = jnp.exp(m_old - m_new)
        l = alpha * l_ref[...]
        acc = alpha * acc_ref[...]
        for tk in range(t_new):
            pk = jnp.exp(cols[tk] - m_new)
            l = l + pk
            acc = acc + pk.astype(BF16).astype(F32) * vn_ref[0, tk:tk + 1, :].astype(BF16).astype(F32)
        o = acc / l
        for i in range(t_new):
            slab = jnp.where(mask8, o[i * B_HEADS:(i + 1) * B_HEADS, :], 0.0)
            o_ref[0, i * B_HEADS:(i + 1) * B_HEADS, :] = jnp.broadcast_to(
                jnp.sum(slab, axis=0, keepdims=True), (B_HEADS, B_WIDTH))


def _attn_sample(page_table, q3, bias3, biasn3, k_new, v_new, cache_kt, cache_vt):
    db, t_new, _ = q3.shape
    n_pages = page_table.shape[1]
    pps = min(ATTN_PAGES_PER_STEP, n_pages)
    steps = n_pages // pps
    span = pps * PAGE_SIZE
    rows = t_new * B_HEADS
    per_b = lambda w: pl.BlockSpec((1, t_new, w), lambda b, j, pt: (b, 0, 0))
    grid_spec = pltpu.PrefetchScalarGridSpec(
        num_scalar_prefetch=1, grid=(db, steps),
        in_specs=[per_b(B_WIDTH),
                  pl.BlockSpec((1, t_new, span), lambda b, j, pt: (b, 0, j)),
                  per_b(LANES), per_b(B_WIDTH), per_b(B_WIDTH)]
        + _page_specs((1, B_WIDTH, PAGE_SIZE), pps)
        + _page_specs((1, B_WIDTH, PAGE_SIZE), pps),
        out_specs=pl.BlockSpec((1, rows, B_WIDTH), lambda b, j, pt: (b, 0, 0)),
        scratch_shapes=[pltpu.VMEM((rows, 1), F32), pltpu.VMEM((rows, 1), F32),
                        pltpu.VMEM((rows, B_WIDTH), F32), pltpu.VMEM((rows, B_WIDTH), BF16)])
    return pl.pallas_call(
        functools.partial(_attn_sample_kernel, t_new, pps), grid_spec=grid_spec,
        out_shape=jax.ShapeDtypeStruct((db, rows, B_WIDTH), F32),
        compiler_params=_cparams(("parallel", "arbitrary")), name="attn_sample",
    )(page_table, q3, bias3, biasn3, k_new, v_new, *([cache_kt] * pps), *([cache_vt] * pps))


def _merge_kernel(x_ref, ma_ref, yb_ref, sgb_ref, yc_ref, sgc_ref, wo_ref, g_ref, y_out):
    mb = (yb_ref[...].astype(F32) * sgb_ref[...].astype(F32)).astype(BF16)
    mc = (yc_ref[...].astype(F32) * sgc_ref[...].astype(F32)).astype(BF16)
    o = (_dot(ma_ref[...], wo_ref[0:A_WIDTH, :])
         + _dot(mb, wo_ref[A_WIDTH:A_WIDTH + B_WIDTH, :])
         + _dot(mc, wo_ref[A_WIDTH + B_WIDTH:, :]))
    z = x_ref[...] + o
    ms = jnp.mean(z * z, axis=-1, keepdims=True)
    y_out[...] = z * lax.rsqrt(ms + EPS) * g_ref[...]


def _merge(x, mixa, yb, sgb, yc, sgc, wo, g_final):
    n = x.shape[0]
    tm = min(MERGE_TM, n)
    row = lambda w: pl.BlockSpec((tm, w), lambda i: (i, 0))
    return pl.pallas_call(
        _merge_kernel, grid=(n // tm,),
        in_specs=[row(D_MODEL), row(A_WIDTH), row(B_WIDTH), row(B_WIDTH), row(C_WIDTH), row(C_WIDTH),
                  pl.BlockSpec(wo.shape, lambda i: (0, 0)), pl.BlockSpec(g_final.shape, lambda i: (0, 0))],
        out_specs=row(D_MODEL),
        out_shape=jax.ShapeDtypeStruct((n, D_MODEL), F32),
        compiler_params=_cparams(("parallel",)), name="merge",
    )(x, mixa, yb, sgb, yc, sgc, wo, g_final)


def _rope_tables(pos):
    inv = ROPE_THETA ** (-jnp.arange(0, HEAD_DIM, 2, dtype=F32) / HEAD_DIM)
    ang = pos.astype(F32)[:, None] * inv[None, :]
    c = jnp.cos(ang)
    s = jnp.sin(ang)
    return jnp.concatenate([c, c, c, c], axis=1), jnp.concatenate([-s, s, -s, s], axis=1)


def _heads_last(t, heads):
    b, _, n = t.shape
    return t.reshape(b, heads, HEAD_DIM, n).transpose(0, 3, 1, 2)


def _channels_first(t):
    p, n, heads, d = t.shape
    return t.transpose(0, 2, 3, 1).reshape(p, heads * d, n)


def kernel(x_prompt, x_sample, mem_prompt, cache_k, cache_v, cache_idx_k, cache_mem_k, cache_mem_v,
           page_table, g_norm, w_in, w_spatial, b_spatial, g_v, w_mem_kv, g_mem, w_out, g_final):
    batch, seq, _ = x_prompt.shape
    db, t_new, _ = x_sample.shape
    n_pages = page_table.shape[1]
    past = n_pages * PAGE_SIZE

    ki_cols = w_in[:, 3328:3392]
    wi_cols = w_in[:, 3392:3400]
    w_r = jnp.concatenate(
        [w_in[:, :3328], w_in[:, 3400:3912], ki_cols, ki_cols, wi_cols,
         jnp.zeros((D_MODEL, LANES - IDX_HEADS), F32)], axis=1).astype(BF16)
    g_norm2 = g_norm.reshape(1, D_MODEL)
    g_v2 = g_v.reshape(1, A_WIDTH)
    bs_prompt = jnp.repeat(b_spatial.T, HEAD_DIM, axis=1)
    reps = CHUNK // t_new
    ws_sample = jnp.tile(w_spatial[:, :t_new, :t_new], (1, reps, reps))
    bs_sample = jnp.tile(jnp.repeat(b_spatial[:, :t_new].T, HEAD_DIM, axis=1), (reps, 1))
    wo_b = w_out.astype(BF16)
    g_final2 = g_final.reshape(1, D_MODEL)

    cos_p, sin_p = _rope_tables(jnp.arange(seq, dtype=jnp.int32))
    pos_s = past + jnp.arange(t_new, dtype=jnp.int32)
    cos_s, sin_s = _rope_tables(jnp.tile(pos_s, TM // t_new))

    xp = x_prompt.reshape(batch * seq, D_MODEL)
    (kt, vt, kit, kbf, vtb, ki2, qz, qiz, wt, sgb, mixa, qc, sgc) = _proj(
        xp, g_norm2, w_r, cos_p, sin_p, w_spatial, bs_prompt, g_v2, CHUNK, seq // TM, False)
    mkt, mvt = _memkv(mem_prompt.reshape(batch * N_MEM, D_MODEL), g_mem.reshape(1, D_MODEL),
                      w_mem_kv.astype(BF16))
    yc = _memattn(qc.reshape(batch * seq // MEMATTN_TM, MEMATTN_TM, C_WIDTH), mkt, mvt,
                  seq // MEMATTN_TM).reshape(batch * seq, C_WIDTH)
    yb = _dsa_prompt(qz, qiz, wt, ki2, kbf, vtb, batch, seq)
    y_prompt = _merge(xp, mixa, yb, sgb, yc, sgc, wo_b, g_final2).reshape(batch, seq, D_MODEL)

    ns = db * t_new
    xs = x_sample.reshape(ns, D_MODEL)
    (q_s, k_s, v_s, qi_s, ki_s, wi_s, sgb_s, mixa_s, qc_s, sgc_s, vn_s) = _proj(
        xs, g_norm2, w_r, cos_s, sin_s, ws_sample, bs_sample, g_v2, t_new, 1, True)
    sc, scn = _idx_sample(page_table, qi_s.reshape(db, t_new * IDX_HEADS, IDX_DIM),
                          wi_s.reshape(db, t_new * IDX_HEADS, 1), ki_s.reshape(db, t_new, IDX_DIM),
                          cache_idx_k.transpose(0, 2, 1))
    topk_s = min(TOPK_MAX, (past + t_new) // 4)
    bias, biasn = _thr_sample(sc.reshape(ns, past), scn.reshape(ns, LANES), topk_s)
    yb_pad = _attn_sample(page_table, q_s.reshape(db, t_new, B_WIDTH), bias.reshape(db, t_new, past),
                          biasn.reshape(db, t_new, LANES), k_s.reshape(db, t_new, B_WIDTH),
                          v_s.reshape(db, t_new, B_WIDTH), _channels_first(cache_k), _channels_first(cache_v))
    yb_s = yb_pad[:, ::B_HEADS, :].reshape(ns, B_WIDTH)
    yc_pad = _memattn_sample(qc_s.reshape(db, t_new, C_WIDTH),
                             _channels_first(cache_mem_k), _channels_first(cache_mem_v))
    yc_s = yc_pad[:, ::SUBLANES, :].reshape(ns, C_WIDTH)
    y_sample = _merge(xs, mixa_s, yb_s, sgb_s, yc_s, sgc_s, wo_b, g_final2).reshape(db, t_new, D_MODEL)

    return (y_prompt, y_sample,
            _heads_last(kt, B_HEADS), _heads_last(vt, B_HEADS), kit.transpose(0, 2, 1),
            _heads_last(mkt, C_HEADS), _heads_last(mvt, C_HEADS),
            k_s.reshape(db, t_new, B_HEADS, HEAD_DIM), v_s.reshape(db, t_new, B_HEADS, HEAD_DIM),
            ki_s.reshape(db, t_new, IDX_DIM), vn_s.reshape(db, t_new, A_WIDTH))
```

```python
import functools

import jax
import jax.numpy as jnp
from jax import lax
from jax.experimental import pallas as pl
from jax.experimental.pallas import tpu as pltpu

F32 = jnp.float32
BF16 = jnp.bfloat16

HEAD_DIM = 64
HALF = HEAD_DIM // 2
D_MODEL = 1024
A_GROUPS = 4
A_WIDTH = A_GROUPS * HEAD_DIM
CHUNK = 128
B_HEADS = 8
B_WIDTH = B_HEADS * HEAD_DIM
IDX_HEADS = 8
IDX_DIM = 64
TOPK_MAX = 256
N_MEM = 256
C_HEADS = 4
C_WIDTH = C_HEADS * HEAD_DIM
PAGE_SIZE = 128
ROPE_THETA = 10000.0
EPS = 1e-6
IDX_W_SCALE = (IDX_HEADS * IDX_DIM) ** -0.5
ATTN_SCALE = HEAD_DIM ** -0.5
LOG2E = 1.4426950408889634

LANES = 128
SUBLANES = 8
NEG = -1e30
FLT_MAX = 3.4028234663852886e38
PACKED_ROWS = 16

TM = 512
MEMATTN_TM = 1024
MEMATTN_SEQS_PER_STEP = 16
MERGE_TM = 1024
QB = 256
KC = 256
V_ROWS = HEAD_DIM + PACKED_ROWS
QK_LEAD = 3
IDX_PAGES_PER_STEP = 64
ATTN_PAGES_PER_STEP = 32
THR_ROWS = 128
VMEM_LIMIT = 52 * 1024 * 1024

OFF_U, OFF_VA, OFF_GA = 0, 256, 512
OFF_Q, OFF_K, OFF_V, OFF_GB, OFF_QI = 768, 1280, 1792, 2304, 2816
OFF_QC, OFF_GC, OFF_KI2, OFF_WI, W_TOTAL = 3328, 3584, 3840, 3968, 4096


def _cparams(sem):
    return pltpu.CompilerParams(dimension_semantics=sem, vmem_limit_bytes=VMEM_LIMIT)


def _iota(shape, dim):
    return lax.broadcasted_iota(jnp.int32, shape, dim)


def _silu(x):
    return x / (1.0 + jnp.exp(-x))


def _dot(a, b):
    return jnp.dot(a, b, preferred_element_type=F32)


def _dot_nt(a, b):
    return lax.dot_general(a, b, (((1,), (1,)), ((), ())), preferred_element_type=F32)


def _key_to_float(k):
    bits = k ^ ((k >> 31) & jnp.int32(0x7FFFFFFF))
    return lax.bitcast_convert_type(bits, F32)


def _bitwise_kth(count_ge, kth, shape, bits):
    cnt0 = count_ge(jnp.zeros(shape, jnp.int32))
    ok0 = cnt0 >= kth
    key = jnp.where(ok0, jnp.int32(0), jnp.int32(-2 ** (bits - 1)))
    cnt_key = jnp.where(ok0, cnt0, jnp.float32(2 ** 24))

    def bit_body(it, carry):
        key, cnt_key = carry
        trial = key + lax.shift_left(jnp.int32(1), bits - 2 - it)
        cnt = count_ge(trial)
        ok = cnt >= kth
        return jnp.where(ok, trial, key), jnp.where(ok, cnt, cnt_key)

    return lax.fori_loop(0, bits - 1, bit_body, (key, cnt_key))


def _kth_largest(count_ge, kth, shape):
    key, cnt_key = _bitwise_kth(lambda k: count_ge(_key_to_float(k)), kth, shape, 32)
    return _key_to_float(key), cnt_key


def _rope(t, cos, sin):
    outs = []
    first = (_iota((t.shape[0], LANES), 1) % HEAD_DIM) < HALF
    for j in range(t.shape[1] // LANES):
        x = t[:, j * LANES:(j + 1) * LANES]
        partner = jnp.where(first, pltpu.roll(x, LANES - HALF, 1), pltpu.roll(x, HALF, 1))
        outs.append(x * cos + partner * sin)
    return outs[0] if len(outs) == 1 else jnp.concatenate(outs, axis=1)


def _proj_kernel(chunk, sample, x_ref, g_ref, w_ref, cos_ref, sin_ref, ws_ref, bs_ref, gv_ref, *outs):
    if sample:
        q_out, k_out, v_out, qi_out, ki_out, wi_out, sgb_out, mixa_out, qc_out, sgc_out, vn_out = outs
    else:
        (kt_out, vt_out, kit_out, kbf_out, vtb_out, ki2_out, qs_out, qis_out, wt_out,
         sgb_out, mixa_out, qc_out, sgc_out) = outs
    tm = x_ref.shape[0]
    x = x_ref[...]
    ms = jnp.mean(x * x, axis=-1, keepdims=True)
    h = (x * lax.rsqrt(ms + EPS) * g_ref[...]).astype(BF16)
    cos = cos_ref[...]
    sin = sin_ref[...]

    def seg(a, b):
        return _dot(h, w_ref[:, a:b])

    q = _rope(seg(OFF_Q, OFF_K), cos, sin)
    k = _rope(seg(OFF_K, OFF_V), cos, sin)
    v = seg(OFF_V, OFF_GB)
    qi = _rope(seg(OFF_QI, OFF_QC), cos, sin)
    ki2 = _rope(seg(OFF_KI2, OFF_WI), cos, sin)
    wi = seg(OFF_WI, W_TOTAL) * IDX_W_SCALE
    qc = seg(OFF_QC, OFF_GC) * ATTN_SCALE
    sgb_out[...] = _silu(seg(OFF_GB, OFF_QI)).astype(BF16)
    sgc_out[...] = _silu(seg(OFF_GC, OFF_KI2)).astype(BF16)
    qc_out[...] = qc.astype(qc_out.dtype)
    if sample:
        q_out[...] = q
        k_out[...] = k
        v_out[...] = v
        qi_out[...] = qi
        ki_out[...] = ki2[:, :IDX_DIM]
        wi_out[...] = wi[:, :IDX_HEADS]
    else:
        qs_out[...] = (q * (ATTN_SCALE * LOG2E)).astype(BF16)
        qis_out[...] = qi.astype(BF16)
        kt_out[0] = k.T
        kbf_out[...] = k.astype(BF16)
        vt = v.T
        vt_out[0] = vt
        vtb = vt.astype(BF16)
        ones = jnp.ones((V_ROWS - HEAD_DIM, KC), BF16)
        for j in range(tm // KC):
            for hd in range(B_HEADS):
                vtb_out[j, hd * V_ROWS:hd * V_ROWS + HEAD_DIM, :] = (
                    vtb[hd * HEAD_DIM:(hd + 1) * HEAD_DIM, j * KC:(j + 1) * KC])
                vtb_out[j, hd * V_ROWS + HEAD_DIM:(hd + 1) * V_ROWS, :] = ones
        kit_out[0] = ki2.T[:IDX_DIM, :]
        ki2_out[...] = ki2.astype(BF16)
        wt_out[...] = wi.T[:IDX_HEADS, :]

    u = seg(OFF_U, OFF_VA)
    va = seg(OFF_VA, OFF_GA)
    ga = seg(OFF_GA, OFF_Q)
    grp_r = _iota((A_WIDTH, A_WIDTH), 0) // HEAD_DIM
    grp_c = _iota((A_WIDTH, A_WIDTH), 1) // HEAD_DIM
    gmat = jnp.where(grp_r == grp_c, 1.0 / HEAD_DIM, 0.0).astype(BF16)

    def group_mean(t):
        hi = t.astype(BF16)
        lo = (t - hi.astype(F32)).astype(BF16)
        return _dot(hi, gmat) + _dot(lo, gmat)

    mu = group_mean(va)
    d = va - mu
    var = group_mean(d * d)
    vn = d * lax.rsqrt(var + EPS) * gv_ref[...]
    if sample:
        vn_out[...] = vn
    vnb = vn.astype(BF16)
    row = _iota((CHUNK, CHUNK), 0)
    col = _iota((CHUNK, CHUNK), 1)
    causal = (row >= col) & ((row // chunk) == (col // chunk))
    lane_grp = _iota((CHUNK, A_WIDTH), 1) // HEAD_DIM
    wms = [jnp.where(causal, ws_ref[g], 0.0).astype(BF16) for g in range(A_GROUPS)]
    mixes = []
    for j in range(tm // CHUNK):
        vc = vnb[j * CHUNK:(j + 1) * CHUNK, :]
        m = bs_ref[...]
        for g in range(A_GROUPS):
            m = m + jnp.where(lane_grp == g, _dot(wms[g], vc), 0.0)
        mixes.append(m)
    mix = jnp.concatenate(mixes, axis=0)
    mixa_out[...] = (_silu(ga) * u * mix).astype(BF16)


def _proj(x, g_norm, w_r, cos_t, sin_t, ws, bs, g_v, chunk, seq_blocks, sample):
    n = x.shape[0]
    nt = n // TM
    nb = nt // seq_blocks
    row = lambda w: pl.BlockSpec((TM, w), lambda i: (i, 0))
    const2 = lambda a: pl.BlockSpec(a.shape, lambda i: (0, 0))
    sds = jax.ShapeDtypeStruct
    in_specs = [
        row(D_MODEL), const2(g_norm), const2(w_r),
        pl.BlockSpec((TM, LANES), lambda i: (i % seq_blocks, 0)),
        pl.BlockSpec((TM, LANES), lambda i: (i % seq_blocks, 0)),
        pl.BlockSpec(ws.shape, lambda i: (0, 0, 0)), const2(bs), const2(g_v),
    ]
    if sample:
        outs = [
            (sds((n, B_WIDTH), F32), row(B_WIDTH)),
            (sds((n, B_WIDTH), F32), row(B_WIDTH)),
            (sds((n, B_WIDTH), F32), row(B_WIDTH)),
            (sds((n, IDX_HEADS * IDX_DIM), F32), row(IDX_HEADS * IDX_DIM)),
            (sds((n, IDX_DIM), F32), row(IDX_DIM)),
            (sds((n, IDX_HEADS), F32), row(IDX_HEADS)),
            (sds((n, B_WIDTH), BF16), row(B_WIDTH)),
            (sds((n, A_WIDTH), BF16), row(A_WIDTH)),
            (sds((n, C_WIDTH), F32), row(C_WIDTH)),
            (sds((n, C_WIDTH), BF16), row(C_WIDTH)),
            (sds((n, A_WIDTH), F32), row(A_WIDTH)),
        ]
    else:
        seq = seq_blocks * TM
        tspec = lambda w: pl.BlockSpec((1, w, TM), lambda i: (i // seq_blocks, 0, i % seq_blocks))
        outs = [
            (sds((nb, B_WIDTH, seq), F32), tspec(B_WIDTH)),
            (sds((nb, B_WIDTH, seq), F32), tspec(B_WIDTH)),
            (sds((nb, IDX_DIM, seq), F32), tspec(IDX_DIM)),
            (sds((n, B_WIDTH), BF16), row(B_WIDTH)),
            (sds((n // KC, B_HEADS * V_ROWS, KC), BF16),
             pl.BlockSpec((TM // KC, B_HEADS * V_ROWS, KC), lambda i: (i, 0, 0))),
            (sds((n, LANES), BF16), row(LANES)),
            (sds((n, B_WIDTH), BF16), row(B_WIDTH)),
            (sds((n, IDX_HEADS * IDX_DIM), BF16), row(IDX_HEADS * IDX_DIM)),
            (sds((IDX_HEADS, n), F32), pl.BlockSpec((IDX_HEADS, TM), lambda i: (0, i))),
            (sds((n, B_WIDTH), BF16), row(B_WIDTH)),
            (sds((n, A_WIDTH), BF16), row(A_WIDTH)),
            (sds((n, C_WIDTH), BF16), row(C_WIDTH)),
            (sds((n, C_WIDTH), BF16), row(C_WIDTH)),
        ]
    return pl.pallas_call(
        functools.partial(_proj_kernel, chunk, sample),
        grid=(nt,), in_specs=in_specs, out_specs=[o[1] for o in outs], out_shape=[o[0] for o in outs],
        compiler_params=_cparams(("parallel",)), name="proj",
    )(x, g_norm, w_r, cos_t, sin_t, ws, bs, g_v)


def _memkv_kernel(m_ref, g_ref, w_ref, mkt_out, mvt_out):
    x = m_ref[...]
    ms = jnp.mean(x * x, axis=-1, keepdims=True)
    h = (x * lax.rsqrt(ms + EPS) * g_ref[...]).astype(BF16)
    kv = _dot(h, w_ref[...])
    mkt_out[0] = kv[:, :C_WIDTH].T
    mvt_out[0] = kv[:, C_WIDTH:].T


def _memkv(mem, g_mem, w_kv):
    n = mem.shape[0]
    nb = n // N_MEM
    return pl.pallas_call(
        _memkv_kernel, grid=(nb,),
        in_specs=[pl.BlockSpec((N_MEM, D_MODEL), lambda i: (i, 0)),
                  pl.BlockSpec(g_mem.shape, lambda i: (0, 0)),
                  pl.BlockSpec(w_kv.shape, lambda i: (0, 0))],
        out_specs=[pl.BlockSpec((1, C_WIDTH, N_MEM), lambda i: (i, 0, 0))] * 2,
        out_shape=[jax.ShapeDtypeStruct((nb, C_WIDTH, N_MEM), F32)] * 2,
        compiler_params=_cparams(("parallel",)), name="memkv",
    )(mem, g_mem, w_kv)


def _memattn_kernel(q_ref, mkt_ref, mvt_ref, o_ref):
    q = q_ref[0]
    mkt = mkt_ref[0].astype(BF16)
    mvt = mvt_ref[0].astype(BF16)
    t = q.shape[0]
    lane_head = _iota((t, C_WIDTH), 1) // HEAD_DIM
    out = jnp.zeros((t, C_WIDTH), F32)
    for hd in range(C_HEADS):
        qh = jnp.where(lane_head == hd, q, 0.0).astype(BF16)
        s = _dot(qh, mkt)
        p = jnp.exp(s - jnp.max(s, axis=-1, keepdims=True))
        o = _dot_nt(p.astype(BF16), mvt) / jnp.sum(p, axis=-1, keepdims=True)
        out = out + jnp.where(lane_head == hd, o, 0.0)
    o_ref[0] = out.astype(o_ref.dtype)


def _memattn(q3, mkt3, mvt3, tiles_per_mem):
    g, t, _ = q3.shape
    mem_spec = pl.BlockSpec((1, C_WIDTH, N_MEM), lambda i: (i // tiles_per_mem, 0, 0))
    return pl.pallas_call(
        _memattn_kernel, grid=(g,),
        in_specs=[pl.BlockSpec((1, t, C_WIDTH), lambda i: (i, 0, 0)), mem_spec, mem_spec],
        out_specs=pl.BlockSpec((1, t, C_WIDTH), lambda i: (i, 0, 0)),
        out_shape=jax.ShapeDtypeStruct((g, t, C_WIDTH), BF16),
        compiler_params=_cparams(("parallel",)), name="memattn",
    )(q3, mkt3, mvt3)


def _memattn_sample_kernel(q_ref, mkt_ref, mvt_ref, o_ref):
    t = q_ref.shape[1]
    mask8 = (_iota((SUBLANES, C_WIDTH), 1) // HEAD_DIM) == _iota((SUBLANES, C_WIDTH), 0)
    for g in range(q_ref.shape[0]):
        q4 = q_ref[g]
        qbd = jnp.concatenate(
            [jnp.where(mask8, jnp.broadcast_to(q4[i:i + 1, :], (SUBLANES, C_WIDTH)), 0.0) for i in range(t)], axis=0)
        s = _dot(qbd.astype(BF16), mkt_ref[g].astype(BF16))
        p = jnp.exp(s - jnp.max(s, axis=-1, keepdims=True))
        o = _dot_nt(p.astype(BF16), mvt_ref[g].astype(BF16)) / jnp.sum(p, axis=-1, keepdims=True)
        for i in range(t):
            slab = jnp.where(mask8, o[i * SUBLANES:(i + 1) * SUBLANES, :], 0.0)
            o_ref[g, i * SUBLANES:(i + 1) * SUBLANES, :] = jnp.broadcast_to(
                jnp.sum(slab, axis=0, keepdims=True), (SUBLANES, C_WIDTH))


def _memattn_sample(q3, mkt3, mvt3):
    n, t, _ = q3.shape
    g = min(MEMATTN_SEQS_PER_STEP, n)
    mem_spec = pl.BlockSpec((g, C_WIDTH, N_MEM), lambda i: (i, 0, 0))
    return pl.pallas_call(
        _memattn_sample_kernel, grid=(n // g,),
        in_specs=[pl.BlockSpec((g, t, C_WIDTH), lambda i: (i, 0, 0)), mem_spec, mem_spec],
        out_specs=pl.BlockSpec((g, t * SUBLANES, C_WIDTH), lambda i: (i, 0, 0)),
        out_shape=jax.ShapeDtypeStruct((n, t * SUBLANES, C_WIDTH), F32),
        compiler_params=_cparams(("parallel",)), name="memattn_sample",
    )(q3, mkt3, mvt3)


def _dsa_prompt_kernel(topk, qs_ref, qis_ref, wt_ref, ki2_ref, kbf_ref, vt_ref, o_ref,
                       sc_ref, acc_ref, m_ref, s_ref, sb_ref, qz_ref, qiz_ref):
    i = pl.program_id(1)
    nk = i + 1
    npair = (nk + 1) // 2
    kth = jnp.float32(topk)

    half_id = _iota((QB, LANES), 1) // HEAD_DIM
    for hd in range(B_HEADS):
        lanes = slice((hd // 2) * LANES, (hd // 2 + 1) * LANES)
        keep = half_id == hd % 2
        qz_ref[hd] = jnp.where(keep, qs_ref[:, lanes].astype(F32), 0.0).astype(BF16)
        qiz_ref[hd] = jnp.where(keep, qis_ref[:, lanes].astype(F32), 0.0).astype(BF16)

    def sum_rows(x):
        return jnp.sum(x.reshape(KC // SUBLANES, SUBLANES, QB), axis=0)

    def max_rows(x):
        return jnp.max(x.reshape(KC // SUBLANES, SUBLANES, QB), axis=0)

    def score_pair(j, _):
        for c in (2 * j, 2 * j + 1):
            keys = ki2_ref[0, c]
            acc = jnp.zeros((KC, QB), F32)
            for hd in range(IDX_HEADS):
                y = _dot_nt(keys, qiz_ref[hd])
                acc = acc + jnp.maximum(y, 0.0) * wt_ref[hd:hd + 1, :]
            kpos = c * KC + _iota((KC, QB), 0)
            qpos = i * QB + _iota((KC, QB), 1)
            masked = jnp.where(kpos <= qpos, acc, -jnp.inf)
            sc_ref[c] = masked
            sb_ref[c] = masked.astype(BF16)
        return 0

    lax.fori_loop(0, npair, score_pair, 0)

    def count(pred):
        def body(j, cnt):
            return (cnt + sum_rows(pred(sc_ref[2 * j]).astype(F32))
                    + sum_rows(pred(sc_ref[2 * j + 1]).astype(F32)))
        return jnp.sum(lax.fori_loop(0, npair, body, jnp.zeros((SUBLANES, QB), F32)), axis=0, keepdims=True)

    def count_ge_packed(t):
        tb = t.astype(BF16)
        def slab_count(c):
            ones = jnp.where(sb_ref[c] >= tb, jnp.ones((), BF16), jnp.zeros((), BF16))
            acc = ones[0:PACKED_ROWS]
            for r in range(1, KC // PACKED_ROWS):
                acc = acc + ones[r * PACKED_ROWS:(r + 1) * PACKED_ROWS]
            return acc

        def body(j, cnt):
            return cnt + slab_count(2 * j) + slab_count(2 * j + 1)

        cnt16 = lax.fori_loop(0, npair, body, jnp.zeros((PACKED_ROWS, QB), BF16))
        return jnp.sum(cnt16.astype(F32), axis=0, keepdims=True)

    def write_ge(thr):
        def body(j, _):
            sc_ref[2 * j] = jnp.where(sc_ref[2 * j] >= thr, 0.0, NEG)
            sc_ref[2 * j + 1] = jnp.where(sc_ref[2 * j + 1] >= thr, 0.0, NEG)
            return 0
        lax.fori_loop(0, npair, body, 0)

    @pl.when(nk * KC <= topk)
    def _():
        write_ge(jnp.full((1, QB), -FLT_MAX, F32))

    @pl.when(nk * KC > topk)
    def _():
        def bf16_key_to_float_key(k16):
            return k16 * jnp.int32(2 ** 16) + jnp.where(k16 < 0, jnp.int32(2 ** 16 - 1), jnp.int32(0))

        tau_k16, _ = _bitwise_kth(
            lambda k16: count_ge_packed(_key_to_float(bf16_key_to_float_key(k16))), kth, (1, QB), 16)
        base = bf16_key_to_float_key(tau_k16) - jnp.int32(2 ** 16)

        def fine_bit(it, carry):
            off, cnt_off = carry
            cand = off + lax.shift_left(jnp.int32(1), 16 - it)
            cnt = count(lambda x: x >= _key_to_float(base + cand))
            ok = cnt >= kth
            return jnp.where(ok, cand, off), jnp.where(ok, cnt, cnt_off)

        off, cnt_thr = lax.fori_loop(0, 17, fine_bit,
                                     (jnp.zeros((1, QB), jnp.int32), jnp.full((1, QB), 2.0 ** 24, F32)))
        thr = _key_to_float(base + off)
        few = (i * QB + 1 + _iota((1, QB), 1)).astype(F32) <= kth
        thr = jnp.where(few, -FLT_MAX, thr)
        cnt_thr = jnp.where(few, kth, cnt_thr)
        has_tie = jnp.max(cnt_thr) > kth

        @pl.when(jnp.logical_not(has_tie))
        def _():
            write_ge(thr)

        @pl.when(has_tie)
        def _():
            need = kth - count(lambda x: x > thr)
            ltri = (_iota((KC, KC), 1) < _iota((KC, KC), 0)).astype(BF16)

            def body(c, carry):
                x = sc_ref[c]
                eq = x == thr
                before = _dot(ltri, eq.astype(BF16)) + carry
                sel = (x > thr) | (eq & (before < need))
                sc_ref[c] = jnp.where(sel, 0.0, NEG)
                return carry + jnp.sum(sum_rows(eq.astype(F32)), axis=0, keepdims=True)

            lax.fori_loop(0, nk, body, jnp.zeros((1, QB), F32))

    m_ref[...] = jnp.full((B_HEADS, QB), NEG, F32)
    acc_ref[...] = jnp.zeros((B_HEADS * V_ROWS, QB), F32)

    def biased_scores(c, hd):
        pair = hd // 2
        kc = kbf_ref[0, c, :, pair * LANES:(pair + 1) * LANES]
        return _dot_nt(kc, qz_ref[hd]) + sc_ref[c]

    for hd in range(B_HEADS):
        s_ref[hd] = biased_scores(0, hd)

    def attend(c, _):
        nxt = jnp.minimum(c + 1, nk - 1)
        ahead = [biased_scores(nxt, hd) for hd in range(QK_LEAD)]
        for hd in range(B_HEADS):
            rows = slice(hd * V_ROWS, (hd + 1) * V_ROWS)
            s = s_ref[hd]
            m_old = m_ref[hd:hd + 1, :]
            m_new = jnp.maximum(m_old, jnp.max(max_rows(s), axis=0, keepdims=True))
            alpha = jnp.exp2(m_old - m_new)
            p = jnp.exp2(s - m_new)
            acc_ref[rows, :] = alpha * acc_ref[rows, :] + _dot(vt_ref[0, c, rows, :], p.astype(BF16))
            m_ref[hd:hd + 1, :] = m_new
            s_ref[hd] = ahead[hd]
            if hd + QK_LEAD < B_HEADS:
                ahead.append(biased_scores(nxt, hd + QK_LEAD))
        return 0

    lax.fori_loop(0, nk, attend, 0)
    outs = []
    for hd in range(B_HEADS):
        base = hd * V_ROWS
        outs.append(acc_ref[base:base + HEAD_DIM, :] / acc_ref[base + HEAD_DIM:base + HEAD_DIM + 1, :])
    o_ref[...] = jnp.concatenate(outs, axis=0).T.astype(o_ref.dtype)


def _dsa_prompt(qs, qis, wt, ki2, kbf, vtb, batch, seq):
    nq = seq // QB
    nc = seq // KC
    topk = min(TOPK_MAX, seq // 4)
    n = batch * seq
    qspec = pl.BlockSpec((QB, B_WIDTH), lambda b, i: (b * nq + i, 0))
    return pl.pallas_call(
        functools.partial(_dsa_prompt_kernel, topk),
        grid=(batch, nq),
        in_specs=[qspec, qspec,
                  pl.BlockSpec((IDX_HEADS, QB), lambda b, i: (0, b * nq + i)),
                  pl.BlockSpec((1, nc, KC, LANES), lambda b, i: (b, 0, 0, 0)),
                  pl.BlockSpec((1, nc, KC, B_WIDTH), lambda b, i: (b, 0, 0, 0)),
                  pl.BlockSpec((1, nc, B_HEADS * V_ROWS, KC), lambda b, i: (b, 0, 0, 0))],
        out_specs=pl.BlockSpec((QB, B_WIDTH), lambda b, i: (b * nq + i, 0)),
        out_shape=jax.ShapeDtypeStruct((n, B_WIDTH), BF16),
        scratch_shapes=[pltpu.VMEM((nc, KC, QB), F32), pltpu.VMEM((B_HEADS * V_ROWS, QB), F32),
                        pltpu.VMEM((B_HEADS, QB), F32), pltpu.VMEM((B_HEADS, KC, QB), F32),
                        pltpu.VMEM((nc, KC, QB), BF16),
                        pltpu.VMEM((B_HEADS, QB, LANES), BF16), pltpu.VMEM((IDX_HEADS, QB, LANES), BF16)],
        compiler_params=_cparams(("parallel", "arbitrary")), name="dsa_prompt",
    )(qs, qis, wt, ki2.reshape(batch, nc, KC, LANES), kbf.reshape(batch, nc, KC, B_WIDTH),
      vtb.reshape(batch, nc, B_HEADS * V_ROWS, KC))


def _page_specs(block, n_pages_per_step):
    def make(r):
        return pl.BlockSpec(block, lambda b, j, pt: (pt[b, j * n_pages_per_step + r], 0, 0))
    return [make(r) for r in range(n_pages_per_step)]


def _idx_sample_kernel(t_new, pps, pt_ref, qi_ref, w_ref, kin_ref, *rest):
    pages = rest[:pps]
    sc_out, scn_out = rest[pps:]
    j = pl.program_id(1)
    qi = qi_ref[0]
    w = w_ref[0]
    qib = qi.astype(BF16)

    def token_scores(y):
        z = jnp.maximum(y, 0.0) * w
        return jnp.sum(z.reshape(t_new, IDX_HEADS, y.shape[1]), axis=1)

    for r in range(pps):
        y = _dot(qib, pages[r][0].astype(BF16))
        sc_out[0, :, r * PAGE_SIZE:(r + 1) * PAGE_SIZE] = token_scores(y)

    @pl.when(j == 0)
    def _():
        qf = qib.astype(F32)
        lane = _iota((t_new, LANES), 1)
        tok = _iota((t_new, LANES), 0)
        new = jnp.full((t_new, LANES), -jnp.inf, F32)
        for tk in range(t_new):
            kr = kin_ref[0, tk:tk + 1, :].astype(BF16).astype(F32)
            col = token_scores(jnp.sum(qf * kr, axis=-1, keepdims=True))
            new = jnp.where((lane == tk) & (tok >= tk), col, new)
        scn_out[0] = new


def _idx_sample(page_table, qi32, w32, ki_new, cache_idx_kt):
    db, rows, _ = qi32.shape
    t_new = rows // IDX_HEADS
    n_pages = page_table.shape[1]
    pps = min(IDX_PAGES_PER_STEP, n_pages)
    steps = n_pages // pps
    span = pps * PAGE_SIZE
    grid_spec = pltpu.PrefetchScalarGridSpec(
        num_scalar_prefetch=1, grid=(db, steps),
        in_specs=[pl.BlockSpec((1, rows, IDX_DIM), lambda b, j, pt: (b, 0, 0)),
                  pl.BlockSpec((1, rows, 1), lambda b, j, pt: (b, 0, 0)),
                  pl.BlockSpec((1, t_new, IDX_DIM), lambda b, j, pt: (b, 0, 0))]
        + _page_specs((1, IDX_DIM, PAGE_SIZE), pps),
        out_specs=[pl.BlockSpec((1, t_new, span), lambda b, j, pt: (b, 0, j)),
                   pl.BlockSpec((1, t_new, LANES), lambda b, j, pt: (b, 0, 0))])
    return pl.pallas_call(
        functools.partial(_idx_sample_kernel, t_new, pps), grid_spec=grid_spec,
        out_shape=[jax.ShapeDtypeStruct((db, t_new, n_pages * PAGE_SIZE), F32),
                   jax.ShapeDtypeStruct((db, t_new, LANES), F32)],
        compiler_params=_cparams(("parallel", "arbitrary")), name="idx_sample",
    )(page_table, qi32, w32, ki_new, *([cache_idx_kt] * pps))


def _thr_sample_kernel(topk, sc_ref, scn_ref, b_out, bn_out):
    kth = jnp.float32(topk)
    n_chunks = sc_ref.shape[1] // LANES

    def count(pred):
        return (jnp.sum(pred(sc_ref[...]).astype(F32), axis=-1, keepdims=True)
                + jnp.sum(pred(scn_ref[...]).astype(F32), axis=-1, keepdims=True))

    thr, cnt_thr = _kth_largest(lambda t: count(lambda x: x >= t), kth, (THR_ROWS, 1))
    has_tie = jnp.max(cnt_thr) > kth

    @pl.when(jnp.logical_not(has_tie))
    def _():
        b_out[...] = jnp.where(sc_ref[...] >= thr, 0.0, NEG)
        bn_out[...] = jnp.where(scn_ref[...] >= thr, 0.0, NEG)

    @pl.when(has_tie)
    def _():
        need = kth - count(lambda x: x > thr)
        utri = (_iota((LANES, LANES), 0) < _iota((LANES, LANES), 1)).astype(BF16)
        carry = jnp.zeros((THR_ROWS, 1), F32)
        for c in range(n_chunks + 1):
            x = sc_ref[:, c * LANES:(c + 1) * LANES] if c < n_chunks else scn_ref[...]
            eq = x == thr
            before = _dot(eq.astype(BF16), utri) + carry
            sel = (x > thr) | (eq & (before < need))
            bias = jnp.where(sel, 0.0, NEG)
            if c < n_chunks:
                b_out[:, c * LANES:(c + 1) * LANES] = bias
            else:
                bn_out[...] = bias
            carry = carry + jnp.sum(eq.astype(F32), axis=-1, keepdims=True)


def _thr_sample(sc, scn, topk):
    n, width = sc.shape
    return pl.pallas_call(
        functools.partial(_thr_sample_kernel, topk), grid=(n // THR_ROWS,),
        in_specs=[pl.BlockSpec((THR_ROWS, width), lambda i: (i, 0)),
                  pl.BlockSpec((THR_ROWS, LANES), lambda i: (i, 0))],
        out_specs=[pl.BlockSpec((THR_ROWS, width), lambda i: (i, 0)),
                   pl.BlockSpec((THR_ROWS, LANES), lambda i: (i, 0))],
        out_shape=[jax.ShapeDtypeStruct((n, width), F32), jax.ShapeDtypeStruct((n, LANES), F32)],
        compiler_params=_cparams(("parallel",)), name="thr_sample",
    )(sc, scn)


def _attn_sample_kernel(t_new, pps, pt_ref, q_ref, bias_ref, biasn_ref, kn_ref, vn_ref, *rest):
    kpages = rest[:pps]
    vpages = rest[pps:2 * pps]
    o_ref, m_ref, l_ref, acc_ref, qbd_ref = rest[2 * pps:]
    j = pl.program_id(1)
    rows = t_new * B_HEADS
    mask8 = (_iota((SUBLANES, B_WIDTH), 1) // HEAD_DIM) == _iota((SUBLANES, B_WIDTH), 0)

    def per_token(ref, width):
        return jnp.concatenate(
            [jnp.broadcast_to(ref[0, i:i + 1, :], (B_HEADS, width)) for i in range(t_new)], axis=0)

    @pl.when(j == 0)
    def _():
        m_ref[...] = jnp.full((rows, 1), NEG, F32)
        l_ref[...] = jnp.zeros((rows, 1), F32)
        acc_ref[...] = jnp.zeros((rows, B_WIDTH), F32)
        qbd_ref[...] = jnp.concatenate(
            [jnp.where(mask8, jnp.broadcast_to(q_ref[0, i:i + 1, :] * ATTN_SCALE, (B_HEADS, B_WIDTH)), 0.0)
             for i in range(t_new)], axis=0).astype(BF16)

    qbd = qbd_ref[...]
    span = pps * PAGE_SIZE
    s = jnp.concatenate([_dot(qbd, kpages[r][0].astype(BF16)) for r in range(pps)], axis=1)
    s = s + per_token(bias_ref, span)
    m_old = m_ref[...]
    m_new = jnp.maximum(m_old, jnp.max(s, axis=-1, keepdims=True))
    alpha = jnp.exp(m_old - m_new)
    p = jnp.exp(s - m_new)
    l_ref[...] = alpha * l_ref[...] + jnp.sum(p, axis=-1, keepdims=True)
    pb = p.astype(BF16)
    pv = jnp.zeros((rows, B_WIDTH), F32)
    for r in range(pps):
        pv = pv + _dot_nt(pb[:, r * PAGE_SIZE:(r + 1) * PAGE_SIZE], vpages[r][0].astype(BF16))
    acc_ref[...] = alpha * acc_ref[...] + pv
    m_ref[...] = m_new

    @pl.when(j == pl.num_programs(1) - 1)
    def _():
        qf = qbd.astype(F32)
        bn = per_token(biasn_ref, LANES)
        cols = []
        for tk in range(t_new):
            kr = kn_ref[0, tk:tk + 1, :].astype(BF16).astype(F32)
            cols.append(jnp.sum(qf * kr, axis=-1, keepdims=True) + bn[:, tk:tk + 1])
        m_old = m_ref[...]
        m_new = m_old
        for cval in cols:
            m_new = jnp.maximum(m_new, cval)
        alpha = jnp.exp(m_old - m_new)
        l = alpha * l_ref[...]
        acc = alpha * acc_ref[...]
        for tk in range(t_new):
            pk = jnp.exp(cols[tk] - m_new)
            l = l + pk
            acc = acc + pk.astype(BF16).astype(F32) * vn_ref[0, tk:tk + 1, :].astype(BF16).astype(F32)
        o = acc / l
        for i in range(t_new):
            slab = jnp.where(mask8, o[i * B_HEADS:(i + 1) * B_HEADS, :], 0.0)
            o_ref[0, i * B_HEADS:(i + 1) * B_HEADS, :] = jnp.broadcast_to(
                jnp.sum(slab, axis=0, keepdims=True), (B_HEADS, B_WIDTH))


def _attn_sample(page_table, q3, bias3, biasn3, k_new, v_new, cache_kt, cache_vt):
    db, t_new, _ = q3.shape
    n_pages = page_table.shape[1]
    pps = min(ATTN_PAGES_PER_STEP, n_pages)
    steps = n_pages // pps
    span = pps * PAGE_SIZE
    rows = t_new * B_HEADS
    per_b = lambda w: pl.BlockSpec((1, t_new, w), lambda b, j, pt: (b, 0, 0))
    grid_spec = pltpu.PrefetchScalarGridSpec(
        num_scalar_prefetch=1, grid=(db, steps),
        in_specs=[per_b(B_WIDTH),
                  pl.BlockSpec((1, t_new, span), lambda b, j, pt: (b, 0, j)),
                  per_b(LANES), per_b(B_WIDTH), per_b(B_WIDTH)]
        + _page_specs((1, B_WIDTH, PAGE_SIZE), pps)
        + _page_specs((1, B_WIDTH, PAGE_SIZE), pps),
        out_specs=pl.BlockSpec((1, rows, B_WIDTH), lambda b, j, pt: (b, 0, 0)),
        scratch_shapes=[pltpu.VMEM((rows, 1), F32), pltpu.VMEM((rows, 1), F32),
                        pltpu.VMEM((rows, B_WIDTH), F32), pltpu.VMEM((rows, B_WIDTH), BF16)])
    return pl.pallas_call(
        functools.partial(_attn_sample_kernel, t_new, pps), grid_spec=grid_spec,
        out_shape=jax.ShapeDtypeStruct((db, rows, B_WIDTH), F32),
        compiler_params=_cparams(("parallel", "arbitrary")), name="attn_sample",
    )(page_table, q3, bias3, biasn3, k_new, v_new, *([cache_kt] * pps), *([cache_vt] * pps))


def _merge_kernel(x_ref, ma_ref, yb_ref, sgb_ref, yc_ref, sgc_ref, wo_ref, g_ref, y_out):
    mb = (yb_ref[...].astype(F32) * sgb_ref[...].astype(F32)).astype(BF16)
    mc = (yc_ref[...].astype(F32) * sgc_ref[...].astype(F32)).astype(BF16)
    o = (_dot(ma_ref[...], wo_ref[0:A_WIDTH, :])
         + _dot(mb, wo_ref[A_WIDTH:A_WIDTH + B_WIDTH, :])
         + _dot(mc, wo_ref[A_WIDTH + B_WIDTH:, :]))
    z = x_ref[...] + o
    ms = jnp.mean(z * z, axis=-1, keepdims=True)
    y_out[...] = z * lax.rsqrt(ms + EPS) * g_ref[...]


def _merge(x, mixa, yb, sgb, yc, sgc, wo, g_final):
    n = x.shape[0]
    tm = min(MERGE_TM, n)
    row = lambda w: pl.BlockSpec((tm, w), lambda i: (i, 0))
    return pl.pallas_call(
        _merge_kernel, grid=(n // tm,),
        in_specs=[row(D_MODEL), row(A_WIDTH), row(B_WIDTH), row(B_WIDTH), row(C_WIDTH), row(C_WIDTH),
                  pl.BlockSpec(wo.shape, lambda i: (0, 0)), pl.BlockSpec(g_final.shape, lambda i: (0, 0))],
        out_specs=row(D_MODEL),
        out_shape=jax.ShapeDtypeStruct((n, D_MODEL), F32),
        compiler_params=_cparams(("parallel",)), name="merge",
    )(x, mixa, yb, sgb, yc, sgc, wo, g_final)


def _rope_tables(pos):
    inv = ROPE_THETA ** (-jnp.arange(0, HEAD_DIM, 2, dtype=F32) / HEAD_DIM)
    ang = pos.astype(F32)[:, None] * inv[None, :]
    c = jnp.cos(ang)
    s = jnp.sin(ang)
    return jnp.concatenate([c, c, c, c], axis=1), jnp.concatenate([-s, s, -s, s], axis=1)


def _heads_last(t, heads):
    b, _, n = t.shape
    return t.reshape(b, heads, HEAD_DIM, n).transpose(0, 3, 1, 2)


def _channels_first(t):
    p, n, heads, d = t.shape
    return t.transpose(0, 2, 3, 1).reshape(p, heads * d, n)


def kernel(x_prompt, x_sample, mem_prompt, cache_k, cache_v, cache_idx_k, cache_mem_k, cache_mem_v,
           page_table, g_norm, w_in, w_spatial, b_spatial, g_v, w_mem_kv, g_mem, w_out, g_final):
    batch, seq, _ = x_prompt.shape
    db, t_new, _ = x_sample.shape
    n_pages = page_table.shape[1]
    past = n_pages * PAGE_SIZE

    ki_cols = w_in[:, 3328:3392]
    wi_cols = w_in[:, 3392:3400]
    w_r = jnp.concatenate(
        [w_in[:, :3328], w_in[:, 3400:3912], ki_cols, ki_cols, wi_cols,
         jnp.zeros((D_MODEL, LANES - IDX_HEADS), F32)], axis=1).astype(BF16)
    g_norm2 = g_norm.reshape(1, D_MODEL)
    g_v2 = g_v.reshape(1, A_WIDTH)
    bs_prompt = jnp.repeat(b_spatial.T, HEAD_DIM, axis=1)
    reps = CHUNK // t_new
    ws_sample = jnp.tile(w_spatial[:, :t_new, :t_new], (1, reps, reps))
    bs_sample = jnp.tile(jnp.repeat(b_spatial[:, :t_new].T, HEAD_DIM, axis=1), (reps, 1))
    wo_b = w_out.astype(BF16)
    g_final2 = g_final.reshape(1, D_MODEL)

    cos_p, sin_p = _rope_tables(jnp.arange(seq, dtype=jnp.int32))
    pos_s = past + jnp.arange(t_new, dtype=jnp.int32)
    cos_s, sin_s = _rope_tables(jnp.tile(pos_s, TM // t_new))

    xp = x_prompt.reshape(batch * seq, D_MODEL)
    (kt, vt, kit, kbf, vtb, ki2, qs, qis, wt, sgb, mixa, qc, sgc) = _proj(
        xp, g_norm2, w_r, cos_p, sin_p, w_spatial, bs_prompt, g_v2, CHUNK, seq // TM, False)
    mkt, mvt = _memkv(mem_prompt.reshape(batch * N_MEM, D_MODEL), g_mem.reshape(1, D_MODEL),
                      w_mem_kv.astype(BF16))
    yc = _memattn(qc.reshape(batch * seq // MEMATTN_TM, MEMATTN_TM, C_WIDTH), mkt, mvt,
                  seq // MEMATTN_TM).reshape(batch * seq, C_WIDTH)
    yb = _dsa_prompt(qs, qis, wt, ki2, kbf, vtb, batch, seq)
    y_prompt = _merge(xp, mixa, yb, sgb, yc, sgc, wo_b, g_final2).reshape(batch, seq, D_MODEL)

    ns = db * t_new
    xs = x_sample.reshape(ns, D_MODEL)
    (q_s, k_s, v_s, qi_s, ki_s, wi_s, sgb_s, mixa_s, qc_s, sgc_s, vn_s) = _proj(
        xs, g_norm2, w_r, cos_s, sin_s, ws_sample, bs_sample, g_v2, t_new, 1, True)
    sc, scn = _idx_sample(page_table, qi_s.reshape(db, t_new * IDX_HEADS, IDX_DIM),
                          wi_s.reshape(db, t_new * IDX_HEADS, 1), ki_s.reshape(db, t_new, IDX_DIM),
                          cache_idx_k.transpose(0, 2, 1))
    topk_s = min(TOPK_MAX, (past + t_new) // 4)
    bias, biasn = _thr_sample(sc.reshape(ns, past), scn.reshape(ns, LANES), topk_s)
    yb_pad = _attn_sample(page_table, q_s.reshape(db, t_new, B_WIDTH), bias.reshape(db, t_new, past),
                          biasn.reshape(db, t_new, LANES), k_s.reshape(db, t_new, B_WIDTH),
                          v_s.reshape(db, t_new, B_WIDTH), _channels_first(cache_k), _channels_first(cache_v))
    yb_s = yb_pad[:, ::B_HEADS, :].reshape(ns, B_WIDTH)
    yc_pad = _memattn_sample(qc_s.reshape(db, t_new, C_WIDTH),
                             _channels_first(cache_mem_k), _channels_first(cache_mem_v))
    yc_s = yc_pad[:, ::SUBLANES, :].reshape(ns, C_WIDTH)
    y_sample = _merge(xs, mixa_s, yb_s, sgb_s, yc_s, sgc_s, wo_b, g_final2).reshape(db, t_new, D_MODEL)

    return (y_prompt, y_sample,
            _heads_last(kt, B_HEADS), _heads_last(vt, B_HEADS), kit.transpose(0, 2, 1),
            _heads_last(mkt, C_HEADS), _heads_last(mvt, C_HEADS),
            k_s.reshape(db, t_new, B_HEADS, HEAD_DIM), v_s.reshape(db, t_new, B_HEADS, HEAD_DIM),
            ki_s.reshape(db, t_new, IDX_DIM), vn_s.reshape(db, t_new, A_WIDTH))
```

```python
import functools

import jax
import jax.numpy as jnp
from jax import lax
from jax.experimental import pallas as pl
from jax.experimental.pallas import tpu as pltpu

F32 = jnp.float32
BF16 = jnp.bfloat16

HEAD_DIM = 64
HALF = HEAD_DIM // 2
D_MODEL = 1024
A_GROUPS = 4
A_WIDTH = A_GROUPS * HEAD_DIM
CHUNK = 128
B_HEADS = 8
B_WIDTH = B_HEADS * HEAD_DIM
IDX_HEADS = 8
IDX_DIM = 64
TOPK_MAX = 256
N_MEM = 256
C_HEADS = 4
C_WIDTH = C_HEADS * HEAD_DIM
PAGE_SIZE = 128
ROPE_THETA = 10000.0
EPS = 1e-6
IDX_W_SCALE = (IDX_HEADS * IDX_DIM) ** -0.5
ATTN_SCALE = HEAD_DIM ** -0.5
LOG2E = 1.4426950408889634

LANES = 128
SUBLANES = 8
NEG = -1e30
FLT_MAX = 3.4028234663852886e38
PACKED_ROWS = 16

TM = 512
MEMATTN_SEQS_PER_STEP = 16
MERGE_TM = 1024
QB = 256
KC = 256
V_ROWS = HEAD_DIM + PACKED_ROWS
QK_LEAD = 3
IDX_PAGES_PER_STEP = 64
ATTN_PAGES_PER_STEP = 32
THR_ROWS = 128
VMEM_LIMIT = 52 * 1024 * 1024

OFF_U, OFF_VA, OFF_GA = 0, 256, 512
OFF_Q, OFF_K, OFF_V, OFF_GB, OFF_QI = 768, 1280, 1792, 2304, 2816
OFF_QC, OFF_GC, OFF_KI2, OFF_WI, W_TOTAL = 3328, 3584, 3840, 3968, 4096


def _cparams(sem):
    return pltpu.CompilerParams(dimension_semantics=sem, vmem_limit_bytes=VMEM_LIMIT)


def _iota(shape, dim):
    return lax.broadcasted_iota(jnp.int32, shape, dim)


def _silu(x):
    return x / (1.0 + jnp.exp(-x))


def _dot(a, b):
    return jnp.dot(a, b, preferred_element_type=F32)


def _dot_nt(a, b):
    return lax.dot_general(a, b, (((1,), (1,)), ((), ())), preferred_element_type=F32)


def _key_to_float(k):
    bits = k ^ ((k >> 31) & jnp.int32(0x7FFFFFFF))
    return lax.bitcast_convert_type(bits, F32)


def _bitwise_kth(count_ge, kth, shape, bits):
    cnt0 = count_ge(jnp.zeros(shape, jnp.int32))
    ok0 = cnt0 >= kth
    key = jnp.where(ok0, jnp.int32(0), jnp.int32(-2 ** (bits - 1)))
    cnt_key = jnp.where(ok0, cnt0, jnp.float32(2 ** 24))

    def bit_body(it, carry):
        key, cnt_key = carry
        trial = key + lax.shift_left(jnp.int32(1), bits - 2 - it)
        cnt = count_ge(trial)
        ok = cnt >= kth
        return jnp.where(ok, trial, key), jnp.where(ok, cnt, cnt_key)

    return lax.fori_loop(0, bits - 1, bit_body, (key, cnt_key))


def _kth_largest(count_ge, kth, shape):
    key, cnt_key = _bitwise_kth(lambda k: count_ge(_key_to_float(k)), kth, shape, 32)
    return _key_to_float(key), cnt_key


def _rope(t, cos, sin):
    outs = []
    first = (_iota((t.shape[0], LANES), 1) % HEAD_DIM) < HALF
    for j in range(t.shape[1] // LANES):
        x = t[:, j * LANES:(j + 1) * LANES]
        partner = jnp.where(first, pltpu.roll(x, LANES - HALF, 1), pltpu.roll(x, HALF, 1))
        outs.append(x * cos + partner * sin)
    return outs[0] if len(outs) == 1 else jnp.concatenate(outs, axis=1)


def _proj_kernel(chunk, sample, x_ref, g_ref, w_ref, cos_ref, sin_ref, ws_ref, bs_ref, gv_ref, *outs):
    if sample:
        q_out, k_out, v_out, qi_out, ki_out, wi_out, sgb_out, mixa_out, qc_out, sgc_out, vn_out = outs
    else:
        (kt_out, vt_out, kit_out, kbf_out, vtb_out, ki2_out, qs_out, qis_out, wt_out,
         sgb_out, mixa_out, qc_out, sgc_out) = outs
    tm = x_ref.shape[0]
    x = x_ref[...]
    ms = jnp.mean(x * x, axis=-1, keepdims=True)
    h = (x * lax.rsqrt(ms + EPS) * g_ref[...]).astype(BF16)
    cos = cos_ref[...]
    sin = sin_ref[...]

    def seg(a, b):
        return _dot(h, w_ref[:, a:b])

    q = _rope(seg(OFF_Q, OFF_K), cos, sin)
    k = _rope(seg(OFF_K, OFF_V), cos, sin)
    v = seg(OFF_V, OFF_GB)
    qi = _rope(seg(OFF_QI, OFF_QC), cos, sin)
    ki2 = _rope(seg(OFF_KI2, OFF_WI), cos, sin)
    wi = seg(OFF_WI, W_TOTAL) * IDX_W_SCALE
    qc = seg(OFF_QC, OFF_GC) * ATTN_SCALE
    sgb_out[...] = _silu(seg(OFF_GB, OFF_QI)).astype(BF16)
    sgc_out[...] = _silu(seg(OFF_GC, OFF_KI2)).astype(BF16)
    qc_out[...] = qc.astype(qc_out.dtype)
    if sample:
        q_out[...] = q
        k_out[...] = k
        v_out[...] = v
        qi_out[...] = qi
        ki_out[...] = ki2[:, :IDX_DIM]
        wi_out[...] = wi[:, :IDX_HEADS]
    else:
        qs_out[...] = (q * (ATTN_SCALE * LOG2E)).astype(BF16)
        qis_out[...] = qi.astype(BF16)
        kt_out[0] = k.T
        kbf_out[...] = k.astype(BF16)
        vt = v.T
        vt_out[0] = vt
        vtb = vt.astype(BF16)
        ones = jnp.ones((V_ROWS - HEAD_DIM, KC), BF16)
        for j in range(tm // KC):
            for hd in range(B_HEADS):
                vtb_out[j, hd * V_ROWS:hd * V_ROWS + HEAD_DIM, :] = (
                    vtb[hd * HEAD_DIM:(hd + 1) * HEAD_DIM, j * KC:(j + 1) * KC])
                vtb_out[j, hd * V_ROWS + HEAD_DIM:(hd + 1) * V_ROWS, :] = ones
        kit_out[0] = ki2.T[:IDX_DIM, :]
        ki2_out[...] = ki2.astype(BF16)
        wt_out[...] = wi.T[:IDX_HEADS, :]

    u = seg(OFF_U, OFF_VA)
    va = seg(OFF_VA, OFF_GA)
    ga = seg(OFF_GA, OFF_Q)
    grp_r = _iota((A_WIDTH, A_WIDTH), 0) // HEAD_DIM
    grp_c = _iota((A_WIDTH, A_WIDTH), 1) // HEAD_DIM
    gmat = jnp.where(grp_r == grp_c, 1.0 / HEAD_DIM, 0.0).astype(BF16)

    def group_mean(t):
        hi = t.astype(BF16)
        lo = (t - hi.astype(F32)).astype(BF16)
        return _dot(hi, gmat) + _dot(lo, gmat)

    mu = group_mean(va)
    d = va - mu
    var = group_mean(d * d)
    vn = d * lax.rsqrt(var + EPS) * gv_ref[...]
    if sample:
        vn_out[...] = vn
    vnb = vn.astype(BF16)
    row = _iota((CHUNK, CHUNK), 0)
    col = _iota((CHUNK, CHUNK), 1)
    causal = (row >= col) & ((row // chunk) == (col // chunk))
    lane_grp = _iota((CHUNK, A_WIDTH), 1) // HEAD_DIM
    wms = [jnp.where(causal, ws_ref[g], 0.0).astype(BF16) for g in range(A_GROUPS)]
    mixes = []
    for j in range(tm // CHUNK):
        vc = vnb[j * CHUNK:(j + 1) * CHUNK, :]
        m = bs_ref[...]
        for g in range(A_GROUPS):
            m = m + jnp.where(lane_grp == g, _dot(wms[g], vc), 0.0)
        mixes.append(m)
    mix = jnp.concatenate(mixes, axis=0)
    mixa_out[...] = (_silu(ga) * u * mix).astype(BF16)


def _proj(x, g_norm, w_r, cos_t, sin_t, ws, bs, g_v, chunk, seq_blocks, sample):
    n = x.shape[0]
    nt = n // TM
    nb = nt // seq_blocks
    row = lambda w: pl.BlockSpec((TM, w), lambda i: (i, 0))
    const2 = lambda a: pl.BlockSpec(a.shape, lambda i: (0, 0))
    sds = jax.ShapeDtypeStruct
    in_specs = [
        row(D_MODEL), const2(g_norm), const2(w_r),
        pl.BlockSpec((TM, LANES), lambda i: (i % seq_blocks, 0)),
        pl.BlockSpec((TM, LANES), lambda i: (i % seq_blocks, 0)),
        pl.BlockSpec(ws.shape, lambda i: (0, 0, 0)), const2(bs), const2(g_v),
    ]
    if sample:
        outs = [
            (sds((n, B_WIDTH), F32), row(B_WIDTH)),
            (sds((n, B_WIDTH), F32), row(B_WIDTH)),
            (sds((n, B_WIDTH), F32), row(B_WIDTH)),
            (sds((n, IDX_HEADS * IDX_DIM), F32), row(IDX_HEADS * IDX_DIM)),
            (sds((n, IDX_DIM), F32), row(IDX_DIM)),
            (sds((n, IDX_HEADS), F32), row(IDX_HEADS)),
            (sds((n, B_WIDTH), BF16), row(B_WIDTH)),
            (sds((n, A_WIDTH), BF16), row(A_WIDTH)),
            (sds((n, C_WIDTH), F32), row(C_WIDTH)),
            (sds((n, C_WIDTH), BF16), row(C_WIDTH)),
            (sds((n, A_WIDTH), F32), row(A_WIDTH)),
        ]
    else:
        seq = seq_blocks * TM
        tspec = lambda w: pl.BlockSpec((1, w, TM), lambda i: (i // seq_blocks, 0, i % seq_blocks))
        outs = [
            (sds((nb, B_WIDTH, seq), F32), tspec(B_WIDTH)),
            (sds((nb, B_WIDTH, seq), F32), tspec(B_WIDTH)),
            (sds((nb, IDX_DIM, seq), F32), tspec(IDX_DIM)),
            (sds((n, B_WIDTH), BF16), row(B_WIDTH)),
            (sds((n // KC, B_HEADS * V_ROWS, KC), BF16),
             pl.BlockSpec((TM // KC, B_HEADS * V_ROWS, KC), lambda i: (i, 0, 0))),
            (sds((n, LANES), BF16), row(LANES)),
            (sds((n, B_WIDTH), BF16), row(B_WIDTH)),
            (sds((n, IDX_HEADS * IDX_DIM), BF16), row(IDX_HEADS * IDX_DIM)),
            (sds((IDX_HEADS, n), F32), pl.BlockSpec((IDX_HEADS, TM), lambda i: (0, i))),
            (sds((n, B_WIDTH), BF16), row(B_WIDTH)),
            (sds((n, A_WIDTH), BF16), row(A_WIDTH)),
            (sds((n, C_WIDTH), BF16), row(C_WIDTH)),
            (sds((n, C_WIDTH), BF16), row(C_WIDTH)),
        ]
    return pl.pallas_call(
        functools.partial(_proj_kernel, chunk, sample),
        grid=(nt,), in_specs=in_specs, out_specs=[o[1] for o in outs], out_shape=[o[0] for o in outs],
        compiler_params=_cparams(("parallel",)), name="proj",
    )(x, g_norm, w_r, cos_t, sin_t, ws, bs, g_v)


def _memkv_kernel(m_ref, g_ref, w_ref, mkt_out, mvt_out):
    x = m_ref[...]
    ms = jnp.mean(x * x, axis=-1, keepdims=True)
    h = (x * lax.rsqrt(ms + EPS) * g_ref[...]).astype(BF16)
    kv = _dot(h, w_ref[...])
    mkt_out[0] = kv[:, :C_WIDTH].T
    mvt_out[0] = kv[:, C_WIDTH:].T


def _memkv(mem, g_mem, w_kv):
    n = mem.shape[0]
    nb = n // N_MEM
    return pl.pallas_call(
        _memkv_kernel, grid=(nb,),
        in_specs=[pl.BlockSpec((N_MEM, D_MODEL), lambda i: (i, 0)),
                  pl.BlockSpec(g_mem.shape, lambda i: (0, 0)),
                  pl.BlockSpec(w_kv.shape, lambda i: (0, 0))],
        out_specs=[pl.BlockSpec((1, C_WIDTH, N_MEM), lambda i: (i, 0, 0))] * 2,
        out_shape=[jax.ShapeDtypeStruct((nb, C_WIDTH, N_MEM), F32)] * 2,
        compiler_params=_cparams(("parallel",)), name="memkv",
    )(mem, g_mem, w_kv)


def _memattn_rows(q, mkt, mvt):
    mkt = mkt.astype(BF16)
    mvt = mvt.astype(BF16)
    t = q.shape[0]
    lane_head = _iota((t, C_WIDTH), 1) // HEAD_DIM
    out = jnp.zeros((t, C_WIDTH), F32)
    for hd in range(C_HEADS):
        qh = jnp.where(lane_head == hd, q, 0.0).astype(BF16)
        s = _dot(qh, mkt)
        p = jnp.exp(s - jnp.max(s, axis=-1, keepdims=True))
        o = _dot_nt(p.astype(BF16), mvt) / jnp.sum(p, axis=-1, keepdims=True)
        out = out + jnp.where(lane_head == hd, o, 0.0)
    return out


def _memattn_sample_kernel(q_ref, mkt_ref, mvt_ref, o_ref):
    t = q_ref.shape[1]
    mask8 = (_iota((SUBLANES, C_WIDTH), 1) // HEAD_DIM) == _iota((SUBLANES, C_WIDTH), 0)
    for g in range(q_ref.shape[0]):
        q4 = q_ref[g]
        qbd = jnp.concatenate(
            [jnp.where(mask8, jnp.broadcast_to(q4[i:i + 1, :], (SUBLANES, C_WIDTH)), 0.0) for i in range(t)], axis=0)
        s = _dot(qbd.astype(BF16), mkt_ref[g].astype(BF16))
        p = jnp.exp(s - jnp.max(s, axis=-1, keepdims=True))
        o = _dot_nt(p.astype(BF16), mvt_ref[g].astype(BF16)) / jnp.sum(p, axis=-1, keepdims=True)
        for i in range(t):
            slab = jnp.where(mask8, o[i * SUBLANES:(i + 1) * SUBLANES, :], 0.0)
            o_ref[g, i * SUBLANES:(i + 1) * SUBLANES, :] = jnp.broadcast_to(
                jnp.sum(slab, axis=0, keepdims=True), (SUBLANES, C_WIDTH))


def _memattn_sample(q3, mkt3, mvt3):
    n, t, _ = q3.shape
    g = min(MEMATTN_SEQS_PER_STEP, n)
    mem_spec = pl.BlockSpec((g, C_WIDTH, N_MEM), lambda i: (i, 0, 0))
    return pl.pallas_call(
        _memattn_sample_kernel, grid=(n // g,),
        in_specs=[pl.BlockSpec((g, t, C_WIDTH), lambda i: (i, 0, 0)), mem_spec, mem_spec],
        out_specs=pl.BlockSpec((g, t * SUBLANES, C_WIDTH), lambda i: (i, 0, 0)),
        out_shape=jax.ShapeDtypeStruct((n, t * SUBLANES, C_WIDTH), F32),
        compiler_params=_cparams(("parallel",)), name="memattn_sample",
    )(q3, mkt3, mvt3)


def _dsa_prompt_kernel(topk, qs_ref, qis_ref, wt_ref, ki2_ref, kbf_ref, vt_ref, o_ref,
                       sc_ref, acc_ref, m_ref, s_ref, sb_ref, qz_ref, qiz_ref):
    i = pl.program_id(1)
    nk = i + 1
    npair = (nk + 1) // 2
    kth = jnp.float32(topk)

    half_id = _iota((QB, LANES), 1) // HEAD_DIM
    for hd in range(B_HEADS):
        lanes = slice((hd // 2) * LANES, (hd // 2 + 1) * LANES)
        keep = half_id == hd % 2
        qz_ref[hd] = jnp.where(keep, qs_ref[:, lanes].astype(F32), 0.0).astype(BF16)
        qiz_ref[hd] = jnp.where(keep, qis_ref[:, lanes].astype(F32), 0.0).astype(BF16)

    def sum_rows(x):
        return jnp.sum(x.reshape(KC // SUBLANES, SUBLANES, QB), axis=0)

    def max_rows(x):
        return jnp.max(x.reshape(KC // SUBLANES, SUBLANES, QB), axis=0)

    def score_pair(j, _):
        for c in (2 * j, 2 * j + 1):
            keys = ki2_ref[0, c]
            acc = jnp.zeros((KC, QB), F32)
            for hd in range(IDX_HEADS):
                y = _dot_nt(keys, qiz_ref[hd])
                acc = acc + jnp.maximum(y, 0.0) * wt_ref[hd:hd + 1, :]
            kpos = c * KC + _iota((KC, QB), 0)
            qpos = i * QB + _iota((KC, QB), 1)
            masked = jnp.where(kpos <= qpos, acc, -jnp.inf)
            sc_ref[c] = masked
            sb_ref[c] = masked.astype(BF16)
        return 0

    lax.fori_loop(0, npair, score_pair, 0)

    def count(pred):
        def body(j, cnt):
            return (cnt + sum_rows(pred(sc_ref[2 * j]).astype(F32))
                    + sum_rows(pred(sc_ref[2 * j + 1]).astype(F32)))
        return jnp.sum(lax.fori_loop(0, npair, body, jnp.zeros((SUBLANES, QB), F32)), axis=0, keepdims=True)

    def count_ge_packed(t):
        tb = t.astype(BF16)
        def slab_count(c):
            ones = jnp.where(sb_ref[c] >= tb, jnp.ones((), BF16), jnp.zeros((), BF16))
            acc = ones[0:PACKED_ROWS]
            for r in range(1, KC // PACKED_ROWS):
                acc = acc + ones[r * PACKED_ROWS:(r + 1) * PACKED_ROWS]
            return acc

        def body(j, cnt):
            return cnt + slab_count(2 * j) + slab_count(2 * j + 1)

        cnt16 = lax.fori_loop(0, npair, body, jnp.zeros((PACKED_ROWS, QB), BF16))
        return jnp.sum(cnt16.astype(F32), axis=0, keepdims=True)

    def write_ge(thr):
        def body(j, _):
            sc_ref[2 * j] = jnp.where(sc_ref[2 * j] >= thr, 0.0, NEG)
            sc_ref[2 * j + 1] = jnp.where(sc_ref[2 * j + 1] >= thr, 0.0, NEG)
            return 0
        lax.fori_loop(0, npair, body, 0)

    @pl.when(nk * KC <= topk)
    def _():
        write_ge(jnp.full((1, QB), -FLT_MAX, F32))

    @pl.when(nk * KC > topk)
    def _():
        def bf16_key_to_float_key(k16):
            return k16 * jnp.int32(2 ** 16) + jnp.where(k16 < 0, jnp.int32(2 ** 16 - 1), jnp.int32(0))

        tau_k16, _ = _bitwise_kth(
            lambda k16: count_ge_packed(_key_to_float(bf16_key_to_float_key(k16))), kth, (1, QB), 16)
        base = bf16_key_to_float_key(tau_k16) - jnp.int32(2 ** 16)

        def fine_bit(it, carry):
            off, cnt_off = carry
            cand = off + lax.shift_left(jnp.int32(1), 16 - it)
            cnt = count(lambda x: x >= _key_to_float(base + cand))
            ok = cnt >= kth
            return jnp.where(ok, cand, off), jnp.where(ok, cnt, cnt_off)

        off, cnt_thr = lax.fori_loop(0, 17, fine_bit,
                                     (jnp.zeros((1, QB), jnp.int32), jnp.full((1, QB), 2.0 ** 24, F32)))
        thr = _key_to_float(base + off)
        few = (i * QB + 1 + _iota((1, QB), 1)).astype(F32) <= kth
        thr = jnp.where(few, -FLT_MAX, thr)
        cnt_thr = jnp.where(few, kth, cnt_thr)
        has_tie = jnp.max(cnt_thr) > kth

        @pl.when(jnp.logical_not(has_tie))
        def _():
            write_ge(thr)

        @pl.when(has_tie)
        def _():
            need = kth - count(lambda x: x > thr)
            ltri = (_iota((KC, KC), 1) < _iota((KC, KC), 0)).astype(BF16)

            def body(c, carry):
                x = sc_ref[c]
                eq = x == thr
                before = _dot(ltri, eq.astype(BF16)) + carry
                sel = (x > thr) | (eq & (before < need))
                sc_ref[c] = jnp.where(sel, 0.0, NEG)
                return carry + jnp.sum(sum_rows(eq.astype(F32)), axis=0, keepdims=True)

            lax.fori_loop(0, nk, body, jnp.zeros((1, QB), F32))

    m_ref[...] = jnp.full((B_HEADS, QB), NEG, F32)
    acc_ref[...] = jnp.zeros((B_HEADS * V_ROWS, QB), F32)

    def biased_scores(c, hd):
        pair = hd // 2
        kc = kbf_ref[0, c, :, pair * LANES:(pair + 1) * LANES]
        return _dot_nt(kc, qz_ref[hd]) + sc_ref[c]

    for hd in range(B_HEADS):
        s_ref[hd] = biased_scores(0, hd)

    def attend(c, _):
        nxt = jnp.minimum(c + 1, nk - 1)
        ahead = [biased_scores(nxt, hd) for hd in range(QK_LEAD)]
        for hd in range(B_HEADS):
            rows = slice(hd * V_ROWS, (hd + 1) * V_ROWS)
            s = s_ref[hd]
            m_old = m_ref[hd:hd + 1, :]
            m_new = jnp.maximum(m_old, jnp.max(max_rows(s), axis=0, keepdims=True))
            alpha = jnp.exp2(m_old - m_new)
            p = jnp.exp2(s - m_new)
            acc_ref[rows, :] = alpha * acc_ref[rows, :] + _dot(vt_ref[0, c, rows, :], p.astype(BF16))
            m_ref[hd:hd + 1, :] = m_new
            s_ref[hd] = ahead[hd]
            if hd + QK_LEAD < B_HEADS:
                ahead.append(biased_scores(nxt, hd + QK_LEAD))
        return 0

    lax.fori_loop(0, nk, attend, 0)
    outs = []
    for hd in range(B_HEADS):
        base = hd * V_ROWS
        outs.append(acc_ref[base:base + HEAD_DIM, :] / acc_ref[base + HEAD_DIM:base + HEAD_DIM + 1, :])
    o_ref[...] = jnp.concatenate(outs, axis=0).T.astype(o_ref.dtype)


def _dsa_prompt(qs, qis, wt, ki2, kbf, vtb, batch, seq):
    nq = seq // QB
    nc = seq // KC
    topk = min(TOPK_MAX, seq // 4)
    n = batch * seq
    qspec = pl.BlockSpec((QB, B_WIDTH), lambda b, i: (b * nq + i, 0))
    return pl.pallas_call(
        functools.partial(_dsa_prompt_kernel, topk),
        grid=(batch, nq),
        in_specs=[qspec, qspec,
                  pl.BlockSpec((IDX_HEADS, QB), lambda b, i: (0, b * nq + i)),
                  pl.BlockSpec((1, nc, KC, LANES), lambda b, i: (b, 0, 0, 0)),
                  pl.BlockSpec((1, nc, KC, B_WIDTH), lambda b, i: (b, 0, 0, 0)),
                  pl.BlockSpec((1, nc, B_HEADS * V_ROWS, KC), lambda b, i: (b, 0, 0, 0))],
        out_specs=pl.BlockSpec((QB, B_WIDTH), lambda b, i: (b * nq + i, 0)),
        out_shape=jax.ShapeDtypeStruct((n, B_WIDTH), BF16),
        scratch_shapes=[pltpu.VMEM((nc, KC, QB), F32), pltpu.VMEM((B_HEADS * V_ROWS, QB), F32),
                        pltpu.VMEM((B_HEADS, QB), F32), pltpu.VMEM((B_HEADS, KC, QB), F32),
                        pltpu.VMEM((nc, KC, QB), BF16),
                        pltpu.VMEM((B_HEADS, QB, LANES), BF16), pltpu.VMEM((IDX_HEADS, QB, LANES), BF16)],
        compiler_params=_cparams(("parallel", "arbitrary")), name="dsa_prompt",
    )(qs, qis, wt, ki2.reshape(batch, nc, KC, LANES), kbf.reshape(batch, nc, KC, B_WIDTH),
      vtb.reshape(batch, nc, B_HEADS * V_ROWS, KC))


def _page_specs(block, n_pages_per_step):
    def make(r):
        return pl.BlockSpec(block, lambda b, j, pt: (pt[b, j * n_pages_per_step + r], 0, 0))
    return [make(r) for r in range(n_pages_per_step)]


def _idx_sample_kernel(t_new, pps, pt_ref, qi_ref, w_ref, kin_ref, *rest):
    pages = rest[:pps]
    sc_out, scn_out = rest[pps:]
    j = pl.program_id(1)
    qi = qi_ref[0]
    w = w_ref[0]
    qib = qi.astype(BF16)

    def token_scores(y):
        z = jnp.maximum(y, 0.0) * w
        return jnp.sum(z.reshape(t_new, IDX_HEADS, y.shape[1]), axis=1)

    for r in range(pps):
        y = _dot(qib, pages[r][0].astype(BF16))
        sc_out[0, :, r * PAGE_SIZE:(r + 1) * PAGE_SIZE] = token_scores(y)

    @pl.when(j == 0)
    def _():
        qf = qib.astype(F32)
        lane = _iota((t_new, LANES), 1)
        tok = _iota((t_new, LANES), 0)
        new = jnp.full((t_new, LANES), -jnp.inf, F32)
        for tk in range(t_new):
            kr = kin_ref[0, tk:tk + 1, :].astype(BF16).astype(F32)
            col = token_scores(jnp.sum(qf * kr, axis=-1, keepdims=True))
            new = jnp.where((lane == tk) & (tok >= tk), col, new)
        scn_out[0] = new


def _idx_sample(page_table, qi32, w32, ki_new, cache_idx_kt):
    db, rows, _ = qi32.shape
    t_new = rows // IDX_HEADS
    n_pages = page_table.shape[1]
    pps = min(IDX_PAGES_PER_STEP, n_pages)
    steps = n_pages // pps
    span = pps * PAGE_SIZE
    grid_spec = pltpu.PrefetchScalarGridSpec(
        num_scalar_prefetch=1, grid=(db, steps),
        in_specs=[pl.BlockSpec((1, rows, IDX_DIM), lambda b, j, pt: (b, 0, 0)),
                  pl.BlockSpec((1, rows, 1), lambda b, j, pt: (b, 0, 0)),
                  pl.BlockSpec((1, t_new, IDX_DIM), lambda b, j, pt: (b, 0, 0))]
        + _page_specs((1, IDX_DIM, PAGE_SIZE), pps),
        out_specs=[pl.BlockSpec((1, t_new, span), lambda b, j, pt: (b, 0, j)),
                   pl.BlockSpec((1, t_new, LANES), lambda b, j, pt: (b, 0, 0))])
    return pl.pallas_call(
        functools.partial(_idx_sample_kernel, t_new, pps), grid_spec=grid_spec,
        out_shape=[jax.ShapeDtypeStruct((db, t_new, n_pages * PAGE_SIZE), F32),
                   jax.ShapeDtypeStruct((db, t_new, LANES), F32)],
        compiler_params=_cparams(("parallel", "arbitrary")), name="idx_sample",
    )(page_table, qi32, w32, ki_new, *([cache_idx_kt] * pps))


def _thr_sample_kernel(topk, sc_ref, scn_ref, b_out, bn_out):
    kth = jnp.float32(topk)
    n_chunks = sc_ref.shape[1] // LANES

    def count(pred):
        return (jnp.sum(pred(sc_ref[...]).astype(F32), axis=-1, keepdims=True)
                + jnp.sum(pred(scn_ref[...]).astype(F32), axis=-1, keepdims=True))

    thr, cnt_thr = _kth_largest(lambda t: count(lambda x: x >= t), kth, (THR_ROWS, 1))
    has_tie = jnp.max(cnt_thr) > kth

    @pl.when(jnp.logical_not(has_tie))
    def _():
        b_out[...] = jnp.where(sc_ref[...] >= thr, 0.0, NEG)
        bn_out[...] = jnp.where(scn_ref[...] >= thr, 0.0, NEG)

    @pl.when(has_tie)
    def _():
        need = kth - count(lambda x: x > thr)
        utri = (_iota((LANES, LANES), 0) < _iota((LANES, LANES), 1)).astype(BF16)
        carry = jnp.zeros((THR_ROWS, 1), F32)
        for c in range(n_chunks + 1):
            x = sc_ref[:, c * LANES:(c + 1) * LANES] if c < n_chunks else scn_ref[...]
            eq = x == thr
            before = _dot(eq.astype(BF16), utri) + carry
            sel = (x > thr) | (eq & (before < need))
            bias = jnp.where(sel, 0.0, NEG)
            if c < n_chunks:
                b_out[:, c * LANES:(c + 1) * LANES] = bias
            else:
                bn_out[...] = bias
            carry = carry + jnp.sum(eq.astype(F32), axis=-1, keepdims=True)


def _thr_sample(sc, scn, topk):
    n, width = sc.shape
    return pl.pallas_call(
        functools.partial(_thr_sample_kernel, topk), grid=(n // THR_ROWS,),
        in_specs=[pl.BlockSpec((THR_ROWS, width), lambda i: (i, 0)),
                  pl.BlockSpec((THR_ROWS, LANES), lambda i: (i, 0))],
        out_specs=[pl.BlockSpec((THR_ROWS, width), lambda i: (i, 0)),
                   pl.BlockSpec((THR_ROWS, LANES), lambda i: (i, 0))],
        out_shape=[jax.ShapeDtypeStruct((n, width), F32), jax.ShapeDtypeStruct((n, LANES), F32)],
        compiler_params=_cparams(("parallel",)), name="thr_sample",
    )(sc, scn)


def _attn_sample_kernel(t_new, pps, pt_ref, q_ref, bias_ref, biasn_ref, kn_ref, vn_ref, *rest):
    kpages = rest[:pps]
    vpages = rest[pps:2 * pps]
    o_ref, m_ref, l_ref, acc_ref, qbd_ref = rest[2 * pps:]
    j = pl.program_id(1)
    rows = t_new * B_HEADS
    mask8 = (_iota((SUBLANES, B_WIDTH), 1) // HEAD_DIM) == _iota((SUBLANES, B_WIDTH), 0)

    def per_token(ref, width):
        return jnp.concatenate(
            [jnp.broadcast_to(ref[0, i:i + 1, :], (B_HEADS, width)) for i in range(t_new)], axis=0)

    @pl.when(j == 0)
    def _():
        m_ref[...] = jnp.full((rows, 1), NEG, F32)
        l_ref[...] = jnp.zeros((rows, 1), F32)
        acc_ref[...] = jnp.zeros((rows, B_WIDTH), F32)
        qbd_ref[...] = jnp.concatenate(
            [jnp.where(mask8, jnp.broadcast_to(q_ref[0, i:i + 1, :] * ATTN_SCALE, (B_HEADS, B_WIDTH)), 0.0)
             for i in range(t_new)], axis=0).astype(BF16)

    qbd = qbd_ref[...]
    span = pps * PAGE_SIZE
    s = jnp.concatenate([_dot(qbd, kpages[r][0].astype(BF16)) for r in range(pps)], axis=1)
    s = s + per_token(bias_ref, span)
    m_old = m_ref[...]
    m_new = jnp.maximum(m_old, jnp.max(s, axis=-1, keepdims=True))
    alpha = jnp.exp(m_old - m_new)
    p = jnp.exp(s - m_new)
    l_ref[...] = alpha * l_ref[...] + jnp.sum(p, axis=-1, keepdims=True)
    pb = p.astype(BF16)
    pv = jnp.zeros((rows, B_WIDTH), F32)
    for r in range(pps):
        pv = pv + _dot_nt(pb[:, r * PAGE_SIZE:(r + 1) * PAGE_SIZE], vpages[r][0].astype(BF16))
    acc_ref[...] = alpha * acc_ref[...] + pv
    m_ref[...] = m_new

    @pl.when(j == pl.num_programs(1) - 1)
    def _():
        qf = qbd.astype(F32)
        bn = per_token(biasn_ref, LANES)
        cols = []
        for tk in range(t_new):
            kr = kn_ref[0, tk:tk + 1, :].astype(BF16).astype(F32)
            cols.append(jnp.sum(qf * kr, axis=-1, keepdims=True) + bn[:, tk:tk + 1])
        m_old = m_ref[...]
        m_new = m_old
        for cval in cols:
            m_new = jnp.maximum(m_new, cval)
        alpha = jnp.exp(m_old - m_new)
        l = alpha * l_ref[...]
        acc = alpha * acc_ref[...]
        for tk in range(t_new):
            pk = jnp.exp(cols[tk] - m_new)
            l = l + pk
            acc = acc + pk.astype(BF16).astype(F32) * vn_ref[0, tk:tk + 1, :].astype(BF16).astype(F32)
        o = acc / l
        for i in range(t_new):
            slab = jnp.where(mask8, o[i * B_HEADS:(i + 1) * B_HEADS, :], 0.0)
            o_ref[0, i * B_HEADS:(i + 1) * B_HEADS, :] = jnp.broadcast_to(
                jnp.sum(slab, axis=0, keepdims=True), (B_HEADS, B_WIDTH))


def _attn_sample(page_table, q3, bias3, biasn3, k_new, v_new, cache_kt, cache_vt):
    db, t_new, _ = q3.shape
    n_pages = page_table.shape[1]
    pps = min(ATTN_PAGES_PER_STEP, n_pages)
    steps = n_pages // pps
    span = pps * PAGE_SIZE
    rows = t_new * B_HEADS
    per_b = lambda w: pl.BlockSpec((1, t_new, w), lambda b, j, pt: (b, 0, 0))
    grid_spec = pltpu.PrefetchScalarGridSpec(
        num_scalar_prefetch=1, grid=(db, steps),
        in_specs=[per_b(B_WIDTH),
                  pl.BlockSpec((1, t_new, span), lambda b, j, pt: (b, 0, j)),
                  per_b(LANES), per_b(B_WIDTH), per_b(B_WIDTH)]
        + _page_specs((1, B_WIDTH, PAGE_SIZE), pps)
        + _page_specs((1, B_WIDTH, PAGE_SIZE), pps),
        out_specs=pl.BlockSpec((1, rows, B_WIDTH), lambda b, j, pt: (b, 0, 0)),
        scratch_shapes=[pltpu.VMEM((rows, 1), F32), pltpu.VMEM((rows, 1), F32),
                        pltpu.VMEM((rows, B_WIDTH), F32), pltpu.VMEM((rows, B_WIDTH), BF16)])
    return pl.pallas_call(
        functools.partial(_attn_sample_kernel, t_new, pps), grid_spec=grid_spec,
        out_shape=jax.ShapeDtypeStruct((db, rows, B_WIDTH), F32),
        compiler_params=_cparams(("parallel", "arbitrary")), name="attn_sample",
    )(page_table, q3, bias3, biasn3, k_new, v_new, *([cache_kt] * pps), *([cache_vt] * pps))


def _merge_kernel(fused_c, x_ref, ma_ref, yb_ref, sgb_ref, c_ref, sgc_ref, wo_ref, g_ref, *rest):
    if fused_c:
        mkt_ref, mvt_ref, y_out = rest
        yc = _memattn_rows(c_ref[...], mkt_ref[0], mvt_ref[0])
    else:
        (y_out,) = rest
        yc = c_ref[...].astype(F32)
    mb = (yb_ref[...].astype(F32) * sgb_ref[...].astype(F32)).astype(BF16)
    mc = (yc * sgc_ref[...].astype(F32)).astype(BF16)
    o = (_dot(ma_ref[...], wo_ref[0:A_WIDTH, :])
         + _dot(mb, wo_ref[A_WIDTH:A_WIDTH + B_WIDTH, :])
         + _dot(mc, wo_ref[A_WIDTH + B_WIDTH:, :]))
    z = x_ref[...] + o
    ms = jnp.mean(z * z, axis=-1, keepdims=True)
    y_out[...] = z * lax.rsqrt(ms + EPS) * g_ref[...]


def _merge(x, mixa, yb, sgb, c, sgc, wo, g_final, mem=None, rows_per_mem=None):
    n = x.shape[0]
    tm = min(MERGE_TM, n)
    row = lambda w: pl.BlockSpec((tm, w), lambda i: (i, 0))
    in_specs = [row(D_MODEL), row(A_WIDTH), row(B_WIDTH), row(B_WIDTH), row(C_WIDTH), row(C_WIDTH),
                pl.BlockSpec(wo.shape, lambda i: (0, 0)), pl.BlockSpec(g_final.shape, lambda i: (0, 0))]
    args = [x, mixa, yb, sgb, c, sgc, wo, g_final]
    if mem is not None:
        tiles_per_mem = rows_per_mem // tm
        mem_spec = pl.BlockSpec((1, C_WIDTH, N_MEM), lambda i: (i // tiles_per_mem, 0, 0))
        in_specs += [mem_spec, mem_spec]
        args += list(mem)
    return pl.pallas_call(
        functools.partial(_merge_kernel, mem is not None), grid=(n // tm,),
        in_specs=in_specs, out_specs=row(D_MODEL),
        out_shape=jax.ShapeDtypeStruct((n, D_MODEL), F32),
        compiler_params=_cparams(("parallel",)), name="merge",
    )(*args)


def _rope_tables(pos):
    inv = ROPE_THETA ** (-jnp.arange(0, HEAD_DIM, 2, dtype=F32) / HEAD_DIM)
    ang = pos.astype(F32)[:, None] * inv[None, :]
    c = jnp.cos(ang)
    s = jnp.sin(ang)
    return jnp.concatenate([c, c, c, c], axis=1), jnp.concatenate([-s, s, -s, s], axis=1)


def _heads_last(t, heads):
    b, _, n = t.shape
    return t.reshape(b, heads, HEAD_DIM, n).transpose(0, 3, 1, 2)


def _channels_first(t):
    p, n, heads, d = t.shape
    return t.transpose(0, 2, 3, 1).reshape(p, heads * d, n)


def kernel(x_prompt, x_sample, mem_prompt, cache_k, cache_v, cache_idx_k, cache_mem_k, cache_mem_v,
           page_table, g_norm, w_in, w_spatial, b_spatial, g_v, w_mem_kv, g_mem, w_out, g_final):
    batch, seq, _ = x_prompt.shape
    db, t_new, _ = x_sample.shape
    n_pages = page_table.shape[1]
    past = n_pages * PAGE_SIZE

    ki_cols = w_in[:, 3328:3392]
    wi_cols = w_in[:, 3392:3400]
    w_r = jnp.concatenate(
        [w_in[:, :3328], w_in[:, 3400:3912], ki_cols, ki_cols, wi_cols,
         jnp.zeros((D_MODEL, LANES - IDX_HEADS), F32)], axis=1).astype(BF16)
    g_norm2 = g_norm.reshape(1, D_MODEL)
    g_v2 = g_v.reshape(1, A_WIDTH)
    bs_prompt = jnp.repeat(b_spatial.T, HEAD_DIM, axis=1)
    reps = CHUNK // t_new
    ws_sample = jnp.tile(w_spatial[:, :t_new, :t_new], (1, reps, reps))
    bs_sample = jnp.tile(jnp.repeat(b_spatial[:, :t_new].T, HEAD_DIM, axis=1), (reps, 1))
    wo_b = w_out.astype(BF16)
    g_final2 = g_final.reshape(1, D_MODEL)

    cos_p, sin_p = _rope_tables(jnp.arange(seq, dtype=jnp.int32))
    pos_s = past + jnp.arange(t_new, dtype=jnp.int32)
    cos_s, sin_s = _rope_tables(jnp.tile(pos_s, TM // t_new))

    xp = x_prompt.reshape(batch * seq, D_MODEL)
    (kt, vt, kit, kbf, vtb, ki2, qs, qis, wt, sgb, mixa, qc, sgc) = _proj(
        xp, g_norm2, w_r, cos_p, sin_p, w_spatial, bs_prompt, g_v2, CHUNK, seq // TM, False)
    mkt, mvt = _memkv(mem_prompt.reshape(batch * N_MEM, D_MODEL), g_mem.reshape(1, D_MODEL),
                      w_mem_kv.astype(BF16))
    yb = _dsa_prompt(qs, qis, wt, ki2, kbf, vtb, batch, seq)
    y_prompt = _merge(xp, mixa, yb, sgb, qc, sgc, wo_b, g_final2, mem=(mkt, mvt),
                      rows_per_mem=seq).reshape(batch, seq, D_MODEL)

    ns = db * t_new
    xs = x_sample.reshape(ns, D_MODEL)
    (q_s, k_s, v_s, qi_s, ki_s, wi_s, sgb_s, mixa_s, qc_s, sgc_s, vn_s) = _proj(
        xs, g_norm2, w_r, cos_s, sin_s, ws_sample, bs_sample, g_v2, t_new, 1, True)
    sc, scn = _idx_sample(page_table, qi_s.reshape(db, t_new * IDX_HEADS, IDX_DIM),
                          wi_s.reshape(db, t_new * IDX_HEADS, 1), ki_s.reshape(db, t_new, IDX_DIM),
                          cache_idx_k.transpose(0, 2, 1))
    topk_s = min(TOPK_MAX, (past + t_new) // 4)
    bias, biasn = _thr_sample(sc.reshape(ns, past), scn.reshape(ns, LANES), topk_s)
    yb_pad = _attn_sample(page_table, q_s.reshape(db, t_new, B_WIDTH), bias.reshape(db, t_new, past),
                          biasn.reshape(db, t_new, LANES), k_s.reshape(db, t_new, B_WIDTH),
                          v_s.reshape(db, t_new, B_WIDTH), _channels_first(cache_k), _channels_first(cache_v))
    yb_s = yb_pad[:, ::B_HEADS, :].reshape(ns, B_WIDTH)
    yc_pad = _memattn_sample(qc_s.reshape(db, t_new, C_WIDTH),
                             _channels_first(cache_mem_k), _channels_first(cache_mem_v))
    yc_s = yc_pad[:, ::SUBLANES, :].reshape(ns, C_WIDTH)
    y_sample = _merge(xs, mixa_s, yb_s, sgb_s, yc_s, sgc_s, wo_b, g_final2).reshape(db, t_new, D_MODEL)

    return (y_prompt, y_sample,
            _heads_last(kt, B_HEADS), _heads_last(vt, B_HEADS), kit.transpose(0, 2, 1),
            _heads_last(mkt, C_HEADS), _heads_last(mvt, C_HEADS),
            k_s.reshape(db, t_new, B_HEADS, HEAD_DIM), v_s.reshape(db, t_new, B_HEADS, HEAD_DIM),
            ki_s.reshape(db, t_new, IDX_DIM), vn_s.reshape(db, t_new, A_WIDTH))
```

```python
import functools

import jax
import jax.numpy as jnp
from jax import lax
from jax.experimental import pallas as pl
from jax.experimental.pallas import tpu as pltpu

F32 = jnp.float32
BF16 = jnp.bfloat16

HEAD_DIM = 64
HALF = HEAD_DIM // 2
D_MODEL = 1024
A_GROUPS = 4
A_WIDTH = A_GROUPS * HEAD_DIM
CHUNK = 128
B_HEADS = 8
B_WIDTH = B_HEADS * HEAD_DIM
IDX_HEADS = 8
IDX_DIM = 64
TOPK_MAX = 256
N_MEM = 256
C_HEADS = 4
C_WIDTH = C_HEADS * HEAD_DIM
PAGE_SIZE = 128
ROPE_THETA = 10000.0
EPS = 1e-6
IDX_W_SCALE = (IDX_HEADS * IDX_DIM) ** -0.5
ATTN_SCALE = HEAD_DIM ** -0.5
LOG2E = 1.4426950408889634

LANES = 128
SUBLANES = 8
NEG = -1e30
FLT_MAX = 3.4028234663852886e38
PACKED_ROWS = 16

TM = 512
MEMATTN_SEQS_PER_STEP = 16
MERGE_TM = 1024
QB = 256
KC = 256
V_ROWS = HEAD_DIM + PACKED_ROWS
QK_LEAD = 3
IDX_PAGES_PER_STEP = 64
ATTN_PAGES_PER_STEP = 32
THR_ROWS = 128
VMEM_LIMIT = 52 * 1024 * 1024

OFF_U, OFF_VA, OFF_GA = 0, 256, 512
OFF_Q, OFF_K, OFF_V, OFF_GB, OFF_QI = 768, 1280, 1792, 2304, 2816
OFF_QC, OFF_GC, OFF_KI2, OFF_WI, W_TOTAL = 3328, 3584, 3840, 3968, 4096


def _cparams(sem):
    return pltpu.CompilerParams(dimension_semantics=sem, vmem_limit_bytes=VMEM_LIMIT)


def _iota(shape, dim):
    return lax.broadcasted_iota(jnp.int32, shape, dim)


def _silu(x):
    return x / (1.0 + jnp.exp(-x))


def _dot(a, b):
    return jnp.dot(a, b, preferred_element_type=F32)


def _dot_nt(a, b):
    return lax.dot_general(a, b, (((1,), (1,)), ((), ())), preferred_element_type=F32)


def _key_to_float(k):
    bits = k ^ ((k >> 31) & jnp.int32(0x7FFFFFFF))
    return lax.bitcast_convert_type(bits, F32)


def _bitwise_kth(count_ge, kth, shape, bits):
    cnt0 = count_ge(jnp.zeros(shape, jnp.int32))
    ok0 = cnt0 >= kth
    key = jnp.where(ok0, jnp.int32(0), jnp.int32(-2 ** (bits - 1)))
    cnt_key = jnp.where(ok0, cnt0, jnp.float32(2 ** 24))

    def bit_body(it, carry):
        key, cnt_key = carry
        trial = key + lax.shift_left(jnp.int32(1), bits - 2 - it)
        cnt = count_ge(trial)
        ok = cnt >= kth
        return jnp.where(ok, trial, key), jnp.where(ok, cnt, cnt_key)

    return lax.fori_loop(0, bits - 1, bit_body, (key, cnt_key))


def _kth_largest(count_ge, kth, shape):
    key, cnt_key = _bitwise_kth(lambda k: count_ge(_key_to_float(k)), kth, shape, 32)
    return _key_to_float(key), cnt_key


def _rope(t, cos, sin):
    outs = []
    first = (_iota((t.shape[0], LANES), 1) % HEAD_DIM) < HALF
    for j in range(t.shape[1] // LANES):
        x = t[:, j * LANES:(j + 1) * LANES]
        partner = jnp.where(first, pltpu.roll(x, LANES - HALF, 1), pltpu.roll(x, HALF, 1))
        outs.append(x * cos + partner * sin)
    return outs[0] if len(outs) == 1 else jnp.concatenate(outs, axis=1)


def _proj_kernel(chunk, sample, x_ref, g_ref, w_ref, cos_ref, sin_ref, ws_ref, bs_ref, gv_ref, *outs):
    if sample:
        q_out, k_out, v_out, qi_out, ki_out, wi_out, sgb_out, mixa_out, qc_out, sgc_out, vn_out = outs
    else:
        (kt_out, vt_out, kit_out, kbf_out, vtb_out, ki2_out, qs_out, qis_out, wt_out,
         sgb_out, mixa_out, qc_out, sgc_out) = outs
    tm = x_ref.shape[0]
    x = x_ref[...]
    ms = jnp.mean(x * x, axis=-1, keepdims=True)
    h = (x * lax.rsqrt(ms + EPS) * g_ref[...]).astype(BF16)
    cos = cos_ref[...]
    sin = sin_ref[...]

    def seg(a, b):
        return _dot(h, w_ref[:, a:b])

    q = _rope(seg(OFF_Q, OFF_K), cos, sin)
    k = _rope(seg(OFF_K, OFF_V), cos, sin)
    v = seg(OFF_V, OFF_GB)
    qi = _rope(seg(OFF_QI, OFF_QC), cos, sin)
    ki2 = _rope(seg(OFF_KI2, OFF_WI), cos, sin)
    wi = seg(OFF_WI, W_TOTAL) * IDX_W_SCALE
    qc = seg(OFF_QC, OFF_GC) * ATTN_SCALE
    sgb_out[...] = _silu(seg(OFF_GB, OFF_QI)).astype(BF16)
    sgc_out[...] = _silu(seg(OFF_GC, OFF_KI2)).astype(BF16)
    qc_out[...] = qc.astype(qc_out.dtype)
    if sample:
        q_out[...] = q
        k_out[...] = k
        v_out[...] = v
        qi_out[...] = qi
        ki_out[...] = ki2[:, :IDX_DIM]
        wi_out[...] = wi[:, :IDX_HEADS]
    else:
        qs_out[...] = (q * (ATTN_SCALE * LOG2E)).astype(BF16)
        qis_out[...] = qi.astype(BF16)
        kt_out[0] = k.T
        kbf_out[...] = k.astype(BF16)
        vt = v.T
        vt_out[0] = vt
        vtb = vt.astype(BF16)
        ones = jnp.ones((V_ROWS - HEAD_DIM, KC), BF16)
        for j in range(tm // KC):
            for hd in range(B_HEADS):
                vtb_out[j, hd * V_ROWS:hd * V_ROWS + HEAD_DIM, :] = (
                    vtb[hd * HEAD_DIM:(hd + 1) * HEAD_DIM, j * KC:(j + 1) * KC])
                vtb_out[j, hd * V_ROWS + HEAD_DIM:(hd + 1) * V_ROWS, :] = ones
        kit_out[0] = ki2.T[:IDX_DIM, :]
        ki2_out[...] = ki2.astype(BF16)
        wt_out[...] = wi.T[:IDX_HEADS, :]

    u = seg(OFF_U, OFF_VA)
    va = seg(OFF_VA, OFF_GA)
    ga = seg(OFF_GA, OFF_Q)
    grp_r = _iota((A_WIDTH, A_WIDTH), 0) // HEAD_DIM
    grp_c = _iota((A_WIDTH, A_WIDTH), 1) // HEAD_DIM
    gmat = jnp.where(grp_r == grp_c, 1.0 / HEAD_DIM, 0.0).astype(BF16)

    def group_mean(t):
        hi = t.astype(BF16)
        lo = (t - hi.astype(F32)).astype(BF16)
        return _dot(hi, gmat) + _dot(lo, gmat)

    mu = group_mean(va)
    d = va - mu
    var = group_mean(d * d)
    vn = d * lax.rsqrt(var + EPS) * gv_ref[...]
    if sample:
        vn_out[...] = vn
    vnb = vn.astype(BF16)
    row = _iota((CHUNK, CHUNK), 0)
    col = _iota((CHUNK, CHUNK), 1)
    causal = (row >= col) & ((row // chunk) == (col // chunk))
    lane_grp = _iota((CHUNK, A_WIDTH), 1) // HEAD_DIM
    wms = [jnp.where(causal, ws_ref[g], 0.0).astype(BF16) for g in range(A_GROUPS)]
    mixes = []
    for j in range(tm // CHUNK):
        vc = vnb[j * CHUNK:(j + 1) * CHUNK, :]
        m = bs_ref[...]
        for g in range(A_GROUPS):
            m = m + jnp.where(lane_grp == g, _dot(wms[g], vc), 0.0)
        mixes.append(m)
    mix = jnp.concatenate(mixes, axis=0)
    mixa_out[...] = (_silu(ga) * u * mix).astype(BF16)


def _proj(x, g_norm, w_r, cos_t, sin_t, ws, bs, g_v, chunk, seq_blocks, sample):
    n = x.shape[0]
    nt = n // TM
    nb = nt // seq_blocks
    row = lambda w: pl.BlockSpec((TM, w), lambda i: (i, 0))
    const2 = lambda a: pl.BlockSpec(a.shape, lambda i: (0, 0))
    sds = jax.ShapeDtypeStruct
    in_specs = [
        row(D_MODEL), const2(g_norm), const2(w_r),
        pl.BlockSpec((TM, LANES), lambda i: (i % seq_blocks, 0)),
        pl.BlockSpec((TM, LANES), lambda i: (i % seq_blocks, 0)),
        pl.BlockSpec(ws.shape, lambda i: (0, 0, 0)), const2(bs), const2(g_v),
    ]
    if sample:
        outs = [
            (sds((n, B_WIDTH), F32), row(B_WIDTH)),
            (sds((n, B_WIDTH), F32), row(B_WIDTH)),
            (sds((n, B_WIDTH), F32), row(B_WIDTH)),
            (sds((n, IDX_HEADS * IDX_DIM), F32), row(IDX_HEADS * IDX_DIM)),
            (sds((n, IDX_DIM), F32), row(IDX_DIM)),
            (sds((n, IDX_HEADS), F32), row(IDX_HEADS)),
            (sds((n, B_WIDTH), BF16), row(B_WIDTH)),
            (sds((n, A_WIDTH), BF16), row(A_WIDTH)),
            (sds((n, C_WIDTH), F32), row(C_WIDTH)),
            (sds((n, C_WIDTH), BF16), row(C_WIDTH)),
            (sds((n, A_WIDTH), F32), row(A_WIDTH)),
        ]
    else:
        seq = seq_blocks * TM
        tspec = lambda w: pl.BlockSpec((1, w, TM), lambda i: (i // seq_blocks, 0, i % seq_blocks))
        outs = [
            (sds((nb, B_WIDTH, seq), F32), tspec(B_WIDTH)),
            (sds((nb, B_WIDTH, seq), F32), tspec(B_WIDTH)),
            (sds((nb, IDX_DIM, seq), F32), tspec(IDX_DIM)),
            (sds((n, B_WIDTH), BF16), row(B_WIDTH)),
            (sds((n // KC, B_HEADS * V_ROWS, KC), BF16),
             pl.BlockSpec((TM // KC, B_HEADS * V_ROWS, KC), lambda i: (i, 0, 0))),
            (sds((n, LANES), BF16), row(LANES)),
            (sds((n, B_WIDTH), BF16), row(B_WIDTH)),
            (sds((n, IDX_HEADS * IDX_DIM), BF16), row(IDX_HEADS * IDX_DIM)),
            (sds((IDX_HEADS, n), F32), pl.BlockSpec((IDX_HEADS, TM), lambda i: (0, i))),
            (sds((n, B_WIDTH), BF16), row(B_WIDTH)),
            (sds((n, A_WIDTH), BF16), row(A_WIDTH)),
            (sds((n, C_WIDTH), BF16), row(C_WIDTH)),
            (sds((n, C_WIDTH), BF16), row(C_WIDTH)),
        ]
    return pl.pallas_call(
        functools.partial(_proj_kernel, chunk, sample),
        grid=(nt,), in_specs=in_specs, out_specs=[o[1] for o in outs], out_shape=[o[0] for o in outs],
        compiler_params=_cparams(("parallel",)), name="proj",
    )(x, g_norm, w_r, cos_t, sin_t, ws, bs, g_v)


def _memkv_kernel(m_ref, g_ref, w_ref, mkt_out, mvt_out):
    x = m_ref[...]
    ms = jnp.mean(x * x, axis=-1, keepdims=True)
    h = (x * lax.rsqrt(ms + EPS) * g_ref[...]).astype(BF16)
    kv = _dot(h, w_ref[...])
    mkt_out[0] = kv[:, :C_WIDTH].T
    mvt_out[0] = kv[:, C_WIDTH:].T


def _memkv(mem, g_mem, w_kv):
    n = mem.shape[0]
    nb = n // N_MEM
    return pl.pallas_call(
        _memkv_kernel, grid=(nb,),
        in_specs=[pl.BlockSpec((N_MEM, D_MODEL), lambda i: (i, 0)),
                  pl.BlockSpec(g_mem.shape, lambda i: (0, 0)),
                  pl.BlockSpec(w_kv.shape, lambda i: (0, 0))],
        out_specs=[pl.BlockSpec((1, C_WIDTH, N_MEM), lambda i: (i, 0, 0))] * 2,
        out_shape=[jax.ShapeDtypeStruct((nb, C_WIDTH, N_MEM), F32)] * 2,
        compiler_params=_cparams(("parallel",)), name="memkv",
    )(mem, g_mem, w_kv)


def _memattn_rows(q, mkt, mvt):
    mkt = mkt.astype(BF16)
    mvt = mvt.astype(BF16)
    t = q.shape[0]
    lane_head = _iota((t, C_WIDTH), 1) // HEAD_DIM
    out = jnp.zeros((t, C_WIDTH), F32)
    for hd in range(C_HEADS):
        qh = jnp.where(lane_head == hd, q, 0.0).astype(BF16)
        s = _dot(qh, mkt)
        p = jnp.exp(s - jnp.max(s, axis=-1, keepdims=True))
        o = _dot_nt(p.astype(BF16), mvt) / jnp.sum(p, axis=-1, keepdims=True)
        out = out + jnp.where(lane_head == hd, o, 0.0)
    return out


def _memattn_sample_kernel(q_ref, mkt_ref, mvt_ref, o_ref):
    t = q_ref.shape[1]
    mask8 = (_iota((SUBLANES, C_WIDTH), 1) // HEAD_DIM) == _iota((SUBLANES, C_WIDTH), 0)
    for g in range(q_ref.shape[0]):
        q4 = q_ref[g]
        qbd = jnp.concatenate(
            [jnp.where(mask8, jnp.broadcast_to(q4[i:i + 1, :], (SUBLANES, C_WIDTH)), 0.0) for i in range(t)], axis=0)
        s = _dot(qbd.astype(BF16), mkt_ref[g].astype(BF16))
        p = jnp.exp(s - jnp.max(s, axis=-1, keepdims=True))
        o = _dot_nt(p.astype(BF16), mvt_ref[g].astype(BF16)) / jnp.sum(p, axis=-1, keepdims=True)
        for i in range(t):
            slab = jnp.where(mask8, o[i * SUBLANES:(i + 1) * SUBLANES, :], 0.0)
            o_ref[g, i * SUBLANES:(i + 1) * SUBLANES, :] = jnp.broadcast_to(
                jnp.sum(slab, axis=0, keepdims=True), (SUBLANES, C_WIDTH))


def _memattn_sample(q3, mkt3, mvt3):
    n, t, _ = q3.shape
    g = min(MEMATTN_SEQS_PER_STEP, n)
    mem_spec = pl.BlockSpec((g, C_WIDTH, N_MEM), lambda i: (i, 0, 0))
    return pl.pallas_call(
        _memattn_sample_kernel, grid=(n // g,),
        in_specs=[pl.BlockSpec((g, t, C_WIDTH), lambda i: (i, 0, 0)), mem_spec, mem_spec],
        out_specs=pl.BlockSpec((g, t * SUBLANES, C_WIDTH), lambda i: (i, 0, 0)),
        out_shape=jax.ShapeDtypeStruct((n, t * SUBLANES, C_WIDTH), F32),
        compiler_params=_cparams(("parallel",)), name="memattn_sample",
    )(q3, mkt3, mvt3)


def _dsa_prompt_kernel(topk, qs_ref, qis_ref, wt_ref, ki2_ref, kbf_ref, vt_ref, o_ref,
                       sc_ref, acc_ref, m_ref, s_ref, sb_ref, qz_ref, qiz_ref):
    i = pl.program_id(1)
    nk = i + 1
    npair = (nk + 1) // 2
    kth = jnp.float32(topk)

    half_id = _iota((QB, LANES), 1) // HEAD_DIM
    for hd in range(B_HEADS):
        lanes = slice((hd // 2) * LANES, (hd // 2 + 1) * LANES)
        keep = half_id == hd % 2
        qz_ref[hd] = jnp.where(keep, qs_ref[:, lanes].astype(F32), 0.0).astype(BF16)
        qiz_ref[hd] = jnp.where(keep, qis_ref[:, lanes].astype(F32), 0.0).astype(BF16)

    def sum_rows(x):
        return jnp.sum(x.reshape(KC // SUBLANES, SUBLANES, QB), axis=0)

    def max_rows(x):
        return jnp.max(x.reshape(KC // SUBLANES, SUBLANES, QB), axis=0)

    def score_pair(j, _):
        for c in (2 * j, 2 * j + 1):
            keys = ki2_ref[0, c]
            acc = jnp.zeros((KC, QB), F32)
            for hd in range(IDX_HEADS):
                y = _dot_nt(keys, qiz_ref[hd])
                acc = acc + jnp.maximum(y, 0.0) * wt_ref[hd:hd + 1, :]
            kpos = c * KC + _iota((KC, QB), 0)
            qpos = i * QB + _iota((KC, QB), 1)
            masked = jnp.where(kpos <= qpos, acc, -jnp.inf)
            sc_ref[c] = masked
            sb_ref[c] = masked.astype(BF16)
        return 0

    lax.fori_loop(0, npair, score_pair, 0)

    def count(pred):
        def body(j, cnt):
            return (cnt + sum_rows(pred(sc_ref[2 * j]).astype(F32))
                    + sum_rows(pred(sc_ref[2 * j + 1]).astype(F32)))
        return jnp.sum(lax.fori_loop(0, npair, body, jnp.zeros((SUBLANES, QB), F32)), axis=0, keepdims=True)

    def count_ge_packed(t):
        tb = t.astype(BF16)
        def slab_count(c):
            ones = jnp.where(sb_ref[c] >= tb, jnp.ones((), BF16), jnp.zeros((), BF16))
            acc = ones[0:PACKED_ROWS]
            for r in range(1, KC // PACKED_ROWS):
                acc = acc + ones[r * PACKED_ROWS:(r + 1) * PACKED_ROWS]
            return acc

        def body(j, cnt):
            return cnt + slab_count(2 * j) + slab_count(2 * j + 1)

        cnt16 = lax.fori_loop(0, npair, body, jnp.zeros((PACKED_ROWS, QB), BF16))
        return jnp.sum(cnt16.astype(F32), axis=0, keepdims=True)

    def write_ge(thr):
        def body(j, _):
            sc_ref[2 * j] = jnp.where(sc_ref[2 * j] >= thr, 0.0, NEG)
            sc_ref[2 * j + 1] = jnp.where(sc_ref[2 * j + 1] >= thr, 0.0, NEG)
            return 0
        lax.fori_loop(0, npair, body, 0)

    @pl.when(nk * KC <= topk)
    def _():
        write_ge(jnp.full((1, QB), -FLT_MAX, F32))

    @pl.when(nk * KC > topk)
    def _():
        def bf16_key_to_float_key(k16):
            return k16 * jnp.int32(2 ** 16) + jnp.where(k16 < 0, jnp.int32(2 ** 16 - 1), jnp.int32(0))

        tau_k16, _ = _bitwise_kth(
            lambda k16: count_ge_packed(_key_to_float(bf16_key_to_float_key(k16))), kth, (1, QB), 16)
        base = bf16_key_to_float_key(tau_k16) - jnp.int32(2 ** 16)

        def fine_bit(it, carry):
            off, cnt_off = carry
            cand = off + lax.shift_left(jnp.int32(1), 16 - it)
            cnt = count(lambda x: x >= _key_to_float(base + cand))
            ok = cnt >= kth
            return jnp.where(ok, cand, off), jnp.where(ok, cnt, cnt_off)

        off, cnt_thr = lax.fori_loop(0, 17, fine_bit,
                                     (jnp.zeros((1, QB), jnp.int32), jnp.full((1, QB), 2.0 ** 24, F32)))
        thr = _key_to_float(base + off)
        few = (i * QB + 1 + _iota((1, QB), 1)).astype(F32) <= kth
        thr = jnp.where(few, -FLT_MAX, thr)
        cnt_thr = jnp.where(few, kth, cnt_thr)
        has_tie = jnp.max(cnt_thr) > kth

        @pl.when(jnp.logical_not(has_tie))
        def _():
            write_ge(thr)

        @pl.when(has_tie)
        def _():
            need = kth - count(lambda x: x > thr)
            ltri = (_iota((KC, KC), 1) < _iota((KC, KC), 0)).astype(BF16)

            def body(c, carry):
                x = sc_ref[c]
                eq = x == thr
                before = _dot(ltri, eq.astype(BF16)) + carry
                sel = (x > thr) | (eq & (before < need))
                sc_ref[c] = jnp.where(sel, 0.0, NEG)
                return carry + jnp.sum(sum_rows(eq.astype(F32)), axis=0, keepdims=True)

            lax.fori_loop(0, nk, body, jnp.zeros((1, QB), F32))

    m_ref[...] = jnp.full((B_HEADS, QB), NEG, F32)
    acc_ref[...] = jnp.zeros((B_HEADS * V_ROWS, QB), F32)

    def biased_scores(c, hd):
        pair = hd // 2
        kc = kbf_ref[0, c, :, pair * LANES:(pair + 1) * LANES]
        return _dot_nt(kc, qz_ref[hd]) + sc_ref[c]

    for hd in range(B_HEADS):
        s_ref[hd] = biased_scores(0, hd)

    def attend(c, _):
        nxt = jnp.minimum(c + 1, nk - 1)
        ahead = [biased_scores(nxt, hd) for hd in range(QK_LEAD)]
        for hd in range(B_HEADS):
            rows = slice(hd * V_ROWS, (hd + 1) * V_ROWS)
            s = s_ref[hd]
            m_old = m_ref[hd:hd + 1, :]
            m_new = jnp.maximum(m_old, jnp.max(max_rows(s), axis=0, keepdims=True))
            alpha = jnp.exp2(m_old - m_new)
            p = jnp.exp2(s - m_new)
            acc_ref[rows, :] = alpha * acc_ref[rows, :] + _dot(vt_ref[0, c, rows, :], p.astype(BF16))
            m_ref[hd:hd + 1, :] = m_new
            s_ref[hd] = ahead[hd]
            if hd + QK_LEAD < B_HEADS:
                ahead.append(biased_scores(nxt, hd + QK_LEAD))
        return 0

    lax.fori_loop(0, nk, attend, 0)
    outs = []
    for hd in range(B_HEADS):
        base = hd * V_ROWS
        outs.append(acc_ref[base:base + HEAD_DIM, :] / acc_ref[base + HEAD_DIM:base + HEAD_DIM + 1, :])
    o_ref[...] = jnp.concatenate(outs, axis=0).T.astype(o_ref.dtype)


def _dsa_prompt(qs, qis, wt, ki2, kbf, vtb, batch, seq):
    nq = seq // QB
    nc = seq // KC
    topk = min(TOPK_MAX, seq // 4)
    n = batch * seq
    qspec = pl.BlockSpec((QB, B_WIDTH), lambda b, i: (b * nq + i, 0))
    return pl.pallas_call(
        functools.partial(_dsa_prompt_kernel, topk),
        grid=(batch, nq),
        in_specs=[qspec, qspec,
                  pl.BlockSpec((IDX_HEADS, QB), lambda b, i: (0, b * nq + i)),
                  pl.BlockSpec((1, nc, KC, LANES), lambda b, i: (b, 0, 0, 0)),
                  pl.BlockSpec((1, nc, KC, B_WIDTH), lambda b, i: (b, 0, 0, 0)),
                  pl.BlockSpec((1, nc, B_HEADS * V_ROWS, KC), lambda b, i: (b, 0, 0, 0))],
        out_specs=pl.BlockSpec((QB, B_WIDTH), lambda b, i: (b * nq + i, 0)),
        out_shape=jax.ShapeDtypeStruct((n, B_WIDTH), BF16),
        scratch_shapes=[pltpu.VMEM((nc, KC, QB), F32), pltpu.VMEM((B_HEADS * V_ROWS, QB), F32),
                        pltpu.VMEM((B_HEADS, QB), F32), pltpu.VMEM((B_HEADS, KC, QB), F32),
                        pltpu.VMEM((nc, KC, QB), BF16),
                        pltpu.VMEM((B_HEADS, QB, LANES), BF16), pltpu.VMEM((IDX_HEADS, QB, LANES), BF16)],
        compiler_params=_cparams(("parallel", "arbitrary")), name="dsa_prompt",
    )(qs, qis, wt, ki2.reshape(batch, nc, KC, LANES), kbf.reshape(batch, nc, KC, B_WIDTH),
      vtb.reshape(batch, nc, B_HEADS * V_ROWS, KC))


def _page_specs(block, n_pages_per_step):
    def make(r):
        return pl.BlockSpec(block, lambda b, j, pt: (pt[b, j * n_pages_per_step + r], 0, 0))
    return [make(r) for r in range(n_pages_per_step)]


def _idx_sample_kernel(t_new, pps, pt_ref, qi_ref, w_ref, kin_ref, *rest):
    pages = rest[:pps]
    sc_out, scn_out = rest[pps:]
    j = pl.program_id(1)
    qi = qi_ref[0]
    w = w_ref[0]
    qib = qi.astype(BF16)

    def token_scores(y):
        z = jnp.maximum(y, 0.0) * w
        return jnp.sum(z.reshape(t_new, IDX_HEADS, y.shape[1]), axis=1)

    for r in range(pps):
        y = _dot(qib, pages[r][0].astype(BF16))
        sc_out[0, :, r * PAGE_SIZE:(r + 1) * PAGE_SIZE] = token_scores(y)

    @pl.when(j == 0)
    def _():
        qf = qib.astype(F32)
        lane = _iota((t_new, LANES), 1)
        tok = _iota((t_new, LANES), 0)
        new = jnp.full((t_new, LANES), -jnp.inf, F32)
        for tk in range(t_new):
            kr = kin_ref[0, tk:tk + 1, :].astype(BF16).astype(F32)
            col = token_scores(jnp.sum(qf * kr, axis=-1, keepdims=True))
            new = jnp.where((lane == tk) & (tok >= tk), col, new)
        scn_out[0] = new


def _idx_sample(page_table, qi32, w32, ki_new, cache_idx_kt):
    db, rows, _ = qi32.shape
    t_new = rows // IDX_HEADS
    n_pages = page_table.shape[1]
    pps = min(IDX_PAGES_PER_STEP, n_pages)
    steps = n_pages // pps
    span = pps * PAGE_SIZE
    grid_spec = pltpu.PrefetchScalarGridSpec(
        num_scalar_prefetch=1, grid=(db, steps),
        in_specs=[pl.BlockSpec((1, rows, IDX_DIM), lambda b, j, pt: (b, 0, 0)),
                  pl.BlockSpec((1, rows, 1), lambda b, j, pt: (b, 0, 0)),
                  pl.BlockSpec((1, t_new, IDX_DIM), lambda b, j, pt: (b, 0, 0))]
        + _page_specs((1, IDX_DIM, PAGE_SIZE), pps),
        out_specs=[pl.BlockSpec((1, t_new, span), lambda b, j, pt: (b, 0, j)),
                   pl.BlockSpec((1, t_new, LANES), lambda b, j, pt: (b, 0, 0))])
    return pl.pallas_call(
        functools.partial(_idx_sample_kernel, t_new, pps), grid_spec=grid_spec,
        out_shape=[jax.ShapeDtypeStruct((db, t_new, n_pages * PAGE_SIZE), F32),
                   jax.ShapeDtypeStruct((db, t_new, LANES), F32)],
        compiler_params=_cparams(("parallel", "arbitrary")), name="idx_sample",
    )(page_table, qi32, w32, ki_new, *([cache_idx_kt] * pps))


def _thr_sample_kernel(topk, sc3_ref, scn3_ref, b_out, bn_out, sc_ref, scn_ref):
    kth = jnp.float32(topk)
    grp, t_new, width = sc3_ref.shape
    n_chunks = width // LANES
    sc_ref[...] = sc3_ref[...].reshape(THR_ROWS, width)
    scn_ref[...] = scn3_ref[...].reshape(THR_ROWS, LANES)

    def count(pred):
        return (jnp.sum(pred(sc_ref[...]).astype(F32), axis=-1, keepdims=True)
                + jnp.sum(pred(scn_ref[...]).astype(F32), axis=-1, keepdims=True))

    thr, cnt_thr = _kth_largest(lambda t: count(lambda x: x >= t), kth, (THR_ROWS, 1))
    has_tie = jnp.max(cnt_thr) > kth

    @pl.when(jnp.logical_not(has_tie))
    def _():
        b_out[...] = jnp.where(sc_ref[...] >= thr, 0.0, NEG).reshape(grp, t_new, width)
        bn_out[...] = jnp.where(scn_ref[...] >= thr, 0.0, NEG).reshape(grp, t_new, LANES)

    @pl.when(has_tie)
    def _():
        need = kth - count(lambda x: x > thr)
        utri = (_iota((LANES, LANES), 0) < _iota((LANES, LANES), 1)).astype(BF16)
        carry = jnp.zeros((THR_ROWS, 1), F32)
        for c in range(n_chunks + 1):
            x = sc_ref[:, c * LANES:(c + 1) * LANES] if c < n_chunks else scn_ref[...]
            eq = x == thr
            before = _dot(eq.astype(BF16), utri) + carry
            sel = (x > thr) | (eq & (before < need))
            bias = jnp.where(sel, 0.0, NEG).reshape(grp, t_new, LANES)
            if c < n_chunks:
                b_out[:, :, c * LANES:(c + 1) * LANES] = bias
            else:
                bn_out[...] = bias
            carry = carry + jnp.sum(eq.astype(F32), axis=-1, keepdims=True)


def _thr_sample(sc, scn, topk):
    db, t_new, width = sc.shape
    grp = THR_ROWS // t_new
    spec = lambda w: pl.BlockSpec((grp, t_new, w), lambda i: (i, 0, 0))
    return pl.pallas_call(
        functools.partial(_thr_sample_kernel, topk), grid=(db // grp,),
        in_specs=[spec(width), spec(LANES)], out_specs=[spec(width), spec(LANES)],
        out_shape=[jax.ShapeDtypeStruct((db, t_new, width), F32), jax.ShapeDtypeStruct((db, t_new, LANES), F32)],
        scratch_shapes=[pltpu.VMEM((THR_ROWS, width), F32), pltpu.VMEM((THR_ROWS, LANES), F32)],
        compiler_params=_cparams(("parallel",)), name="thr_sample",
    )(sc, scn)


def _attn_sample_kernel(t_new, pps, pt_ref, q_ref, bias_ref, biasn_ref, kn_ref, vn_ref, *rest):
    kpages = rest[:pps]
    vpages = rest[pps:2 * pps]
    o_ref, m_ref, l_ref, acc_ref, qbd_ref = rest[2 * pps:]
    j = pl.program_id(1)
    rows = t_new * B_HEADS
    mask8 = (_iota((SUBLANES, B_WIDTH), 1) // HEAD_DIM) == _iota((SUBLANES, B_WIDTH), 0)

    def per_token(ref, width):
        return jnp.concatenate(
            [jnp.broadcast_to(ref[0, i:i + 1, :], (B_HEADS, width)) for i in range(t_new)], axis=0)

    @pl.when(j == 0)
    def _():
        m_ref[...] = jnp.full((rows, 1), NEG, F32)
        l_ref[...] = jnp.zeros((rows, 1), F32)
        acc_ref[...] = jnp.zeros((rows, B_WIDTH), F32)
        qbd_ref[...] = jnp.concatenate(
            [jnp.where(mask8, jnp.broadcast_to(q_ref[0, i:i + 1, :] * ATTN_SCALE, (B_HEADS, B_WIDTH)), 0.0)
             for i in range(t_new)], axis=0).astype(BF16)

    qbd = qbd_ref[...]
    span = pps * PAGE_SIZE
    s = jnp.concatenate([_dot(qbd, kpages[r][0].astype(BF16)) for r in range(pps)], axis=1)
    s = s + per_token(bias_ref, span)
    m_old = m_ref[...]
    m_new = jnp.maximum(m_old, jnp.max(s, axis=-1, keepdims=True))
    alpha = jnp.exp(m_old - m_new)
    p = jnp.exp(s - m_new)
    l_ref[...] = alpha * l_ref[...] + jnp.sum(p, axis=-1, keepdims=True)
    pb = p.astype(BF16)
    pv = jnp.zeros((rows, B_WIDTH), F32)
    for r in range(pps):
        pv = pv + _dot_nt(pb[:, r * PAGE_SIZE:(r + 1) * PAGE_SIZE], vpages[r][0].astype(BF16))
    acc_ref[...] = alpha * acc_ref[...] + pv
    m_ref[...] = m_new

    @pl.when(j == pl.num_programs(1) - 1)
    def _():
        qf = qbd.astype(F32)
        bn = per_token(biasn_ref, LANES)
        cols = []
        for tk in range(t_new):
            kr = kn_ref[0, tk:tk + 1, :].astype(BF16).astype(F32)
            cols.append(jnp.sum(qf * kr, axis=-1, keepdims=True) + bn[:, tk:tk + 1])
        m_old = m_ref[...]
        m_new = m_old
        for cval in cols:
            m_new = jnp.maximum(m_new, cval)
        alpha = jnp.exp(m_old - m_new)
        l = alpha * l_ref[...]
        acc = alpha * acc_ref[...]
        for tk in range(t_new):
            pk = jnp.exp(cols[tk] - m_new)
            l = l + pk
            acc = acc + pk.astype(BF16).astype(F32) * vn_ref[0, tk:tk + 1, :].astype(BF16).astype(F32)
        o = acc / l
        for i in range(t_new):
            slab = jnp.where(mask8, o[i * B_HEADS:(i + 1) * B_HEADS, :], 0.0)
            o_ref[0, i * B_HEADS:(i + 1) * B_HEADS, :] = jnp.broadcast_to(
                jnp.sum(slab, axis=0, keepdims=True), (B_HEADS, B_WIDTH))


def _attn_sample(page_table, q3, bias3, biasn3, k_new, v_new, cache_kt, cache_vt):
    db, t_new, _ = q3.shape
    n_pages = page_table.shape[1]
    pps = min(ATTN_PAGES_PER_STEP, n_pages)
    steps = n_pages // pps
    span = pps * PAGE_SIZE
    rows = t_new * B_HEADS
    per_b = lambda w: pl.BlockSpec((1, t_new, w), lambda b, j, pt: (b, 0, 0))
    grid_spec = pltpu.PrefetchScalarGridSpec(
        num_scalar_prefetch=1, grid=(db, steps),
        in_specs=[per_b(B_WIDTH),
                  pl.BlockSpec((1, t_new, span), lambda b, j, pt: (b, 0, j)),
                  per_b(LANES), per_b(B_WIDTH), per_b(B_WIDTH)]
        + _page_specs((1, B_WIDTH, PAGE_SIZE), pps)
        + _page_specs((1, B_WIDTH, PAGE_SIZE), pps),
        out_specs=pl.BlockSpec((1, rows, B_WIDTH), lambda b, j, pt: (b, 0, 0)),
        scratch_shapes=[pltpu.VMEM((rows, 1), F32), pltpu.VMEM((rows, 1), F32),
                        pltpu.VMEM((rows, B_WIDTH), F32), pltpu.VMEM((rows, B_WIDTH), BF16)])
    return pl.pallas_call(
        functools.partial(_attn_sample_kernel, t_new, pps), grid_spec=grid_spec,
        out_shape=jax.ShapeDtypeStruct((db, rows, B_WIDTH), F32),
        compiler_params=_cparams(("parallel", "arbitrary")), name="attn_sample",
    )(page_table, q3, bias3, biasn3, k_new, v_new, *([cache_kt] * pps), *([cache_vt] * pps))


def _merge_kernel(fused_c, x_ref, ma_ref, yb_ref, sgb_ref, c_ref, sgc_ref, wo_ref, g_ref, *rest):
    if fused_c:
        mkt_ref, mvt_ref, y_out = rest
        yc = _memattn_rows(c_ref[...], mkt_ref[0], mvt_ref[0])
    else:
        (y_out,) = rest
        yc = c_ref[...].astype(F32)
    mb = (yb_ref[...].astype(F32) * sgb_ref[...].astype(F32)).astype(BF16)
    mc = (yc * sgc_ref[...].astype(F32)).astype(BF16)
    o = (_dot(ma_ref[...], wo_ref[0:A_WIDTH, :])
         + _dot(mb, wo_ref[A_WIDTH:A_WIDTH + B_WIDTH, :])
         + _dot(mc, wo_ref[A_WIDTH + B_WIDTH:, :]))
    z = x_ref[...] + o
    ms = jnp.mean(z * z, axis=-1, keepdims=True)
    y_out[...] = z * lax.rsqrt(ms + EPS) * g_ref[...]


def _merge(x, mixa, yb, sgb, c, sgc, wo, g_final, mem=None, rows_per_mem=None):
    n = x.shape[0]
    tm = min(MERGE_TM, n)
    row = lambda w: pl.BlockSpec((tm, w), lambda i: (i, 0))
    in_specs = [row(D_MODEL), row(A_WIDTH), row(B_WIDTH), row(B_WIDTH), row(C_WIDTH), row(C_WIDTH),
                pl.BlockSpec(wo.shape, lambda i: (0, 0)), pl.BlockSpec(g_final.shape, lambda i: (0, 0))]
    args = [x, mixa, yb, sgb, c, sgc, wo, g_final]
    if mem is not None:
        tiles_per_mem = rows_per_mem // tm
        mem_spec = pl.BlockSpec((1, C_WIDTH, N_MEM), lambda i: (i // tiles_per_mem, 0, 0))
        in_specs += [mem_spec, mem_spec]
        args += list(mem)
    return pl.pallas_call(
        functools.partial(_merge_kernel, mem is not None), grid=(n // tm,),
        in_specs=in_specs, out_specs=row(D_MODEL),
        out_shape=jax.ShapeDtypeStruct((n, D_MODEL), F32),
        compiler_params=_cparams(("parallel",)), name="merge",
    )(*args)


def _rope_tables(pos):
    inv = ROPE_THETA ** (-jnp.arange(0, HEAD_DIM, 2, dtype=F32) / HEAD_DIM)
    ang = pos.astype(F32)[:, None] * inv[None, :]
    c = jnp.cos(ang)
    s = jnp.sin(ang)
    return jnp.concatenate([c, c, c, c], axis=1), jnp.concatenate([-s, s, -s, s], axis=1)


def _heads_last(t, heads):
    b, _, n = t.shape
    return t.reshape(b, heads, HEAD_DIM, n).transpose(0, 3, 1, 2)


def _channels_first(t):
    p, n, heads, d = t.shape
    return t.transpose(0, 2, 3, 1).reshape(p, heads * d, n)


def kernel(x_prompt, x_sample, mem_prompt, cache_k, cache_v, cache_idx_k, cache_mem_k, cache_mem_v,
           page_table, g_norm, w_in, w_spatial, b_spatial, g_v, w_mem_kv, g_mem, w_out, g_final):
    batch, seq, _ = x_prompt.shape
    db, t_new, _ = x_sample.shape
    n_pages = page_table.shape[1]
    past = n_pages * PAGE_SIZE

    ki_cols = w_in[:, 3328:3392]
    wi_cols = w_in[:, 3392:3400]
    w_r = jnp.concatenate(
        [w_in[:, :3328], w_in[:, 3400:3912], ki_cols, ki_cols, wi_cols,
         jnp.zeros((D_MODEL, LANES - IDX_HEADS), F32)], axis=1).astype(BF16)
    g_norm2 = g_norm.reshape(1, D_MODEL)
    g_v2 = g_v.reshape(1, A_WIDTH)
    bs_prompt = jnp.repeat(b_spatial.T, HEAD_DIM, axis=1)
    reps = CHUNK // t_new
    ws_sample = jnp.tile(w_spatial[:, :t_new, :t_new], (1, reps, reps))
    bs_sample = jnp.tile(jnp.repeat(b_spatial[:, :t_new].T, HEAD_DIM, axis=1), (reps, 1))
    wo_b = w_out.astype(BF16)
    g_final2 = g_final.reshape(1, D_MODEL)

    cos_p, sin_p = _rope_tables(jnp.arange(seq, dtype=jnp.int32))
    pos_s = past + jnp.arange(t_new, dtype=jnp.int32)
    cos_s, sin_s = _rope_tables(jnp.tile(pos_s, TM // t_new))

    xp = x_prompt.reshape(batch * seq, D_MODEL)
    (kt, vt, kit, kbf, vtb, ki2, qs, qis, wt, sgb, mixa, qc, sgc) = _proj(
        xp, g_norm2, w_r, cos_p, sin_p, w_spatial, bs_prompt, g_v2, CHUNK, seq // TM, False)
    mkt, mvt = _memkv(mem_prompt.reshape(batch * N_MEM, D_MODEL), g_mem.reshape(1, D_MODEL),
                      w_mem_kv.astype(BF16))
    yb = _dsa_prompt(qs, qis, wt, ki2, kbf, vtb, batch, seq)
    y_prompt = _merge(xp, mixa, yb, sgb, qc, sgc, wo_b, g_final2, mem=(mkt, mvt),
                      rows_per_mem=seq).reshape(batch, seq, D_MODEL)

    ns = db * t_new
    xs = x_sample.reshape(ns, D_MODEL)
    (q_s, k_s, v_s, qi_s, ki_s, wi_s, sgb_s, mixa_s, qc_s, sgc_s, vn_s) = _proj(
        xs, g_norm2, w_r, cos_s, sin_s, ws_sample, bs_sample, g_v2, t_new, 1, True)
    sc, scn = _idx_sample(page_table, qi_s.reshape(db, t_new * IDX_HEADS, IDX_DIM),
                          wi_s.reshape(db, t_new * IDX_HEADS, 1), ki_s.reshape(db, t_new, IDX_DIM),
                          cache_idx_k.transpose(0, 2, 1))
    topk_s = min(TOPK_MAX, (past + t_new) // 4)
    bias, biasn = _thr_sample(sc, scn, topk_s)
    yb_pad = _attn_sample(page_table, q_s.reshape(db, t_new, B_WIDTH), bias, biasn,
                          k_s.reshape(db, t_new, B_WIDTH),
                          v_s.reshape(db, t_new, B_WIDTH), _channels_first(cache_k), _channels_first(cache_v))
    yb_s = yb_pad[:, ::B_HEADS, :].reshape(ns, B_WIDTH)
    yc_pad = _memattn_sample(qc_s.reshape(db, t_new, C_WIDTH),
                             _channels_first(cache_mem_k), _channels_first(cache_mem_v))
    yc_s = yc_pad[:, ::SUBLANES, :].reshape(ns, C_WIDTH)
    y_sample = _merge(xs, mixa_s, yb_s, sgb_s, yc_s, sgc_s, wo_b, g_final2).reshape(db, t_new, D_MODEL)

    return (y_prompt, y_sample,
            _heads_last(kt, B_HEADS), _heads_last(vt, B_HEADS), kit.transpose(0, 2, 1),
            _heads_last(mkt, C_HEADS), _heads_last(mvt, C_HEADS),
            k_s.reshape(db, t_new, B_HEADS, HEAD_DIM), v_s.reshape(db, t_new, B_HEADS, HEAD_DIM),
            ki_s.reshape(db, t_new, IDX_DIM), vn_s.reshape(db, t_new, A_WIDTH))
```
